```python
import math
import jax
import jax.numpy as jnp
from jax import lax
import numpy as np

D_MODEL = 1024
BATCH = 8
SEQ = 2048
DEPTH = 1
DEC_BATCH = 128
DEC_SEQ = 4
PAST_LEN = 16384
PAGE_SIZE = 128

N_META = 16
MIX_WIDTH = D_MODEL
RET_WIDTH = MIX_WIDTH // 2
RET_HEADS = 4
RET_DK = RET_WIDTH // RET_HEADS
RET_DV = RET_WIDTH // RET_HEADS
RET_CHUNK = 128
ROPE_THETA = 10000.0
POOL_WIDTH = MIX_WIDTH - RET_WIDTH
POOL_WINDOWS = (2, 4, 8, 16)
POOL_GROUP = POOL_WIDTH // len(POOL_WINDOWS)
POOL_BUF = max(POOL_WINDOWS) - 1
IN_WIDTH = 4 * RET_WIDTH + POOL_WIDTH
D_FF = 2816
ALPHA = (2.0 * DEPTH) ** 0.25
BETA = (8.0 * DEPTH) ** -0.25
LN_EPS = 1e-5
GN_EPS = 1e-5

kernel_name = "hymba_retention_pool_macaron_deepnorm"


def _layer_norm(x, g, b):
    xf = x.astype(jnp.float32)
    mu = jnp.mean(xf, axis=-1, keepdims=True)
    var = jnp.mean(jnp.square(xf - mu), axis=-1, keepdims=True)
    y = (xf - mu) * lax.rsqrt(var + LN_EPS) * g.astype(jnp.float32) + b.astype(jnp.float32)
    return y.astype(x.dtype)


def _swiglu(x, w_gate, w_up, w_down):
    return (jax.nn.silu(x @ w_gate) * (x @ w_up)) @ w_down


def _log_gamma():
    return jnp.log(1.0 - jnp.exp2(-5.0 - jnp.arange(RET_HEADS, dtype=jnp.float32)))


def _rope(x, pos):
    d = x.shape[-1]
    inv_freq = ROPE_THETA ** (-jnp.arange(0, d, 2, dtype=jnp.float32) / d)
    ang = pos.astype(jnp.float32)[:, None] * inv_freq[None, :]
    cos, sin = jnp.cos(ang), jnp.sin(ang)
    x1, x2 = x[..., : d // 2], x[..., d // 2:]
    return jnp.concatenate([x1 * cos - x2 * sin, x2 * cos + x1 * sin], axis=-1)


def _retention_chunk(q, k, v, s):
    lg = _log_gamma()
    c = q.shape[2]
    idx = jnp.arange(c, dtype=jnp.float32)
    diff = idx[:, None] - idx[None, :]
    decay = jnp.where(diff >= 0, jnp.exp(lg[:, None, None] * jnp.maximum(diff, 0.0)), 0.0)
    scores = jnp.einsum('bhid,bhjd->bhij', q, k) * decay[None]
    q_dec = q * jnp.exp(lg[:, None] * (idx + 1.0)[None, :])[None, :, :, None]
    o = jnp.einsum('bhij,bhjv->bhiv', scores, v) + jnp.einsum('bhid,bhdv->bhiv', q_dec, s)
    k_dec = k * jnp.exp(lg[:, None] * (c - 1.0 - idx)[None, :])[None, :, :, None]
    s_new = jnp.exp(lg * c)[None, :, None, None] * s + jnp.einsum('bhjd,bhjv->bhdv', k_dec, v)
    return o, s_new


def _retention(q, k, v, s0, lead):
    o_lead, s = _retention_chunk(q[:, :, :lead], k[:, :, :lead], v[:, :, :lead], s0)
    rest = q.shape[2] - lead
    if rest == 0:
        return o_lead, s
    n_chunks = rest // RET_CHUNK
    bsz, heads = q.shape[0], q.shape[1]

    def to_chunks(t):
        return t[:, :, lead:].reshape(bsz, heads, n_chunks, RET_CHUNK, t.shape[-1]).transpose(2, 0, 1, 3, 4)

    def step(state, qkv):
        o, state = _retention_chunk(qkv[0], qkv[1], qkv[2], state)
        return state, o

    s, o_rest = lax.scan(step, s, (to_chunks(q), to_chunks(k), to_chunks(v)))
    o_rest = o_rest.transpose(1, 2, 0, 3, 4).reshape(bsz, heads, rest, v.shape[-1])
    return jnp.concatenate([o_lead, o_rest], axis=2), s


def _multi_pool(p, prefix, pos, pool_w, pool_scale):
    bsz, seq_len, _ = p.shape
    xp = jnp.concatenate([prefix.astype(jnp.float32), p.astype(jnp.float32)], axis=1)
    c0 = jnp.concatenate([jnp.zeros((bsz, 1, POOL_WIDTH), jnp.float32), jnp.cumsum(xp, axis=1)], axis=1)
    end = c0[:, POOL_BUF + 1: POOL_BUF + 1 + seq_len]
    cur = xp[:, POOL_BUF:]
    outs = []
    for gi, w in enumerate(POOL_WINDOWS):
        sl = slice(gi * POOL_GROUP, (gi + 1) * POOL_GROUP)
        start = c0[:, POOL_BUF + 1 - w: POOL_BUF + 1 - w + seq_len, sl]
        cnt = jnp.minimum(pos + 1, w).astype(jnp.float32)[None, :, None]
        d = (end[..., sl] - start) / cnt - cur[..., sl]
        outs.append(jnp.einsum('blc,cd->bld', d.astype(p.dtype), pool_w[gi]))
    out = jnp.concatenate(outs, axis=-1) * pool_scale
    new_buf = xp[:, -POOL_BUF:].astype(p.dtype)
    return out, new_buf


def _mixer(h, pos, s_ret, pool_prefix, lead, w_in, pool_w, pool_scale, w_out):
    bsz, seq_len, _ = h.shape
    proj = h @ w_in
    q, k, v, g, p = jnp.split(proj, [RET_WIDTH, 2 * RET_WIDTH, 3 * RET_WIDTH, 4 * RET_WIDTH], axis=-1)

    def heads(t, d):
        return t.reshape(bsz, seq_len, RET_HEADS, d).transpose(0, 2, 1, 3).astype(jnp.float32)

    qh = _rope(heads(q, RET_DK), pos) * (RET_DK ** -0.5)
    kh = _rope(heads(k, RET_DK), pos)
    vh = heads(v, RET_DV)
    o, s_new = _retention(qh, kh, vh, s_ret.astype(jnp.float32), lead)
    mu = jnp.mean(o, axis=-1, keepdims=True)
    var = jnp.mean(jnp.square(o - mu), axis=-1, keepdims=True)
    o = (o - mu) * lax.rsqrt(var + GN_EPS)
    o = o.transpose(0, 2, 1, 3).reshape(bsz, seq_len, RET_WIDTH).astype(h.dtype)
    ret_out = jax.nn.silu(g) * o
    pool_out, new_buf = _multi_pool(p, pool_prefix, pos, pool_w, pool_scale)
    y = jnp.concatenate([ret_out, pool_out], axis=-1) @ w_out
    return y, s_new.astype(h.dtype), new_buf


def _layer(x, pos, s_ret, pool_prefix, lead,
           f1_gate, f1_up, f1_down, ln1_g, ln1_b, w_in, pool_w, pool_scale, w_out,
           ln2_g, ln2_b, f2_gate, f2_up, f2_down, ln3_g, ln3_b):
    x = _layer_norm(ALPHA * x + 0.5 * _swiglu(x, f1_gate, f1_up, f1_down), ln1_g, ln1_b)
    m, s_new, buf = _mixer(x, pos, s_ret, pool_prefix, lead, w_in, pool_w, pool_scale, w_out)
    x = _layer_norm(ALPHA * x + m, ln2_g, ln2_b)
    x = _layer_norm(ALPHA * x + 0.5 * _swiglu(x, f2_gate, f2_up, f2_down), ln3_g, ln3_b)
    return x, s_new, buf


def setup_inputs(seed: int = 0) -> dict:
    key = jax.random.key(seed)
    ks = jax.random.split(key, 24)
    f32 = jnp.float32

    def nrm(k, shape, scale=1.0):
        return jax.random.normal(k, shape, f32) * scale

    v_scale = jnp.ones((IN_WIDTH,), f32).at[2 * RET_WIDTH: 3 * RET_WIDTH].set(BETA)
    return {
        "x_prompt": nrm(ks[0], (BATCH, SEQ, D_MODEL)),
        "x_sample": nrm(ks[1], (DEC_BATCH, DEC_SEQ, D_MODEL)),
        "state_ret": nrm(ks[2], (DEPTH, DEC_BATCH, RET_HEADS, RET_DK, RET_DV)),
        "state_pool": nrm(ks[3], (DEPTH, DEC_BATCH, POOL_BUF, POOL_WIDTH)),
        "meta_tokens": nrm(ks[4], (N_META, D_MODEL)),
        "ffn1_w_gate": nrm(ks[5], (DEPTH, D_MODEL, D_FF), D_MODEL ** -0.5),
        "ffn1_w_up": nrm(ks[6], (DEPTH, D_MODEL, D_FF), D_MODEL ** -0.5),
        "ffn1_w_down": nrm(ks[7], (DEPTH, D_FF, D_MODEL), BETA * D_FF ** -0.5),
        "ln1_g": 1.0 + nrm(ks[8], (DEPTH, D_MODEL), 0.05),
        "ln1_b": nrm(ks[9], (DEPTH, D_MODEL), 0.02),
        "w_in": nrm(ks[10], (DEPTH, D_MODEL, IN_WIDTH), D_MODEL ** -0.5) * v_scale,
        "pool_w": nrm(ks[11], (DEPTH, len(POOL_WINDOWS), POOL_GROUP, POOL_GROUP), POOL_GROUP ** -0.5),
        "pool_scale": 1.0 + nrm(ks[12], (DEPTH, POOL_WIDTH), 0.1),
        "w_out": nrm(ks[13], (DEPTH, MIX_WIDTH, D_MODEL), BETA * MIX_WIDTH ** -0.5),
        "ln2_g": 1.0 + nrm(ks[14], (DEPTH, D_MODEL), 0.05),
        "ln2_b": nrm(ks[15], (DEPTH, D_MODEL), 0.02),
        "ffn2_w_gate": nrm(ks[16], (DEPTH, D_MODEL, D_FF), D_MODEL ** -0.5),
        "ffn2_w_up": nrm(ks[17], (DEPTH, D_MODEL, D_FF), D_MODEL ** -0.5),
        "ffn2_w_down": nrm(ks[18], (DEPTH, D_FF, D_MODEL), BETA * D_FF ** -0.5),
        "ln3_g": 1.0 + nrm(ks[19], (DEPTH, D_MODEL), 0.05),
        "ln3_b": nrm(ks[20], (DEPTH, D_MODEL), 0.02),
    }


def reference(x_prompt, x_sample, state_ret, state_pool, meta_tokens,
              ffn1_w_gate, ffn1_w_up, ffn1_w_down, ln1_g, ln1_b, w_in, pool_w, pool_scale, w_out,
              ln2_g, ln2_b, ffn2_w_gate, ffn2_w_up, ffn2_w_down, ln3_g, ln3_b):
    bp = x_prompt.shape[0]
    meta = jnp.broadcast_to(meta_tokens.astype(x_prompt.dtype)[None], (bp, N_META, x_prompt.shape[2]))
    hp = jnp.concatenate([meta, x_prompt], axis=1)
    hs = x_sample
    pos_p = jnp.arange(hp.shape[1], dtype=jnp.int32)
    pos_s = PAST_LEN + jnp.arange(hs.shape[1], dtype=jnp.int32)
    ret_p, pool_p, ret_s, pool_s = [], [], [], []
    for i in range(DEPTH):
        lp = (ffn1_w_gate[i], ffn1_w_up[i], ffn1_w_down[i], ln1_g[i], ln1_b[i], w_in[i], pool_w[i],
              pool_scale[i], w_out[i], ln2_g[i], ln2_b[i], ffn2_w_gate[i], ffn2_w_up[i], ffn2_w_down[i],
              ln3_g[i], ln3_b[i])
        s0 = jnp.zeros((bp, RET_HEADS, RET_DK, RET_DV), hp.dtype)
        b0 = jnp.zeros((bp, POOL_BUF, POOL_WIDTH), hp.dtype)
        hp, sr, sb = _layer(hp, pos_p, s0, b0, N_META, *lp)
        ret_p.append(sr)
        pool_p.append(sb)
        hs, sr, sb = _layer(hs, pos_s, state_ret[i], state_pool[i], hs.shape[1], *lp)
        ret_s.append(sr)
        pool_s.append(sb)
    y_prompt = hp[:, N_META:]
    return (y_prompt, hs, jnp.stack(ret_p), jnp.stack(pool_p), jnp.stack(ret_s), jnp.stack(pool_s))
```

```python
import functools
import math

import jax
import jax.numpy as jnp
import numpy as np
from jax import lax
from jax.experimental import pallas as pl
from jax.experimental.pallas import tpu as pltpu

F32 = jnp.float32
BF16 = jnp.bfloat16

D_MODEL = 1024
D_FF = 2816
N_META = 16
PAST_LEN = 16384
RET_HEADS = 4
HEAD_DIM = 128
RET_WIDTH = RET_HEADS * HEAD_DIM
RET_CHUNK = 128
ROPE_THETA = 10000.0
POOL_WINDOWS = (2, 4, 8, 16)
POOL_GROUP = 128
POOL_WIDTH = POOL_GROUP * len(POOL_WINDOWS)
POOL_BUF = max(POOL_WINDOWS) - 1
IN_WIDTH = 4 * RET_WIDTH + POOL_WIDTH
DEPTH = 1
ALPHA = (2.0 * DEPTH) ** 0.25
LN_EPS = 1e-5
GN_EPS = 1e-5
QK_SCALE = HEAD_DIM ** -0.5
GAMMAS = tuple(1.0 - 2.0 ** (-5.0 - h) for h in range(RET_HEADS))

VMEM_LIMIT_BYTES = 56 * 1024 * 1024
FFN_TOKEN_TILE = 512
FFN_COL_CHUNK = 256
MIX_TOKEN_TILE = 256
DEC_SEQ_BLOCK = 16
BF16_ROWS = 16


def _layer_norm(z, g, b):
    mu = jnp.mean(z, axis=-1, keepdims=True)
    zc = z - mu
    var = jnp.mean(zc * zc, axis=-1, keepdims=True)
    return zc * lax.rsqrt(var + LN_EPS) * g + b


def _silu(x):
    return x * jax.nn.sigmoid(x)


def _dot(a, b):
    return jnp.dot(a, b, preferred_element_type=F32)


def _dot_nt(a, b):
    return lax.dot_general(a, b, (((1,), (1,)), ((), ())), preferred_element_type=F32)


def _dot_tn(a, b):
    return lax.dot_general(a, b, (((0,), (0,)), ((), ())), preferred_element_type=F32)


def _ffn_kernel(x_ref, wg_ref, wu_ref, wd_ref, lng_ref, lnb_ref, *rest, with_proj):
    if with_proj:
        win_ref, o_ref, proj_ref, h_ref = rest
    else:
        o_ref, h_ref = rest
    x = x_ref[...]
    xb = x.astype(BF16)
    for c in range(D_FF // FFN_COL_CHUNK):
        sl = slice(c * FFN_COL_CHUNK, (c + 1) * FFN_COL_CHUNK)
        g = _dot(xb, wg_ref[:, sl])
        u = _dot(xb, wu_ref[:, sl])
        h_ref[:, sl] = (_silu(g) * u).astype(BF16)
    y = _dot(h_ref[...], wd_ref[...])
    out = _layer_norm(ALPHA * x + 0.5 * y, lng_ref[...], lnb_ref[...])
    o_ref[...] = out
    if with_proj:
        proj_ref[...] = _dot(out.astype(BF16), win_ref[...])


def _const_spec(shape):
    zeros = (0,) * len(shape)
    return pl.BlockSpec(shape, lambda *_: zeros, pipeline_mode=pl.Buffered(1))


def _ffn(x, wg, wu, wd, lng, lnb, w_in=None):
    n = x.shape[0]
    tm = FFN_TOKEN_TILE if n % FFN_TOKEN_TILE == 0 else n
    with_proj = w_in is not None
    row = lambda i: (i, 0)
    in_specs = [
        pl.BlockSpec((tm, D_MODEL), row),
        _const_spec((D_MODEL, D_FF)),
        _const_spec((D_MODEL, D_FF)),
        _const_spec((D_FF, D_MODEL)),
        _const_spec((1, D_MODEL)),
        _const_spec((1, D_MODEL)),
    ]
    args = [x, wg, wu, wd, lng, lnb]
    out_shape = [jax.ShapeDtypeStruct((n, D_MODEL), F32)]
    out_specs = [pl.BlockSpec((tm, D_MODEL), row)]
    if with_proj:
        in_specs.append(_const_spec((D_MODEL, IN_WIDTH)))
        args.append(w_in)
        out_shape.append(jax.ShapeDtypeStruct((n, IN_WIDTH), F32))
        out_specs.append(pl.BlockSpec((tm, IN_WIDTH), row))
    return pl.pallas_call(
        functools.partial(_ffn_kernel, with_proj=with_proj),
        grid=(n // tm,),
        in_specs=in_specs,
        out_specs=out_specs,
        out_shape=out_shape,
        scratch_shapes=[pltpu.VMEM((tm, D_FF), BF16)],
        compiler_params=pltpu.CompilerParams(
            dimension_semantics=("arbitrary",), vmem_limit_bytes=VMEM_LIMIT_BYTES),
        name="ffn_proj" if with_proj else "ffn",
    )(*args)


def _rope_tables(positions):
    half = HEAD_DIM // 2
    inv_freq = ROPE_THETA ** (-np.arange(0, HEAD_DIM, 2, dtype=np.float64) / HEAD_DIM)
    ang = np.asarray(positions, np.float64)[:, None] * inv_freq[None, :]
    cos, sin = np.cos(ang), np.sin(ang)
    assert cos.shape[1] == half
    return (np.concatenate([cos, cos], axis=1).astype(np.float32),
            np.concatenate([-sin, sin], axis=1).astype(np.float32))


def _decay_tables(chunk, seq_len):
    r = np.arange(chunk)
    seq, idx = r // seq_len, (r % seq_len).astype(np.float64)
    same = seq[:, None] == seq[None, :]
    diff = idx[:, None] - idx[None, :]
    mask, qdec, kdec = [], [], []
    for gamma in GAMMAS:
        lg = math.log(gamma)
        mask.append(np.where(same & (diff >= 0), np.exp(lg * np.maximum(diff, 0.0)), 0.0) * QK_SCALE)
        qdec.append(np.broadcast_to((np.exp(lg * (idx + 1.0)) * QK_SCALE)[:, None], (chunk, HEAD_DIM)))
        kdec.append(np.broadcast_to(np.exp(lg * (seq_len - 1.0 - idx))[:, None], (chunk, HEAD_DIM)))
    to32 = lambda t: np.stack(t).astype(np.float32)
    return to32(mask), to32(qdec), to32(kdec)


def _rope(x, cos, sin):
    return x * cos + pltpu.roll(x, HEAD_DIM // 2, 1) * sin


def _group_norm(o):
    mu = jnp.mean(o, axis=-1, keepdims=True)
    oc = o - mu
    var = jnp.mean(oc * oc, axis=-1, keepdims=True)
    return oc * lax.rsqrt(var + GN_EPS)


def _prompt_mixer_kernel(q_ref, k_ref, v_ref, g_ref, p_ref, x1_ref, cos_ref, sin_ref,
                         km_ref, vm_ref, pm_ref, cosm_ref, sinm_ref, kdecm_ref,
                         mask_ref, qdec_ref, kdec_ref, poolw_ref, pscale_ref, wout_ref,
                         lng_ref, lnb_ref,
                         o_ref, sret_ref, spool_ref,
                         s_ref, xp_ref, mix_ref):
    c = pl.program_id(1)
    tile = q_ref.shape[0]
    hist = N_META

    @pl.when(c == 0)
    def _init_from_meta():
        for h in range(RET_HEADS):
            hs = slice(h * HEAD_DIM, (h + 1) * HEAD_DIM)
            kr = _rope(km_ref[:, hs], cosm_ref[...], sinm_ref[...])
            kd = (kr * kdecm_ref[h]).astype(BF16)
            s_ref[h] = _dot_tn(kd, vm_ref[:, hs].astype(BF16))
        xp_ref[0:hist, :] = pm_ref[...]

    for ci in range(tile // RET_CHUNK):
        rows = slice(ci * RET_CHUNK, (ci + 1) * RET_CHUNK)
        cos, sin = cos_ref[rows, :], sin_ref[rows, :]
        for h in range(RET_HEADS):
            hs = slice(h * HEAD_DIM, (h + 1) * HEAD_DIM)
            qr = _rope(q_ref[rows, hs], cos, sin)
            kr = _rope(k_ref[rows, hs], cos, sin)
            vb = v_ref[rows, hs].astype(BF16)
            scores = _dot_nt(qr.astype(BF16), kr.astype(BF16)) * mask_ref[h]
            s = s_ref[h]
            o = _dot(scores.astype(BF16), vb) + _dot((qr * qdec_ref[h]).astype(BF16), s.astype(BF16))
            kd = (kr * kdec_ref[h]).astype(BF16)
            s_ref[h] = (GAMMAS[h] ** RET_CHUNK) * s + _dot_tn(kd, vb)
            gate = g_ref[rows, hs]
            mix_ref[rows, hs] = (_silu(gate) * _group_norm(o)).astype(BF16)

    xp_ref[hist:hist + tile, :] = p_ref[...]
    for gi, w in enumerate(POOL_WINDOWS):
        gs = slice(gi * POOL_GROUP, (gi + 1) * POOL_GROUP)
        cur = xp_ref[hist:hist + tile, gs]
        wsum = cur
        for t in range(1, w):
            wsum = wsum + xp_ref[hist - t:hist - t + tile, gs]
        d = wsum * (1.0 / w) - cur
        pooled = _dot(d.astype(BF16), poolw_ref[gi]) * pscale_ref[:, gs]
        mix_ref[:, RET_WIDTH + gi * POOL_GROUP:RET_WIDTH + (gi + 1) * POOL_GROUP] = pooled.astype(BF16)

    @pl.when(c == pl.num_programs(1) - 1)
    def _emit_state():
        sret_ref[0] = s_ref[...]
        spool_ref[0] = xp_ref[hist + tile - POOL_BUF:hist + tile, :]

    xp_ref[0:hist, :] = xp_ref[tile:tile + hist, :]

    y = _dot(mix_ref[...], wout_ref[...])
    o_ref[...] = _layer_norm(ALPHA * x1_ref[...] + y, lng_ref[...], lnb_ref[...])


def _prompt_mixer(proj, x1, proj_small, meta_row_block, pool_w, pool_scale, w_out, lng, lnb, batch, seq):
    tile = MIX_TOKEN_TILE
    steps = seq // tile
    cos, sin = _rope_tables(N_META + np.arange(seq))
    cosm, sinm = _rope_tables(np.arange(N_META))
    mask, qdec, kdec = _decay_tables(RET_CHUNK, RET_CHUNK)
    _, _, kdecm = _decay_tables(N_META, N_META)

    def col(j):
        return pl.BlockSpec((tile, RET_WIDTH), lambda b, c: (b * steps + c, j))

    def meta_col(j):
        return pl.BlockSpec((N_META, RET_WIDTH), lambda b, c: (meta_row_block, j))

    in_specs = [
        col(0), col(1), col(2), col(3), col(4),
        pl.BlockSpec((tile, D_MODEL), lambda b, c: (b * steps + c, 0)),
        pl.BlockSpec((tile, HEAD_DIM), lambda b, c: (c, 0)),
        pl.BlockSpec((tile, HEAD_DIM), lambda b, c: (c, 0)),
        meta_col(1), meta_col(2), meta_col(4),
        _const_spec(cosm.shape), _const_spec(sinm.shape), _const_spec(kdecm.shape),
        _const_spec(mask.shape), _const_spec(qdec.shape), _const_spec(kdec.shape),
        _const_spec(pool_w.shape), _const_spec(pool_scale.shape), _const_spec(w_out.shape),
        _const_spec(lng.shape), _const_spec(lnb.shape),
    ]
    out_shape = [
        jax.ShapeDtypeStruct((batch * seq, D_MODEL), F32),
        jax.ShapeDtypeStruct((batch, RET_HEADS, HEAD_DIM, HEAD_DIM), F32),
        jax.ShapeDtypeStruct((batch, POOL_BUF, POOL_WIDTH), F32),
    ]
    out_specs = [
        pl.BlockSpec((tile, D_MODEL), lambda b, c: (b * steps + c, 0)),
        pl.BlockSpec((1, RET_HEADS, HEAD_DIM, HEAD_DIM), lambda b, c: (b, 0, 0, 0)),
        pl.BlockSpec((1, POOL_BUF, POOL_WIDTH), lambda b, c: (b, 0, 0)),
    ]
    return pl.pallas_call(
        _prompt_mixer_kernel,
        grid=(batch, steps),
        in_specs=in_specs,
        out_specs=out_specs,
        out_shape=out_shape,
        scratch_shapes=[
            pltpu.VMEM((RET_HEADS, HEAD_DIM, HEAD_DIM), F32),
            pltpu.VMEM((N_META + tile, POOL_WIDTH), F32),
            pltpu.VMEM((tile, D_MODEL), BF16),
        ],
        compiler_params=pltpu.CompilerParams(
            dimension_semantics=("arbitrary", "arbitrary"), vmem_limit_bytes=VMEM_LIMIT_BYTES),
        name="prompt_mixer",
    )(proj, proj, proj, proj, proj, x1, cos, sin, proj_small, proj_small, proj_small,
      cosm, sinm, kdecm, mask, qdec, kdec, pool_w, pool_scale, w_out, lng, lnb)


def _decode_mixer_kernel(q_ref, k_ref, v_ref, g_ref, p_ref, x1_ref, s0_ref, pref_ref,
                         cos_ref, sin_ref, mask_ref, qdec_ref, kdec_ref,
                         poolw_ref, pscale_ref, wout_ref, lng_ref, lnb_ref,
                         o_ref, sret_ref, spool_ref,
                         xs_ref, d_ref, mix_ref, *, dec_seq):
    rows = q_ref.shape[0]
    nseq = rows // dec_seq
    seq_per_group = BF16_ROWS // dec_seq
    cos, sin = cos_ref[...], sin_ref[...]
    row_seq = lax.broadcasted_iota(jnp.int32, (BF16_ROWS, HEAD_DIM), 0) // dec_seq

    for h in range(RET_HEADS):
        hs = slice(h * HEAD_DIM, (h + 1) * HEAD_DIM)
        qr = _rope(q_ref[:, hs], cos, sin)
        kr = _rope(k_ref[:, hs], cos, sin)
        v = v_ref[:, hs]
        vb = v.astype(BF16)
        scores = _dot_nt(qr.astype(BF16), kr.astype(BF16)) * mask_ref[h]
        o_inner = _dot(scores.astype(BF16), vb)
        qd = qr * qdec_ref[h]
        kd = kr * kdec_ref[h]
        o_cross = []
        for grp in range(rows // BF16_ROWS):
            gr = slice(grp * BF16_ROWS, (grp + 1) * BF16_ROWS)
            qd_g = qd[gr].astype(BF16)
            kd_g = kd[gr].astype(BF16)
            v_g = v[gr]
            acc = jnp.zeros((BF16_ROWS, HEAD_DIM), F32)
            for j in range(seq_per_group):
                b = grp * seq_per_group + j
                s = s0_ref[b, h]
                acc = jnp.where(row_seq == j, _dot(qd_g, s.astype(BF16)), acc)
                v_b = jnp.where(row_seq == j, v_g, 0.0).astype(BF16)
                sret_ref[b, h] = (GAMMAS[h] ** dec_seq) * s + _dot_tn(kd_g, v_b)
            o_cross.append(acc)
        o = o_inner + jnp.concatenate(o_cross, axis=0)
        mix_ref[:, hs] = (_silu(g_ref[:, hs]) * _group_norm(o)).astype(BF16)

    first = BF16_ROWS
    for b in range(nseq):
        xs_ref[b, first - POOL_BUF:first, :] = pref_ref[b]
        xs_ref[b, first:first + dec_seq, :] = p_ref[b * dec_seq:(b + 1) * dec_seq, :]
    for b in range(nseq):
        for gi, w in enumerate(POOL_WINDOWS):
            gs = slice(gi * POOL_GROUP, (gi + 1) * POOL_GROUP)
            cur = xs_ref[b, first:first + dec_seq, gs]
            wsum = cur
            for t in range(1, w):
                wsum = wsum + xs_ref[b, first - t:first - t + dec_seq, gs]
            d_ref[b * dec_seq:(b + 1) * dec_seq, gs] = wsum * (1.0 / w) - cur
        spool_ref[b] = xs_ref[b, first + dec_seq - POOL_BUF:first + dec_seq, :]
    for gi in range(len(POOL_WINDOWS)):
        gs = slice(gi * POOL_GROUP, (gi + 1) * POOL_GROUP)
        pooled = _dot(d_ref[:, gs].astype(BF16), poolw_ref[gi]) * pscale_ref[:, gs]
        mix_ref[:, RET_WIDTH + gi * POOL_GROUP:RET_WIDTH + (gi + 1) * POOL_GROUP] = pooled.astype(BF16)

    y = _dot(mix_ref[...], wout_ref[...])
    o_ref[...] = _layer_norm(ALPHA * x1_ref[...] + y, lng_ref[...], lnb_ref[...])


def _decode_mixer(proj, x1, state_ret, state_pool, pool_w, pool_scale, w_out, lng, lnb, nseq, dec_seq):
    assert BF16_ROWS % dec_seq == 0 and dec_seq <= POOL_BUF
    rows = DEC_SEQ_BLOCK * dec_seq
    steps = nseq // DEC_SEQ_BLOCK
    cos, sin = _rope_tables(PAST_LEN + (np.arange(rows) % dec_seq))
    mask, qdec, kdec = _decay_tables(rows, dec_seq)

    def col(j):
        return pl.BlockSpec((rows, RET_WIDTH), lambda i: (i, j))

    state_spec = pl.BlockSpec((DEC_SEQ_BLOCK, RET_HEADS, HEAD_DIM, HEAD_DIM), lambda i: (i, 0, 0, 0))
    pool_spec = pl.BlockSpec((DEC_SEQ_BLOCK, POOL_BUF, POOL_WIDTH), lambda i: (i, 0, 0))
    in_specs = [
        col(0), col(1), col(2), col(3), col(4),
        pl.BlockSpec((rows, D_MODEL), lambda i: (i, 0)),
        state_spec, pool_spec,
        _const_spec(cos.shape), _const_spec(sin.shape),
        _const_spec(mask.shape), _const_spec(qdec.shape), _const_spec(kdec.shape),
        _const_spec(pool_w.shape), _const_spec(pool_scale.shape), _const_spec(w_out.shape),
        _const_spec(lng.shape), _const_spec(lnb.shape),
    ]
    out_shape = [
        jax.ShapeDtypeStruct((nseq * dec_seq, D_MODEL), F32),
        jax.ShapeDtypeStruct(state_ret.shape, F32),
        jax.ShapeDtypeStruct(state_pool.shape, F32),
    ]
    out_specs = [pl.BlockSpec((rows, D_MODEL), lambda i: (i, 0)), state_spec, pool_spec]
    return pl.pallas_call(
        functools.partial(_decode_mixer_kernel, dec_seq=dec_seq),
        grid=(steps,),
        in_specs=in_specs,
        out_specs=out_specs,
        out_shape=out_shape,
        scratch_shapes=[
            pltpu.VMEM((DEC_SEQ_BLOCK, BF16_ROWS + 8, POOL_WIDTH), F32),
            pltpu.VMEM((rows, POOL_WIDTH), F32),
            pltpu.VMEM((rows, D_MODEL), BF16),
        ],
        compiler_params=pltpu.CompilerParams(
            dimension_semantics=("arbitrary",), vmem_limit_bytes=VMEM_LIMIT_BYTES),
        name="decode_mixer",
    )(proj, proj, proj, proj, proj, x1, state_ret, state_pool, cos, sin, mask, qdec, kdec,
      pool_w, pool_scale, w_out, lng, lnb)


def kernel(x_prompt, x_sample, state_ret, state_pool, meta_tokens, ffn1_w_gate, ffn1_w_up, ffn1_w_down,
           ln1_g, ln1_b, w_in, pool_w, pool_scale, w_out, ln2_g, ln2_b, ffn2_w_gate, ffn2_w_up,
           ffn2_w_down, ln3_g, ln3_b):
    assert ffn1_w_gate.shape[0] == DEPTH == 1
    batch, seq, _ = x_prompt.shape
    nseq, dec_seq, _ = x_sample.shape
    n_dec = nseq * dec_seq
    assert n_dec % N_META == 0

    bf = lambda w: w[0].astype(BF16)
    row = lambda v: v[0].reshape(1, -1)
    f1 = (bf(ffn1_w_gate), bf(ffn1_w_up), bf(ffn1_w_down), row(ln1_g), row(ln1_b))
    f2 = (bf(ffn2_w_gate), bf(ffn2_w_up), bf(ffn2_w_down), row(ln3_g), row(ln3_b))
    w_in_b, w_out_b, pool_w_b = bf(w_in), bf(w_out), bf(pool_w)
    pscale, g2, b2 = row(pool_scale), row(ln2_g), row(ln2_b)

    xp = x_prompt.reshape(batch * seq, D_MODEL)
    x_small = jnp.concatenate([x_sample.reshape(n_dec, D_MODEL), meta_tokens.astype(x_prompt.dtype)], axis=0)

    x1p, projp = _ffn(xp, *f1, w_in=w_in_b)
    x1s, projs = _ffn(x_small, *f1, w_in=w_in_b)

    x2p, ret_p, pool_p = _prompt_mixer(projp, x1p, projs, n_dec // N_META, pool_w_b, pscale, w_out_b,
                                       g2, b2, batch, seq)
    x2s, ret_s, pool_s = _decode_mixer(projs, x1s, state_ret[0], state_pool[0], pool_w_b, pscale,
                                       w_out_b, g2, b2, nseq, dec_seq)

    (y_prompt,) = _ffn(x2p, *f2)
    (y_sample,) = _ffn(x2s, *f2)
    return (y_prompt.reshape(batch, seq, D_MODEL), y_sample.reshape(nseq, dec_seq, D_MODEL),
            ret_p[None], pool_p[None], ret_s[None], pool_s[None])
```

```python
import functools
import math

import jax
import jax.numpy as jnp
import numpy as np
from jax import lax
from jax.experimental import pallas as pl
from jax.experimental.pallas import tpu as pltpu

F32 = jnp.float32
BF16 = jnp.bfloat16

D_MODEL = 1024
D_FF = 2816
N_META = 16
PAST_LEN = 16384
RET_HEADS = 4
HEAD_DIM = 128
RET_WIDTH = RET_HEADS * HEAD_DIM
RET_CHUNK = 128
ROPE_THETA = 10000.0
POOL_WINDOWS = (2, 4, 8, 16)
POOL_GROUP = 128
POOL_WIDTH = POOL_GROUP * len(POOL_WINDOWS)
POOL_BUF = max(POOL_WINDOWS) - 1
IN_WIDTH = 4 * RET_WIDTH + POOL_WIDTH
DEPTH = 1
ALPHA = (2.0 * DEPTH) ** 0.25
LN_EPS = 1e-5
GN_EPS = 1e-5
QK_SCALE = HEAD_DIM ** -0.5
GAMMAS = tuple(1.0 - 2.0 ** (-5.0 - h) for h in range(RET_HEADS))

VMEM_LIMIT_BYTES = 56 * 1024 * 1024
FFN_TOKEN_TILE = 512
FFN_COL_CHUNK = 256
MIX_TOKEN_TILE = 512
DEC_SEQ_BLOCK = 16
BF16_ROWS = 16


def _layer_norm(z, g, b):
    mu = jnp.mean(z, axis=-1, keepdims=True)
    zc = z - mu
    var = jnp.mean(zc * zc, axis=-1, keepdims=True)
    return zc * lax.rsqrt(var + LN_EPS) * g + b


def _silu(x):
    return x * jax.nn.sigmoid(x)


def _dot(a, b):
    return jnp.dot(a, b, preferred_element_type=F32)


def _dot_nt(a, b):
    return lax.dot_general(a, b, (((1,), (1,)), ((), ())), preferred_element_type=F32)


def _dot_tn(a, b):
    return lax.dot_general(a, b, (((0,), (0,)), ((), ())), preferred_element_type=F32)


def _ffn_body(x, wg_ref, wu_ref, wd_ref, lng_ref, lnb_ref, h_ref):
    xb = x.astype(BF16)
    for c in range(D_FF // FFN_COL_CHUNK):
        sl = slice(c * FFN_COL_CHUNK, (c + 1) * FFN_COL_CHUNK)
        g = _dot(xb, wg_ref[:, sl])
        u = _dot(xb, wu_ref[:, sl])
        h_ref[:, sl] = (_silu(g) * u).astype(BF16)
    y = _dot(h_ref[...], wd_ref[...])
    return _layer_norm(ALPHA * x + 0.5 * y, lng_ref[...], lnb_ref[...])


def _ffn_kernel(x_ref, wg_ref, wu_ref, wd_ref, lng_ref, lnb_ref, *rest, with_proj):
    if with_proj:
        win_ref, o_ref, proj_ref, h_ref = rest
    else:
        o_ref, h_ref = rest
    out = _ffn_body(x_ref[...], wg_ref, wu_ref, wd_ref, lng_ref, lnb_ref, h_ref)
    o_ref[...] = out
    if with_proj:
        proj_ref[...] = _dot(out.astype(BF16), win_ref[...])


def _const_spec(shape):
    zeros = (0,) * len(shape)
    return pl.BlockSpec(shape, lambda *_: zeros, pipeline_mode=pl.Buffered(1))


def _ffn(x, wg, wu, wd, lng, lnb, w_in=None):
    n = x.shape[0]
    tm = FFN_TOKEN_TILE if n % FFN_TOKEN_TILE == 0 else n
    with_proj = w_in is not None
    row = lambda i: (i, 0)
    in_specs = [
        pl.BlockSpec((tm, D_MODEL), row),
        _const_spec((D_MODEL, D_FF)),
        _const_spec((D_MODEL, D_FF)),
        _const_spec((D_FF, D_MODEL)),
        _const_spec((1, D_MODEL)),
        _const_spec((1, D_MODEL)),
    ]
    args = [x, wg, wu, wd, lng, lnb]
    out_shape = [jax.ShapeDtypeStruct((n, D_MODEL), F32)]
    out_specs = [pl.BlockSpec((tm, D_MODEL), row)]
    if with_proj:
        in_specs.append(_const_spec((D_MODEL, IN_WIDTH)))
        args.append(w_in)
        out_shape.append(jax.ShapeDtypeStruct((n, IN_WIDTH), F32))
        out_specs.append(pl.BlockSpec((tm, IN_WIDTH), row))
    return pl.pallas_call(
        functools.partial(_ffn_kernel, with_proj=with_proj),
        grid=(n // tm,),
        in_specs=in_specs,
        out_specs=out_specs,
        out_shape=out_shape,
        scratch_shapes=[pltpu.VMEM((tm, D_FF), BF16)],
        compiler_params=pltpu.CompilerParams(
            dimension_semantics=("arbitrary",), vmem_limit_bytes=VMEM_LIMIT_BYTES),
        name="ffn_proj" if with_proj else "ffn",
    )(*args)


def _rope_tables(positions):
    half = HEAD_DIM // 2
    inv_freq = ROPE_THETA ** (-np.arange(0, HEAD_DIM, 2, dtype=np.float64) / HEAD_DIM)
    ang = np.asarray(positions, np.float64)[:, None] * inv_freq[None, :]
    cos, sin = np.cos(ang), np.sin(ang)
    assert cos.shape[1] == half
    return (np.concatenate([cos, cos], axis=1).astype(np.float32),
            np.concatenate([-sin, sin], axis=1).astype(np.float32))


def _decay_tables(chunk, seq_len):
    r = np.arange(chunk)
    seq, idx = r // seq_len, (r % seq_len).astype(np.float64)
    same = seq[:, None] == seq[None, :]
    diff = idx[:, None] - idx[None, :]
    mask, qdec, kdec = [], [], []
    for gamma in GAMMAS:
        lg = math.log(gamma)
        mask.append(np.where(same & (diff >= 0), np.exp(lg * np.maximum(diff, 0.0)), 0.0) * QK_SCALE)
        qdec.append(np.broadcast_to((np.exp(lg * (idx + 1.0)) * QK_SCALE)[:, None], (chunk, HEAD_DIM)))
        kdec.append(np.broadcast_to(np.exp(lg * (seq_len - 1.0 - idx))[:, None], (chunk, HEAD_DIM)))
    to32 = lambda t: np.stack(t).astype(np.float32)
    return to32(mask), to32(qdec), to32(kdec)


def _rope(x, cos, sin):
    return x * cos + pltpu.roll(x, HEAD_DIM // 2, 1) * sin


def _group_norm(o):
    mu = jnp.mean(o, axis=-1, keepdims=True)
    oc = o - mu
    var = jnp.mean(oc * oc, axis=-1, keepdims=True)
    return oc * lax.rsqrt(var + GN_EPS)


def _prompt_mixer_ffn_kernel(q_ref, k_ref, v_ref, g_ref, p_ref, x1_ref, cos_ref, sin_ref,
                             km_ref, vm_ref, pm_ref, cosm_ref, sinm_ref, kdecm_ref,
                             mask_ref, qdec_ref, kdec_ref, poolw_ref, pscale_ref, wout_ref,
                             ln2g_ref, ln2b_ref, wg_ref, wu_ref, wd_ref, ln3g_ref, ln3b_ref,
                             y_ref, sret_ref, spool_ref,
                             s_ref, xp_ref, mix_ref, x2_ref, ypre_ref, h_ref, *, steps_per_seq):
    t = pl.program_id(0)
    n_tiles = pl.num_programs(0) - 2
    c = t % steps_per_seq
    tile = q_ref.shape[0]
    hist = N_META
    slot = t % 2
    row_blocks = [slice(r, r + RET_CHUNK) for r in range(0, tile, RET_CHUNK)]

    @pl.when(jnp.logical_and(c == 0, t < n_tiles))
    def _init_from_meta():
        for h in range(RET_HEADS):
            hs = slice(h * HEAD_DIM, (h + 1) * HEAD_DIM)
            kr = _rope(km_ref[:, hs], cosm_ref[...], sinm_ref[...])
            kd = (kr * kdecm_ref[h]).astype(BF16)
            s_ref[h] = _dot_tn(kd, vm_ref[:, hs].astype(BF16))
        xp_ref[0:hist, :] = pm_ref[...]

    def interleave(major, minor):
        done = 0
        for i, piece in enumerate(major):
            piece()
            upto = (len(minor) * (i + 1)) // len(major)
            for m in minor[done:upto]:
                m()
            done = upto

    def ln3_piece(rows):
        def run():
            y_ref[rows, :] = _layer_norm(ypre_ref[rows, :], ln3g_ref[...], ln3b_ref[...])
        return run

    state = [None] * RET_HEADS

    def retention_piece(ci, h):
        def run():
            rows = slice(ci * RET_CHUNK, (ci + 1) * RET_CHUNK)
            hs = slice(h * HEAD_DIM, (h + 1) * HEAD_DIM)
            cos, sin = cos_ref[rows, :], sin_ref[rows, :]
            qr = _rope(q_ref[rows, hs], cos, sin)
            kr = _rope(k_ref[rows, hs], cos, sin)
            vb = v_ref[rows, hs].astype(BF16)
            scores = _dot_nt(qr.astype(BF16), kr.astype(BF16)) * mask_ref[h]
            s = state[h]
            o = _dot(scores.astype(BF16), vb) + _dot((qr * qdec_ref[h]).astype(BF16), s.astype(BF16))
            kd = (kr * kdec_ref[h]).astype(BF16)
            state[h] = (GAMMAS[h] ** RET_CHUNK) * s + _dot_tn(kd, vb)
            gate = g_ref[rows, hs]
            mix_ref[rows, hs] = (_silu(gate) * _group_norm(o)).astype(BF16)
        return run

    def pool_piece(gi, w):
        def run():
            gs = slice(gi * POOL_GROUP, (gi + 1) * POOL_GROUP)
            xp_ref[hist:hist + tile, gs] = p_ref[:, gs]
            cur = xp_ref[hist:hist + tile, gs]
            wsum = cur
            for back in range(1, w):
                wsum = wsum + xp_ref[hist - back:hist - back + tile, gs]
            d = wsum * (1.0 / w) - cur
            pooled = _dot(d.astype(BF16), poolw_ref[gi]) * pscale_ref[:, gs]
            mix_ref[:, RET_WIDTH + gi * POOL_GROUP:RET_WIDTH + (gi + 1) * POOL_GROUP] = pooled.astype(BF16)
            xp_ref[0:hist, gs] = xp_ref[tile:tile + hist, gs]
        return run

    def _run_step(with_a, with_b, with_c):
        vector_pieces, tail_pieces = [], []
        if with_c:
            vector_pieces += [ln3_piece(rows) for rows in row_blocks]
        if with_a:
            for h in range(RET_HEADS):
                state[h] = s_ref[h]
            vector_pieces += [retention_piece(ci, h) for ci in range(len(row_blocks))
                              for h in range(RET_HEADS)]
            vector_pieces += [pool_piece(gi, w) for gi, w in enumerate(POOL_WINDOWS)]

        if with_b:
            xb = x2_ref[1 - slot].astype(BF16)

            def gate_up_piece(ck):
                def run():
                    sl = slice(ck * FFN_COL_CHUNK, (ck + 1) * FFN_COL_CHUNK)
                    g = _dot(xb, wg_ref[:, sl])
                    u = _dot(xb, wu_ref[:, sl])
                    h_ref[:, sl] = (_silu(g) * u).astype(BF16)
                return run

            interleave([gate_up_piece(ck) for ck in range(D_FF // FFN_COL_CHUNK)], vector_pieces)
        else:
            for piece in vector_pieces:
                piece()

        if with_a:
            for h in range(RET_HEADS):
                s_ref[h] = state[h]
            mixed = _dot(mix_ref[...], wout_ref[...])

            def ln2_piece(rows):
                def run():
                    x2_ref[slot, rows, :] = _layer_norm(ALPHA * x1_ref[rows, :] + mixed[rows, :],
                                                        ln2g_ref[...], ln2b_ref[...])
                return run

            tail_pieces = [ln2_piece(rows) for rows in row_blocks]

        if with_b:
            def down_piece(nk):
                def run():
                    cs = slice(nk * FFN_COL_CHUNK, (nk + 1) * FFN_COL_CHUNK)
                    ypre_ref[:, cs] = ALPHA * x2_ref[1 - slot, :, cs] + 0.5 * _dot(h_ref[...], wd_ref[:, cs])
                return run

            interleave([down_piece(nk) for nk in range(D_MODEL // FFN_COL_CHUNK)], tail_pieces)
        else:
            for piece in tail_pieces:
                piece()

    @pl.when(t == 0)
    def _clear_pipeline():
        x2_ref[1] = jnp.zeros((tile, D_MODEL), F32)
        ypre_ref[...] = jnp.zeros(ypre_ref.shape, F32)

    @pl.when(t <= n_tiles)
    def _steady():
        _run_step(True, True, True)

    @pl.when(t == n_tiles + 1)
    def _drain_last():
        _run_step(False, False, True)

    @pl.when(jnp.logical_and(c == steps_per_seq - 1, t < n_tiles))
    def _emit_state():
        sret_ref[0] = s_ref[...]
        spool_ref[0] = xp_ref[hist - POOL_BUF:hist, :]


def _prompt_mixer_ffn(proj, x1, proj_small, meta_row_block, pool_w, pool_scale, w_out, ln2g, ln2b,
                      wg, wu, wd, ln3g, ln3b, batch, seq):
    tile = MIX_TOKEN_TILE
    steps = seq // tile
    n_tiles = batch * steps
    cos, sin = _rope_tables(N_META + np.arange(seq))
    cosm, sinm = _rope_tables(np.arange(N_META))
    mask, qdec, kdec = _decay_tables(RET_CHUNK, RET_CHUNK)
    _, _, kdecm = _decay_tables(N_META, N_META)

    mix_tile = lambda t: jnp.minimum(t, n_tiles - 1)
    ffn_tile = lambda t: jnp.maximum(t - 2, 0)

    def col(j):
        return pl.BlockSpec((tile, RET_WIDTH), lambda t: (mix_tile(t), j))

    def meta_col(j):
        return pl.BlockSpec((N_META, RET_WIDTH), lambda t: (meta_row_block, j))

    in_specs = [
        col(0), col(1), col(2), col(3), col(4),
        pl.BlockSpec((tile, D_MODEL), lambda t: (mix_tile(t), 0)),
        pl.BlockSpec((tile, HEAD_DIM), lambda t: (mix_tile(t) % steps, 0)),
        pl.BlockSpec((tile, HEAD_DIM), lambda t: (mix_tile(t) % steps, 0)),
        meta_col(1), meta_col(2), meta_col(4),
        _const_spec(cosm.shape), _const_spec(sinm.shape), _const_spec(kdecm.shape),
        _const_spec(mask.shape), _const_spec(qdec.shape), _const_spec(kdec.shape),
        _const_spec(pool_w.shape), _const_spec(pool_scale.shape), _const_spec(w_out.shape),
        _const_spec(ln2g.shape), _const_spec(ln2b.shape),
        _const_spec(wg.shape), _const_spec(wu.shape), _const_spec(wd.shape),
        _const_spec(ln3g.shape), _const_spec(ln3b.shape),
    ]
    out_shape = [
        jax.ShapeDtypeStruct((batch * seq, D_MODEL), F32),
        jax.ShapeDtypeStruct((batch, RET_HEADS, HEAD_DIM, HEAD_DIM), F32),
        jax.ShapeDtypeStruct((batch, POOL_BUF, POOL_WIDTH), F32),
    ]
    out_specs = [
        pl.BlockSpec((tile, D_MODEL), lambda t: (ffn_tile(t), 0)),
        pl.BlockSpec((1, RET_HEADS, HEAD_DIM, HEAD_DIM), lambda t: (mix_tile(t) // steps, 0, 0, 0)),
        pl.BlockSpec((1, POOL_BUF, POOL_WIDTH), lambda t: (mix_tile(t) // steps, 0, 0)),
    ]
    return pl.pallas_call(
        functools.partial(_prompt_mixer_ffn_kernel, steps_per_seq=steps),
        grid=(n_tiles + 2,),
        in_specs=in_specs,
        out_specs=out_specs,
        out_shape=out_shape,
        scratch_shapes=[
            pltpu.VMEM((RET_HEADS, HEAD_DIM, HEAD_DIM), F32),
            pltpu.VMEM((N_META + tile, POOL_WIDTH), F32),
            pltpu.VMEM((tile, D_MODEL), BF16),
            pltpu.VMEM((2, tile, D_MODEL), F32),
            pltpu.VMEM((tile, D_MODEL), F32),
            pltpu.VMEM((tile, D_FF), BF16),
        ],
        compiler_params=pltpu.CompilerParams(
            dimension_semantics=("arbitrary",), vmem_limit_bytes=VMEM_LIMIT_BYTES),
        name="prompt_mixer_ffn",
    )(proj, proj, proj, proj, proj, x1, cos, sin, proj_small, proj_small, proj_small,
      cosm, sinm, kdecm, mask, qdec, kdec, pool_w, pool_scale, w_out, ln2g, ln2b,
      wg, wu, wd, ln3g, ln3b)


def _decode_mixer_kernel(q_ref, k_ref, v_ref, g_ref, p_ref, x1_ref, s0_ref, pref_ref,
                         cos_ref, sin_ref, mask_ref, qdec_ref, kdec_ref,
                         poolw_ref, pscale_ref, wout_ref, lng_ref, lnb_ref,
                         o_ref, sret_ref, spool_ref,
                         xs_ref, d_ref, mix_ref, *, dec_seq):
    rows = q_ref.shape[0]
    nseq = rows // dec_seq
    seq_per_group = BF16_ROWS // dec_seq
    cos, sin = cos_ref[...], sin_ref[...]
    row_seq = lax.broadcasted_iota(jnp.int32, (BF16_ROWS, HEAD_DIM), 0) // dec_seq

    for h in range(RET_HEADS):
        hs = slice(h * HEAD_DIM, (h + 1) * HEAD_DIM)
        qr = _rope(q_ref[:, hs], cos, sin)
        kr = _rope(k_ref[:, hs], cos, sin)
        v = v_ref[:, hs]
        vb = v.astype(BF16)
        scores = _dot_nt(qr.astype(BF16), kr.astype(BF16)) * mask_ref[h]
        o_inner = _dot(scores.astype(BF16), vb)
        qd = qr * qdec_ref[h]
        kd = kr * kdec_ref[h]
        o_cross = []
        for grp in range(rows // BF16_ROWS):
            gr = slice(grp * BF16_ROWS, (grp + 1) * BF16_ROWS)
            qd_g = qd[gr].astype(BF16)
            kd_g = kd[gr].astype(BF16)
            v_g = v[gr]
            acc = jnp.zeros((BF16_ROWS, HEAD_DIM), F32)
            for j in range(seq_per_group):
                b = grp * seq_per_group + j
                s = s0_ref[b, h]
                acc = jnp.where(row_seq == j, _dot(qd_g, s.astype(BF16)), acc)
                v_b = jnp.where(row_seq == j, v_g, 0.0).astype(BF16)
                sret_ref[b, h] = (GAMMAS[h] ** dec_seq) * s + _dot_tn(kd_g, v_b)
            o_cross.append(acc)
        o = o_inner + jnp.concatenate(o_cross, axis=0)
        mix_ref[:, hs] = (_silu(g_ref[:, hs]) * _group_norm(o)).astype(BF16)

    first = BF16_ROWS
    for b in range(nseq):
        xs_ref[b, first - POOL_BUF:first, :] = pref_ref[b]
        xs_ref[b, first:first + dec_seq, :] = p_ref[b * dec_seq:(b + 1) * dec_seq, :]
    for b in range(nseq):
        for gi, w in enumerate(POOL_WINDOWS):
            gs = slice(gi * POOL_GROUP, (gi + 1) * POOL_GROUP)
            cur = xs_ref[b, first:first + dec_seq, gs]
            wsum = cur
            for t in range(1, w):
                wsum = wsum + xs_ref[b, first - t:first - t + dec_seq, gs]
            d_ref[b * dec_seq:(b + 1) * dec_seq, gs] = wsum * (1.0 / w) - cur
        spool_ref[b] = xs_ref[b, first + dec_seq - POOL_BUF:first + dec_seq, :]
    for gi in range(len(POOL_WINDOWS)):
        gs = slice(gi * POOL_GROUP, (gi + 1) * POOL_GROUP)
        pooled = _dot(d_ref[:, gs].astype(BF16), poolw_ref[gi]) * pscale_ref[:, gs]
        mix_ref[:, RET_WIDTH + gi * POOL_GROUP:RET_WIDTH + (gi + 1) * POOL_GROUP] = pooled.astype(BF16)

    y = _dot(mix_ref[...], wout_ref[...])
    o_ref[...] = _layer_norm(ALPHA * x1_ref[...] + y, lng_ref[...], lnb_ref[...])


def _decode_mixer(proj, x1, state_ret, state_pool, pool_w, pool_scale, w_out, lng, lnb, nseq, dec_seq):
    assert BF16_ROWS % dec_seq == 0 and dec_seq <= POOL_BUF
    rows = DEC_SEQ_BLOCK * dec_seq
    steps = nseq // DEC_SEQ_BLOCK
    cos, sin = _rope_tables(PAST_LEN + (np.arange(rows) % dec_seq))
    mask, qdec, kdec = _decay_tables(rows, dec_seq)

    def col(j):
        return pl.BlockSpec((rows, RET_WIDTH), lambda i: (i, j))

    state_spec = pl.BlockSpec((DEC_SEQ_BLOCK, RET_HEADS, HEAD_DIM, HEAD_DIM), lambda i: (i, 0, 0, 0))
    pool_spec = pl.BlockSpec((DEC_SEQ_BLOCK, POOL_BUF, POOL_WIDTH), lambda i: (i, 0, 0))
    in_specs = [
        col(0), col(1), col(2), col(3), col(4),
        pl.BlockSpec((rows, D_MODEL), lambda i: (i, 0)),
        state_spec, pool_spec,
        _const_spec(cos.shape), _const_spec(sin.shape),
        _const_spec(mask.shape), _const_spec(qdec.shape), _const_spec(kdec.shape),
        _const_spec(pool_w.shape), _const_spec(pool_scale.shape), _const_spec(w_out.shape),
        _const_spec(lng.shape), _const_spec(lnb.shape),
    ]
    out_shape = [
        jax.ShapeDtypeStruct((nseq * dec_seq, D_MODEL), F32),
        jax.ShapeDtypeStruct(state_ret.shape, F32),
        jax.ShapeDtypeStruct(state_pool.shape, F32),
    ]
    out_specs = [pl.BlockSpec((rows, D_MODEL), lambda i: (i, 0)), state_spec, pool_spec]
    return pl.pallas_call(
        functools.partial(_decode_mixer_kernel, dec_seq=dec_seq),
        grid=(steps,),
        in_specs=in_specs,
        out_specs=out_specs,
        out_shape=out_shape,
        scratch_shapes=[
            pltpu.VMEM((DEC_SEQ_BLOCK, BF16_ROWS + 8, POOL_WIDTH), F32),
            pltpu.VMEM((rows, POOL_WIDTH), F32),
            pltpu.VMEM((rows, D_MODEL), BF16),
        ],
        compiler_params=pltpu.CompilerParams(
            dimension_semantics=("arbitrary",), vmem_limit_bytes=VMEM_LIMIT_BYTES),
        name="decode_mixer",
    )(proj, proj, proj, proj, proj, x1, state_ret, state_pool, cos, sin, mask, qdec, kdec,
      pool_w, pool_scale, w_out, lng, lnb)


def kernel(x_prompt, x_sample, state_ret, state_pool, meta_tokens, ffn1_w_gate, ffn1_w_up, ffn1_w_down,
           ln1_g, ln1_b, w_in, pool_w, pool_scale, w_out, ln2_g, ln2_b, ffn2_w_gate, ffn2_w_up,
           ffn2_w_down, ln3_g, ln3_b):
    assert ffn1_w_gate.shape[0] == DEPTH == 1
    batch, seq, _ = x_prompt.shape
    nseq, dec_seq, _ = x_sample.shape
    n_dec = nseq * dec_seq
    assert n_dec % N_META == 0

    bf = lambda w: w[0].astype(BF16)
    row = lambda v: v[0].reshape(1, -1)
    f1 = (bf(ffn1_w_gate), bf(ffn1_w_up), bf(ffn1_w_down), row(ln1_g), row(ln1_b))
    f2 = (bf(ffn2_w_gate), bf(ffn2_w_up), bf(ffn2_w_down), row(ln3_g), row(ln3_b))
    w_in_b, w_out_b, pool_w_b = bf(w_in), bf(w_out), bf(pool_w)
    pscale, g2, b2 = row(pool_scale), row(ln2_g), row(ln2_b)

    xp = x_prompt.reshape(batch * seq, D_MODEL)
    x_small = jnp.concatenate([x_sample.reshape(n_dec, D_MODEL), meta_tokens.astype(x_prompt.dtype)], axis=0)

    x1p, projp = _ffn(xp, *f1, w_in=w_in_b)
    x1s, projs = _ffn(x_small, *f1, w_in=w_in_b)

    y_prompt, ret_p, pool_p = _prompt_mixer_ffn(projp, x1p, projs, n_dec // N_META, pool_w_b, pscale,
                                                w_out_b, g2, b2, *f2, batch, seq)
    x2s, ret_s, pool_s = _decode_mixer(projs, x1s, state_ret[0], state_pool[0], pool_w_b, pscale,
                                       w_out_b, g2, b2, nseq, dec_seq)
    (y_sample,) = _ffn(x2s, *f2)
    return (y_prompt.reshape(batch, seq, D_MODEL), y_sample.reshape(nseq, dec_seq, D_MODEL),
            ret_p[None], pool_p[None], ret_s[None], pool_s[None])
```

```python
import functools
import math

import jax
import jax.numpy as jnp
import numpy as np
from jax import lax
from jax.experimental import pallas as pl
from jax.experimental.pallas import tpu as pltpu

F32 = jnp.float32
BF16 = jnp.bfloat16

D_MODEL = 1024
D_FF = 2816
N_META = 16
PAST_LEN = 16384
RET_HEADS = 4
HEAD_DIM = 128
RET_WIDTH = RET_HEADS * HEAD_DIM
RET_CHUNK = 128
ROPE_THETA = 10000.0
POOL_WINDOWS = (2, 4, 8, 16)
POOL_GROUP = 128
POOL_WIDTH = POOL_GROUP * len(POOL_WINDOWS)
POOL_BUF = max(POOL_WINDOWS) - 1
IN_WIDTH = 4 * RET_WIDTH + POOL_WIDTH
DEPTH = 1
ALPHA = (2.0 * DEPTH) ** 0.25
LN_EPS = 1e-5
GN_EPS = 1e-5
QK_SCALE = HEAD_DIM ** -0.5
GAMMAS = tuple(1.0 - 2.0 ** (-5.0 - h) for h in range(RET_HEADS))

VMEM_LIMIT_BYTES = 56 * 1024 * 1024
FFN_TOKEN_TILE = 512
FFN_COL_CHUNK = 256
MIX_TOKEN_TILE = 512
DEC_SEQ_BLOCK = 16
BF16_ROWS = 16
PAIR_WIDTH = 2 * HEAD_DIM


def _layer_norm(z, g, b):
    mu = jnp.mean(z, axis=-1, keepdims=True)
    zc = z - mu
    var = jnp.mean(zc * zc, axis=-1, keepdims=True)
    return zc * lax.rsqrt(var + LN_EPS) * g + b


def _silu(x):
    return x * jax.nn.sigmoid(x)


def _dot(a, b):
    return jnp.dot(a, b, preferred_element_type=F32)


def _dot_nt(a, b):
    return lax.dot_general(a, b, (((1,), (1,)), ((), ())), preferred_element_type=F32)


def _dot_tn(a, b):
    return lax.dot_general(a, b, (((0,), (0,)), ((), ())), preferred_element_type=F32)


def _ffn_body(x, wg_ref, wu_ref, wd_ref, lng_ref, lnb_ref, h_ref):
    xb = x.astype(BF16)
    for c in range(D_FF // FFN_COL_CHUNK):
        sl = slice(c * FFN_COL_CHUNK, (c + 1) * FFN_COL_CHUNK)
        g = _dot(xb, wg_ref[:, sl])
        u = _dot(xb, wu_ref[:, sl])
        h_ref[:, sl] = (_silu(g) * u).astype(BF16)
    y = _dot(h_ref[...], wd_ref[...])
    return _layer_norm(ALPHA * x + 0.5 * y, lng_ref[...], lnb_ref[...])


def _ffn_kernel(x_ref, wg_ref, wu_ref, wd_ref, lng_ref, lnb_ref, *rest, with_proj):
    if with_proj:
        win_ref, o_ref, proj_ref, h_ref = rest
    else:
        o_ref, h_ref = rest
    out = _ffn_body(x_ref[...], wg_ref, wu_ref, wd_ref, lng_ref, lnb_ref, h_ref)
    o_ref[...] = out
    if with_proj:
        proj_ref[...] = _dot(out.astype(BF16), win_ref[...])


def _const_spec(shape):
    zeros = (0,) * len(shape)
    return pl.BlockSpec(shape, lambda *_: zeros, pipeline_mode=pl.Buffered(1))


def _ffn(x, wg, wu, wd, lng, lnb, w_in=None):
    n = x.shape[0]
    tm = FFN_TOKEN_TILE if n % FFN_TOKEN_TILE == 0 else n
    with_proj = w_in is not None
    row = lambda i: (i, 0)
    in_specs = [
        pl.BlockSpec((tm, D_MODEL), row),
        _const_spec((D_MODEL, D_FF)),
        _const_spec((D_MODEL, D_FF)),
        _const_spec((D_FF, D_MODEL)),
        _const_spec((1, D_MODEL)),
        _const_spec((1, D_MODEL)),
    ]
    args = [x, wg, wu, wd, lng, lnb]
    out_shape = [jax.ShapeDtypeStruct((n, D_MODEL), F32)]
    out_specs = [pl.BlockSpec((tm, D_MODEL), row)]
    if with_proj:
        in_specs.append(_const_spec((D_MODEL, IN_WIDTH)))
        args.append(w_in)
        out_shape.append(jax.ShapeDtypeStruct((n, IN_WIDTH), F32))
        out_specs.append(pl.BlockSpec((tm, IN_WIDTH), row))
    return pl.pallas_call(
        functools.partial(_ffn_kernel, with_proj=with_proj),
        grid=(n // tm,),
        in_specs=in_specs,
        out_specs=out_specs,
        out_shape=out_shape,
        scratch_shapes=[pltpu.VMEM((tm, D_FF), BF16)],
        compiler_params=pltpu.CompilerParams(
            dimension_semantics=("arbitrary",), vmem_limit_bytes=VMEM_LIMIT_BYTES),
        name="ffn_proj" if with_proj else "ffn",
    )(*args)


def _rope_tables(positions):
    half = HEAD_DIM // 2
    inv_freq = ROPE_THETA ** (-np.arange(0, HEAD_DIM, 2, dtype=np.float64) / HEAD_DIM)
    ang = np.asarray(positions, np.float64)[:, None] * inv_freq[None, :]
    cos, sin = np.cos(ang), np.sin(ang)
    assert cos.shape[1] == half
    return (np.concatenate([cos, cos], axis=1).astype(np.float32),
            np.concatenate([-sin, sin], axis=1).astype(np.float32))


def _decay_tables(chunk, seq_len):
    r = np.arange(chunk)
    seq, idx = r // seq_len, (r % seq_len).astype(np.float64)
    same = seq[:, None] == seq[None, :]
    diff = idx[:, None] - idx[None, :]
    mask, qdec, kdec = [], [], []
    for gamma in GAMMAS:
        lg = math.log(gamma)
        mask.append(np.where(same & (diff >= 0), np.exp(lg * np.maximum(diff, 0.0)), 0.0) * QK_SCALE)
        qdec.append(np.broadcast_to((np.exp(lg * (idx + 1.0)) * QK_SCALE)[:, None], (chunk, HEAD_DIM)))
        kdec.append(np.broadcast_to(np.exp(lg * (seq_len - 1.0 - idx))[:, None], (chunk, HEAD_DIM)))
    to32 = lambda t: np.stack(t).astype(np.float32)
    return to32(mask), to32(qdec), to32(kdec)


def _rope(x, cos, sin):
    return x * cos + pltpu.roll(x, HEAD_DIM // 2, 1) * sin


def _group_norm(o):
    mu = jnp.mean(o, axis=-1, keepdims=True)
    oc = o - mu
    var = jnp.mean(oc * oc, axis=-1, keepdims=True)
    return oc * lax.rsqrt(var + GN_EPS)


def _prompt_mixer_ffn_kernel(q_ref, k_ref, v_ref, g_ref, p_ref, x1_ref, cos_ref, sin_ref,
                             km_ref, vm_ref, pm_ref, cosm_ref, sinm_ref, kdecm_ref,
                             mask_ref, qdec_ref, kdec_ref, poolw_ref, pscale_ref, wout_ref,
                             ln2g_ref, ln2b_ref, wg_ref, wu_ref, wd_ref, ln3g_ref, ln3b_ref,
                             y_ref, sret_ref, spool_ref,
                             s_ref, xp_ref, mix_ref, x2_ref, xb_ref, ypre_ref, h_ref, kb_ref, vs_ref,
                             *, steps_per_seq):
    t = pl.program_id(0)
    n_tiles = pl.num_programs(0) - 2
    c = t % steps_per_seq
    tile = q_ref.shape[0]
    hist = N_META
    row_blocks = [slice(r, r + RET_CHUNK) for r in range(0, tile, RET_CHUNK)]

    @pl.when(jnp.logical_and(c == 0, t < n_tiles))
    def _init_from_meta():
        for h in range(RET_HEADS):
            hs = slice(h * HEAD_DIM, (h + 1) * HEAD_DIM)
            kr = _rope(km_ref[:, hs], cosm_ref[...], sinm_ref[...])
            kd = (kr * kdecm_ref[h]).astype(BF16)
            s_ref[h] = _dot_tn(kd, vm_ref[:, hs].astype(BF16))
        xp_ref[0:hist, :] = pm_ref[...]

    def interleave(major, starts):
        live = []
        for i, piece in enumerate(major):
            piece()
            live += [make() for make in starts.get(i, [])]
            live = [g for g in live if next(g, "done") != "done"]
        while live:
            live = [g for g in live if next(g, "done") != "done"]

    def ln3_fetch_piece(rows):
        def run():
            y_ref[rows, :] = ypre_ref[rows, :]
            yield
        return run

    def ln3_piece(rows):
        def run():
            y_ref[rows, :] = _layer_norm(y_ref[rows, :], ln3g_ref[...], ln3b_ref[...])
            yield
        return run

    state = [None] * RET_HEADS

    def retention_piece(ci, hp):
        def run():
            rows = slice(ci * RET_CHUNK, (ci + 1) * RET_CHUNK)
            pair = slice(hp * PAIR_WIDTH, (hp + 1) * PAIR_WIDTH)
            buf = ci * 2 + hp
            cos, sin = cos_ref[rows, :], sin_ref[rows, :]
            qr, kr = [], []
            for j in range(2):
                h = 2 * hp + j
                hs = slice(h * HEAD_DIM, (h + 1) * HEAD_DIM)
                blk = slice(j * HEAD_DIM, (j + 1) * HEAD_DIM)
                qr.append(_rope(q_ref[rows, hs], cos, sin))
                kr.append(_rope(k_ref[rows, hs], cos, sin))
                kb_ref[buf, blk, blk] = kr[j].astype(BF16)
                vs_ref[buf, blk, blk] = v_ref[rows, hs].astype(BF16)
                vs_ref[buf, PAIR_WIDTH + j * HEAD_DIM:PAIR_WIDTH + (j + 1) * HEAD_DIM, blk] = (
                    state[h].astype(BF16))
            q2 = jnp.concatenate(qr, axis=1)
            k2 = jnp.concatenate(kr, axis=1)
            scores = _dot_nt(q2.astype(BF16), kb_ref[buf])
            kd2 = (k2 * kdec_ref[hp]).astype(BF16)
            upd = _dot_tn(kd2, v_ref[rows, pair].astype(BF16))
            qd2 = (q2 * qdec_ref[hp]).astype(BF16)
            yield
            lhs = jnp.concatenate([(scores * mask_ref[hp]).astype(BF16), qd2], axis=1)
            o2 = _dot(lhs, vs_ref[buf])
            for j in range(2):
                blk = slice(j * HEAD_DIM, (j + 1) * HEAD_DIM)
                state[2 * hp + j] = (GAMMAS[2 * hp + j] ** RET_CHUNK) * state[2 * hp + j] + upd[blk, blk]
            yield
            for j in range(2):
                hs = slice((2 * hp + j) * HEAD_DIM, (2 * hp + j + 1) * HEAD_DIM)
                blk = slice(j * HEAD_DIM, (j + 1) * HEAD_DIM)
                mix_ref[rows, hs] = (_silu(g_ref[rows, hs]) * _group_norm(o2[:, blk])).astype(BF16)
        return run

    def pool_piece(pp):
        def run():
            d2 = []
            for j in range(2):
                gi = 2 * pp + j
                w = POOL_WINDOWS[gi]
                gs = slice(gi * POOL_GROUP, (gi + 1) * POOL_GROUP)
                xp_ref[hist:hist + tile, gs] = p_ref[:, gs]
                rows_all = xp_ref[:, gs]
                wsum, shift = rows_all, 1
                while shift < w:
                    wsum = wsum + pltpu.roll(wsum, shift, 0)
                    shift *= 2
                d2.append((wsum[hist:] * (1.0 / w) - rows_all[hist:]).astype(BF16))
                xp_ref[0:hist, gs] = xp_ref[tile:tile + hist, gs]
                if j == 0:
                    yield
            pair = slice(pp * PAIR_WIDTH, (pp + 1) * PAIR_WIDTH)
            pooled = _dot(jnp.concatenate(d2, axis=1), poolw_ref[pp])
            yield
            pooled = pooled * pscale_ref[:, pair]
            mix_ref[:, RET_WIDTH + pp * PAIR_WIDTH:RET_WIDTH + (pp + 1) * PAIR_WIDTH] = pooled.astype(BF16)
        return run

    def _run_step(with_a, with_b, with_c):
        starts, tail_pieces = {}, []

        def start_at(i, piece):
            starts.setdefault(i, []).append(piece)

        if with_c:
            for i, rows in enumerate(row_blocks):
                start_at(0, ln3_fetch_piece(rows))
        if with_a:
            for h in range(RET_HEADS):
                state[h] = s_ref[h]
            n_ret = 0
            for ci in range(len(row_blocks)):
                for hp in range(RET_HEADS // 2):
                    start_at(n_ret, retention_piece(ci, hp))
                    n_ret += 1
            for pp in range(len(POOL_WINDOWS) // 2):
                start_at(2 * pp, pool_piece(pp))

        if with_b:
            def gate_up_piece(ck):
                def run():
                    sl = slice(ck * FFN_COL_CHUNK, (ck + 1) * FFN_COL_CHUNK)
                    g = _dot(xb_ref[...], wg_ref[:, sl])
                    u = _dot(xb_ref[...], wu_ref[:, sl])
                    h_ref[:, sl] = (_silu(g) * u).astype(BF16)
                return run

            def residual_piece():
                ypre_ref[...] = ALPHA * x2_ref[...]
                yield

            start_at(1, residual_piece)
            if with_c:
                n_slots = D_FF // FFN_COL_CHUNK
                for i, rows in enumerate(row_blocks):
                    start_at(n_slots - len(row_blocks) + i, ln3_piece(rows))
            interleave([gate_up_piece(ck) for ck in range(D_FF // FFN_COL_CHUNK)], starts)
        else:
            for i in sorted(starts):
                for make in starts[i]:
                    for _ in make():
                        pass

        tail_starts = {}
        if with_a:
            for h in range(RET_HEADS):
                s_ref[h] = state[h]
            x2_ref[...] = ALPHA * x1_ref[...] + _dot(mix_ref[...], wout_ref[...])

            def ln2_piece(rows):
                def run():
                    x2_rows = _layer_norm(x2_ref[rows, :], ln2g_ref[...], ln2b_ref[...])
                    x2_ref[rows, :] = x2_rows
                    xb_ref[rows, :] = x2_rows.astype(BF16)
                    yield
                return run

            tail_starts = {i: [ln2_piece(rows)] for i, rows in enumerate(row_blocks)}
        if with_c and not with_b:
            for i, rows in enumerate(row_blocks):
                tail_starts.setdefault(i, []).append(ln3_piece(rows))

        if with_b:
            def down_piece(nk):
                def run():
                    cs = slice(nk * FFN_COL_CHUNK, (nk + 1) * FFN_COL_CHUNK)
                    ypre_ref[:, cs] = ypre_ref[:, cs] + 0.5 * _dot(h_ref[...], wd_ref[:, cs])
                return run

            interleave([down_piece(nk) for nk in range(D_MODEL // FFN_COL_CHUNK)], tail_starts)
        else:
            for i in sorted(tail_starts):
                for make in tail_starts[i]:
                    for _ in make():
                        pass

    @pl.when(t == 0)
    def _clear_pipeline():
        x2_ref[...] = jnp.zeros(x2_ref.shape, F32)
        xb_ref[...] = jnp.zeros(xb_ref.shape, BF16)
        kb_ref[...] = jnp.zeros(kb_ref.shape, BF16)
        vs_ref[...] = jnp.zeros(vs_ref.shape, BF16)
        ypre_ref[...] = jnp.zeros(ypre_ref.shape, F32)

    @pl.when(t <= n_tiles)
    def _steady():
        _run_step(True, True, True)

    @pl.when(t == n_tiles + 1)
    def _drain_last():
        _run_step(False, False, True)

    @pl.when(jnp.logical_and(c == steps_per_seq - 1, t < n_tiles))
    def _emit_state():
        sret_ref[0] = s_ref[...]
        spool_ref[0] = xp_ref[hist - POOL_BUF:hist, :]


def _prompt_mixer_ffn(proj, x1, proj_small, meta_row_block, pool_w, pool_scale, w_out, ln2g, ln2b,
                      wg, wu, wd, ln3g, ln3b, batch, seq):
    tile = MIX_TOKEN_TILE
    steps = seq // tile
    n_tiles = batch * steps
    cos, sin = _rope_tables(N_META + np.arange(seq))
    cosm, sinm = _rope_tables(np.arange(N_META))
    pair_up = lambda tab: np.concatenate([tab[0::2], tab[1::2]], axis=2)
    mask, qdec, kdec = (pair_up(tab) for tab in _decay_tables(RET_CHUNK, RET_CHUNK))
    _, _, kdecm = _decay_tables(N_META, N_META)
    zero_blk = jnp.zeros_like(pool_w[0])
    pool_w = jnp.stack([jnp.block([[pool_w[2 * pp], zero_blk], [zero_blk, pool_w[2 * pp + 1]]])
                        for pp in range(len(POOL_WINDOWS) // 2)])

    mix_tile = lambda t: jnp.minimum(t, n_tiles - 1)
    ffn_tile = lambda t: jnp.maximum(t - 2, 0)

    def col(j):
        return pl.BlockSpec((tile, RET_WIDTH), lambda t: (mix_tile(t), j))

    def meta_col(j):
        return pl.BlockSpec((N_META, RET_WIDTH), lambda t: (meta_row_block, j))

    in_specs = [
        col(0), col(1), col(2), col(3), col(4),
        pl.BlockSpec((tile, D_MODEL), lambda t: (mix_tile(t), 0)),
        pl.BlockSpec((tile, HEAD_DIM), lambda t: (mix_tile(t) % steps, 0)),
        pl.BlockSpec((tile, HEAD_DIM), lambda t: (mix_tile(t) % steps, 0)),
        meta_col(1), meta_col(2), meta_col(4),
        _const_spec(cosm.shape), _const_spec(sinm.shape), _const_spec(kdecm.shape),
        _const_spec(mask.shape), _const_spec(qdec.shape), _const_spec(kdec.shape),
        _const_spec(pool_w.shape), _const_spec(pool_scale.shape), _const_spec(w_out.shape),
        _const_spec(ln2g.shape), _const_spec(ln2b.shape),
        _const_spec(wg.shape), _const_spec(wu.shape), _const_spec(wd.shape),
        _const_spec(ln3g.shape), _const_spec(ln3b.shape),
    ]
    out_shape = [
        jax.ShapeDtypeStruct((batch * seq, D_MODEL), F32),
        jax.ShapeDtypeStruct((batch, RET_HEADS, HEAD_DIM, HEAD_DIM), F32),
        jax.ShapeDtypeStruct((batch, POOL_BUF, POOL_WIDTH), F32),
    ]
    out_specs = [
        pl.BlockSpec((tile, D_MODEL), lambda t: (ffn_tile(t), 0)),
        pl.BlockSpec((1, RET_HEADS, HEAD_DIM, HEAD_DIM), lambda t: (mix_tile(t) // steps, 0, 0, 0)),
        pl.BlockSpec((1, POOL_BUF, POOL_WIDTH), lambda t: (mix_tile(t) // steps, 0, 0)),
    ]
    return pl.pallas_call(
        functools.partial(_prompt_mixer_ffn_kernel, steps_per_seq=steps),
        grid=(n_tiles + 2,),
        in_specs=in_specs,
        out_specs=out_specs,
        out_shape=out_shape,
        scratch_shapes=[
            pltpu.VMEM((RET_HEADS, HEAD_DIM, HEAD_DIM), F32),
            pltpu.VMEM((N_META + tile, POOL_WIDTH), F32),
            pltpu.VMEM((tile, D_MODEL), BF16),
            pltpu.VMEM((tile, D_MODEL), F32),
            pltpu.VMEM((tile, D_MODEL), BF16),
            pltpu.VMEM((tile, D_MODEL), F32),
            pltpu.VMEM((tile, D_FF), BF16),
            pltpu.VMEM((8, PAIR_WIDTH, PAIR_WIDTH), BF16),
            pltpu.VMEM((8, 2 * PAIR_WIDTH, PAIR_WIDTH), BF16),
        ],
        compiler_params=pltpu.CompilerParams(
            dimension_semantics=("arbitrary",), vmem_limit_bytes=VMEM_LIMIT_BYTES),
        name="prompt_mixer_ffn",
    )(proj, proj, proj, proj, proj, x1, cos, sin, proj_small, proj_small, proj_small,
      cosm, sinm, kdecm, mask, qdec, kdec, pool_w, pool_scale, w_out, ln2g, ln2b,
      wg, wu, wd, ln3g, ln3b)


def _decode_mixer_kernel(q_ref, k_ref, v_ref, g_ref, p_ref, x1_ref, s0_ref, pref_ref,
                         cos_ref, sin_ref, mask_ref, qdec_ref, kdec_ref,
                         poolw_ref, pscale_ref, wout_ref, lng_ref, lnb_ref,
                         o_ref, sret_ref, spool_ref,
                         xs_ref, d_ref, mix_ref, *, dec_seq):
    rows = q_ref.shape[0]
    nseq = rows // dec_seq
    seq_per_group = BF16_ROWS // dec_seq
    cos, sin = cos_ref[...], sin_ref[...]
    row_seq = lax.broadcasted_iota(jnp.int32, (BF16_ROWS, HEAD_DIM), 0) // dec_seq

    for h in range(RET_HEADS):
        hs = slice(h * HEAD_DIM, (h + 1) * HEAD_DIM)
        qr = _rope(q_ref[:, hs], cos, sin)
        kr = _rope(k_ref[:, hs], cos, sin)
        v = v_ref[:, hs]
        vb = v.astype(BF16)
        scores = _dot_nt(qr.astype(BF16), kr.astype(BF16)) * mask_ref[h]
        o_inner = _dot(scores.astype(BF16), vb)
        qd = qr * qdec_ref[h]
        kd = kr * kdec_ref[h]
        o_cross = []
        for grp in range(rows // BF16_ROWS):
            gr = slice(grp * BF16_ROWS, (grp + 1) * BF16_ROWS)
            qd_g = qd[gr].astype(BF16)
            kd_g = kd[gr].astype(BF16)
            v_g = v[gr]
            acc = jnp.zeros((BF16_ROWS, HEAD_DIM), F32)
            for j in range(seq_per_group):
                b = grp * seq_per_group + j
                s = s0_ref[b, h]
                acc = jnp.where(row_seq == j, _dot(qd_g, s.astype(BF16)), acc)
                v_b = jnp.where(row_seq == j, v_g, 0.0).astype(BF16)
                sret_ref[b, h] = (GAMMAS[h] ** dec_seq) * s + _dot_tn(kd_g, v_b)
            o_cross.append(acc)
        o = o_inner + jnp.concatenate(o_cross, axis=0)
        mix_ref[:, hs] = (_silu(g_ref[:, hs]) * _group_norm(o)).astype(BF16)

    first = BF16_ROWS
    for b in range(nseq):
        xs_ref[b, first - POOL_BUF:first, :] = pref_ref[b]
        xs_ref[b, first:first + dec_seq, :] = p_ref[b * dec_seq:(b + 1) * dec_seq, :]
    for b in range(nseq):
        for gi, w in enumerate(POOL_WINDOWS):
            gs = slice(gi * POOL_GROUP, (gi + 1) * POOL_GROUP)
            cur = xs_ref[b, first:first + dec_seq, gs]
            wsum = cur
            for t in range(1, w):
                wsum = wsum + xs_ref[b, first - t:first - t + dec_seq, gs]
            d_ref[b * dec_seq:(b + 1) * dec_seq, gs] = wsum * (1.0 / w) - cur
        spool_ref[b] = xs_ref[b, first + dec_seq - POOL_BUF:first + dec_seq, :]
    for gi in range(len(POOL_WINDOWS)):
        gs = slice(gi * POOL_GROUP, (gi + 1) * POOL_GROUP)
        pooled = _dot(d_ref[:, gs].astype(BF16), poolw_ref[gi]) * pscale_ref[:, gs]
        mix_ref[:, RET_WIDTH + gi * POOL_GROUP:RET_WIDTH + (gi + 1) * POOL_GROUP] = pooled.astype(BF16)

    y = _dot(mix_ref[...], wout_ref[...])
    o_ref[...] = _layer_norm(ALPHA * x1_ref[...] + y, lng_ref[...], lnb_ref[...])


def _decode_mixer(proj, x1, state_ret, state_pool, pool_w, pool_scale, w_out, lng, lnb, nseq, dec_seq):
    assert BF16_ROWS % dec_seq == 0 and dec_seq <= POOL_BUF
    rows = DEC_SEQ_BLOCK * dec_seq
    steps = nseq // DEC_SEQ_BLOCK
    cos, sin = _rope_tables(PAST_LEN + (np.arange(rows) % dec_seq))
    mask, qdec, kdec = _decay_tables(rows, dec_seq)

    def col(j):
        return pl.BlockSpec((rows, RET_WIDTH), lambda i: (i, j))

    state_spec = pl.BlockSpec((DEC_SEQ_BLOCK, RET_HEADS, HEAD_DIM, HEAD_DIM), lambda i: (i, 0, 0, 0))
    pool_spec = pl.BlockSpec((DEC_SEQ_BLOCK, POOL_BUF, POOL_WIDTH), lambda i: (i, 0, 0))
    in_specs = [
        col(0), col(1), col(2), col(3), col(4),
        pl.BlockSpec((rows, D_MODEL), lambda i: (i, 0)),
        state_spec, pool_spec,
        _const_spec(cos.shape), _const_spec(sin.shape),
        _const_spec(mask.shape), _const_spec(qdec.shape), _const_spec(kdec.shape),
        _const_spec(pool_w.shape), _const_spec(pool_scale.shape), _const_spec(w_out.shape),
        _const_spec(lng.shape), _const_spec(lnb.shape),
    ]
    out_shape = [
        jax.ShapeDtypeStruct((nseq * dec_seq, D_MODEL), F32),
        jax.ShapeDtypeStruct(state_ret.shape, F32),
        jax.ShapeDtypeStruct(state_pool.shape, F32),
    ]
    out_specs = [pl.BlockSpec((rows, D_MODEL), lambda i: (i, 0)), state_spec, pool_spec]
    return pl.pallas_call(
        functools.partial(_decode_mixer_kernel, dec_seq=dec_seq),
        grid=(steps,),
        in_specs=in_specs,
        out_specs=out_specs,
        out_shape=out_shape,
        scratch_shapes=[
            pltpu.VMEM((DEC_SEQ_BLOCK, BF16_ROWS + 8, POOL_WIDTH), F32),
            pltpu.VMEM((rows, POOL_WIDTH), F32),
            pltpu.VMEM((rows, D_MODEL), BF16),
        ],
        compiler_params=pltpu.CompilerParams(
            dimension_semantics=("arbitrary",), vmem_limit_bytes=VMEM_LIMIT_BYTES),
        name="decode_mixer",
    )(proj, proj, proj, proj, proj, x1, state_ret, state_pool, cos, sin, mask, qdec, kdec,
      pool_w, pool_scale, w_out, lng, lnb)


def kernel(x_prompt, x_sample, state_ret, state_pool, meta_tokens, ffn1_w_gate, ffn1_w_up, ffn1_w_down,
           ln1_g, ln1_b, w_in, pool_w, pool_scale, w_out, ln2_g, ln2_b, ffn2_w_gate, ffn2_w_up,
           ffn2_w_down, ln3_g, ln3_b):
    assert ffn1_w_gate.shape[0] == DEPTH == 1
    batch, seq, _ = x_prompt.shape
    nseq, dec_seq, _ = x_sample.shape
    n_dec = nseq * dec_seq
    assert n_dec % N_META == 0

    bf = lambda w: w[0].astype(BF16)
    row = lambda v: v[0].reshape(1, -1)
    f1 = (bf(ffn1_w_gate), bf(ffn1_w_up), bf(ffn1_w_down), row(ln1_g), row(ln1_b))
    f2 = (bf(ffn2_w_gate), bf(ffn2_w_up), bf(ffn2_w_down), row(ln3_g), row(ln3_b))
    w_in_b, w_out_b, pool_w_b = bf(w_in), bf(w_out), bf(pool_w)
    pscale, g2, b2 = row(pool_scale), row(ln2_g), row(ln2_b)

    xp = x_prompt.reshape(batch * seq, D_MODEL)
    x_small = jnp.concatenate([x_sample.reshape(n_dec, D_MODEL), meta_tokens.astype(x_prompt.dtype)], axis=0)

    x1p, projp = _ffn(xp, *f1, w_in=w_in_b)
    x1s, projs = _ffn(x_small, *f1, w_in=w_in_b)

    y_prompt, ret_p, pool_p = _prompt_mixer_ffn(projp, x1p, projs, n_dec // N_META, pool_w_b, pscale,
                                                w_out_b, g2, b2, *f2, batch, seq)
    x2s, ret_s, pool_s = _decode_mixer(projs, x1s, state_ret[0], state_pool[0], pool_w_b, pscale,
                                       w_out_b, g2, b2, nseq, dec_seq)
    (y_sample,) = _ffn(x2s, *f2)
    return (y_prompt.reshape(batch, seq, D_MODEL), y_sample.reshape(nseq, dec_seq, D_MODEL),
            ret_p[None], pool_p[None], ret_s[None], pool_s[None])
```

```python
import functools
import math

import jax
import jax.numpy as jnp
import numpy as np
from jax import lax
from jax.experimental import pallas as pl
from jax.experimental.pallas import tpu as pltpu

F32 = jnp.float32
BF16 = jnp.bfloat16

D_MODEL = 1024
D_FF = 2816
N_META = 16
PAST_LEN = 16384
RET_HEADS = 4
HEAD_DIM = 128
RET_WIDTH = RET_HEADS * HEAD_DIM
RET_CHUNK = 128
ROPE_THETA = 10000.0
POOL_WINDOWS = (2, 4, 8, 16)
POOL_GROUP = 128
POOL_WIDTH = POOL_GROUP * len(POOL_WINDOWS)
POOL_BUF = max(POOL_WINDOWS) - 1
IN_WIDTH = 4 * RET_WIDTH + POOL_WIDTH
DEPTH = 1
ALPHA = (2.0 * DEPTH) ** 0.25
LN_EPS = 1e-5
GN_EPS = 1e-5
QK_SCALE = HEAD_DIM ** -0.5
GAMMAS = tuple(1.0 - 2.0 ** (-5.0 - h) for h in range(RET_HEADS))

VMEM_LIMIT_BYTES = 56 * 1024 * 1024
FFN_TOKEN_TILE = 512
FFN_COL_CHUNK = 256
MIX_TOKEN_TILE = 512
DEC_SEQ_BLOCK = 16
BF16_ROWS = 16
PAIR_WIDTH = 2 * HEAD_DIM


def _layer_norm(z, g, b):
    mu = jnp.mean(z, axis=-1, keepdims=True)
    zc = z - mu
    var = jnp.mean(zc * zc, axis=-1, keepdims=True)
    return zc * lax.rsqrt(var + LN_EPS) * g + b


def _silu(x):
    return x * jax.nn.sigmoid(x)


def _dot(a, b):
    return jnp.dot(a, b, preferred_element_type=F32)


def _dot_nt(a, b):
    return lax.dot_general(a, b, (((1,), (1,)), ((), ())), preferred_element_type=F32)


def _dot_tn(a, b):
    return lax.dot_general(a, b, (((0,), (0,)), ((), ())), preferred_element_type=F32)


def _ffn_body(x, wg_ref, wu_ref, wd_ref, lng_ref, lnb_ref, h_ref):
    xb = x.astype(BF16)
    for c in range(D_FF // FFN_COL_CHUNK):
        sl = slice(c * FFN_COL_CHUNK, (c + 1) * FFN_COL_CHUNK)
        g = _dot(xb, wg_ref[:, sl])
        u = _dot(xb, wu_ref[:, sl])
        h_ref[:, sl] = (_silu(g) * u).astype(BF16)
    y = _dot(h_ref[...], wd_ref[...])
    return _layer_norm(ALPHA * x + 0.5 * y, lng_ref[...], lnb_ref[...])


def _ffn_kernel(x_ref, wg_ref, wu_ref, wd_ref, lng_ref, lnb_ref, *rest, with_proj):
    if with_proj:
        win_ref, o_ref, proj_ref, h_ref = rest
    else:
        o_ref, h_ref = rest
    out = _ffn_body(x_ref[...], wg_ref, wu_ref, wd_ref, lng_ref, lnb_ref, h_ref)
    o_ref[...] = out
    if with_proj:
        proj_ref[...] = _dot(out.astype(BF16), win_ref[...])


def _const_spec(shape):
    zeros = (0,) * len(shape)
    return pl.BlockSpec(shape, lambda *_: zeros, pipeline_mode=pl.Buffered(1))


def _ffn(x, wg, wu, wd, lng, lnb, w_in=None):
    n = x.shape[0]
    tm = FFN_TOKEN_TILE if n % FFN_TOKEN_TILE == 0 else n
    with_proj = w_in is not None
    row = lambda i: (i, 0)
    in_specs = [
        pl.BlockSpec((tm, D_MODEL), row),
        _const_spec((D_MODEL, D_FF)),
        _const_spec((D_MODEL, D_FF)),
        _const_spec((D_FF, D_MODEL)),
        _const_spec((1, D_MODEL)),
        _const_spec((1, D_MODEL)),
    ]
    args = [x, wg, wu, wd, lng, lnb]
    out_shape = [jax.ShapeDtypeStruct((n, D_MODEL), F32)]
    out_specs = [pl.BlockSpec((tm, D_MODEL), row)]
    if with_proj:
        in_specs.append(_const_spec((D_MODEL, IN_WIDTH)))
        args.append(w_in)
        out_shape.append(jax.ShapeDtypeStruct((n, IN_WIDTH), F32))
        out_specs.append(pl.BlockSpec((tm, IN_WIDTH), row))
    return pl.pallas_call(
        functools.partial(_ffn_kernel, with_proj=with_proj),
        grid=(n // tm,),
        in_specs=in_specs,
        out_specs=out_specs,
        out_shape=out_shape,
        scratch_shapes=[pltpu.VMEM((tm, D_FF), BF16)],
        compiler_params=pltpu.CompilerParams(
            dimension_semantics=("arbitrary",), vmem_limit_bytes=VMEM_LIMIT_BYTES),
        name="ffn_proj" if with_proj else "ffn",
    )(*args)


def _interleave(major, starts):
    live = []
    for i, piece in enumerate(major):
        piece()
        live += [make() for make in starts.get(i, [])]
        live = [g for g in live if next(g, "done") != "done"]
    while live:
        live = [g for g in live if next(g, "done") != "done"]


def _run_all(starts):
    for i in sorted(starts):
        for make in starts[i]:
            for _ in make():
                pass


def _ffn_proj_pipelined_kernel(x_ref, wg_ref, wu_ref, wd_ref, lng_ref, lnb_ref, win_ref,
                               x1_ref, proj_ref, xb_ref, z1_ref, x1b_ref, h_ref):
    t = pl.program_id(0)
    n_tiles = pl.num_programs(0) - 1
    tile = x_ref.shape[0]
    row_blocks = [slice(r, r + RET_CHUNK) for r in range(0, tile, RET_CHUNK)]
    proj_chunk = 2 * FFN_COL_CHUNK

    def ln1_piece(rows):
        def run():
            x1_rows = _layer_norm(z1_ref[rows, :], lng_ref[...], lnb_ref[...])
            x1_ref[rows, :] = x1_rows
            x1b_ref[rows, :] = x1_rows.astype(BF16)
            yield
        return run

    def proj_piece(nk):
        def run():
            cs = slice(nk * proj_chunk, (nk + 1) * proj_chunk)
            proj_ref[:, cs] = _dot(x1b_ref[...], win_ref[:, cs])
        return run

    def gate_up_piece(ck):
        def run():
            sl = slice(ck * FFN_COL_CHUNK, (ck + 1) * FFN_COL_CHUNK)
            g = _dot(xb_ref[...], wg_ref[:, sl])
            u = _dot(xb_ref[...], wu_ref[:, sl])
            h_ref[:, sl] = (_silu(g) * u).astype(BF16)
        return run

    def down_piece(nk):
        def run():
            cs = slice(nk * FFN_COL_CHUNK, (nk + 1) * FFN_COL_CHUNK)
            z1_ref[:, cs] = ALPHA * x_ref[:, cs] + 0.5 * _dot(h_ref[...], wd_ref[:, cs])
        return run

    ln1_starts = {i: [ln1_piece(rows)] for i, rows in enumerate(row_blocks)}
    proj_pieces = [proj_piece(nk) for nk in range(IN_WIDTH // proj_chunk)]

    @pl.when(t == 0)
    def _clear_pipeline():
        z1_ref[...] = jnp.zeros(z1_ref.shape, F32)

    @pl.when(t < n_tiles)
    def _steady():
        xb_ref[...] = x_ref[...].astype(BF16)
        gate_up = [gate_up_piece(ck) for ck in range(D_FF // FFN_COL_CHUNK)]
        first_proj = len(row_blocks) + 1
        major = gate_up[:first_proj]
        for i, piece in enumerate(gate_up[first_proj:]):
            major += proj_pieces[i:i + 1] + [piece]
        major += proj_pieces[len(gate_up) - first_proj:]
        _interleave(major, ln1_starts)
        for nk in range(D_MODEL // FFN_COL_CHUNK):
            down_piece(nk)()

    @pl.when(t == n_tiles)
    def _drain():
        _run_all(ln1_starts)
        for piece in proj_pieces:
            piece()


def _ffn_proj_pipelined(x, wg, wu, wd, lng, lnb, w_in):
    n = x.shape[0]
    tm = FFN_TOKEN_TILE
    n_tiles = n // tm
    in_tile = lambda t: (jnp.minimum(t, n_tiles - 1), 0)
    out_tile = lambda t: (jnp.maximum(t - 1, 0), 0)
    return pl.pallas_call(
        _ffn_proj_pipelined_kernel,
        grid=(n_tiles + 1,),
        in_specs=[
            pl.BlockSpec((tm, D_MODEL), in_tile),
            _const_spec(wg.shape), _const_spec(wu.shape), _const_spec(wd.shape),
            _const_spec(lng.shape), _const_spec(lnb.shape), _const_spec(w_in.shape),
        ],
        out_specs=[pl.BlockSpec((tm, D_MODEL), out_tile), pl.BlockSpec((tm, IN_WIDTH), out_tile)],
        out_shape=[jax.ShapeDtypeStruct((n, D_MODEL), F32), jax.ShapeDtypeStruct((n, IN_WIDTH), F32)],
        scratch_shapes=[
            pltpu.VMEM((tm, D_MODEL), BF16),
            pltpu.VMEM((tm, D_MODEL), F32),
            pltpu.VMEM((tm, D_MODEL), BF16),
            pltpu.VMEM((tm, D_FF), BF16),
        ],
        compiler_params=pltpu.CompilerParams(
            dimension_semantics=("arbitrary",), vmem_limit_bytes=VMEM_LIMIT_BYTES),
        name="ffn_proj_pipelined",
    )(x, wg, wu, wd, lng, lnb, w_in)


def _rope_tables(positions):
    half = HEAD_DIM // 2
    inv_freq = ROPE_THETA ** (-np.arange(0, HEAD_DIM, 2, dtype=np.float64) / HEAD_DIM)
    ang = np.asarray(positions, np.float64)[:, None] * inv_freq[None, :]
    cos, sin = np.cos(ang), np.sin(ang)
    assert cos.shape[1] == half
    return (np.concatenate([cos, cos], axis=1).astype(np.float32),
            np.concatenate([-sin, sin], axis=1).astype(np.float32))


def _decay_tables(chunk, seq_len):
    r = np.arange(chunk)
    seq, idx = r // seq_len, (r % seq_len).astype(np.float64)
    same = seq[:, None] == seq[None, :]
    diff = idx[:, None] - idx[None, :]
    mask, qdec, kdec = [], [], []
    for gamma in GAMMAS:
        lg = math.log(gamma)
        mask.append(np.where(same & (diff >= 0), np.exp(lg * np.maximum(diff, 0.0)), 0.0) * QK_SCALE)
        qdec.append(np.broadcast_to((np.exp(lg * (idx + 1.0)) * QK_SCALE)[:, None], (chunk, HEAD_DIM)))
        kdec.append(np.broadcast_to(np.exp(lg * (seq_len - 1.0 - idx))[:, None], (chunk, HEAD_DIM)))
    to32 = lambda t: np.stack(t).astype(np.float32)
    return to32(mask), to32(qdec), to32(kdec)


def _rope(x, cos, sin):
    return x * cos + pltpu.roll(x, HEAD_DIM // 2, 1) * sin


def _group_norm(o):
    mu = jnp.mean(o, axis=-1, keepdims=True)
    oc = o - mu
    var = jnp.mean(oc * oc, axis=-1, keepdims=True)
    return oc * lax.rsqrt(var + GN_EPS)


def _prompt_mixer_ffn_kernel(q_ref, k_ref, v_ref, g_ref, p_ref, x1_ref, cos_ref, sin_ref,
                             km_ref, vm_ref, pm_ref, cosm_ref, sinm_ref, kdecm_ref,
                             mask_ref, qdec_ref, kdec_ref, poolw_ref, pscale_ref, wout_ref,
                             ln2g_ref, ln2b_ref, wg_ref, wu_ref, wd_ref, ln3g_ref, ln3b_ref,
                             y_ref, sret_ref, spool_ref,
                             s_ref, xp_ref, mix_ref, x2_ref, xb_ref, ypre_ref, h_ref, kb_ref, vs_ref,
                             *, steps_per_seq):
    t = pl.program_id(0)
    n_tiles = pl.num_programs(0) - 2
    c = t % steps_per_seq
    tile = q_ref.shape[0]
    hist = N_META
    row_blocks = [slice(r, r + RET_CHUNK) for r in range(0, tile, RET_CHUNK)]

    @pl.when(jnp.logical_and(c == 0, t < n_tiles))
    def _init_from_meta():
        for h in range(RET_HEADS):
            hs = slice(h * HEAD_DIM, (h + 1) * HEAD_DIM)
            kr = _rope(km_ref[:, hs], cosm_ref[...], sinm_ref[...])
            kd = (kr * kdecm_ref[h]).astype(BF16)
            s_ref[h] = _dot_tn(kd, vm_ref[:, hs].astype(BF16))
        xp_ref[0:hist, :] = pm_ref[...]

    def ln3_fetch_piece(rows):
        def run():
            y_ref[rows, :] = ypre_ref[rows, :]
            yield
        return run

    def ln3_piece(rows):
        def run():
            y_ref[rows, :] = _layer_norm(y_ref[rows, :], ln3g_ref[...], ln3b_ref[...])
            yield
        return run

    state = [None] * RET_HEADS

    def retention_piece(ci, hp):
        def run():
            rows = slice(ci * RET_CHUNK, (ci + 1) * RET_CHUNK)
            pair = slice(hp * PAIR_WIDTH, (hp + 1) * PAIR_WIDTH)
            buf = ci * 2 + hp
            cos, sin = cos_ref[rows, :], sin_ref[rows, :]
            qr, kr = [], []
            for j in range(2):
                h = 2 * hp + j
                hs = slice(h * HEAD_DIM, (h + 1) * HEAD_DIM)
                blk = slice(j * HEAD_DIM, (j + 1) * HEAD_DIM)
                qr.append(_rope(q_ref[rows, hs], cos, sin))
                kr.append(_rope(k_ref[rows, hs], cos, sin))
                kb_ref[buf, blk, blk] = kr[j].astype(BF16)
                vs_ref[buf, blk, blk] = v_ref[rows, hs].astype(BF16)
                vs_ref[buf, PAIR_WIDTH + j * HEAD_DIM:PAIR_WIDTH + (j + 1) * HEAD_DIM, blk] = (
                    state[h].astype(BF16))
            q2 = jnp.concatenate(qr, axis=1)
            k2 = jnp.concatenate(kr, axis=1)
            scores = _dot_nt(q2.astype(BF16), kb_ref[buf])
            kd2 = (k2 * kdec_ref[hp]).astype(BF16)
            upd = _dot_tn(kd2, v_ref[rows, pair].astype(BF16))
            qd2 = (q2 * qdec_ref[hp]).astype(BF16)
            yield
            lhs = jnp.concatenate([(scores * mask_ref[hp]).astype(BF16), qd2], axis=1)
            o2 = _dot(lhs, vs_ref[buf])
            for j in range(2):
                blk = slice(j * HEAD_DIM, (j + 1) * HEAD_DIM)
                state[2 * hp + j] = (GAMMAS[2 * hp + j] ** RET_CHUNK) * state[2 * hp + j] + upd[blk, blk]
            yield
            for j in range(2):
                hs = slice((2 * hp + j) * HEAD_DIM, (2 * hp + j + 1) * HEAD_DIM)
                blk = slice(j * HEAD_DIM, (j + 1) * HEAD_DIM)
                mix_ref[rows, hs] = (_silu(g_ref[rows, hs]) * _group_norm(o2[:, blk])).astype(BF16)
        return run

    def pool_piece(pp):
        def run():
            d2 = []
            for j in range(2):
                gi = 2 * pp + j
                w = POOL_WINDOWS[gi]
                gs = slice(gi * POOL_GROUP, (gi + 1) * POOL_GROUP)
                xp_ref[hist:hist + tile, gs] = p_ref[:, gs]
                rows_all = xp_ref[:, gs]
                wsum, shift = rows_all, 1
                while shift < w:
                    wsum = wsum + pltpu.roll(wsum, shift, 0)
                    shift *= 2
                d2.append((wsum[hist:] * (1.0 / w) - rows_all[hist:]).astype(BF16))
                xp_ref[0:hist, gs] = xp_ref[tile:tile + hist, gs]
                if j == 0:
                    yield
            pair = slice(pp * PAIR_WIDTH, (pp + 1) * PAIR_WIDTH)
            pooled = _dot(jnp.concatenate(d2, axis=1), poolw_ref[pp])
            yield
            pooled = pooled * pscale_ref[:, pair]
            mix_ref[:, RET_WIDTH + pp * PAIR_WIDTH:RET_WIDTH + (pp + 1) * PAIR_WIDTH] = pooled.astype(BF16)
        return run

    def _run_step(with_a, with_b, with_c):
        starts, tail_pieces = {}, []

        def start_at(i, piece):
            starts.setdefault(i, []).append(piece)

        if with_c:
            for i, rows in enumerate(row_blocks):
                start_at(0, ln3_fetch_piece(rows))
        if with_a:
            for h in range(RET_HEADS):
                state[h] = s_ref[h]
            n_ret = 0
            for ci in range(len(row_blocks)):
                for hp in range(RET_HEADS // 2):
                    start_at(n_ret, retention_piece(ci, hp))
                    n_ret += 1
            for pp in range(len(POOL_WINDOWS) // 2):
                start_at(2 * pp, pool_piece(pp))

        if with_b:
            def gate_up_piece(ck):
                def run():
                    sl = slice(ck * FFN_COL_CHUNK, (ck + 1) * FFN_COL_CHUNK)
                    g = _dot(xb_ref[...], wg_ref[:, sl])
                    u = _dot(xb_ref[...], wu_ref[:, sl])
                    h_ref[:, sl] = (_silu(g) * u).astype(BF16)
                return run

            def residual_piece():
                ypre_ref[...] = ALPHA * x2_ref[...]
                yield

            start_at(1, residual_piece)
            if with_c:
                n_slots = D_FF // FFN_COL_CHUNK
                for i, rows in enumerate(row_blocks):
                    start_at(n_slots - len(row_blocks) + i, ln3_piece(rows))
            _interleave([gate_up_piece(ck) for ck in range(D_FF // FFN_COL_CHUNK)], starts)
        else:
            _run_all(starts)

        tail_starts = {}
        if with_a:
            for h in range(RET_HEADS):
                s_ref[h] = state[h]
            x2_ref[...] = ALPHA * x1_ref[...] + _dot(mix_ref[...], wout_ref[...])

            def ln2_piece(rows):
                def run():
                    x2_rows = _layer_norm(x2_ref[rows, :], ln2g_ref[...], ln2b_ref[...])
                    x2_ref[rows, :] = x2_rows
                    xb_ref[rows, :] = x2_rows.astype(BF16)
                    yield
                return run

            tail_starts = {i: [ln2_piece(rows)] for i, rows in enumerate(row_blocks)}
        if with_c and not with_b:
            for i, rows in enumerate(row_blocks):
                tail_starts.setdefault(i, []).append(ln3_piece(rows))

        if with_b:
            def down_piece(nk):
                def run():
                    cs = slice(nk * FFN_COL_CHUNK, (nk + 1) * FFN_COL_CHUNK)
                    ypre_ref[:, cs] = ypre_ref[:, cs] + 0.5 * _dot(h_ref[...], wd_ref[:, cs])
                return run

            _interleave([down_piece(nk) for nk in range(D_MODEL // FFN_COL_CHUNK)], tail_starts)
        else:
            _run_all(tail_starts)

    @pl.when(t == 0)
    def _clear_pipeline():
        x2_ref[...] = jnp.zeros(x2_ref.shape, F32)
        xb_ref[...] = jnp.zeros(xb_ref.shape, BF16)
        kb_ref[...] = jnp.zeros(kb_ref.shape, BF16)
        vs_ref[...] = jnp.zeros(vs_ref.shape, BF16)
        ypre_ref[...] = jnp.zeros(ypre_ref.shape, F32)

    @pl.when(t <= n_tiles)
    def _steady():
        _run_step(True, True, True)

    @pl.when(t == n_tiles + 1)
    def _drain_last():
        _run_step(False, False, True)

    @pl.when(jnp.logical_and(c == steps_per_seq - 1, t < n_tiles))
    def _emit_state():
        sret_ref[0] = s_ref[...]
        spool_ref[0] = xp_ref[hist - POOL_BUF:hist, :]


def _prompt_mixer_ffn(proj, x1, proj_small, meta_row_block, pool_w, pool_scale, w_out, ln2g, ln2b,
                      wg, wu, wd, ln3g, ln3b, batch, seq):
    tile = MIX_TOKEN_TILE
    steps = seq // tile
    n_tiles = batch * steps
    cos, sin = _rope_tables(N_META + np.arange(seq))
    cosm, sinm = _rope_tables(np.arange(N_META))
    pair_up = lambda tab: np.concatenate([tab[0::2], tab[1::2]], axis=2)
    mask, qdec, kdec = (pair_up(tab) for tab in _decay_tables(RET_CHUNK, RET_CHUNK))
    _, _, kdecm = _decay_tables(N_META, N_META)
    zero_blk = jnp.zeros_like(pool_w[0])
    pool_w = jnp.stack([jnp.block([[pool_w[2 * pp], zero_blk], [zero_blk, pool_w[2 * pp + 1]]])
                        for pp in range(len(POOL_WINDOWS) // 2)])

    mix_tile = lambda t: jnp.minimum(t, n_tiles - 1)
    ffn_tile = lambda t: jnp.maximum(t - 2, 0)

    def col(j):
        return pl.BlockSpec((tile, RET_WIDTH), lambda t: (mix_tile(t), j))

    def meta_col(j):
        return pl.BlockSpec((N_META, RET_WIDTH), lambda t: (meta_row_block, j))

    in_specs = [
        col(0), col(1), col(2), col(3), col(4),
        pl.BlockSpec((tile, D_MODEL), lambda t: (mix_tile(t), 0)),
        pl.BlockSpec((tile, HEAD_DIM), lambda t: (mix_tile(t) % steps, 0)),
        pl.BlockSpec((tile, HEAD_DIM), lambda t: (mix_tile(t) % steps, 0)),
        meta_col(1), meta_col(2), meta_col(4),
        _const_spec(cosm.shape), _const_spec(sinm.shape), _const_spec(kdecm.shape),
        _const_spec(mask.shape), _const_spec(qdec.shape), _const_spec(kdec.shape),
        _const_spec(pool_w.shape), _const_spec(pool_scale.shape), _const_spec(w_out.shape),
        _const_spec(ln2g.shape), _const_spec(ln2b.shape),
        _const_spec(wg.shape), _const_spec(wu.shape), _const_spec(wd.shape),
        _const_spec(ln3g.shape), _const_spec(ln3b.shape),
    ]
    out_shape = [
        jax.ShapeDtypeStruct((batch * seq, D_MODEL), F32),
        jax.ShapeDtypeStruct((batch, RET_HEADS, HEAD_DIM, HEAD_DIM), F32),
        jax.ShapeDtypeStruct((batch, POOL_BUF, POOL_WIDTH), F32),
    ]
    out_specs = [
        pl.BlockSpec((tile, D_MODEL), lambda t: (ffn_tile(t), 0)),
        pl.BlockSpec((1, RET_HEADS, HEAD_DIM, HEAD_DIM), lambda t: (mix_tile(t) // steps, 0, 0, 0)),
        pl.BlockSpec((1, POOL_BUF, POOL_WIDTH), lambda t: (mix_tile(t) // steps, 0, 0)),
    ]
    return pl.pallas_call(
        functools.partial(_prompt_mixer_ffn_kernel, steps_per_seq=steps),
        grid=(n_tiles + 2,),
        in_specs=in_specs,
        out_specs=out_specs,
        out_shape=out_shape,
        scratch_shapes=[
            pltpu.VMEM((RET_HEADS, HEAD_DIM, HEAD_DIM), F32),
            pltpu.VMEM((N_META + tile, POOL_WIDTH), F32),
            pltpu.VMEM((tile, D_MODEL), BF16),
            pltpu.VMEM((tile, D_MODEL), F32),
            pltpu.VMEM((tile, D_MODEL), BF16),
            pltpu.VMEM((tile, D_MODEL), F32),
            pltpu.VMEM((tile, D_FF), BF16),
            pltpu.VMEM((8, PAIR_WIDTH, PAIR_WIDTH), BF16),
            pltpu.VMEM((8, 2 * PAIR_WIDTH, PAIR_WIDTH), BF16),
        ],
        compiler_params=pltpu.CompilerParams(
            dimension_semantics=("arbitrary",), vmem_limit_bytes=VMEM_LIMIT_BYTES),
        name="prompt_mixer_ffn",
    )(proj, proj, proj, proj, proj, x1, cos, sin, proj_small, proj_small, proj_small,
      cosm, sinm, kdecm, mask, qdec, kdec, pool_w, pool_scale, w_out, ln2g, ln2b,
      wg, wu, wd, ln3g, ln3b)


def _decode_mixer_kernel(q_ref, k_ref, v_ref, g_ref, p_ref, x1_ref, s0_ref, pref_ref,
                         cos_ref, sin_ref, mask_ref, qdec_ref, kdec_ref,
                         poolw_ref, pscale_ref, wout_ref, lng_ref, lnb_ref,
                         o_ref, sret_ref, spool_ref,
                         xs_ref, d_ref, mix_ref, *, dec_seq):
    rows = q_ref.shape[0]
    nseq = rows // dec_seq
    seq_per_group = BF16_ROWS // dec_seq
    cos, sin = cos_ref[...], sin_ref[...]
    row_seq = lax.broadcasted_iota(jnp.int32, (BF16_ROWS, HEAD_DIM), 0) // dec_seq

    for h in range(RET_HEADS):
        hs = slice(h * HEAD_DIM, (h + 1) * HEAD_DIM)
        qr = _rope(q_ref[:, hs], cos, sin)
        kr = _rope(k_ref[:, hs], cos, sin)
        v = v_ref[:, hs]
        vb = v.astype(BF16)
        scores = _dot_nt(qr.astype(BF16), kr.astype(BF16)) * mask_ref[h]
        o_inner = _dot(scores.astype(BF16), vb)
        qd = qr * qdec_ref[h]
        kd = kr * kdec_ref[h]
        o_cross = []
        for grp in range(rows // BF16_ROWS):
            gr = slice(grp * BF16_ROWS, (grp + 1) * BF16_ROWS)
            qd_g = qd[gr].astype(BF16)
            kd_g = kd[gr].astype(BF16)
            v_g = v[gr]
            acc = jnp.zeros((BF16_ROWS, HEAD_DIM), F32)
            for j in range(seq_per_group):
                b = grp * seq_per_group + j
                s = s0_ref[b, h]
                acc = jnp.where(row_seq == j, _dot(qd_g, s.astype(BF16)), acc)
                v_b = jnp.where(row_seq == j, v_g, 0.0).astype(BF16)
                sret_ref[b, h] = (GAMMAS[h] ** dec_seq) * s + _dot_tn(kd_g, v_b)
            o_cross.append(acc)
        o = o_inner + jnp.concatenate(o_cross, axis=0)
        mix_ref[:, hs] = (_silu(g_ref[:, hs]) * _group_norm(o)).astype(BF16)

    first = BF16_ROWS
    for b in range(nseq):
        xs_ref[b, first - POOL_BUF:first, :] = pref_ref[b]
        xs_ref[b, first:first + dec_seq, :] = p_ref[b * dec_seq:(b + 1) * dec_seq, :]
    for b in range(nseq):
        for gi, w in enumerate(POOL_WINDOWS):
            gs = slice(gi * POOL_GROUP, (gi + 1) * POOL_GROUP)
            cur = xs_ref[b, first:first + dec_seq, gs]
            wsum = cur
            for t in range(1, w):
                wsum = wsum + xs_ref[b, first - t:first - t + dec_seq, gs]
            d_ref[b * dec_seq:(b + 1) * dec_seq, gs] = wsum * (1.0 / w) - cur
        spool_ref[b] = xs_ref[b, first + dec_seq - POOL_BUF:first + dec_seq, :]
    for gi in range(len(POOL_WINDOWS)):
        gs = slice(gi * POOL_GROUP, (gi + 1) * POOL_GROUP)
        pooled = _dot(d_ref[:, gs].astype(BF16), poolw_ref[gi]) * pscale_ref[:, gs]
        mix_ref[:, RET_WIDTH + gi * POOL_GROUP:RET_WIDTH + (gi + 1) * POOL_GROUP] = pooled.astype(BF16)

    y = _dot(mix_ref[...], wout_ref[...])
    o_ref[...] = _layer_norm(ALPHA * x1_ref[...] + y, lng_ref[...], lnb_ref[...])


def _decode_mixer(proj, x1, state_ret, state_pool, pool_w, pool_scale, w_out, lng, lnb, nseq, dec_seq):
    assert BF16_ROWS % dec_seq == 0 and dec_seq <= POOL_BUF
    rows = DEC_SEQ_BLOCK * dec_seq
    steps = nseq // DEC_SEQ_BLOCK
    cos, sin = _rope_tables(PAST_LEN + (np.arange(rows) % dec_seq))
    mask, qdec, kdec = _decay_tables(rows, dec_seq)

    def col(j):
        return pl.BlockSpec((rows, RET_WIDTH), lambda i: (i, j))

    state_spec = pl.BlockSpec((DEC_SEQ_BLOCK, RET_HEADS, HEAD_DIM, HEAD_DIM), lambda i: (i, 0, 0, 0))
    pool_spec = pl.BlockSpec((DEC_SEQ_BLOCK, POOL_BUF, POOL_WIDTH), lambda i: (i, 0, 0))
    in_specs = [
        col(0), col(1), col(2), col(3), col(4),
        pl.BlockSpec((rows, D_MODEL), lambda i: (i, 0)),
        state_spec, pool_spec,
        _const_spec(cos.shape), _const_spec(sin.shape),
        _const_spec(mask.shape), _const_spec(qdec.shape), _const_spec(kdec.shape),
        _const_spec(pool_w.shape), _const_spec(pool_scale.shape), _const_spec(w_out.shape),
        _const_spec(lng.shape), _const_spec(lnb.shape),
    ]
    out_shape = [
        jax.ShapeDtypeStruct((nseq * dec_seq, D_MODEL), F32),
        jax.ShapeDtypeStruct(state_ret.shape, F32),
        jax.ShapeDtypeStruct(state_pool.shape, F32),
    ]
    out_specs = [pl.BlockSpec((rows, D_MODEL), lambda i: (i, 0)), state_spec, pool_spec]
    return pl.pallas_call(
        functools.partial(_decode_mixer_kernel, dec_seq=dec_seq),
        grid=(steps,),
        in_specs=in_specs,
        out_specs=out_specs,
        out_shape=out_shape,
        scratch_shapes=[
            pltpu.VMEM((DEC_SEQ_BLOCK, BF16_ROWS + 8, POOL_WIDTH), F32),
            pltpu.VMEM((rows, POOL_WIDTH), F32),
            pltpu.VMEM((rows, D_MODEL), BF16),
        ],
        compiler_params=pltpu.CompilerParams(
            dimension_semantics=("arbitrary",), vmem_limit_bytes=VMEM_LIMIT_BYTES),
        name="decode_mixer",
    )(proj, proj, proj, proj, proj, x1, state_ret, state_pool, cos, sin, mask, qdec, kdec,
      pool_w, pool_scale, w_out, lng, lnb)


def kernel(x_prompt, x_sample, state_ret, state_pool, meta_tokens, ffn1_w_gate, ffn1_w_up, ffn1_w_down,
           ln1_g, ln1_b, w_in, pool_w, pool_scale, w_out, ln2_g, ln2_b, ffn2_w_gate, ffn2_w_up,
           ffn2_w_down, ln3_g, ln3_b):
    assert ffn1_w_gate.shape[0] == DEPTH == 1
    batch, seq, _ = x_prompt.shape
    nseq, dec_seq, _ = x_sample.shape
    n_dec = nseq * dec_seq
    assert n_dec % N_META == 0

    bf = lambda w: w[0].astype(BF16)
    row = lambda v: v[0].reshape(1, -1)
    f1 = (bf(ffn1_w_gate), bf(ffn1_w_up), bf(ffn1_w_down), row(ln1_g), row(ln1_b))
    f2 = (bf(ffn2_w_gate), bf(ffn2_w_up), bf(ffn2_w_down), row(ln3_g), row(ln3_b))
    w_in_b, w_out_b, pool_w_b = bf(w_in), bf(w_out), bf(pool_w)
    pscale, g2, b2 = row(pool_scale), row(ln2_g), row(ln2_b)

    xp = x_prompt.reshape(batch * seq, D_MODEL)
    x_small = jnp.concatenate([x_sample.reshape(n_dec, D_MODEL), meta_tokens.astype(x_prompt.dtype)], axis=0)

    x1p, projp = _ffn_proj_pipelined(xp, *f1, w_in_b)
    x1s, projs = _ffn(x_small, *f1, w_in=w_in_b)

    y_prompt, ret_p, pool_p = _prompt_mixer_ffn(projp, x1p, projs, n_dec // N_META, pool_w_b, pscale,
                                                w_out_b, g2, b2, *f2, batch, seq)
    x2s, ret_s, pool_s = _decode_mixer(projs, x1s, state_ret[0], state_pool[0], pool_w_b, pscale,
                                       w_out_b, g2, b2, nseq, dec_seq)
    (y_sample,) = _ffn(x2s, *f2)
    return (y_prompt.reshape(batch, seq, D_MODEL), y_sample.reshape(nseq, dec_seq, D_MODEL),
            ret_p[None], pool_p[None], ret_s[None], pool_s[None])
```

```python
import functools
import math

import jax
import jax.numpy as jnp
import numpy as np
from jax import lax
from jax.experimental import pallas as pl
from jax.experimental.pallas import tpu as pltpu

F32 = jnp.float32
BF16 = jnp.bfloat16

D_MODEL = 1024
D_FF = 2816
N_META = 16
PAST_LEN = 16384
RET_HEADS = 4
HEAD_DIM = 128
RET_WIDTH = RET_HEADS * HEAD_DIM
RET_CHUNK = 128
ROPE_THETA = 10000.0
POOL_WINDOWS = (2, 4, 8, 16)
POOL_GROUP = 128
POOL_WIDTH = POOL_GROUP * len(POOL_WINDOWS)
POOL_BUF = max(POOL_WINDOWS) - 1
IN_WIDTH = 4 * RET_WIDTH + POOL_WIDTH
DEPTH = 1
ALPHA = (2.0 * DEPTH) ** 0.25
LN_EPS = 1e-5
GN_EPS = 1e-5
QK_SCALE = HEAD_DIM ** -0.5
GAMMAS = tuple(1.0 - 2.0 ** (-5.0 - h) for h in range(RET_HEADS))

VMEM_LIMIT_BYTES = 56 * 1024 * 1024
FFN_TOKEN_TILE = 512
FFN_COL_CHUNK = 256
MIX_TOKEN_TILE = 512
DEC_SEQ_BLOCK = 16
BF16_ROWS = 16
PAIR_WIDTH = 2 * HEAD_DIM


def _layer_norm(z, g, b):
    mu = jnp.mean(z, axis=-1, keepdims=True)
    zc = z - mu
    var = jnp.mean(zc * zc, axis=-1, keepdims=True)
    return zc * lax.rsqrt(var + LN_EPS) * g + b


def _silu(x):
    return x * jax.nn.sigmoid(x)


def _dot(a, b):
    return jnp.dot(a, b, preferred_element_type=F32)


def _dot_nt(a, b):
    return lax.dot_general(a, b, (((1,), (1,)), ((), ())), preferred_element_type=F32)


def _dot_tn(a, b):
    return lax.dot_general(a, b, (((0,), (0,)), ((), ())), preferred_element_type=F32)


def _ffn_body(x, wg_ref, wu_ref, wd_ref, lng_ref, lnb_ref, h_ref):
    xb = x.astype(BF16)
    for c in range(D_FF // FFN_COL_CHUNK):
        sl = slice(c * FFN_COL_CHUNK, (c + 1) * FFN_COL_CHUNK)
        g = _dot(xb, wg_ref[:, sl])
        u = _dot(xb, wu_ref[:, sl])
        h_ref[:, sl] = (_silu(g) * u).astype(BF16)
    y = _dot(h_ref[...], wd_ref[...])
    return _layer_norm(ALPHA * x + 0.5 * y, lng_ref[...], lnb_ref[...])


def _ffn_kernel(x_ref, wg_ref, wu_ref, wd_ref, lng_ref, lnb_ref, *rest, with_proj):
    if with_proj:
        win_ref, o_ref, proj_ref, h_ref = rest
    else:
        o_ref, h_ref = rest
    out = _ffn_body(x_ref[...], wg_ref, wu_ref, wd_ref, lng_ref, lnb_ref, h_ref)
    o_ref[...] = out
    if with_proj:
        proj_ref[...] = _dot(out.astype(BF16), win_ref[...])


def _const_spec(shape):
    zeros = (0,) * len(shape)
    return pl.BlockSpec(shape, lambda *_: zeros, pipeline_mode=pl.Buffered(1))


def _ffn(x, wg, wu, wd, lng, lnb, w_in=None):
    n = x.shape[0]
    tm = FFN_TOKEN_TILE if n % FFN_TOKEN_TILE == 0 else n
    with_proj = w_in is not None
    row = lambda i: (i, 0)
    in_specs = [
        pl.BlockSpec((tm, D_MODEL), row),
        _const_spec((D_MODEL, D_FF)),
        _const_spec((D_MODEL, D_FF)),
        _const_spec((D_FF, D_MODEL)),
        _const_spec((1, D_MODEL)),
        _const_spec((1, D_MODEL)),
    ]
    args = [x, wg, wu, wd, lng, lnb]
    out_shape = [jax.ShapeDtypeStruct((n, D_MODEL), F32)]
    out_specs = [pl.BlockSpec((tm, D_MODEL), row)]
    if with_proj:
        in_specs.append(_const_spec((D_MODEL, IN_WIDTH)))
        args.append(w_in)
        out_shape.append(jax.ShapeDtypeStruct((n, IN_WIDTH), F32))
        out_specs.append(pl.BlockSpec((tm, IN_WIDTH), row))
    return pl.pallas_call(
        functools.partial(_ffn_kernel, with_proj=with_proj),
        grid=(n // tm,),
        in_specs=in_specs,
        out_specs=out_specs,
        out_shape=out_shape,
        scratch_shapes=[pltpu.VMEM((tm, D_FF), BF16)],
        compiler_params=pltpu.CompilerParams(
            dimension_semantics=("arbitrary",), vmem_limit_bytes=VMEM_LIMIT_BYTES),
        name="ffn_proj" if with_proj else "ffn",
    )(*args)


def _interleave(major, starts):
    live = []
    for i, piece in enumerate(major):
        piece()
        live += [make() for make in starts.get(i, [])]
        live = [g for g in live if next(g, "done") != "done"]
    while live:
        live = [g for g in live if next(g, "done") != "done"]


def _run_all(starts):
    for i in sorted(starts):
        for make in starts[i]:
            for _ in make():
                pass


def _ffn_proj_pipelined_kernel(x_ref, wg_ref, wu_ref, wd_ref, lng_ref, lnb_ref, win_ref, *rest, n_cast):
    cast_in, rest = rest[:n_cast], rest[n_cast:]
    x1_ref, proj_ref = rest[:2]
    cast_out, (xb_ref, z1_ref, x1b_ref, h_ref) = rest[2:2 + n_cast], rest[2 + n_cast:]
    t = pl.program_id(0)
    n_tiles = pl.num_programs(0) - 1
    tile = x_ref.shape[0]
    row_blocks = [slice(r, r + RET_CHUNK) for r in range(0, tile, RET_CHUNK)]
    proj_chunk = 2 * FFN_COL_CHUNK

    def ln1_piece(rows):
        def run():
            x1_rows = _layer_norm(z1_ref[rows, :], lng_ref[...], lnb_ref[...])
            x1_ref[rows, :] = x1_rows
            x1b_ref[rows, :] = x1_rows.astype(BF16)
            yield
        return run

    def proj_piece(nk):
        def run():
            cs = slice(nk * proj_chunk, (nk + 1) * proj_chunk)
            proj_ref[:, cs] = _dot(x1b_ref[...], win_ref[:, cs])
        return run

    def gate_up_piece(ck):
        def run():
            sl = slice(ck * FFN_COL_CHUNK, (ck + 1) * FFN_COL_CHUNK)
            g = _dot(xb_ref[...], wg_ref[:, sl])
            u = _dot(xb_ref[...], wu_ref[:, sl])
            h_ref[:, sl] = (_silu(g) * u).astype(BF16)
        return run

    def down_piece(nk):
        def run():
            cs = slice(nk * FFN_COL_CHUNK, (nk + 1) * FFN_COL_CHUNK)
            z1_ref[:, cs] = ALPHA * x_ref[:, cs] + 0.5 * _dot(h_ref[...], wd_ref[:, cs])
        return run

    ln1_starts = {i: [ln1_piece(rows)] for i, rows in enumerate(row_blocks)}
    proj_pieces = [proj_piece(nk) for nk in range(IN_WIDTH // proj_chunk)]

    @pl.when(t == 0)
    def _clear_pipeline():
        z1_ref[...] = jnp.zeros(z1_ref.shape, F32)

    @pl.when(t < n_tiles)
    def _steady():
        xb_ref[...] = x_ref[...].astype(BF16)
        for src_ref, dst_ref in zip(cast_in, cast_out):
            dst_ref[...] = src_ref[...].astype(BF16)
        gate_up = [gate_up_piece(ck) for ck in range(D_FF // FFN_COL_CHUNK)]
        first_proj = len(row_blocks) + 1
        major = gate_up[:first_proj]
        for i, piece in enumerate(gate_up[first_proj:]):
            major += proj_pieces[i:i + 1] + [piece]
        major += proj_pieces[len(gate_up) - first_proj:]
        _interleave(major, ln1_starts)
        for nk in range(D_MODEL // FFN_COL_CHUNK):
            down_piece(nk)()

    @pl.when(t == n_tiles)
    def _drain():
        _run_all(ln1_starts)
        for piece in proj_pieces:
            piece()


def _slab_rows(rows, max_slabs):
    for slab in range(BF16_ROWS, rows + 1, BF16_ROWS):
        if rows % slab == 0 and rows // slab <= max_slabs:
            return slab
    raise ValueError(f"no slab size for {rows} rows in {max_slabs} steps")


def _ffn_proj_pipelined(x, wg, wu, wd, lng, lnb, w_in, cast_weights):
    n = x.shape[0]
    tm = FFN_TOKEN_TILE
    n_tiles = n // tm
    in_tile = lambda t: (jnp.minimum(t, n_tiles - 1), 0)
    out_tile = lambda t: (jnp.maximum(t - 1, 0), 0)

    def slab_spec(w):
        slab = _slab_rows(w.shape[0], n_tiles)
        last = w.shape[0] // slab - 1
        return pl.BlockSpec((slab, w.shape[1]), lambda t: (jnp.minimum(t, last), 0))

    cast_specs = [slab_spec(w) for w in cast_weights]
    return pl.pallas_call(
        functools.partial(_ffn_proj_pipelined_kernel, n_cast=len(cast_weights)),
        grid=(n_tiles + 1,),
        in_specs=[
            pl.BlockSpec((tm, D_MODEL), in_tile),
            _const_spec(wg.shape), _const_spec(wu.shape), _const_spec(wd.shape),
            _const_spec(lng.shape), _const_spec(lnb.shape), _const_spec(w_in.shape),
        ] + cast_specs,
        out_specs=[pl.BlockSpec((tm, D_MODEL), out_tile), pl.BlockSpec((tm, IN_WIDTH), out_tile)]
        + cast_specs,
        out_shape=[jax.ShapeDtypeStruct((n, D_MODEL), F32), jax.ShapeDtypeStruct((n, IN_WIDTH), F32)]
        + [jax.ShapeDtypeStruct(w.shape, BF16) for w in cast_weights],
        scratch_shapes=[
            pltpu.VMEM((tm, D_MODEL), BF16),
            pltpu.VMEM((tm, D_MODEL), F32),
            pltpu.VMEM((tm, D_MODEL), BF16),
            pltpu.VMEM((tm, D_FF), BF16),
        ],
        compiler_params=pltpu.CompilerParams(
            dimension_semantics=("arbitrary",), vmem_limit_bytes=VMEM_LIMIT_BYTES),
        name="ffn_proj_pipelined",
    )(x, wg, wu, wd, lng, lnb, w_in, *cast_weights)


def _rope_tables(positions):
    half = HEAD_DIM // 2
    inv_freq = ROPE_THETA ** (-np.arange(0, HEAD_DIM, 2, dtype=np.float64) / HEAD_DIM)
    ang = np.asarray(positions, np.float64)[:, None] * inv_freq[None, :]
    cos, sin = np.cos(ang), np.sin(ang)
    assert cos.shape[1] == half
    return (np.concatenate([cos, cos], axis=1).astype(np.float32),
            np.concatenate([-sin, sin], axis=1).astype(np.float32))


def _decay_tables(chunk, seq_len):
    r = np.arange(chunk)
    seq, idx = r // seq_len, (r % seq_len).astype(np.float64)
    same = seq[:, None] == seq[None, :]
    diff = idx[:, None] - idx[None, :]
    mask, qdec, kdec = [], [], []
    for gamma in GAMMAS:
        lg = math.log(gamma)
        mask.append(np.where(same & (diff >= 0), np.exp(lg * np.maximum(diff, 0.0)), 0.0) * QK_SCALE)
        qdec.append(np.broadcast_to((np.exp(lg * (idx + 1.0)) * QK_SCALE)[:, None], (chunk, HEAD_DIM)))
        kdec.append(np.broadcast_to(np.exp(lg * (seq_len - 1.0 - idx))[:, None], (chunk, HEAD_DIM)))
    to32 = lambda t: np.stack(t).astype(np.float32)
    return to32(mask), to32(qdec), to32(kdec)


def _rope(x, cos, sin):
    return x * cos + pltpu.roll(x, HEAD_DIM // 2, 1) * sin


def _group_norm(o):
    mu = jnp.mean(o, axis=-1, keepdims=True)
    oc = o - mu
    var = jnp.mean(oc * oc, axis=-1, keepdims=True)
    return oc * lax.rsqrt(var + GN_EPS)


def _prompt_mixer_ffn_kernel(q_ref, k_ref, v_ref, g_ref, p_ref, x1_ref, cos_ref, sin_ref,
                             km_ref, vm_ref, pm_ref, cosm_ref, sinm_ref, kdecm_ref,
                             mask_ref, qdec_ref, kdec_ref, poolw_ref, pscale_ref, wout_ref,
                             ln2g_ref, ln2b_ref, wg_ref, wu_ref, wd_ref, ln3g_ref, ln3b_ref,
                             y_ref, sret_ref, spool_ref,
                             s_ref, xp_ref, mix_ref, x2_ref, xb_ref, ypre_ref, h_ref, kb_ref, vs_ref,
                             *, steps_per_seq):
    t = pl.program_id(0)
    n_tiles = pl.num_programs(0) - 2
    c = t % steps_per_seq
    tile = q_ref.shape[0]
    hist = N_META
    row_blocks = [slice(r, r + RET_CHUNK) for r in range(0, tile, RET_CHUNK)]

    @pl.when(jnp.logical_and(c == 0, t < n_tiles))
    def _init_from_meta():
        for h in range(RET_HEADS):
            hs = slice(h * HEAD_DIM, (h + 1) * HEAD_DIM)
            kr = _rope(km_ref[:, hs], cosm_ref[...], sinm_ref[...])
            kd = (kr * kdecm_ref[h]).astype(BF16)
            s_ref[h] = _dot_tn(kd, vm_ref[:, hs].astype(BF16))
        xp_ref[0:hist, :] = pm_ref[...]

    def ln3_fetch_piece(rows):
        def run():
            y_ref[rows, :] = ypre_ref[rows, :]
            yield
        return run

    def ln3_piece(rows):
        def run():
            y_ref[rows, :] = _layer_norm(y_ref[rows, :], ln3g_ref[...], ln3b_ref[...])
            yield
        return run

    state = [None] * RET_HEADS

    def retention_piece(ci, hp):
        def run():
            rows = slice(ci * RET_CHUNK, (ci + 1) * RET_CHUNK)
            pair = slice(hp * PAIR_WIDTH, (hp + 1) * PAIR_WIDTH)
            buf = ci * 2 + hp
            cos, sin = cos_ref[rows, :], sin_ref[rows, :]
            qr, kr = [], []
            for j in range(2):
                h = 2 * hp + j
                hs = slice(h * HEAD_DIM, (h + 1) * HEAD_DIM)
                blk = slice(j * HEAD_DIM, (j + 1) * HEAD_DIM)
                qr.append(_rope(q_ref[rows, hs], cos, sin))
                kr.append(_rope(k_ref[rows, hs], cos, sin))
                kb_ref[buf, blk, blk] = kr[j].astype(BF16)
                vs_ref[buf, blk, blk] = v_ref[rows, hs].astype(BF16)
                vs_ref[buf, PAIR_WIDTH + j * HEAD_DIM:PAIR_WIDTH + (j + 1) * HEAD_DIM, blk] = (
                    state[h].astype(BF16))
            q2 = jnp.concatenate(qr, axis=1)
            k2 = jnp.concatenate(kr, axis=1)
            scores = _dot_nt(q2.astype(BF16), kb_ref[buf])
            kd2 = (k2 * kdec_ref[hp]).astype(BF16)
            upd = _dot_tn(kd2, v_ref[rows, pair].astype(BF16))
            qd2 = (q2 * qdec_ref[hp]).astype(BF16)
            yield
            lhs = jnp.concatenate([(scores * mask_ref[hp]).astype(BF16), qd2], axis=1)
            o2 = _dot(lhs, vs_ref[buf])
            for j in range(2):
                blk = slice(j * HEAD_DIM, (j + 1) * HEAD_DIM)
                state[2 * hp + j] = (GAMMAS[2 * hp + j] ** RET_CHUNK) * state[2 * hp + j] + upd[blk, blk]
            yield
            for j in range(2):
                hs = slice((2 * hp + j) * HEAD_DIM, (2 * hp + j + 1) * HEAD_DIM)
                blk = slice(j * HEAD_DIM, (j + 1) * HEAD_DIM)
                mix_ref[rows, hs] = (_silu(g_ref[rows, hs]) * _group_norm(o2[:, blk])).astype(BF16)
        return run

    def pool_piece(pp):
        def run():
            d2 = []
            for j in range(2):
                gi = 2 * pp + j
                w = POOL_WINDOWS[gi]
                gs = slice(gi * POOL_GROUP, (gi + 1) * POOL_GROUP)
                xp_ref[hist:hist + tile, gs] = p_ref[:, gs]
                rows_all = xp_ref[:, gs]
                wsum, shift = rows_all, 1
                while shift < w:
                    wsum = wsum + pltpu.roll(wsum, shift, 0)
                    shift *= 2
                d2.append((wsum[hist:] * (1.0 / w) - rows_all[hist:]).astype(BF16))
                xp_ref[0:hist, gs] = xp_ref[tile:tile + hist, gs]
                if j == 0:
                    yield
            pair = slice(pp * PAIR_WIDTH, (pp + 1) * PAIR_WIDTH)
            pooled = _dot(jnp.concatenate(d2, axis=1), poolw_ref[pp])
            yield
            pooled = pooled * pscale_ref[:, pair]
            mix_ref[:, RET_WIDTH + pp * PAIR_WIDTH:RET_WIDTH + (pp + 1) * PAIR_WIDTH] = pooled.astype(BF16)
        return run

    def _run_step(with_a, with_b, with_c):
        starts, tail_pieces = {}, []

        def start_at(i, piece):
            starts.setdefault(i, []).append(piece)

        if with_c:
            for i, rows in enumerate(row_blocks):
                start_at(0, ln3_fetch_piece(rows))
        if with_a:
            for h in range(RET_HEADS):
                state[h] = s_ref[h]
            n_ret = 0
            for ci in range(len(row_blocks)):
                for hp in range(RET_HEADS // 2):
                    start_at(n_ret, retention_piece(ci, hp))
                    n_ret += 1
            for pp in range(len(POOL_WINDOWS) // 2):
                start_at(2 * pp, pool_piece(pp))

        if with_b:
            def gate_up_piece(ck):
                def run():
                    sl = slice(ck * FFN_COL_CHUNK, (ck + 1) * FFN_COL_CHUNK)
                    g = _dot(xb_ref[...], wg_ref[:, sl])
                    u = _dot(xb_ref[...], wu_ref[:, sl])
                    h_ref[:, sl] = (_silu(g) * u).astype(BF16)
                return run

            def residual_piece():
                ypre_ref[...] = ALPHA * x2_ref[...]
                yield

            start_at(1, residual_piece)
            if with_c:
                n_slots = D_FF // FFN_COL_CHUNK
                for i, rows in enumerate(row_blocks):
                    start_at(n_slots - len(row_blocks) + i, ln3_piece(rows))
            _interleave([gate_up_piece(ck) for ck in range(D_FF // FFN_COL_CHUNK)], starts)
        else:
            _run_all(starts)

        tail_starts = {}
        if with_a:
            for h in range(RET_HEADS):
                s_ref[h] = state[h]
            x2_ref[...] = ALPHA * x1_ref[...] + _dot(mix_ref[...], wout_ref[...])

            def ln2_piece(rows):
                def run():
                    x2_rows = _layer_norm(x2_ref[rows, :], ln2g_ref[...], ln2b_ref[...])
                    x2_ref[rows, :] = x2_rows
                    xb_ref[rows, :] = x2_rows.astype(BF16)
                    yield
                return run

            tail_starts = {i: [ln2_piece(rows)] for i, rows in enumerate(row_blocks)}
        if with_c and not with_b:
            for i, rows in enumerate(row_blocks):
                tail_starts.setdefault(i, []).append(ln3_piece(rows))

        if with_b:
            def down_piece(nk):
                def run():
                    cs = slice(nk * FFN_COL_CHUNK, (nk + 1) * FFN_COL_CHUNK)
                    ypre_ref[:, cs] = ypre_ref[:, cs] + 0.5 * _dot(h_ref[...], wd_ref[:, cs])
                return run

            _interleave([down_piece(nk) for nk in range(D_MODEL // FFN_COL_CHUNK)], tail_starts)
        else:
            _run_all(tail_starts)

    @pl.when(t == 0)
    def _fill_first():
        kb_ref[...] = jnp.zeros(kb_ref.shape, BF16)
        vs_ref[...] = jnp.zeros(vs_ref.shape, BF16)
        ypre_ref[...] = jnp.zeros(ypre_ref.shape, F32)
        _run_step(True, False, False)

    @pl.when(jnp.logical_and(t > 0, t <= n_tiles))
    def _steady():
        _run_step(True, True, True)

    @pl.when(t == n_tiles + 1)
    def _drain_last():
        _run_step(False, False, True)

    @pl.when(jnp.logical_and(c == steps_per_seq - 1, t < n_tiles))
    def _emit_state():
        sret_ref[0] = s_ref[...]
        spool_ref[0] = xp_ref[hist - POOL_BUF:hist, :]


def _prompt_mixer_ffn(proj, x1, proj_small, meta_row_block, pool_w, pool_scale, w_out, ln2g, ln2b,
                      wg, wu, wd, ln3g, ln3b, batch, seq):
    tile = MIX_TOKEN_TILE
    steps = seq // tile
    n_tiles = batch * steps
    cos, sin = _rope_tables(N_META + np.arange(seq))
    cosm, sinm = _rope_tables(np.arange(N_META))
    pair_up = lambda tab: np.concatenate([tab[0::2], tab[1::2]], axis=2)
    mask, qdec, kdec = (pair_up(tab) for tab in _decay_tables(RET_CHUNK, RET_CHUNK))
    _, _, kdecm = _decay_tables(N_META, N_META)
    zero_blk = jnp.zeros_like(pool_w[0])
    pool_w = jnp.stack([jnp.block([[pool_w[2 * pp], zero_blk], [zero_blk, pool_w[2 * pp + 1]]])
                        for pp in range(len(POOL_WINDOWS) // 2)])

    mix_tile = lambda t: jnp.minimum(t, n_tiles - 1)
    ffn_tile = lambda t: jnp.maximum(t - 2, 0)

    def col(j):
        return pl.BlockSpec((tile, RET_WIDTH), lambda t: (mix_tile(t), j))

    def meta_col(j):
        return pl.BlockSpec((N_META, RET_WIDTH), lambda t: (meta_row_block, j))

    in_specs = [
        col(0), col(1), col(2), col(3), col(4),
        pl.BlockSpec((tile, D_MODEL), lambda t: (mix_tile(t), 0)),
        pl.BlockSpec((tile, HEAD_DIM), lambda t: (mix_tile(t) % steps, 0)),
        pl.BlockSpec((tile, HEAD_DIM), lambda t: (mix_tile(t) % steps, 0)),
        meta_col(1), meta_col(2), meta_col(4),
        _const_spec(cosm.shape), _const_spec(sinm.shape), _const_spec(kdecm.shape),
        _const_spec(mask.shape), _const_spec(qdec.shape), _const_spec(kdec.shape),
        _const_spec(pool_w.shape), _const_spec(pool_scale.shape), _const_spec(w_out.shape),
        _const_spec(ln2g.shape), _const_spec(ln2b.shape),
        _const_spec(wg.shape), _const_spec(wu.shape), _const_spec(wd.shape),
        _const_spec(ln3g.shape), _const_spec(ln3b.shape),
    ]
    out_shape = [
        jax.ShapeDtypeStruct((batch * seq, D_MODEL), F32),
        jax.ShapeDtypeStruct((batch, RET_HEADS, HEAD_DIM, HEAD_DIM), F32),
        jax.ShapeDtypeStruct((batch, POOL_BUF, POOL_WIDTH), F32),
    ]
    out_specs = [
        pl.BlockSpec((tile, D_MODEL), lambda t: (ffn_tile(t), 0)),
        pl.BlockSpec((1, RET_HEADS, HEAD_DIM, HEAD_DIM), lambda t: (mix_tile(t) // steps, 0, 0, 0)),
        pl.BlockSpec((1, POOL_BUF, POOL_WIDTH), lambda t: (mix_tile(t) // steps, 0, 0)),
    ]
    return pl.pallas_call(
        functools.partial(_prompt_mixer_ffn_kernel, steps_per_seq=steps),
        grid=(n_tiles + 2,),
        in_specs=in_specs,
        out_specs=out_specs,
        out_shape=out_shape,
        scratch_shapes=[
            pltpu.VMEM((RET_HEADS, HEAD_DIM, HEAD_DIM), F32),
            pltpu.VMEM((N_META + tile, POOL_WIDTH), F32),
            pltpu.VMEM((tile, D_MODEL), BF16),
            pltpu.VMEM((tile, D_MODEL), F32),
            pltpu.VMEM((tile, D_MODEL), BF16),
            pltpu.VMEM((tile, D_MODEL), F32),
            pltpu.VMEM((tile, D_FF), BF16),
            pltpu.VMEM((8, PAIR_WIDTH, PAIR_WIDTH), BF16),
            pltpu.VMEM((8, 2 * PAIR_WIDTH, PAIR_WIDTH), BF16),
        ],
        compiler_params=pltpu.CompilerParams(
            dimension_semantics=("arbitrary",), vmem_limit_bytes=VMEM_LIMIT_BYTES),
        name="prompt_mixer_ffn",
    )(proj, proj, proj, proj, proj, x1, cos, sin, proj_small, proj_small, proj_small,
      cosm, sinm, kdecm, mask, qdec, kdec, pool_w, pool_scale, w_out, ln2g, ln2b,
      wg, wu, wd, ln3g, ln3b)


def _decode_mixer_kernel(q_ref, k_ref, v_ref, g_ref, p_ref, x1_ref, s0_ref, pref_ref,
                         cos_ref, sin_ref, mask_ref, qdec_ref, kdec_ref,
                         poolw_ref, pscale_ref, wout_ref, lng_ref, lnb_ref,
                         o_ref, sret_ref, spool_ref,
                         xs_ref, d_ref, mix_ref, *, dec_seq):
    rows = q_ref.shape[0]
    nseq = rows // dec_seq
    seq_per_group = BF16_ROWS // dec_seq
    cos, sin = cos_ref[...], sin_ref[...]
    row_seq = lax.broadcasted_iota(jnp.int32, (BF16_ROWS, HEAD_DIM), 0) // dec_seq

    for h in range(RET_HEADS):
        hs = slice(h * HEAD_DIM, (h + 1) * HEAD_DIM)
        qr = _rope(q_ref[:, hs], cos, sin)
        kr = _rope(k_ref[:, hs], cos, sin)
        v = v_ref[:, hs]
        vb = v.astype(BF16)
        scores = _dot_nt(qr.astype(BF16), kr.astype(BF16)) * mask_ref[h]
        o_inner = _dot(scores.astype(BF16), vb)
        qd = qr * qdec_ref[h]
        kd = kr * kdec_ref[h]
        o_cross = []
        for grp in range(rows // BF16_ROWS):
            gr = slice(grp * BF16_ROWS, (grp + 1) * BF16_ROWS)
            qd_g = qd[gr].astype(BF16)
            kd_g = kd[gr].astype(BF16)
            v_g = v[gr]
            acc = jnp.zeros((BF16_ROWS, HEAD_DIM), F32)
            for j in range(seq_per_group):
                b = grp * seq_per_group + j
                s = s0_ref[b, h]
                acc = jnp.where(row_seq == j, _dot(qd_g, s.astype(BF16)), acc)
                v_b = jnp.where(row_seq == j, v_g, 0.0).astype(BF16)
                sret_ref[b, h] = (GAMMAS[h] ** dec_seq) * s + _dot_tn(kd_g, v_b)
            o_cross.append(acc)
        o = o_inner + jnp.concatenate(o_cross, axis=0)
        mix_ref[:, hs] = (_silu(g_ref[:, hs]) * _group_norm(o)).astype(BF16)

    first = BF16_ROWS
    for b in range(nseq):
        xs_ref[b, first - POOL_BUF:first, :] = pref_ref[b]
        xs_ref[b, first:first + dec_seq, :] = p_ref[b * dec_seq:(b + 1) * dec_seq, :]
    for b in range(nseq):
        for gi, w in enumerate(POOL_WINDOWS):
            gs = slice(gi * POOL_GROUP, (gi + 1) * POOL_GROUP)
            cur = xs_ref[b, first:first + dec_seq, gs]
            wsum = cur
            for t in range(1, w):
                wsum = wsum + xs_ref[b, first - t:first - t + dec_seq, gs]
            d_ref[b * dec_seq:(b + 1) * dec_seq, gs] = wsum * (1.0 / w) - cur
        spool_ref[b] = xs_ref[b, first + dec_seq - POOL_BUF:first + dec_seq, :]
    for gi in range(len(POOL_WINDOWS)):
        gs = slice(gi * POOL_GROUP, (gi + 1) * POOL_GROUP)
        pooled = _dot(d_ref[:, gs].astype(BF16), poolw_ref[gi]) * pscale_ref[:, gs]
        mix_ref[:, RET_WIDTH + gi * POOL_GROUP:RET_WIDTH + (gi + 1) * POOL_GROUP] = pooled.astype(BF16)

    y = _dot(mix_ref[...], wout_ref[...])
    o_ref[...] = _layer_norm(ALPHA * x1_ref[...] + y, lng_ref[...], lnb_ref[...])


def _decode_mixer(proj, x1, state_ret, state_pool, pool_w, pool_scale, w_out, lng, lnb, nseq, dec_seq):
    assert BF16_ROWS % dec_seq == 0 and dec_seq <= POOL_BUF
    rows = DEC_SEQ_BLOCK * dec_seq
    steps = nseq // DEC_SEQ_BLOCK
    cos, sin = _rope_tables(PAST_LEN + (np.arange(rows) % dec_seq))
    mask, qdec, kdec = _decay_tables(rows, dec_seq)

    def col(j):
        return pl.BlockSpec((rows, RET_WIDTH), lambda i: (i, j))

    state_spec = pl.BlockSpec((DEC_SEQ_BLOCK, RET_HEADS, HEAD_DIM, HEAD_DIM), lambda i: (i, 0, 0, 0))
    pool_spec = pl.BlockSpec((DEC_SEQ_BLOCK, POOL_BUF, POOL_WIDTH), lambda i: (i, 0, 0))
    in_specs = [
        col(0), col(1), col(2), col(3), col(4),
        pl.BlockSpec((rows, D_MODEL), lambda i: (i, 0)),
        state_spec, pool_spec,
        _const_spec(cos.shape), _const_spec(sin.shape),
        _const_spec(mask.shape), _const_spec(qdec.shape), _const_spec(kdec.shape),
        _const_spec(pool_w.shape), _const_spec(pool_scale.shape), _const_spec(w_out.shape),
        _const_spec(lng.shape), _const_spec(lnb.shape),
    ]
    out_shape = [
        jax.ShapeDtypeStruct((nseq * dec_seq, D_MODEL), F32),
        jax.ShapeDtypeStruct(state_ret.shape, F32),
        jax.ShapeDtypeStruct(state_pool.shape, F32),
    ]
    out_specs = [pl.BlockSpec((rows, D_MODEL), lambda i: (i, 0)), state_spec, pool_spec]
    return pl.pallas_call(
        functools.partial(_decode_mixer_kernel, dec_seq=dec_seq),
        grid=(steps,),
        in_specs=in_specs,
        out_specs=out_specs,
        out_shape=out_shape,
        scratch_shapes=[
            pltpu.VMEM((DEC_SEQ_BLOCK, BF16_ROWS + 8, POOL_WIDTH), F32),
            pltpu.VMEM((rows, POOL_WIDTH), F32),
            pltpu.VMEM((rows, D_MODEL), BF16),
        ],
        compiler_params=pltpu.CompilerParams(
            dimension_semantics=("arbitrary",), vmem_limit_bytes=VMEM_LIMIT_BYTES),
        name="decode_mixer",
    )(proj, proj, proj, proj, proj, x1, state_ret, state_pool, cos, sin, mask, qdec, kdec,
      pool_w, pool_scale, w_out, lng, lnb)


def kernel(x_prompt, x_sample, state_ret, state_pool, meta_tokens, ffn1_w_gate, ffn1_w_up, ffn1_w_down,
           ln1_g, ln1_b, w_in, pool_w, pool_scale, w_out, ln2_g, ln2_b, ffn2_w_gate, ffn2_w_up,
           ffn2_w_down, ln3_g, ln3_b):
    assert ffn1_w_gate.shape[0] == DEPTH == 1
    batch, seq, _ = x_prompt.shape
    nseq, dec_seq, _ = x_sample.shape
    n_dec = nseq * dec_seq
    assert n_dec % N_META == 0

    bf = lambda w: w[0].astype(BF16)
    row = lambda v: v[0].reshape(1, -1)
    f1 = (bf(ffn1_w_gate), bf(ffn1_w_up), bf(ffn1_w_down), row(ln1_g), row(ln1_b))
    w_in_b, pool_w_b = bf(w_in), bf(pool_w)
    pscale, g2, b2 = row(pool_scale), row(ln2_g), row(ln2_b)

    xp = x_prompt.reshape(batch * seq, D_MODEL)
    x_small = jnp.concatenate([x_sample.reshape(n_dec, D_MODEL), meta_tokens.astype(x_prompt.dtype)], axis=0)

    x1p, projp, wg2, wu2, wd2, w_out_b = _ffn_proj_pipelined(
        xp, *f1, w_in_b, (ffn2_w_gate[0], ffn2_w_up[0], ffn2_w_down[0], w_out[0]))
    f2 = (wg2, wu2, wd2, row(ln3_g), row(ln3_b))
    x1s, projs = _ffn(x_small, *f1, w_in=w_in_b)

    y_prompt, ret_p, pool_p = _prompt_mixer_ffn(projp, x1p, projs, n_dec // N_META, pool_w_b, pscale,
                                                w_out_b, g2, b2, *f2, batch, seq)
    x2s, ret_s, pool_s = _decode_mixer(projs, x1s, state_ret[0], state_pool[0], pool_w_b, pscale,
                                       w_out_b, g2, b2, nseq, dec_seq)
    (y_sample,) = _ffn(x2s, *f2)
    return (y_prompt.reshape(batch, seq, D_MODEL), y_sample.reshape(nseq, dec_seq, D_MODEL),
            ret_p[None], pool_p[None], ret_s[None], pool_s[None])
```

```python
import functools
import math

import jax
import jax.numpy as jnp
import numpy as np
from jax import lax
from jax.experimental import pallas as pl
from jax.experimental.pallas import tpu as pltpu

F32 = jnp.float32
BF16 = jnp.bfloat16

D_MODEL = 1024
D_FF = 2816
N_META = 16
PAST_LEN = 16384
RET_HEADS = 4
HEAD_DIM = 128
RET_WIDTH = RET_HEADS * HEAD_DIM
RET_CHUNK = 128
ROPE_THETA = 10000.0
POOL_WINDOWS = (2, 4, 8, 16)
POOL_GROUP = 128
POOL_WIDTH = POOL_GROUP * len(POOL_WINDOWS)
POOL_BUF = max(POOL_WINDOWS) - 1
IN_WIDTH = 4 * RET_WIDTH + POOL_WIDTH
DEPTH = 1
ALPHA = (2.0 * DEPTH) ** 0.25
LN_EPS = 1e-5
GN_EPS = 1e-5
QK_SCALE = HEAD_DIM ** -0.5
GAMMAS = tuple(1.0 - 2.0 ** (-5.0 - h) for h in range(RET_HEADS))

VMEM_LIMIT_BYTES = 56 * 1024 * 1024
FFN_TOKEN_TILE = 512
FFN_COL_CHUNK = 256
MIX_TOKEN_TILE = 512
DEC_SEQ_BLOCK = 16
BF16_ROWS = 16
PAIR_WIDTH = 2 * HEAD_DIM


def _layer_norm(z, g, b):
    mu = jnp.mean(z, axis=-1, keepdims=True)
    zc = z - mu
    var = jnp.mean(zc * zc, axis=-1, keepdims=True)
    return zc * lax.rsqrt(var + LN_EPS) * g + b


def _silu(x):
    return x * jax.nn.sigmoid(x)


def _dot(a, b):
    return jnp.dot(a, b, preferred_element_type=F32)


def _dot_nt(a, b):
    return lax.dot_general(a, b, (((1,), (1,)), ((), ())), preferred_element_type=F32)


def _dot_tn(a, b):
    return lax.dot_general(a, b, (((0,), (0,)), ((), ())), preferred_element_type=F32)


def _ffn_body(x, wg_ref, wu_ref, wd_ref, lng_ref, lnb_ref, h_ref):
    xb = x.astype(BF16)
    for c in range(D_FF // FFN_COL_CHUNK):
        sl = slice(c * FFN_COL_CHUNK, (c + 1) * FFN_COL_CHUNK)
        g = _dot(xb, wg_ref[:, sl])
        u = _dot(xb, wu_ref[:, sl])
        h_ref[:, sl] = (_silu(g) * u).astype(BF16)
    y = _dot(h_ref[...], wd_ref[...])
    return _layer_norm(ALPHA * x + 0.5 * y, lng_ref[...], lnb_ref[...])


def _ffn_kernel(x_ref, wg_ref, wu_ref, wd_ref, lng_ref, lnb_ref, *rest, with_proj):
    if with_proj:
        win_ref, o_ref, proj_ref, h_ref = rest
    else:
        o_ref, h_ref = rest
    out = _ffn_body(x_ref[...], wg_ref, wu_ref, wd_ref, lng_ref, lnb_ref, h_ref)
    o_ref[...] = out
    if with_proj:
        proj_ref[...] = _dot(out.astype(BF16), win_ref[...])


def _const_spec(shape):
    zeros = (0,) * len(shape)
    return pl.BlockSpec(shape, lambda *_: zeros, pipeline_mode=pl.Buffered(1))


def _ffn(x, wg, wu, wd, lng, lnb, w_in=None):
    n = x.shape[0]
    tm = FFN_TOKEN_TILE if n % FFN_TOKEN_TILE == 0 else n
    with_proj = w_in is not None
    row = lambda i: (i, 0)
    in_specs = [
        pl.BlockSpec((tm, D_MODEL), row),
        _const_spec((D_MODEL, D_FF)),
        _const_spec((D_MODEL, D_FF)),
        _const_spec((D_FF, D_MODEL)),
        _const_spec((1, D_MODEL)),
        _const_spec((1, D_MODEL)),
    ]
    args = [x, wg, wu, wd, lng, lnb]
    out_shape = [jax.ShapeDtypeStruct((n, D_MODEL), F32)]
    out_specs = [pl.BlockSpec((tm, D_MODEL), row)]
    if with_proj:
        in_specs.append(_const_spec((D_MODEL, IN_WIDTH)))
        args.append(w_in)
        out_shape.append(jax.ShapeDtypeStruct((n, IN_WIDTH), F32))
        out_specs.append(pl.BlockSpec((tm, IN_WIDTH), row))
    return pl.pallas_call(
        functools.partial(_ffn_kernel, with_proj=with_proj),
        grid=(n // tm,),
        in_specs=in_specs,
        out_specs=out_specs,
        out_shape=out_shape,
        scratch_shapes=[pltpu.VMEM((tm, D_FF), BF16)],
        compiler_params=pltpu.CompilerParams(
            dimension_semantics=("arbitrary",), vmem_limit_bytes=VMEM_LIMIT_BYTES),
        name="ffn_proj" if with_proj else "ffn",
    )(*args)


def _ffn_proj_cast_kernel(x_ref, wg_ref, wu_ref, wd_ref, lng_ref, lnb_ref, win_ref,
                          x1_ref, proj_ref, wgb_ref, wub_ref, wdb_ref, winb_ref, acc_ref, xb_ref):
    k = pl.program_id(0)

    @pl.when(k == 0)
    def _start():
        xb_ref[...] = x_ref[...].astype(BF16)
        acc_ref[...] = jnp.zeros(acc_ref.shape, F32)

    wg, wu, wd = wg_ref[...].astype(BF16), wu_ref[...].astype(BF16), wd_ref[...].astype(BF16)
    wgb_ref[...] = wg
    wub_ref[...] = wu
    wdb_ref[...] = wd
    h = (_silu(_dot(xb_ref[...], wg)) * _dot(xb_ref[...], wu)).astype(BF16)
    acc_ref[...] += _dot(h, wd)

    @pl.when(k == pl.num_programs(0) - 1)
    def _finish():
        x1 = _layer_norm(ALPHA * x_ref[...] + 0.5 * acc_ref[...], lng_ref[...], lnb_ref[...])
        x1_ref[...] = x1
        w_in = win_ref[...].astype(BF16)
        winb_ref[...] = w_in
        proj_ref[...] = _dot(x1.astype(BF16), w_in)


def _ffn_proj_cast(x, wg, wu, wd, lng, lnb, w_in):
    n = x.shape[0]
    ck = FFN_COL_CHUNK
    col_chunk = pl.BlockSpec((D_MODEL, ck), lambda k: (0, k))
    row_chunk = pl.BlockSpec((ck, D_MODEL), lambda k: (k, 0))
    whole = lambda shape: pl.BlockSpec(shape, lambda k: (0,) * len(shape))
    return pl.pallas_call(
        _ffn_proj_cast_kernel,
        grid=(D_FF // ck,),
        in_specs=[_const_spec(x.shape), col_chunk, col_chunk, row_chunk,
                  _const_spec(lng.shape), _const_spec(lnb.shape), _const_spec(w_in.shape)],
        out_specs=[whole((n, D_MODEL)), whole((n, IN_WIDTH)), col_chunk, col_chunk, row_chunk,
                   whole(w_in.shape)],
        out_shape=[jax.ShapeDtypeStruct((n, D_MODEL), F32), jax.ShapeDtypeStruct((n, IN_WIDTH), F32),
                   jax.ShapeDtypeStruct(wg.shape, BF16), jax.ShapeDtypeStruct(wu.shape, BF16),
                   jax.ShapeDtypeStruct(wd.shape, BF16), jax.ShapeDtypeStruct(w_in.shape, BF16)],
        scratch_shapes=[pltpu.VMEM((n, D_MODEL), F32), pltpu.VMEM((n, D_MODEL), BF16)],
        compiler_params=pltpu.CompilerParams(
            dimension_semantics=("arbitrary",), vmem_limit_bytes=VMEM_LIMIT_BYTES),
        name="ffn_proj_cast",
    )(x, wg, wu, wd, lng, lnb, w_in)


def _interleave(major, starts):
    live = []
    for i, piece in enumerate(major):
        piece()
        live += [make() for make in starts.get(i, [])]
        live = [g for g in live if next(g, "done") != "done"]
    while live:
        live = [g for g in live if next(g, "done") != "done"]


def _run_all(starts):
    for i in sorted(starts):
        for make in starts[i]:
            for _ in make():
                pass


def _ffn_proj_pipelined_kernel(x_ref, wg_ref, wu_ref, wd_ref, lng_ref, lnb_ref, win_ref, *rest, n_cast):
    cast_in, rest = rest[:n_cast], rest[n_cast:]
    x1_ref, proj_ref = rest[:2]
    cast_out, (xb_ref, z1_ref, x1b_ref, h_ref) = rest[2:2 + n_cast], rest[2 + n_cast:]
    t = pl.program_id(0)
    n_tiles = pl.num_programs(0) - 1
    tile = x_ref.shape[0]
    row_blocks = [slice(r, r + RET_CHUNK) for r in range(0, tile, RET_CHUNK)]
    proj_chunk = 2 * FFN_COL_CHUNK

    def ln1_piece(rows):
        def run():
            x1_rows = _layer_norm(z1_ref[rows, :], lng_ref[...], lnb_ref[...])
            x1_ref[rows, :] = x1_rows
            x1b_ref[rows, :] = x1_rows.astype(BF16)
            yield
        return run

    def proj_piece(nk):
        def run():
            cs = slice(nk * proj_chunk, (nk + 1) * proj_chunk)
            proj_ref[:, cs] = _dot(x1b_ref[...], win_ref[:, cs])
        return run

    def gate_up_piece(ck):
        def run():
            sl = slice(ck * FFN_COL_CHUNK, (ck + 1) * FFN_COL_CHUNK)
            g = _dot(xb_ref[...], wg_ref[:, sl])
            u = _dot(xb_ref[...], wu_ref[:, sl])
            h_ref[:, sl] = (_silu(g) * u).astype(BF16)
        return run

    def down_piece(nk):
        def run():
            cs = slice(nk * FFN_COL_CHUNK, (nk + 1) * FFN_COL_CHUNK)
            z1_ref[:, cs] = ALPHA * x_ref[:, cs] + 0.5 * _dot(h_ref[...], wd_ref[:, cs])
        return run

    ln1_starts = {i: [ln1_piece(rows)] for i, rows in enumerate(row_blocks)}
    proj_pieces = [proj_piece(nk) for nk in range(IN_WIDTH // proj_chunk)]

    @pl.when(t == 0)
    def _clear_pipeline():
        z1_ref[...] = jnp.zeros(z1_ref.shape, F32)

    @pl.when(t < n_tiles)
    def _steady():
        xb_ref[...] = x_ref[...].astype(BF16)
        for src_ref, dst_ref in zip(cast_in, cast_out):
            dst_ref[...] = src_ref[...].astype(BF16)
        gate_up = [gate_up_piece(ck) for ck in range(D_FF // FFN_COL_CHUNK)]
        first_proj = len(row_blocks) + 1
        major = gate_up[:first_proj]
        for i, piece in enumerate(gate_up[first_proj:]):
            major += proj_pieces[i:i + 1] + [piece]
        major += proj_pieces[len(gate_up) - first_proj:]
        _interleave(major, ln1_starts)
        for nk in range(D_MODEL // FFN_COL_CHUNK):
            down_piece(nk)()

    @pl.when(t == n_tiles)
    def _drain():
        _run_all(ln1_starts)
        for piece in proj_pieces:
            piece()


def _slab_rows(rows, max_slabs):
    for slab in range(BF16_ROWS, rows + 1, BF16_ROWS):
        if rows % slab == 0 and rows // slab <= max_slabs:
            return slab
    raise ValueError(f"no slab size for {rows} rows in {max_slabs} steps")


def _ffn_proj_pipelined(x, wg, wu, wd, lng, lnb, w_in, cast_weights):
    n = x.shape[0]
    tm = FFN_TOKEN_TILE
    n_tiles = n // tm
    in_tile = lambda t: (jnp.minimum(t, n_tiles - 1), 0)
    out_tile = lambda t: (jnp.maximum(t - 1, 0), 0)

    def slab_spec(w):
        slab = _slab_rows(w.shape[0], n_tiles)
        last = w.shape[0] // slab - 1
        return pl.BlockSpec((slab, w.shape[1]), lambda t: (jnp.minimum(t, last), 0))

    cast_specs = [slab_spec(w) for w in cast_weights]
    return pl.pallas_call(
        functools.partial(_ffn_proj_pipelined_kernel, n_cast=len(cast_weights)),
        grid=(n_tiles + 1,),
        in_specs=[
            pl.BlockSpec((tm, D_MODEL), in_tile),
            _const_spec(wg.shape), _const_spec(wu.shape), _const_spec(wd.shape),
            _const_spec(lng.shape), _const_spec(lnb.shape), _const_spec(w_in.shape),
        ] + cast_specs,
        out_specs=[pl.BlockSpec((tm, D_MODEL), out_tile), pl.BlockSpec((tm, IN_WIDTH), out_tile)]
        + cast_specs,
        out_shape=[jax.ShapeDtypeStruct((n, D_MODEL), F32), jax.ShapeDtypeStruct((n, IN_WIDTH), F32)]
        + [jax.ShapeDtypeStruct(w.shape, BF16) for w in cast_weights],
        scratch_shapes=[
            pltpu.VMEM((tm, D_MODEL), BF16),
            pltpu.VMEM((tm, D_MODEL), F32),
            pltpu.VMEM((tm, D_MODEL), BF16),
            pltpu.VMEM((tm, D_FF), BF16),
        ],
        compiler_params=pltpu.CompilerParams(
            dimension_semantics=("arbitrary",), vmem_limit_bytes=VMEM_LIMIT_BYTES),
        name="ffn_proj_pipelined",
    )(x, wg, wu, wd, lng, lnb, w_in, *cast_weights)


def _rope_tables(positions):
    half = HEAD_DIM // 2
    inv_freq = ROPE_THETA ** (-np.arange(0, HEAD_DIM, 2, dtype=np.float64) / HEAD_DIM)
    ang = np.asarray(positions, np.float64)[:, None] * inv_freq[None, :]
    cos, sin = np.cos(ang), np.sin(ang)
    assert cos.shape[1] == half
    return (np.concatenate([cos, cos], axis=1).astype(np.float32),
            np.concatenate([-sin, sin], axis=1).astype(np.float32))


def _decay_tables(chunk, seq_len):
    r = np.arange(chunk)
    seq, idx = r // seq_len, (r % seq_len).astype(np.float64)
    same = seq[:, None] == seq[None, :]
    diff = idx[:, None] - idx[None, :]
    mask, qdec, kdec = [], [], []
    for gamma in GAMMAS:
        lg = math.log(gamma)
        mask.append(np.where(same & (diff >= 0), np.exp(lg * np.maximum(diff, 0.0)), 0.0) * QK_SCALE)
        qdec.append(np.broadcast_to((np.exp(lg * (idx + 1.0)) * QK_SCALE)[:, None], (chunk, HEAD_DIM)))
        kdec.append(np.broadcast_to(np.exp(lg * (seq_len - 1.0 - idx))[:, None], (chunk, HEAD_DIM)))
    to32 = lambda t: np.stack(t).astype(np.float32)
    return to32(mask), to32(qdec), to32(kdec)


def _rope(x, cos, sin):
    return x * cos + pltpu.roll(x, HEAD_DIM // 2, 1) * sin


def _group_norm(o):
    mu = jnp.mean(o, axis=-1, keepdims=True)
    oc = o - mu
    var = jnp.mean(oc * oc, axis=-1, keepdims=True)
    return oc * lax.rsqrt(var + GN_EPS)


def _prompt_mixer_ffn_kernel(q_ref, k_ref, v_ref, g_ref, p_ref, x1_ref, cos_ref, sin_ref,
                             km_ref, vm_ref, pm_ref, cosm_ref, sinm_ref, kdecm_ref,
                             mask_ref, qdec_ref, kdec_ref, poolw_ref, pscale_ref, wout_ref,
                             ln2g_ref, ln2b_ref, wg_ref, wu_ref, wd_ref, ln3g_ref, ln3b_ref,
                             y_ref, sret_ref, spool_ref,
                             s_ref, xp_ref, mix_ref, x2_ref, xb_ref, ypre_ref, h_ref, kb_ref, vs_ref,
                             *, steps_per_seq):
    t = pl.program_id(0)
    n_tiles = pl.num_programs(0) - 2
    c = t % steps_per_seq
    tile = q_ref.shape[0]
    hist = N_META
    row_blocks = [slice(r, r + RET_CHUNK) for r in range(0, tile, RET_CHUNK)]

    @pl.when(jnp.logical_and(c == 0, t < n_tiles))
    def _init_from_meta():
        for h in range(RET_HEADS):
            hs = slice(h * HEAD_DIM, (h + 1) * HEAD_DIM)
            kr = _rope(km_ref[:, hs], cosm_ref[...], sinm_ref[...])
            kd = (kr * kdecm_ref[h]).astype(BF16)
            s_ref[h] = _dot_tn(kd, vm_ref[:, hs].astype(BF16))
        xp_ref[0:hist, :] = pm_ref[...]

    def ln3_fetch_piece(rows):
        def run():
            y_ref[rows, :] = ypre_ref[rows, :]
            yield
        return run

    def ln3_piece(rows):
        def run():
            y_ref[rows, :] = _layer_norm(y_ref[rows, :], ln3g_ref[...], ln3b_ref[...])
            yield
        return run

    state = [None] * RET_HEADS

    def retention_piece(ci, hp):
        def run():
            rows = slice(ci * RET_CHUNK, (ci + 1) * RET_CHUNK)
            pair = slice(hp * PAIR_WIDTH, (hp + 1) * PAIR_WIDTH)
            buf = ci * 2 + hp
            cos, sin = cos_ref[rows, :], sin_ref[rows, :]
            qr, kr = [], []
            for j in range(2):
                h = 2 * hp + j
                hs = slice(h * HEAD_DIM, (h + 1) * HEAD_DIM)
                blk = slice(j * HEAD_DIM, (j + 1) * HEAD_DIM)
                qr.append(_rope(q_ref[rows, hs], cos, sin))
                kr.append(_rope(k_ref[rows, hs], cos, sin))
                kb_ref[buf, blk, blk] = kr[j].astype(BF16)
                vs_ref[buf, blk, blk] = v_ref[rows, hs].astype(BF16)
                vs_ref[buf, PAIR_WIDTH + j * HEAD_DIM:PAIR_WIDTH + (j + 1) * HEAD_DIM, blk] = (
                    state[h].astype(BF16))
            q2 = jnp.concatenate(qr, axis=1)
            k2 = jnp.concatenate(kr, axis=1)
            scores = _dot_nt(q2.astype(BF16), kb_ref[buf])
            kd2 = (k2 * kdec_ref[hp]).astype(BF16)
            upd = _dot_tn(kd2, v_ref[rows, pair].astype(BF16))
            qd2 = (q2 * qdec_ref[hp]).astype(BF16)
            yield
            lhs = jnp.concatenate([(scores * mask_ref[hp]).astype(BF16), qd2], axis=1)
            o2 = _dot(lhs, vs_ref[buf])
            for j in range(2):
                blk = slice(j * HEAD_DIM, (j + 1) * HEAD_DIM)
                state[2 * hp + j] = (GAMMAS[2 * hp + j] ** RET_CHUNK) * state[2 * hp + j] + upd[blk, blk]
            yield
            for j in range(2):
                hs = slice((2 * hp + j) * HEAD_DIM, (2 * hp + j + 1) * HEAD_DIM)
                blk = slice(j * HEAD_DIM, (j + 1) * HEAD_DIM)
                mix_ref[rows, hs] = (_silu(g_ref[rows, hs]) * _group_norm(o2[:, blk])).astype(BF16)
        return run

    def pool_piece(pp):
        def run():
            d2 = []
            for j in range(2):
                gi = 2 * pp + j
                w = POOL_WINDOWS[gi]
                gs = slice(gi * POOL_GROUP, (gi + 1) * POOL_GROUP)
                xp_ref[hist:hist + tile, gs] = p_ref[:, gs]
                rows_all = xp_ref[:, gs]
                wsum, shift = rows_all, 1
                while shift < w:
                    wsum = wsum + pltpu.roll(wsum, shift, 0)
                    shift *= 2
                d2.append((wsum[hist:] * (1.0 / w) - rows_all[hist:]).astype(BF16))
                xp_ref[0:hist, gs] = xp_ref[tile:tile + hist, gs]
                if j == 0:
                    yield
            pair = slice(pp * PAIR_WIDTH, (pp + 1) * PAIR_WIDTH)
            pooled = _dot(jnp.concatenate(d2, axis=1), poolw_ref[pp])
            yield
            pooled = pooled * pscale_ref[:, pair]
            mix_ref[:, RET_WIDTH + pp * PAIR_WIDTH:RET_WIDTH + (pp + 1) * PAIR_WIDTH] = pooled.astype(BF16)
        return run

    def _run_step(with_a, with_b, with_c):
        starts, tail_pieces = {}, []

        def start_at(i, piece):
            starts.setdefault(i, []).append(piece)

        if with_c:
            for i, rows in enumerate(row_blocks):
                start_at(0, ln3_fetch_piece(rows))
        if with_a:
            for h in range(RET_HEADS):
                state[h] = s_ref[h]
            n_ret = 0
            for ci in range(len(row_blocks)):
                for hp in range(RET_HEADS // 2):
                    start_at(n_ret, retention_piece(ci, hp))
                    n_ret += 1
            for pp in range(len(POOL_WINDOWS) // 2):
                start_at(2 * pp, pool_piece(pp))

        if with_b:
            def gate_up_piece(ck):
                def run():
                    sl = slice(ck * FFN_COL_CHUNK, (ck + 1) * FFN_COL_CHUNK)
                    g = _dot(xb_ref[...], wg_ref[:, sl])
                    u = _dot(xb_ref[...], wu_ref[:, sl])
                    h_ref[:, sl] = (_silu(g) * u).astype(BF16)
                return run

            def residual_piece():
                ypre_ref[...] = ALPHA * x2_ref[...]
                yield

            start_at(1, residual_piece)
            if with_c:
                n_slots = D_FF // FFN_COL_CHUNK
                for i, rows in enumerate(row_blocks):
                    start_at(n_slots - len(row_blocks) + i, ln3_piece(rows))
            _interleave([gate_up_piece(ck) for ck in range(D_FF // FFN_COL_CHUNK)], starts)
        else:
            _run_all(starts)

        tail_starts = {}
        if with_a:
            for h in range(RET_HEADS):
                s_ref[h] = state[h]
            x2_ref[...] = ALPHA * x1_ref[...] + _dot(mix_ref[...], wout_ref[...])

            def ln2_piece(rows):
                def run():
                    x2_rows = _layer_norm(x2_ref[rows, :], ln2g_ref[...], ln2b_ref[...])
                    x2_ref[rows, :] = x2_rows
                    xb_ref[rows, :] = x2_rows.astype(BF16)
                    yield
                return run

            tail_starts = {i: [ln2_piece(rows)] for i, rows in enumerate(row_blocks)}
        if with_c and not with_b:
            for i, rows in enumerate(row_blocks):
                tail_starts.setdefault(i, []).append(ln3_piece(rows))

        if with_b:
            def down_piece(nk):
                def run():
                    cs = slice(nk * FFN_COL_CHUNK, (nk + 1) * FFN_COL_CHUNK)
                    ypre_ref[:, cs] = ypre_ref[:, cs] + 0.5 * _dot(h_ref[...], wd_ref[:, cs])
                return run

            _interleave([down_piece(nk) for nk in range(D_MODEL // FFN_COL_CHUNK)], tail_starts)
        else:
            _run_all(tail_starts)

    @pl.when(t == 0)
    def _clear_pipeline():
        x2_ref[...] = jnp.zeros(x2_ref.shape, F32)
        xb_ref[...] = jnp.zeros(xb_ref.shape, BF16)
        kb_ref[...] = jnp.zeros(kb_ref.shape, BF16)
        vs_ref[...] = jnp.zeros(vs_ref.shape, BF16)
        ypre_ref[...] = jnp.zeros(ypre_ref.shape, F32)

    @pl.when(t <= n_tiles)
    def _steady():
        _run_step(True, True, True)

    @pl.when(t == n_tiles + 1)
    def _drain_last():
        _run_step(False, False, True)

    @pl.when(jnp.logical_and(c == steps_per_seq - 1, t < n_tiles))
    def _emit_state():
        sret_ref[0] = s_ref[...]
        spool_ref[0] = xp_ref[hist - POOL_BUF:hist, :]


def _prompt_mixer_ffn(proj, x1, proj_small, meta_row_block, pool_w, pool_scale, w_out, ln2g, ln2b,
                      wg, wu, wd, ln3g, ln3b, batch, seq):
    tile = MIX_TOKEN_TILE
    steps = seq // tile
    n_tiles = batch * steps
    cos, sin = _rope_tables(N_META + np.arange(seq))
    cosm, sinm = _rope_tables(np.arange(N_META))
    pair_up = lambda tab: np.concatenate([tab[0::2], tab[1::2]], axis=2)
    mask, qdec, kdec = (pair_up(tab) for tab in _decay_tables(RET_CHUNK, RET_CHUNK))
    _, _, kdecm = _decay_tables(N_META, N_META)
    zero_blk = jnp.zeros_like(pool_w[0])
    pool_w = jnp.stack([jnp.block([[pool_w[2 * pp], zero_blk], [zero_blk, pool_w[2 * pp + 1]]])
                        for pp in range(len(POOL_WINDOWS) // 2)])

    mix_tile = lambda t: jnp.minimum(t, n_tiles - 1)
    ffn_tile = lambda t: jnp.maximum(t - 2, 0)

    def col(j):
        return pl.BlockSpec((tile, RET_WIDTH), lambda t: (mix_tile(t), j))

    def meta_col(j):
        return pl.BlockSpec((N_META, RET_WIDTH), lambda t: (meta_row_block, j))

    in_specs = [
        col(0), col(1), col(2), col(3), col(4),
        pl.BlockSpec((tile, D_MODEL), lambda t: (mix_tile(t), 0)),
        pl.BlockSpec((tile, HEAD_DIM), lambda t: (mix_tile(t) % steps, 0)),
        pl.BlockSpec((tile, HEAD_DIM), lambda t: (mix_tile(t) % steps, 0)),
        meta_col(1), meta_col(2), meta_col(4),
        _const_spec(cosm.shape), _const_spec(sinm.shape), _const_spec(kdecm.shape),
        _const_spec(mask.shape), _const_spec(qdec.shape), _const_spec(kdec.shape),
        _const_spec(pool_w.shape), _const_spec(pool_scale.shape), _const_spec(w_out.shape),
        _const_spec(ln2g.shape), _const_spec(ln2b.shape),
        _const_spec(wg.shape), _const_spec(wu.shape), _const_spec(wd.shape),
        _const_spec(ln3g.shape), _const_spec(ln3b.shape),
    ]
    out_shape = [
        jax.ShapeDtypeStruct((batch * seq, D_MODEL), F32),
        jax.ShapeDtypeStruct((batch, RET_HEADS, HEAD_DIM, HEAD_DIM), F32),
        jax.ShapeDtypeStruct((batch, POOL_BUF, POOL_WIDTH), F32),
    ]
    out_specs = [
        pl.BlockSpec((tile, D_MODEL), lambda t: (ffn_tile(t), 0)),
        pl.BlockSpec((1, RET_HEADS, HEAD_DIM, HEAD_DIM), lambda t: (mix_tile(t) // steps, 0, 0, 0)),
        pl.BlockSpec((1, POOL_BUF, POOL_WIDTH), lambda t: (mix_tile(t) // steps, 0, 0)),
    ]
    return pl.pallas_call(
        functools.partial(_prompt_mixer_ffn_kernel, steps_per_seq=steps),
        grid=(n_tiles + 2,),
        in_specs=in_specs,
        out_specs=out_specs,
        out_shape=out_shape,
        scratch_shapes=[
            pltpu.VMEM((RET_HEADS, HEAD_DIM, HEAD_DIM), F32),
            pltpu.VMEM((N_META + tile, POOL_WIDTH), F32),
            pltpu.VMEM((tile, D_MODEL), BF16),
            pltpu.VMEM((tile, D_MODEL), F32),
            pltpu.VMEM((tile, D_MODEL), BF16),
            pltpu.VMEM((tile, D_MODEL), F32),
            pltpu.VMEM((tile, D_FF), BF16),
            pltpu.VMEM((8, PAIR_WIDTH, PAIR_WIDTH), BF16),
            pltpu.VMEM((8, 2 * PAIR_WIDTH, PAIR_WIDTH), BF16),
        ],
        compiler_params=pltpu.CompilerParams(
            dimension_semantics=("arbitrary",), vmem_limit_bytes=VMEM_LIMIT_BYTES),
        name="prompt_mixer_ffn",
    )(proj, proj, proj, proj, proj, x1, cos, sin, proj_small, proj_small, proj_small,
      cosm, sinm, kdecm, mask, qdec, kdec, pool_w, pool_scale, w_out, ln2g, ln2b,
      wg, wu, wd, ln3g, ln3b)


def _decode_mixer_kernel(q_ref, k_ref, v_ref, g_ref, *rest, dec_seq):
    p_refs, rest = rest[:len(POOL_WINDOWS)], rest[len(POOL_WINDOWS):]
    (x1_ref, s0_ref, pref_ref, cos_ref, sin_ref, mask_ref, qdec_ref, kdec_ref,
     poolw_ref, pscale_ref, wout_ref, lng_ref, lnb_ref,
     o_ref, sret_ref, spool_ref, d_ref, mix_ref) = rest
    rows = q_ref.shape[0]
    nseq = rows // dec_seq
    seq_per_group = BF16_ROWS // dec_seq
    cos, sin = cos_ref[...], sin_ref[...]
    row_seq = lax.broadcasted_iota(jnp.int32, (BF16_ROWS, HEAD_DIM), 0) // dec_seq

    for h in range(RET_HEADS):
        hs = slice(h * HEAD_DIM, (h + 1) * HEAD_DIM)
        qr = _rope(q_ref[:, hs], cos, sin)
        kr = _rope(k_ref[:, hs], cos, sin)
        v = v_ref[:, hs]
        vb = v.astype(BF16)
        scores = _dot_nt(qr.astype(BF16), kr.astype(BF16)) * mask_ref[h]
        o_inner = _dot(scores.astype(BF16), vb)
        qd = qr * qdec_ref[h]
        kd = kr * kdec_ref[h]
        o_cross = []
        for grp in range(rows // BF16_ROWS):
            gr = slice(grp * BF16_ROWS, (grp + 1) * BF16_ROWS)
            qd_g = qd[gr].astype(BF16)
            kd_g = kd[gr].astype(BF16)
            v_g = v[gr]
            acc = jnp.zeros((BF16_ROWS, HEAD_DIM), F32)
            for j in range(seq_per_group):
                b = grp * seq_per_group + j
                s = s0_ref[b, h]
                acc = jnp.where(row_seq == j, _dot(qd_g, s.astype(BF16)), acc)
                v_b = jnp.where(row_seq == j, v_g, 0.0).astype(BF16)
                sret_ref[b, h] = (GAMMAS[h] ** dec_seq) * s + _dot_tn(kd_g, v_b)
            o_cross.append(acc)
        o = o_inner + jnp.concatenate(o_cross, axis=0)
        mix_ref[:, hs] = (_silu(g_ref[:, hs]) * _group_norm(o)).astype(BF16)

    for gi, (w, pg_ref) in enumerate(zip(POOL_WINDOWS, p_refs)):
        gs = slice(gi * POOL_GROUP, (gi + 1) * POOL_GROUP)
        steps = [pref_ref[j, :, gs] for j in range(POOL_BUF)]
        steps += [pg_ref[pl.ds(i, nseq, stride=dec_seq), :] for i in range(dec_seq)]
        for i in range(dec_seq):
            now = POOL_BUF + i
            wsum = steps[now]
            for back in range(1, w):
                wsum = wsum + steps[now - back]
            d_ref[gi, pl.ds(i, nseq, stride=dec_seq), :] = wsum * (1.0 / w) - steps[now]
        for j in range(POOL_BUF):
            spool_ref[j, :, gs] = steps[dec_seq + j]
        pooled = _dot(d_ref[gi].astype(BF16), poolw_ref[gi]) * pscale_ref[:, gs]
        mix_ref[:, RET_WIDTH + gi * POOL_GROUP:RET_WIDTH + (gi + 1) * POOL_GROUP] = pooled.astype(BF16)

    y = _dot(mix_ref[...], wout_ref[...])
    o_ref[...] = _layer_norm(ALPHA * x1_ref[...] + y, lng_ref[...], lnb_ref[...])


def _decode_mixer(proj, x1, state_ret, state_pool, pool_w, pool_scale, w_out, lng, lnb, nseq, dec_seq):
    assert BF16_ROWS % dec_seq == 0 and dec_seq <= POOL_BUF
    rows = DEC_SEQ_BLOCK * dec_seq
    steps = nseq // DEC_SEQ_BLOCK
    cos, sin = _rope_tables(PAST_LEN + (np.arange(rows) % dec_seq))
    mask, qdec, kdec = _decay_tables(rows, dec_seq)

    def col(j):
        return pl.BlockSpec((rows, RET_WIDTH), lambda i: (i, j))

    state_spec = pl.BlockSpec((DEC_SEQ_BLOCK, RET_HEADS, HEAD_DIM, HEAD_DIM), lambda i: (i, 0, 0, 0))
    state_pool = jnp.transpose(state_pool, (1, 0, 2))
    pool_spec = pl.BlockSpec((POOL_BUF, DEC_SEQ_BLOCK, POOL_WIDTH), lambda i: (0, i, 0))
    groups = len(POOL_WINDOWS)
    p_cols = 4 * RET_WIDTH // POOL_GROUP
    in_specs = [
        col(0), col(1), col(2), col(3),
        *[pl.BlockSpec((rows, POOL_GROUP), lambda i, gi=gi: (i, p_cols + gi)) for gi in range(groups)],
        pl.BlockSpec((rows, D_MODEL), lambda i: (i, 0)),
        state_spec, pool_spec,
        _const_spec(cos.shape), _const_spec(sin.shape),
        _const_spec(mask.shape), _const_spec(qdec.shape), _const_spec(kdec.shape),
        _const_spec(pool_w.shape), _const_spec(pool_scale.shape), _const_spec(w_out.shape),
        _const_spec(lng.shape), _const_spec(lnb.shape),
    ]
    out_shape = [
        jax.ShapeDtypeStruct((nseq * dec_seq, D_MODEL), F32),
        jax.ShapeDtypeStruct(state_ret.shape, F32),
        jax.ShapeDtypeStruct(state_pool.shape, F32),
    ]
    out_specs = [pl.BlockSpec((rows, D_MODEL), lambda i: (i, 0)), state_spec, pool_spec]
    x2, new_ret, new_pool = pl.pallas_call(
        functools.partial(_decode_mixer_kernel, dec_seq=dec_seq),
        grid=(steps,),
        in_specs=in_specs,
        out_specs=out_specs,
        out_shape=out_shape,
        scratch_shapes=[
            pltpu.VMEM((groups, rows, POOL_GROUP), F32),
            pltpu.VMEM((rows, D_MODEL), BF16),
        ],
        compiler_params=pltpu.CompilerParams(
            dimension_semantics=("arbitrary",), vmem_limit_bytes=VMEM_LIMIT_BYTES),
        name="decode_mixer",
    )(*([proj] * (4 + groups)), x1, state_ret, state_pool, cos, sin, mask, qdec, kdec,
      pool_w, pool_scale, w_out, lng, lnb)
    return x2, new_ret, jnp.transpose(new_pool, (1, 0, 2))


def kernel(x_prompt, x_sample, state_ret, state_pool, meta_tokens, ffn1_w_gate, ffn1_w_up, ffn1_w_down,
           ln1_g, ln1_b, w_in, pool_w, pool_scale, w_out, ln2_g, ln2_b, ffn2_w_gate, ffn2_w_up,
           ffn2_w_down, ln3_g, ln3_b):
    assert ffn1_w_gate.shape[0] == DEPTH == 1
    batch, seq, _ = x_prompt.shape
    nseq, dec_seq, _ = x_sample.shape
    n_dec = nseq * dec_seq
    assert n_dec % N_META == 0

    bf = lambda w: w[0].astype(BF16)
    row = lambda v: v[0].reshape(1, -1)
    pool_w_b = bf(pool_w)
    pscale, g2, b2 = row(pool_scale), row(ln2_g), row(ln2_b)

    xp = x_prompt.reshape(batch * seq, D_MODEL)
    x_small = jnp.concatenate([x_sample.reshape(n_dec, D_MODEL), meta_tokens.astype(x_prompt.dtype)], axis=0)

    x1s, projs, wg1, wu1, wd1, w_in_b = _ffn_proj_cast(
        x_small, ffn1_w_gate[0], ffn1_w_up[0], ffn1_w_down[0], row(ln1_g), row(ln1_b), w_in[0])
    x1p, projp, wg2, wu2, wd2, w_out_b = _ffn_proj_pipelined(
        xp, wg1, wu1, wd1, row(ln1_g), row(ln1_b), w_in_b,
        (ffn2_w_gate[0], ffn2_w_up[0], ffn2_w_down[0], w_out[0]))
    f2 = (wg2, wu2, wd2, row(ln3_g), row(ln3_b))

    y_prompt, ret_p, pool_p = _prompt_mixer_ffn(projp, x1p, projs, n_dec // N_META, pool_w_b, pscale,
                                                w_out_b, g2, b2, *f2, batch, seq)
    x2s, ret_s, pool_s = _decode_mixer(projs, x1s, state_ret[0], state_pool[0], pool_w_b, pscale,
                                       w_out_b, g2, b2, nseq, dec_seq)
    (y_sample,) = _ffn(x2s, *f2)
    return (y_prompt.reshape(batch, seq, D_MODEL), y_sample.reshape(nseq, dec_seq, D_MODEL),
            ret_p[None], pool_p[None], ret_s[None], pool_s[None])
```

```python
import functools
import math

import jax
import jax.numpy as jnp
import numpy as np
from jax import lax
from jax.experimental import pallas as pl
from jax.experimental.pallas import tpu as pltpu

F32 = jnp.float32
BF16 = jnp.bfloat16

D_MODEL = 1024
D_FF = 2816
N_META = 16
PAST_LEN = 16384
RET_HEADS = 4
HEAD_DIM = 128
RET_WIDTH = RET_HEADS * HEAD_DIM
RET_CHUNK = 128
ROPE_THETA = 10000.0
POOL_WINDOWS = (2, 4, 8, 16)
POOL_GROUP = 128
POOL_WIDTH = POOL_GROUP * len(POOL_WINDOWS)
POOL_BUF = max(POOL_WINDOWS) - 1
IN_WIDTH = 4 * RET_WIDTH + POOL_WIDTH
DEPTH = 1
ALPHA = (2.0 * DEPTH) ** 0.25
LN_EPS = 1e-5
GN_EPS = 1e-5
QK_SCALE = HEAD_DIM ** -0.5
GAMMAS = tuple(1.0 - 2.0 ** (-5.0 - h) for h in range(RET_HEADS))

VMEM_LIMIT_BYTES = 56 * 1024 * 1024
FFN_TOKEN_TILE = 512
FFN_COL_CHUNK = 256
MIX_TOKEN_TILE = 512
DEC_SEQ_BLOCK = 16
BF16_ROWS = 16
PAIR_WIDTH = 2 * HEAD_DIM


def _layer_norm(z, g, b):
    mu = jnp.mean(z, axis=-1, keepdims=True)
    zc = z - mu
    var = jnp.mean(zc * zc, axis=-1, keepdims=True)
    return zc * lax.rsqrt(var + LN_EPS) * g + b


def _silu(x):
    return x * jax.nn.sigmoid(x)


def _dot(a, b):
    return jnp.dot(a, b, preferred_element_type=F32)


def _dot_nt(a, b):
    return lax.dot_general(a, b, (((1,), (1,)), ((), ())), preferred_element_type=F32)


def _dot_tn(a, b):
    return lax.dot_general(a, b, (((0,), (0,)), ((), ())), preferred_element_type=F32)


def _ffn_body(x, wg_ref, wu_ref, wd_ref, lng_ref, lnb_ref, h_ref):
    xb = x.astype(BF16)
    for c in range(D_FF // FFN_COL_CHUNK):
        sl = slice(c * FFN_COL_CHUNK, (c + 1) * FFN_COL_CHUNK)
        g = _dot(xb, wg_ref[:, sl])
        u = _dot(xb, wu_ref[:, sl])
        h_ref[:, sl] = (_silu(g) * u).astype(BF16)
    y = _dot(h_ref[...], wd_ref[...])
    return _layer_norm(ALPHA * x + 0.5 * y, lng_ref[...], lnb_ref[...])


def _ffn_kernel(x_ref, wg_ref, wu_ref, wd_ref, lng_ref, lnb_ref, *rest, with_proj):
    if with_proj:
        win_ref, o_ref, proj_ref, h_ref = rest
    else:
        o_ref, h_ref = rest
    out = _ffn_body(x_ref[...], wg_ref, wu_ref, wd_ref, lng_ref, lnb_ref, h_ref)
    o_ref[...] = out
    if with_proj:
        proj_ref[...] = _dot(out.astype(BF16), win_ref[...])


def _const_spec(shape):
    zeros = (0,) * len(shape)
    return pl.BlockSpec(shape, lambda *_: zeros, pipeline_mode=pl.Buffered(1))


def _ffn(x, wg, wu, wd, lng, lnb, w_in=None):
    n = x.shape[0]
    tm = FFN_TOKEN_TILE if n % FFN_TOKEN_TILE == 0 else n
    with_proj = w_in is not None
    row = lambda i: (i, 0)
    in_specs = [
        pl.BlockSpec((tm, D_MODEL), row),
        _const_spec((D_MODEL, D_FF)),
        _const_spec((D_MODEL, D_FF)),
        _const_spec((D_FF, D_MODEL)),
        _const_spec((1, D_MODEL)),
        _const_spec((1, D_MODEL)),
    ]
    args = [x, wg, wu, wd, lng, lnb]
    out_shape = [jax.ShapeDtypeStruct((n, D_MODEL), F32)]
    out_specs = [pl.BlockSpec((tm, D_MODEL), row)]
    if with_proj:
        in_specs.append(_const_spec((D_MODEL, IN_WIDTH)))
        args.append(w_in)
        out_shape.append(jax.ShapeDtypeStruct((n, IN_WIDTH), F32))
        out_specs.append(pl.BlockSpec((tm, IN_WIDTH), row))
    return pl.pallas_call(
        functools.partial(_ffn_kernel, with_proj=with_proj),
        grid=(n // tm,),
        in_specs=in_specs,
        out_specs=out_specs,
        out_shape=out_shape,
        scratch_shapes=[pltpu.VMEM((tm, D_FF), BF16)],
        compiler_params=pltpu.CompilerParams(
            dimension_semantics=("arbitrary",), vmem_limit_bytes=VMEM_LIMIT_BYTES),
        name="ffn_proj" if with_proj else "ffn",
    )(*args)


def _ffn_proj_cast_kernel(x_ref, wg_ref, wu_ref, wd_ref, lng_ref, lnb_ref, win_ref,
                          x1_ref, proj_ref, wgb_ref, wub_ref, wdb_ref, winb_ref, acc_ref, xb_ref):
    k = pl.program_id(0)

    @pl.when(k == 0)
    def _start():
        xb_ref[...] = x_ref[...].astype(BF16)
        acc_ref[...] = jnp.zeros(acc_ref.shape, F32)

    wg, wu, wd = wg_ref[...].astype(BF16), wu_ref[...].astype(BF16), wd_ref[...].astype(BF16)
    wgb_ref[...] = wg
    wub_ref[...] = wu
    wdb_ref[...] = wd
    h = (_silu(_dot(xb_ref[...], wg)) * _dot(xb_ref[...], wu)).astype(BF16)
    acc_ref[...] += _dot(h, wd)

    @pl.when(k == pl.num_programs(0) - 1)
    def _finish():
        x1 = _layer_norm(ALPHA * x_ref[...] + 0.5 * acc_ref[...], lng_ref[...], lnb_ref[...])
        x1_ref[...] = x1
        w_in = win_ref[...].astype(BF16)
        winb_ref[...] = w_in
        proj_ref[...] = _dot(x1.astype(BF16), w_in)


def _ffn_proj_cast(x, wg, wu, wd, lng, lnb, w_in):
    n = x.shape[0]
    ck = FFN_COL_CHUNK
    col_chunk = pl.BlockSpec((D_MODEL, ck), lambda k: (0, k))
    row_chunk = pl.BlockSpec((ck, D_MODEL), lambda k: (k, 0))
    whole = lambda shape: pl.BlockSpec(shape, lambda k: (0,) * len(shape))
    return pl.pallas_call(
        _ffn_proj_cast_kernel,
        grid=(D_FF // ck,),
        in_specs=[_const_spec(x.shape), col_chunk, col_chunk, row_chunk,
                  _const_spec(lng.shape), _const_spec(lnb.shape), _const_spec(w_in.shape)],
        out_specs=[whole((n, D_MODEL)), whole((n, IN_WIDTH)), col_chunk, col_chunk, row_chunk,
                   whole(w_in.shape)],
        out_shape=[jax.ShapeDtypeStruct((n, D_MODEL), F32), jax.ShapeDtypeStruct((n, IN_WIDTH), F32),
                   jax.ShapeDtypeStruct(wg.shape, BF16), jax.ShapeDtypeStruct(wu.shape, BF16),
                   jax.ShapeDtypeStruct(wd.shape, BF16), jax.ShapeDtypeStruct(w_in.shape, BF16)],
        scratch_shapes=[pltpu.VMEM((n, D_MODEL), F32), pltpu.VMEM((n, D_MODEL), BF16)],
        compiler_params=pltpu.CompilerParams(
            dimension_semantics=("arbitrary",), vmem_limit_bytes=VMEM_LIMIT_BYTES),
        name="ffn_proj_cast",
    )(x, wg, wu, wd, lng, lnb, w_in)


def _interleave(major, starts):
    live = []
    for i, piece in enumerate(major):
        piece()
        live += [make() for make in starts.get(i, [])]
        live = [g for g in live if next(g, "done") != "done"]
    while live:
        live = [g for g in live if next(g, "done") != "done"]


def _run_all(starts):
    for i in sorted(starts):
        for make in starts[i]:
            for _ in make():
                pass


def _ffn_proj_pipelined_kernel(x_ref, wg_ref, wu_ref, wd_ref, lng_ref, lnb_ref, win_ref,
                               cos_ref, sin_ref, *rest, n_cast):
    cast_in, rest = rest[:n_cast], rest[n_cast:]
    x1_ref, proj_ref = rest[:2]
    cast_out, (xb_ref, z1_ref, x1b_ref, h_ref) = rest[2:2 + n_cast], rest[2 + n_cast:]
    t = pl.program_id(0)
    n_tiles = pl.num_programs(0) - 1
    tile = x_ref.shape[0]
    row_blocks = [slice(r, r + RET_CHUNK) for r in range(0, tile, RET_CHUNK)]
    proj_chunk = 2 * FFN_COL_CHUNK

    def ln1_piece(rows):
        def run():
            x1_rows = _layer_norm(z1_ref[rows, :], lng_ref[...], lnb_ref[...])
            x1_ref[rows, :] = x1_rows
            x1b_ref[rows, :] = x1_rows.astype(BF16)
            yield
        return run

    def proj_piece(nk):
        def run():
            cs = slice(nk * proj_chunk, (nk + 1) * proj_chunk)
            chunk = _dot(x1b_ref[...], win_ref[:, cs])
            if nk < 2:
                chunk = jnp.concatenate(
                    [_rope(chunk[:, h * HEAD_DIM:(h + 1) * HEAD_DIM], cos_ref[...], sin_ref[...])
                     for h in range(RET_HEADS)], axis=1)
            proj_ref[:, cs] = chunk
        return run

    def gate_up_piece(ck):
        def run():
            sl = slice(ck * FFN_COL_CHUNK, (ck + 1) * FFN_COL_CHUNK)
            g = _dot(xb_ref[...], wg_ref[:, sl])
            u = _dot(xb_ref[...], wu_ref[:, sl])
            h_ref[:, sl] = (_silu(g) * u).astype(BF16)
        return run

    def down_piece(nk):
        def run():
            cs = slice(nk * FFN_COL_CHUNK, (nk + 1) * FFN_COL_CHUNK)
            z1_ref[:, cs] = ALPHA * x_ref[:, cs] + 0.5 * _dot(h_ref[...], wd_ref[:, cs])
        return run

    ln1_starts = {i: [ln1_piece(rows)] for i, rows in enumerate(row_blocks)}
    proj_pieces = [proj_piece(nk) for nk in range(IN_WIDTH // proj_chunk)]

    @pl.when(t == 0)
    def _clear_pipeline():
        z1_ref[...] = jnp.zeros(z1_ref.shape, F32)

    @pl.when(t < n_tiles)
    def _steady():
        xb_ref[...] = x_ref[...].astype(BF16)
        for src_ref, dst_ref in zip(cast_in, cast_out):
            dst_ref[...] = src_ref[...].astype(BF16)
        gate_up = [gate_up_piece(ck) for ck in range(D_FF // FFN_COL_CHUNK)]
        first_proj = len(row_blocks) + 1
        major = gate_up[:first_proj]
        for i, piece in enumerate(gate_up[first_proj:]):
            major += proj_pieces[i:i + 1] + [piece]
        major += proj_pieces[len(gate_up) - first_proj:]
        _interleave(major, ln1_starts)
        for nk in range(D_MODEL // FFN_COL_CHUNK):
            down_piece(nk)()

    @pl.when(t == n_tiles)
    def _drain():
        _run_all(ln1_starts)
        for piece in proj_pieces:
            piece()


def _slab_rows(rows, max_slabs):
    for slab in range(BF16_ROWS, rows + 1, BF16_ROWS):
        if rows % slab == 0 and rows // slab <= max_slabs:
            return slab
    raise ValueError(f"no slab size for {rows} rows in {max_slabs} steps")


def _ffn_proj_pipelined(x, wg, wu, wd, lng, lnb, w_in, cast_weights, seq):
    n = x.shape[0]
    tm = FFN_TOKEN_TILE
    n_tiles = n // tm
    assert seq % tm == 0 and IN_WIDTH % (2 * FFN_COL_CHUNK) == 0 and RET_WIDTH == 2 * FFN_COL_CHUNK
    cos, sin = _rope_tables(N_META + np.arange(seq))
    in_tile = lambda t: (jnp.minimum(t, n_tiles - 1), 0)
    out_tile = lambda t: (jnp.maximum(t - 1, 0), 0)
    rope_tile = pl.BlockSpec((tm, HEAD_DIM), lambda t: (jnp.maximum(t - 1, 0) % (seq // tm), 0))

    def slab_spec(w):
        slab = _slab_rows(w.shape[0], n_tiles)
        last = w.shape[0] // slab - 1
        return pl.BlockSpec((slab, w.shape[1]), lambda t: (jnp.minimum(t, last), 0))

    cast_specs = [slab_spec(w) for w in cast_weights]
    return pl.pallas_call(
        functools.partial(_ffn_proj_pipelined_kernel, n_cast=len(cast_weights)),
        grid=(n_tiles + 1,),
        in_specs=[
            pl.BlockSpec((tm, D_MODEL), in_tile),
            _const_spec(wg.shape), _const_spec(wu.shape), _const_spec(wd.shape),
            _const_spec(lng.shape), _const_spec(lnb.shape), _const_spec(w_in.shape),
            rope_tile, rope_tile,
        ] + cast_specs,
        out_specs=[pl.BlockSpec((tm, D_MODEL), out_tile), pl.BlockSpec((tm, IN_WIDTH), out_tile)]
        + cast_specs,
        out_shape=[jax.ShapeDtypeStruct((n, D_MODEL), F32), jax.ShapeDtypeStruct((n, IN_WIDTH), F32)]
        + [jax.ShapeDtypeStruct(w.shape, BF16) for w in cast_weights],
        scratch_shapes=[
            pltpu.VMEM((tm, D_MODEL), BF16),
            pltpu.VMEM((tm, D_MODEL), F32),
            pltpu.VMEM((tm, D_MODEL), BF16),
            pltpu.VMEM((tm, D_FF), BF16),
        ],
        compiler_params=pltpu.CompilerParams(
            dimension_semantics=("arbitrary",), vmem_limit_bytes=VMEM_LIMIT_BYTES),
        name="ffn_proj_pipelined",
    )(x, wg, wu, wd, lng, lnb, w_in, cos, sin, *cast_weights)


def _rope_tables(positions):
    half = HEAD_DIM // 2
    inv_freq = ROPE_THETA ** (-np.arange(0, HEAD_DIM, 2, dtype=np.float64) / HEAD_DIM)
    ang = np.asarray(positions, np.float64)[:, None] * inv_freq[None, :]
    cos, sin = np.cos(ang), np.sin(ang)
    assert cos.shape[1] == half
    return (np.concatenate([cos, cos], axis=1).astype(np.float32),
            np.concatenate([-sin, sin], axis=1).astype(np.float32))


def _decay_tables(chunk, seq_len):
    r = np.arange(chunk)
    seq, idx = r // seq_len, (r % seq_len).astype(np.float64)
    same = seq[:, None] == seq[None, :]
    diff = idx[:, None] - idx[None, :]
    mask, qdec, kdec = [], [], []
    for gamma in GAMMAS:
        lg = math.log(gamma)
        mask.append(np.where(same & (diff >= 0), np.exp(lg * np.maximum(diff, 0.0)), 0.0) * QK_SCALE)
        qdec.append(np.broadcast_to((np.exp(lg * (idx + 1.0)) * QK_SCALE)[:, None], (chunk, HEAD_DIM)))
        kdec.append(np.broadcast_to(np.exp(lg * (seq_len - 1.0 - idx))[:, None], (chunk, HEAD_DIM)))
    to32 = lambda t: np.stack(t).astype(np.float32)
    return to32(mask), to32(qdec), to32(kdec)


def _rope(x, cos, sin):
    return x * cos + pltpu.roll(x, HEAD_DIM // 2, 1) * sin


def _group_norm(o):
    mu = jnp.mean(o, axis=-1, keepdims=True)
    oc = o - mu
    var = jnp.mean(oc * oc, axis=-1, keepdims=True)
    return oc * lax.rsqrt(var + GN_EPS)


def _prompt_mixer_ffn_kernel(proj_ref, x1_ref,
                             km_ref, vm_ref, pm_ref, cosm_ref, sinm_ref, kdecm_ref,
                             mask_ref, qdec_ref, kdec_ref, poolw_ref, pscale_ref, wout_ref,
                             ln2g_ref, ln2b_ref, wg_ref, wu_ref, wd_ref, ln3g_ref, ln3b_ref,
                             y_ref, sret_ref, spool_ref,
                             s_ref, xp_ref, mix_ref, x2_ref, xb_ref, ypre_ref, h_ref, kb_ref, vs_ref,
                             *, steps_per_seq):
    t = pl.program_id(0)
    n_tiles = pl.num_programs(0) - 2
    c = t % steps_per_seq
    tile = proj_ref.shape[0]
    q_ref, k_ref, v_ref, g_ref, p_ref = (
        proj_ref.at[:, j * RET_WIDTH:(j + 1) * RET_WIDTH] for j in range(5))
    hist = N_META
    row_blocks = [slice(r, r + RET_CHUNK) for r in range(0, tile, RET_CHUNK)]

    @pl.when(jnp.logical_and(c == 0, t < n_tiles))
    def _init_from_meta():
        for h in range(RET_HEADS):
            hs = slice(h * HEAD_DIM, (h + 1) * HEAD_DIM)
            kr = _rope(km_ref[:, hs], cosm_ref[...], sinm_ref[...])
            kd = (kr * kdecm_ref[h]).astype(BF16)
            s_ref[h] = _dot_tn(kd, vm_ref[:, hs].astype(BF16))
        xp_ref[0:hist, :] = pm_ref[...]

    def ln3_fetch_piece(rows):
        def run():
            y_ref[rows, :] = ypre_ref[rows, :]
            yield
        return run

    def ln3_piece(rows):
        def run():
            y_ref[rows, :] = _layer_norm(y_ref[rows, :], ln3g_ref[...], ln3b_ref[...])
            yield
        return run

    state = [None] * RET_HEADS

    def retention_piece(ci, hp):
        def run():
            rows = slice(ci * RET_CHUNK, (ci + 1) * RET_CHUNK)
            pair = slice(hp * PAIR_WIDTH, (hp + 1) * PAIR_WIDTH)
            buf = ci * 2 + hp
            qr, kr = [], []
            for j in range(2):
                h = 2 * hp + j
                hs = slice(h * HEAD_DIM, (h + 1) * HEAD_DIM)
                blk = slice(j * HEAD_DIM, (j + 1) * HEAD_DIM)
                qr.append(q_ref[rows, hs])
                kr.append(k_ref[rows, hs])
                kb_ref[buf, blk, blk] = kr[j].astype(BF16)
                vs_ref[buf, blk, blk] = v_ref[rows, hs].astype(BF16)
                vs_ref[buf, PAIR_WIDTH + j * HEAD_DIM:PAIR_WIDTH + (j + 1) * HEAD_DIM, blk] = (
                    state[h].astype(BF16))
            q2 = jnp.concatenate(qr, axis=1)
            k2 = jnp.concatenate(kr, axis=1)
            scores = _dot_nt(q2.astype(BF16), kb_ref[buf])
            kd2 = (k2 * kdec_ref[hp]).astype(BF16)
            upd = _dot_tn(kd2, v_ref[rows, pair].astype(BF16))
            qd2 = (q2 * qdec_ref[hp]).astype(BF16)
            yield
            lhs = jnp.concatenate([(scores * mask_ref[hp]).astype(BF16), qd2], axis=1)
            o2 = _dot(lhs, vs_ref[buf])
            for j in range(2):
                blk = slice(j * HEAD_DIM, (j + 1) * HEAD_DIM)
                state[2 * hp + j] = (GAMMAS[2 * hp + j] ** RET_CHUNK) * state[2 * hp + j] + upd[blk, blk]
            yield
            for j in range(2):
                hs = slice((2 * hp + j) * HEAD_DIM, (2 * hp + j + 1) * HEAD_DIM)
                blk = slice(j * HEAD_DIM, (j + 1) * HEAD_DIM)
                mix_ref[rows, hs] = (_silu(g_ref[rows, hs]) * _group_norm(o2[:, blk])).astype(BF16)
        return run

    def pool_piece(pp):
        def run():
            d2 = []
            for j in range(2):
                gi = 2 * pp + j
                w = POOL_WINDOWS[gi]
                gs = slice(gi * POOL_GROUP, (gi + 1) * POOL_GROUP)
                xp_ref[hist:hist + tile, gs] = p_ref[:, gs]
                rows_all = xp_ref[:, gs]
                wsum, shift = rows_all, 1
                while shift < w:
                    wsum = wsum + pltpu.roll(wsum, shift, 0)
                    shift *= 2
                d2.append((wsum[hist:] * (1.0 / w) - rows_all[hist:]).astype(BF16))
                xp_ref[0:hist, gs] = xp_ref[tile:tile + hist, gs]
                if j == 0:
                    yield
            pair = slice(pp * PAIR_WIDTH, (pp + 1) * PAIR_WIDTH)
            pooled = _dot(jnp.concatenate(d2, axis=1), poolw_ref[pp])
            yield
            pooled = pooled * pscale_ref[:, pair]
            mix_ref[:, RET_WIDTH + pp * PAIR_WIDTH:RET_WIDTH + (pp + 1) * PAIR_WIDTH] = pooled.astype(BF16)
        return run

    def gate_up_piece(ck):
        def run():
            sl = slice(ck * FFN_COL_CHUNK, (ck + 1) * FFN_COL_CHUNK)
            g = _dot(xb_ref[...], wg_ref[:, sl])
            u = _dot(xb_ref[...], wu_ref[:, sl])
            h_ref[:, sl] = (_silu(g) * u).astype(BF16)
        return run

    def residual_piece():
        ypre_ref[...] = ALPHA * x2_ref[...]
        yield

    def w_out_piece():
        for h in range(RET_HEADS):
            s_ref[h] = state[h]
        x2_ref[...] = ALPHA * x1_ref[...] + _dot(mix_ref[...], wout_ref[...])

    def ln2_piece(rows):
        def run():
            x2_ref[rows, :] = _layer_norm(x2_ref[rows, :], ln2g_ref[...], ln2b_ref[...])
            yield
        return run

    def xb_piece(rows):
        def run():
            xb_ref[rows, :] = x2_ref[rows, :].astype(BF16)
            yield
        return run

    def down_piece(nk):
        def run():
            cs = slice(nk * FFN_COL_CHUNK, (nk + 1) * FFN_COL_CHUNK)
            ypre_ref[:, cs] = ypre_ref[:, cs] + 0.5 * _dot(h_ref[...], wd_ref[:, cs])
        return run

    def steady_body():
        starts = {}

        def start_at(i, piece):
            starts.setdefault(i, []).append(piece)

        for rows in row_blocks:
            start_at(0, ln3_fetch_piece(rows))
        start_at(1, residual_piece)
        for h in range(RET_HEADS):
            state[h] = s_ref[h]
        gate_up = [gate_up_piece(ck) for ck in range(D_FF // FFN_COL_CHUNK)]
        pieces = [(ci, hp) for ci in range(len(row_blocks)) for hp in range(RET_HEADS // 2)]
        w_out_at = len(pieces) + 1
        for n, (ci, hp) in enumerate(pieces):
            start_at(max(n - 1, 0), retention_piece(ci, hp))
        for pp in range(len(POOL_WINDOWS) // 2):
            start_at(2 * pp + 1, pool_piece(pp))
        major = gate_up[:w_out_at] + [w_out_piece] + gate_up[w_out_at:]
        for i, rows in enumerate(row_blocks):
            start_at(min(w_out_at + i, len(major) - 1), ln2_piece(rows))
        _interleave(major, starts)

        tail_starts = {i: [ln3_piece(rows), xb_piece(rows)] for i, rows in enumerate(row_blocks)}
        _interleave([down_piece(nk) for nk in range(D_MODEL // FFN_COL_CHUNK)], tail_starts)

    def drain_body():
        for rows in row_blocks:
            for make in (ln3_fetch_piece(rows), ln3_piece(rows)):
                for _ in make():
                    pass

    @pl.when(t == 0)
    def _clear_pipeline():
        x2_ref[...] = jnp.zeros(x2_ref.shape, F32)
        xb_ref[...] = jnp.zeros(xb_ref.shape, BF16)
        kb_ref[...] = jnp.zeros(kb_ref.shape, BF16)
        vs_ref[...] = jnp.zeros(vs_ref.shape, BF16)
        ypre_ref[...] = jnp.zeros(ypre_ref.shape, F32)

    @pl.when(t <= n_tiles)
    def _steady():
        steady_body()

    @pl.when(t == n_tiles + 1)
    def _drain_last():
        drain_body()

    @pl.when(jnp.logical_and(c == steps_per_seq - 1, t < n_tiles))
    def _emit_state():
        sret_ref[0] = s_ref[...]
        spool_ref[0] = xp_ref[hist - POOL_BUF:hist, :]


def _prompt_mixer_ffn(proj, x1, proj_small, meta_row_block, pool_w, pool_scale, w_out, ln2g, ln2b,
                      wg, wu, wd, ln3g, ln3b, batch, seq):
    tile = MIX_TOKEN_TILE
    steps = seq // tile
    n_tiles = batch * steps
    cosm, sinm = _rope_tables(np.arange(N_META))
    pair_up = lambda tab: np.concatenate([tab[0::2], tab[1::2]], axis=2)
    mask, qdec, kdec = (pair_up(tab) for tab in _decay_tables(RET_CHUNK, RET_CHUNK))
    _, _, kdecm = _decay_tables(N_META, N_META)
    zero_blk = jnp.zeros_like(pool_w[0])
    pool_w = jnp.stack([jnp.block([[pool_w[2 * pp], zero_blk], [zero_blk, pool_w[2 * pp + 1]]])
                        for pp in range(len(POOL_WINDOWS) // 2)])

    mix_tile = lambda t: jnp.minimum(t, n_tiles - 1)
    ffn_tile = lambda t: jnp.maximum(t - 2, 0)

    def meta_col(j):
        return pl.BlockSpec((N_META, RET_WIDTH), lambda t: (meta_row_block, j))

    in_specs = [
        pl.BlockSpec((tile, IN_WIDTH), lambda t: (mix_tile(t), 0)),
        pl.BlockSpec((tile, D_MODEL), lambda t: (mix_tile(t), 0)),
        meta_col(1), meta_col(2), meta_col(4),
        _const_spec(cosm.shape), _const_spec(sinm.shape), _const_spec(kdecm.shape),
        _const_spec(mask.shape), _const_spec(qdec.shape), _const_spec(kdec.shape),
        _const_spec(pool_w.shape), _const_spec(pool_scale.shape), _const_spec(w_out.shape),
        _const_spec(ln2g.shape), _const_spec(ln2b.shape),
        _const_spec(wg.shape), _const_spec(wu.shape), _const_spec(wd.shape),
        _const_spec(ln3g.shape), _const_spec(ln3b.shape),
    ]
    out_shape = [
        jax.ShapeDtypeStruct((batch * seq, D_MODEL), F32),
        jax.ShapeDtypeStruct((batch, RET_HEADS, HEAD_DIM, HEAD_DIM), F32),
        jax.ShapeDtypeStruct((batch, POOL_BUF, POOL_WIDTH), F32),
    ]
    out_specs = [
        pl.BlockSpec((tile, D_MODEL), lambda t: (ffn_tile(t), 0)),
        pl.BlockSpec((1, RET_HEADS, HEAD_DIM, HEAD_DIM), lambda t: (mix_tile(t) // steps, 0, 0, 0)),
        pl.BlockSpec((1, POOL_BUF, POOL_WIDTH), lambda t: (mix_tile(t) // steps, 0, 0)),
    ]
    return pl.pallas_call(
        functools.partial(_prompt_mixer_ffn_kernel, steps_per_seq=steps),
        grid=(n_tiles + 2,),
        in_specs=in_specs,
        out_specs=out_specs,
        out_shape=out_shape,
        scratch_shapes=[
            pltpu.VMEM((RET_HEADS, HEAD_DIM, HEAD_DIM), F32),
            pltpu.VMEM((N_META + tile, POOL_WIDTH), F32),
            pltpu.VMEM((tile, D_MODEL), BF16),
            pltpu.VMEM((tile, D_MODEL), F32),
            pltpu.VMEM((tile, D_MODEL), BF16),
            pltpu.VMEM((tile, D_MODEL), F32),
            pltpu.VMEM((tile, D_FF), BF16),
            pltpu.VMEM((8, PAIR_WIDTH, PAIR_WIDTH), BF16),
            pltpu.VMEM((8, 2 * PAIR_WIDTH, PAIR_WIDTH), BF16),
        ],
        compiler_params=pltpu.CompilerParams(
            dimension_semantics=("arbitrary",), vmem_limit_bytes=VMEM_LIMIT_BYTES),
        name="prompt_mixer_ffn",
    )(proj, x1, proj_small, proj_small, proj_small,
      cosm, sinm, kdecm, mask, qdec, kdec, pool_w, pool_scale, w_out, ln2g, ln2b,
      wg, wu, wd, ln3g, ln3b)


def _decode_mixer_kernel(q_ref, k_ref, v_ref, g_ref, *rest, dec_seq):
    p_refs, rest = rest[:len(POOL_WINDOWS)], rest[len(POOL_WINDOWS):]
    (x1_ref, s0_ref, pref_ref, cos_ref, sin_ref, mask_ref, qdec_ref, kdec_ref,
     poolw_ref, pscale_ref, wout_ref, lng_ref, lnb_ref,
     o_ref, sret_ref, spool_ref, d_ref, mix_ref) = rest
    rows = q_ref.shape[0]
    nseq = rows // dec_seq
    seq_per_group = BF16_ROWS // dec_seq
    cos, sin = cos_ref[...], sin_ref[...]
    row_seq = lax.broadcasted_iota(jnp.int32, (BF16_ROWS, HEAD_DIM), 0) // dec_seq

    for h in range(RET_HEADS):
        hs = slice(h * HEAD_DIM, (h + 1) * HEAD_DIM)
        qr = _rope(q_ref[:, hs], cos, sin)
        kr = _rope(k_ref[:, hs], cos, sin)
        v = v_ref[:, hs]
        vb = v.astype(BF16)
        scores = _dot_nt(qr.astype(BF16), kr.astype(BF16)) * mask_ref[h]
        o_inner = _dot(scores.astype(BF16), vb)
        qd = qr * qdec_ref[h]
        kd = kr * kdec_ref[h]
        o_cross = []
        for grp in range(rows // BF16_ROWS):
            gr = slice(grp * BF16_ROWS, (grp + 1) * BF16_ROWS)
            qd_g = qd[gr].astype(BF16)
            kd_g = kd[gr].astype(BF16)
            v_g = v[gr]
            acc = jnp.zeros((BF16_ROWS, HEAD_DIM), F32)
            for j in range(seq_per_group):
                b = grp * seq_per_group + j
                s = s0_ref[b, h]
                acc = jnp.where(row_seq == j, _dot(qd_g, s.astype(BF16)), acc)
                v_b = jnp.where(row_seq == j, v_g, 0.0).astype(BF16)
                sret_ref[b, h] = (GAMMAS[h] ** dec_seq) * s + _dot_tn(kd_g, v_b)
            o_cross.append(acc)
        o = o_inner + jnp.concatenate(o_cross, axis=0)
        mix_ref[:, hs] = (_silu(g_ref[:, hs]) * _group_norm(o)).astype(BF16)

    for gi, (w, pg_ref) in enumerate(zip(POOL_WINDOWS, p_refs)):
        gs = slice(gi * POOL_GROUP, (gi + 1) * POOL_GROUP)
        steps = [pref_ref[j, :, gs] for j in range(POOL_BUF)]
        steps += [pg_ref[pl.ds(i, nseq, stride=dec_seq), :] for i in range(dec_seq)]
        for i in range(dec_seq):
            now = POOL_BUF + i
            wsum = steps[now]
            for back in range(1, w):
                wsum = wsum + steps[now - back]
            d_ref[gi, pl.ds(i, nseq, stride=dec_seq), :] = wsum * (1.0 / w) - steps[now]
        for j in range(POOL_BUF):
            spool_ref[j, :, gs] = steps[dec_seq + j]
        pooled = _dot(d_ref[gi].astype(BF16), poolw_ref[gi]) * pscale_ref[:, gs]
        mix_ref[:, RET_WIDTH + gi * POOL_GROUP:RET_WIDTH + (gi + 1) * POOL_GROUP] = pooled.astype(BF16)

    y = _dot(mix_ref[...], wout_ref[...])
    o_ref[...] = _layer_norm(ALPHA * x1_ref[...] + y, lng_ref[...], lnb_ref[...])


def _decode_mixer(proj, x1, state_ret, state_pool, pool_w, pool_scale, w_out, lng, lnb, nseq, dec_seq):
    assert BF16_ROWS % dec_seq == 0 and dec_seq <= POOL_BUF
    rows = DEC_SEQ_BLOCK * dec_seq
    steps = nseq // DEC_SEQ_BLOCK
    cos, sin = _rope_tables(PAST_LEN + (np.arange(rows) % dec_seq))
    mask, qdec, kdec = _decay_tables(rows, dec_seq)

    def col(j):
        return pl.BlockSpec((rows, RET_WIDTH), lambda i: (i, j))

    state_spec = pl.BlockSpec((DEC_SEQ_BLOCK, RET_HEADS, HEAD_DIM, HEAD_DIM), lambda i: (i, 0, 0, 0))
    state_pool = jnp.transpose(state_pool, (1, 0, 2))
    pool_spec = pl.BlockSpec((POOL_BUF, DEC_SEQ_BLOCK, POOL_WIDTH), lambda i: (0, i, 0))
    groups = len(POOL_WINDOWS)
    p_cols = 4 * RET_WIDTH // POOL_GROUP
    in_specs = [
        col(0), col(1), col(2), col(3),
        *[pl.BlockSpec((rows, POOL_GROUP), lambda i, gi=gi: (i, p_cols + gi)) for gi in range(groups)],
        pl.BlockSpec((rows, D_MODEL), lambda i: (i, 0)),
        state_spec, pool_spec,
        _const_spec(cos.shape), _const_spec(sin.shape),
        _const_spec(mask.shape), _const_spec(qdec.shape), _const_spec(kdec.shape),
        _const_spec(pool_w.shape), _const_spec(pool_scale.shape), _const_spec(w_out.shape),
        _const_spec(lng.shape), _const_spec(lnb.shape),
    ]
    out_shape = [
        jax.ShapeDtypeStruct((nseq * dec_seq, D_MODEL), F32),
        jax.ShapeDtypeStruct(state_ret.shape, F32),
        jax.ShapeDtypeStruct(state_pool.shape, F32),
    ]
    out_specs = [pl.BlockSpec((rows, D_MODEL), lambda i: (i, 0)), state_spec, pool_spec]
    x2, new_ret, new_pool = pl.pallas_call(
        functools.partial(_decode_mixer_kernel, dec_seq=dec_seq),
        grid=(steps,),
        in_specs=in_specs,
        out_specs=out_specs,
        out_shape=out_shape,
        scratch_shapes=[
            pltpu.VMEM((groups, rows, POOL_GROUP), F32),
            pltpu.VMEM((rows, D_MODEL), BF16),
        ],
        compiler_params=pltpu.CompilerParams(
            dimension_semantics=("arbitrary",), vmem_limit_bytes=VMEM_LIMIT_BYTES),
        name="decode_mixer",
    )(*([proj] * (4 + groups)), x1, state_ret, state_pool, cos, sin, mask, qdec, kdec,
      pool_w, pool_scale, w_out, lng, lnb)
    return x2, new_ret, jnp.transpose(new_pool, (1, 0, 2))


def kernel(x_prompt, x_sample, state_ret, state_pool, meta_tokens, ffn1_w_gate, ffn1_w_up, ffn1_w_down,
           ln1_g, ln1_b, w_in, pool_w, pool_scale, w_out, ln2_g, ln2_b, ffn2_w_gate, ffn2_w_up,
           ffn2_w_down, ln3_g, ln3_b):
    assert ffn1_w_gate.shape[0] == DEPTH == 1
    batch, seq, _ = x_prompt.shape
    nseq, dec_seq, _ = x_sample.shape
    n_dec = nseq * dec_seq
    assert n_dec % N_META == 0

    bf = lambda w: w[0].astype(BF16)
    row = lambda v: v[0].reshape(1, -1)
    pool_w_b = bf(pool_w)
    pscale, g2, b2 = row(pool_scale), row(ln2_g), row(ln2_b)

    xp = x_prompt.reshape(batch * seq, D_MODEL)
    x_small = jnp.concatenate([x_sample.reshape(n_dec, D_MODEL), meta_tokens.astype(x_prompt.dtype)], axis=0)

    x1s, projs, wg1, wu1, wd1, w_in_b = _ffn_proj_cast(
        x_small, ffn1_w_gate[0], ffn1_w_up[0], ffn1_w_down[0], row(ln1_g), row(ln1_b), w_in[0])
    x1p, projp, wg2, wu2, wd2, w_out_b = _ffn_proj_pipelined(
        xp, wg1, wu1, wd1, row(ln1_g), row(ln1_b), w_in_b,
        (ffn2_w_gate[0], ffn2_w_up[0], ffn2_w_down[0], w_out[0]), seq)
    f2 = (wg2, wu2, wd2, row(ln3_g), row(ln3_b))

    y_prompt, ret_p, pool_p = _prompt_mixer_ffn(projp, x1p, projs, n_dec // N_META, pool_w_b, pscale,
                                                w_out_b, g2, b2, *f2, batch, seq)
    x2s, ret_s, pool_s = _decode_mixer(projs, x1s, state_ret[0], state_pool[0], pool_w_b, pscale,
                                       w_out_b, g2, b2, nseq, dec_seq)
    (y_sample,) = _ffn(x2s, *f2)
    return (y_prompt.reshape(batch, seq, D_MODEL), y_sample.reshape(nseq, dec_seq, D_MODEL),
            ret_p[None], pool_p[None], ret_s[None], pool_s[None])
```

```python
import functools
import math

import jax
import jax.numpy as jnp
import numpy as np
from jax import lax
from jax.experimental import pallas as pl
from jax.experimental.pallas import tpu as pltpu

F32 = jnp.float32
BF16 = jnp.bfloat16

D_MODEL = 1024
D_FF = 2816
N_META = 16
PAST_LEN = 16384
RET_HEADS = 4
HEAD_DIM = 128
RET_WIDTH = RET_HEADS * HEAD_DIM
RET_CHUNK = 128
ROPE_THETA = 10000.0
POOL_WINDOWS = (2, 4, 8, 16)
POOL_GROUP = 128
POOL_WIDTH = POOL_GROUP * len(POOL_WINDOWS)
POOL_BUF = max(POOL_WINDOWS) - 1
IN_WIDTH = 4 * RET_WIDTH + POOL_WIDTH
DEPTH = 1
ALPHA = (2.0 * DEPTH) ** 0.25
LN_EPS = 1e-5
GN_EPS = 1e-5
QK_SCALE = HEAD_DIM ** -0.5
GAMMAS = tuple(1.0 - 2.0 ** (-5.0 - h) for h in range(RET_HEADS))

VMEM_LIMIT_BYTES = 56 * 1024 * 1024
FFN_TOKEN_TILE = 512
FFN_COL_CHUNK = 256
MIX_TOKEN_TILE = 512
DEC_SEQ_BLOCK = 16
BF16_ROWS = 16
PAIR_WIDTH = 2 * HEAD_DIM


def _layer_norm(z, g, b):
    mu = jnp.mean(z, axis=-1, keepdims=True)
    zc = z - mu
    var = jnp.mean(zc * zc, axis=-1, keepdims=True)
    return zc * lax.rsqrt(var + LN_EPS) * g + b


def _silu(x):
    return x * jax.nn.sigmoid(x)


def _dot(a, b):
    return jnp.dot(a, b, preferred_element_type=F32)


def _dot_nt(a, b):
    return lax.dot_general(a, b, (((1,), (1,)), ((), ())), preferred_element_type=F32)


def _dot_tn(a, b):
    return lax.dot_general(a, b, (((0,), (0,)), ((), ())), preferred_element_type=F32)


def _const_spec(shape):
    zeros = (0,) * len(shape)
    return pl.BlockSpec(shape, lambda *_: zeros, pipeline_mode=pl.Buffered(1))


def _ffn_proj_cast_kernel(x_ref, wg_ref, wu_ref, wd_ref, lng_ref, lnb_ref, win_ref,
                          x1_ref, proj_ref, wgb_ref, wub_ref, wdb_ref, winb_ref, acc_ref, xb_ref):
    k = pl.program_id(0)

    @pl.when(k == 0)
    def _start():
        xb_ref[...] = x_ref[...].astype(BF16)
        acc_ref[...] = jnp.zeros(acc_ref.shape, F32)

    wg, wu, wd = wg_ref[...].astype(BF16), wu_ref[...].astype(BF16), wd_ref[...].astype(BF16)
    wgb_ref[...] = wg
    wub_ref[...] = wu
    wdb_ref[...] = wd
    h = (_silu(_dot(xb_ref[...], wg)) * _dot(xb_ref[...], wu)).astype(BF16)
    acc_ref[...] += _dot(h, wd)

    @pl.when(k == pl.num_programs(0) - 1)
    def _finish():
        x1 = _layer_norm(ALPHA * x_ref[...] + 0.5 * acc_ref[...], lng_ref[...], lnb_ref[...])
        x1_ref[...] = x1
        w_in = win_ref[...].astype(BF16)
        winb_ref[...] = w_in
        proj_ref[...] = _dot(x1.astype(BF16), w_in)


def _ffn_proj_cast(x, wg, wu, wd, lng, lnb, w_in):
    n = x.shape[0]
    ck = FFN_COL_CHUNK
    col_chunk = pl.BlockSpec((D_MODEL, ck), lambda k: (0, k))
    row_chunk = pl.BlockSpec((ck, D_MODEL), lambda k: (k, 0))
    whole = lambda shape: pl.BlockSpec(shape, lambda k: (0,) * len(shape))
    return pl.pallas_call(
        _ffn_proj_cast_kernel,
        grid=(D_FF // ck,),
        in_specs=[_const_spec(x.shape), col_chunk, col_chunk, row_chunk,
                  _const_spec(lng.shape), _const_spec(lnb.shape), _const_spec(w_in.shape)],
        out_specs=[whole((n, D_MODEL)), whole((n, IN_WIDTH)), col_chunk, col_chunk, row_chunk,
                   whole(w_in.shape)],
        out_shape=[jax.ShapeDtypeStruct((n, D_MODEL), F32), jax.ShapeDtypeStruct((n, IN_WIDTH), F32),
                   jax.ShapeDtypeStruct(wg.shape, BF16), jax.ShapeDtypeStruct(wu.shape, BF16),
                   jax.ShapeDtypeStruct(wd.shape, BF16), jax.ShapeDtypeStruct(w_in.shape, BF16)],
        scratch_shapes=[pltpu.VMEM((n, D_MODEL), F32), pltpu.VMEM((n, D_MODEL), BF16)],
        compiler_params=pltpu.CompilerParams(
            dimension_semantics=("arbitrary",), vmem_limit_bytes=VMEM_LIMIT_BYTES),
        name="ffn_proj_cast",
    )(x, wg, wu, wd, lng, lnb, w_in)


def _interleave(major, starts):
    live = []
    for i, piece in enumerate(major):
        piece()
        live += [make() for make in starts.get(i, [])]
        live = [g for g in live if next(g, "done") != "done"]
    while live:
        live = [g for g in live if next(g, "done") != "done"]


def _run_all(starts):
    for i in sorted(starts):
        for make in starts[i]:
            for _ in make():
                pass


def _ffn_proj_pipelined_kernel(x_ref, wg_ref, wu_ref, wd_ref, lng_ref, lnb_ref, win_ref,
                               cos_ref, sin_ref, *rest, n_cast):
    cast_in, rest = rest[:n_cast], rest[n_cast:]
    x1_ref, proj_ref = rest[:2]
    cast_out, (xb_ref, z1_ref, x1b_ref, h_ref) = rest[2:2 + n_cast], rest[2 + n_cast:]
    t = pl.program_id(0)
    n_tiles = pl.num_programs(0) - 1
    tile = x_ref.shape[0]
    row_blocks = [slice(r, r + RET_CHUNK) for r in range(0, tile, RET_CHUNK)]
    proj_chunk = 2 * FFN_COL_CHUNK

    def ln1_piece(rows):
        def run():
            x1_rows = _layer_norm(z1_ref[rows, :], lng_ref[...], lnb_ref[...])
            x1_ref[rows, :] = x1_rows
            x1b_ref[rows, :] = x1_rows.astype(BF16)
            yield
        return run

    def proj_piece(nk):
        def run():
            cs = slice(nk * proj_chunk, (nk + 1) * proj_chunk)
            chunk = _dot(x1b_ref[...], win_ref[:, cs])
            if nk < 2:
                chunk = jnp.concatenate(
                    [_rope(chunk[:, h * HEAD_DIM:(h + 1) * HEAD_DIM], cos_ref[...], sin_ref[...])
                     for h in range(RET_HEADS)], axis=1)
            proj_ref[:, cs] = chunk
        return run

    def gate_up_piece(ck):
        def run():
            sl = slice(ck * FFN_COL_CHUNK, (ck + 1) * FFN_COL_CHUNK)
            g = _dot(xb_ref[...], wg_ref[:, sl])
            u = _dot(xb_ref[...], wu_ref[:, sl])
            h_ref[:, sl] = (_silu(g) * u).astype(BF16)
        return run

    def down_piece(nk):
        def run():
            cs = slice(nk * FFN_COL_CHUNK, (nk + 1) * FFN_COL_CHUNK)
            z1_ref[:, cs] = ALPHA * x_ref[:, cs] + 0.5 * _dot(h_ref[...], wd_ref[:, cs])
        return run

    ln1_starts = {i: [ln1_piece(rows)] for i, rows in enumerate(row_blocks)}
    proj_pieces = [proj_piece(nk) for nk in range(IN_WIDTH // proj_chunk)]

    @pl.when(t == 0)
    def _clear_pipeline():
        z1_ref[...] = jnp.zeros(z1_ref.shape, F32)

    @pl.when(t < n_tiles)
    def _steady():
        xb_ref[...] = x_ref[...].astype(BF16)
        for src_ref, dst_ref in zip(cast_in, cast_out):
            dst_ref[...] = src_ref[...].astype(BF16)
        gate_up = [gate_up_piece(ck) for ck in range(D_FF // FFN_COL_CHUNK)]
        first_proj = len(row_blocks) + 1
        major = gate_up[:first_proj]
        for i, piece in enumerate(gate_up[first_proj:]):
            major += proj_pieces[i:i + 1] + [piece]
        major += proj_pieces[len(gate_up) - first_proj:]
        _interleave(major, ln1_starts)
        for nk in range(D_MODEL // FFN_COL_CHUNK):
            down_piece(nk)()

    @pl.when(t == n_tiles)
    def _drain():
        _run_all(ln1_starts)
        for piece in proj_pieces:
            piece()


def _slab_rows(rows, max_slabs):
    for slab in range(BF16_ROWS, rows + 1, BF16_ROWS):
        if rows % slab == 0 and rows // slab <= max_slabs:
            return slab
    raise ValueError(f"no slab size for {rows} rows in {max_slabs} steps")


def _ffn_proj_pipelined(x, wg, wu, wd, lng, lnb, w_in, cast_weights, seq):
    n = x.shape[0]
    tm = FFN_TOKEN_TILE
    n_tiles = n // tm
    assert seq % tm == 0 and IN_WIDTH % (2 * FFN_COL_CHUNK) == 0 and RET_WIDTH == 2 * FFN_COL_CHUNK
    cos, sin = _rope_tables(N_META + np.arange(seq))
    in_tile = lambda t: (jnp.minimum(t, n_tiles - 1), 0)
    out_tile = lambda t: (jnp.maximum(t - 1, 0), 0)
    rope_tile = pl.BlockSpec((tm, HEAD_DIM), lambda t: (jnp.maximum(t - 1, 0) % (seq // tm), 0))

    def slab_spec(w):
        slab = _slab_rows(w.shape[0], n_tiles)
        last = w.shape[0] // slab - 1
        return pl.BlockSpec((slab, w.shape[1]), lambda t: (jnp.minimum(t, last), 0))

    cast_specs = [slab_spec(w) for w in cast_weights]
    return pl.pallas_call(
        functools.partial(_ffn_proj_pipelined_kernel, n_cast=len(cast_weights)),
        grid=(n_tiles + 1,),
        in_specs=[
            pl.BlockSpec((tm, D_MODEL), in_tile),
            _const_spec(wg.shape), _const_spec(wu.shape), _const_spec(wd.shape),
            _const_spec(lng.shape), _const_spec(lnb.shape), _const_spec(w_in.shape),
            rope_tile, rope_tile,
        ] + cast_specs,
        out_specs=[pl.BlockSpec((tm, D_MODEL), out_tile), pl.BlockSpec((tm, IN_WIDTH), out_tile)]
        + cast_specs,
        out_shape=[jax.ShapeDtypeStruct((n, D_MODEL), F32), jax.ShapeDtypeStruct((n, IN_WIDTH), F32)]
        + [jax.ShapeDtypeStruct(w.shape, BF16) for w in cast_weights],
        scratch_shapes=[
            pltpu.VMEM((tm, D_MODEL), BF16),
            pltpu.VMEM((tm, D_MODEL), F32),
            pltpu.VMEM((tm, D_MODEL), BF16),
            pltpu.VMEM((tm, D_FF), BF16),
        ],
        compiler_params=pltpu.CompilerParams(
            dimension_semantics=("arbitrary",), vmem_limit_bytes=VMEM_LIMIT_BYTES),
        name="ffn_proj_pipelined",
    )(x, wg, wu, wd, lng, lnb, w_in, cos, sin, *cast_weights)


def _rope_tables(positions):
    half = HEAD_DIM // 2
    inv_freq = ROPE_THETA ** (-np.arange(0, HEAD_DIM, 2, dtype=np.float64) / HEAD_DIM)
    ang = np.asarray(positions, np.float64)[:, None] * inv_freq[None, :]
    cos, sin = np.cos(ang), np.sin(ang)
    assert cos.shape[1] == half
    return (np.concatenate([cos, cos], axis=1).astype(np.float32),
            np.concatenate([-sin, sin], axis=1).astype(np.float32))


def _decay_tables(chunk, seq_len):
    r = np.arange(chunk)
    seq, idx = r // seq_len, (r % seq_len).astype(np.float64)
    same = seq[:, None] == seq[None, :]
    diff = idx[:, None] - idx[None, :]
    mask, qdec, kdec = [], [], []
    for gamma in GAMMAS:
        lg = math.log(gamma)
        mask.append(np.where(same & (diff >= 0), np.exp(lg * np.maximum(diff, 0.0)), 0.0) * QK_SCALE)
        qdec.append(np.broadcast_to((np.exp(lg * (idx + 1.0)) * QK_SCALE)[:, None], (chunk, HEAD_DIM)))
        kdec.append(np.broadcast_to(np.exp(lg * (seq_len - 1.0 - idx))[:, None], (chunk, HEAD_DIM)))
    to32 = lambda t: np.stack(t).astype(np.float32)
    return to32(mask), to32(qdec), to32(kdec)


def _rope(x, cos, sin):
    return x * cos + pltpu.roll(x, HEAD_DIM // 2, 1) * sin


def _group_norm(o):
    mu = jnp.mean(o, axis=-1, keepdims=True)
    oc = o - mu
    var = jnp.mean(oc * oc, axis=-1, keepdims=True)
    return oc * lax.rsqrt(var + GN_EPS)


def _prompt_mixer_ffn_kernel(proj_ref, x1_ref,
                             km_ref, vm_ref, pm_ref, cosm_ref, sinm_ref, kdecm_ref,
                             mask_ref, qdec_ref, kdec_ref, poolw_ref, pscale_ref, wout_ref,
                             ln2g_ref, ln2b_ref, wg_ref, wu_ref, wd_ref, ln3g_ref, ln3b_ref,
                             x2dec_hbm,
                             y_ref, sret_ref, spool_ref, ydec_hbm,
                             s_ref, xp_ref, mix_ref, x2_ref, xb_ref, ypre_ref, h_ref, kb_ref, vs_ref,
                             *, steps_per_seq):
    t = pl.program_id(0)
    n_tiles = pl.num_programs(0) - 2
    c = t % steps_per_seq
    tile = proj_ref.shape[0]
    q_ref, k_ref, v_ref, g_ref, p_ref = (
        proj_ref.at[:, j * RET_WIDTH:(j + 1) * RET_WIDTH] for j in range(5))
    hist = N_META
    row_blocks = [slice(r, r + RET_CHUNK) for r in range(0, tile, RET_CHUNK)]

    @pl.when(jnp.logical_and(c == 0, t < n_tiles))
    def _init_from_meta():
        for h in range(RET_HEADS):
            hs = slice(h * HEAD_DIM, (h + 1) * HEAD_DIM)
            kr = _rope(km_ref[:, hs], cosm_ref[...], sinm_ref[...])
            kd = (kr * kdecm_ref[h]).astype(BF16)
            s_ref[h] = _dot_tn(kd, vm_ref[:, hs].astype(BF16))
        xp_ref[0:hist, :] = pm_ref[...]

    def ln3_fetch_piece(rows):
        def run():
            y_ref[rows, :] = ypre_ref[rows, :]
            yield
        return run

    def ln3_piece(rows):
        def run():
            y_ref[rows, :] = _layer_norm(y_ref[rows, :], ln3g_ref[...], ln3b_ref[...])
            yield
        return run

    state = [None] * RET_HEADS

    def retention_piece(ci, hp):
        def run():
            rows = slice(ci * RET_CHUNK, (ci + 1) * RET_CHUNK)
            pair = slice(hp * PAIR_WIDTH, (hp + 1) * PAIR_WIDTH)
            buf = ci * 2 + hp
            qr, kr = [], []
            for j in range(2):
                h = 2 * hp + j
                hs = slice(h * HEAD_DIM, (h + 1) * HEAD_DIM)
                blk = slice(j * HEAD_DIM, (j + 1) * HEAD_DIM)
                qr.append(q_ref[rows, hs])
                kr.append(k_ref[rows, hs])
                kb_ref[buf, blk, blk] = kr[j].astype(BF16)
                vs_ref[buf, blk, blk] = v_ref[rows, hs].astype(BF16)
                vs_ref[buf, PAIR_WIDTH + j * HEAD_DIM:PAIR_WIDTH + (j + 1) * HEAD_DIM, blk] = (
                    state[h].astype(BF16))
            q2 = jnp.concatenate(qr, axis=1)
            k2 = jnp.concatenate(kr, axis=1)
            scores = _dot_nt(q2.astype(BF16), kb_ref[buf])
            kd2 = (k2 * kdec_ref[hp]).astype(BF16)
            upd = _dot_tn(kd2, v_ref[rows, pair].astype(BF16))
            qd2 = (q2 * qdec_ref[hp]).astype(BF16)
            yield
            lhs = jnp.concatenate([(scores * mask_ref[hp]).astype(BF16), qd2], axis=1)
            o2 = _dot(lhs, vs_ref[buf])
            for j in range(2):
                blk = slice(j * HEAD_DIM, (j + 1) * HEAD_DIM)
                state[2 * hp + j] = (GAMMAS[2 * hp + j] ** RET_CHUNK) * state[2 * hp + j] + upd[blk, blk]
            yield
            for j in range(2):
                hs = slice((2 * hp + j) * HEAD_DIM, (2 * hp + j + 1) * HEAD_DIM)
                blk = slice(j * HEAD_DIM, (j + 1) * HEAD_DIM)
                mix_ref[rows, hs] = (_silu(g_ref[rows, hs]) * _group_norm(o2[:, blk])).astype(BF16)
        return run

    def pool_piece(pp):
        def run():
            d2 = []
            for j in range(2):
                gi = 2 * pp + j
                w = POOL_WINDOWS[gi]
                gs = slice(gi * POOL_GROUP, (gi + 1) * POOL_GROUP)
                xp_ref[hist:hist + tile, gs] = p_ref[:, gs]
                rows_all = xp_ref[:, gs]
                wsum, shift = rows_all, 1
                while shift < w:
                    wsum = wsum + pltpu.roll(wsum, shift, 0)
                    shift *= 2
                d2.append((wsum[hist:] * (1.0 / w) - rows_all[hist:]).astype(BF16))
                xp_ref[0:hist, gs] = xp_ref[tile:tile + hist, gs]
                if j == 0:
                    yield
            pair = slice(pp * PAIR_WIDTH, (pp + 1) * PAIR_WIDTH)
            pooled = _dot(jnp.concatenate(d2, axis=1), poolw_ref[pp])
            yield
            pooled = pooled * pscale_ref[:, pair]
            mix_ref[:, RET_WIDTH + pp * PAIR_WIDTH:RET_WIDTH + (pp + 1) * PAIR_WIDTH] = pooled.astype(BF16)
        return run

    def gate_up_piece(ck):
        def run():
            sl = slice(ck * FFN_COL_CHUNK, (ck + 1) * FFN_COL_CHUNK)
            g = _dot(xb_ref[...], wg_ref[:, sl])
            u = _dot(xb_ref[...], wu_ref[:, sl])
            h_ref[:, sl] = (_silu(g) * u).astype(BF16)
        return run

    def residual_piece():
        ypre_ref[...] = ALPHA * x2_ref[...]
        yield

    def w_out_piece():
        for h in range(RET_HEADS):
            s_ref[h] = state[h]
        x2_ref[...] = ALPHA * x1_ref[...] + _dot(mix_ref[...], wout_ref[...])

    def ln2_piece(rows):
        def run():
            x2_ref[rows, :] = _layer_norm(x2_ref[rows, :], ln2g_ref[...], ln2b_ref[...])
            yield
        return run

    def xb_piece(rows):
        def run():
            xb_ref[rows, :] = x2_ref[rows, :].astype(BF16)
            yield
        return run

    def down_piece(nk):
        def run():
            cs = slice(nk * FFN_COL_CHUNK, (nk + 1) * FFN_COL_CHUNK)
            ypre_ref[:, cs] = ypre_ref[:, cs] + 0.5 * _dot(h_ref[...], wd_ref[:, cs])
        return run

    def steady_body():
        starts = {}

        def start_at(i, piece):
            starts.setdefault(i, []).append(piece)

        for rows in row_blocks:
            start_at(0, ln3_fetch_piece(rows))
        start_at(1, residual_piece)
        for h in range(RET_HEADS):
            state[h] = s_ref[h]
        gate_up = [gate_up_piece(ck) for ck in range(D_FF // FFN_COL_CHUNK)]
        pieces = [(ci, hp) for ci in range(len(row_blocks)) for hp in range(RET_HEADS // 2)]
        w_out_at = len(pieces)
        for n, (ci, hp) in enumerate(pieces):
            start_at(n // 2 if n < 4 else n - 2, retention_piece(ci, hp))
        for pp in range(len(POOL_WINDOWS) // 2):
            start_at(2 * pp + 1, pool_piece(pp))
        major = gate_up[:w_out_at] + [w_out_piece] + gate_up[w_out_at:]
        for i, rows in enumerate(row_blocks):
            start_at(min(w_out_at + i, len(major) - 1), ln2_piece(rows))
        _interleave(major, starts)

        tail_starts = {i: [ln3_piece(rows), xb_piece(rows)] for i, rows in enumerate(row_blocks)}
        _interleave([down_piece(nk) for nk in range(D_MODEL // FFN_COL_CHUNK)], tail_starts)

    def drain_body():
        for rows in row_blocks:
            for make in (ln3_fetch_piece(rows), ln3_piece(rows)):
                for _ in make():
                    pass

    @pl.when(t == 0)
    def _prime_pipeline():
        pltpu.sync_copy(x2dec_hbm, x2_ref)
        xb_ref[...] = x2_ref[...].astype(BF16)
        kb_ref[...] = jnp.zeros(kb_ref.shape, BF16)
        vs_ref[...] = jnp.zeros(vs_ref.shape, BF16)
        ypre_ref[...] = jnp.zeros(ypre_ref.shape, F32)

    @pl.when(t <= n_tiles)
    def _steady():
        steady_body()

    @pl.when(t == 1)
    def _emit_decode_rows():
        pltpu.sync_copy(y_ref, ydec_hbm)

    @pl.when(t == n_tiles + 1)
    def _drain_last():
        drain_body()

    @pl.when(jnp.logical_and(c == steps_per_seq - 1, t < n_tiles))
    def _emit_state():
        sret_ref[0] = s_ref[...]
        spool_ref[0] = xp_ref[hist - POOL_BUF:hist, :]


def _prompt_mixer_ffn(proj, x1, proj_small, meta_row_block, pool_w, pool_scale, w_out, ln2g, ln2b,
                      wg, wu, wd, ln3g, ln3b, x2_dec, batch, seq):
    tile = MIX_TOKEN_TILE
    assert x2_dec.shape == (tile, D_MODEL)
    steps = seq // tile
    n_tiles = batch * steps
    cosm, sinm = _rope_tables(np.arange(N_META))
    pair_up = lambda tab: np.concatenate([tab[0::2], tab[1::2]], axis=2)
    mask, qdec, kdec = (pair_up(tab) for tab in _decay_tables(RET_CHUNK, RET_CHUNK))
    _, _, kdecm = _decay_tables(N_META, N_META)
    zero_blk = jnp.zeros_like(pool_w[0])
    pool_w = jnp.stack([jnp.block([[pool_w[2 * pp], zero_blk], [zero_blk, pool_w[2 * pp + 1]]])
                        for pp in range(len(POOL_WINDOWS) // 2)])

    mix_tile = lambda t: jnp.minimum(t, n_tiles - 1)
    ffn_tile = lambda t: jnp.maximum(t - 2, 0)

    def meta_col(j):
        return pl.BlockSpec((N_META, RET_WIDTH), lambda t: (meta_row_block, j))

    in_specs = [
        pl.BlockSpec((tile, IN_WIDTH), lambda t: (mix_tile(t), 0)),
        pl.BlockSpec((tile, D_MODEL), lambda t: (mix_tile(t), 0)),
        meta_col(1), meta_col(2), meta_col(4),
        _const_spec(cosm.shape), _const_spec(sinm.shape), _const_spec(kdecm.shape),
        _const_spec(mask.shape), _const_spec(qdec.shape), _const_spec(kdec.shape),
        _const_spec(pool_w.shape), _const_spec(pool_scale.shape), _const_spec(w_out.shape),
        _const_spec(ln2g.shape), _const_spec(ln2b.shape),
        _const_spec(wg.shape), _const_spec(wu.shape), _const_spec(wd.shape),
        _const_spec(ln3g.shape), _const_spec(ln3b.shape),
        pl.BlockSpec(memory_space=pl.ANY),
    ]
    out_shape = [
        jax.ShapeDtypeStruct((batch * seq, D_MODEL), F32),
        jax.ShapeDtypeStruct((batch, RET_HEADS, HEAD_DIM, HEAD_DIM), F32),
        jax.ShapeDtypeStruct((batch, POOL_BUF, POOL_WIDTH), F32),
        jax.ShapeDtypeStruct((tile, D_MODEL), F32),
    ]
    out_specs = [
        pl.BlockSpec((tile, D_MODEL), lambda t: (ffn_tile(t), 0)),
        pl.BlockSpec((1, RET_HEADS, HEAD_DIM, HEAD_DIM), lambda t: (mix_tile(t) // steps, 0, 0, 0)),
        pl.BlockSpec((1, POOL_BUF, POOL_WIDTH), lambda t: (mix_tile(t) // steps, 0, 0)),
        pl.BlockSpec(memory_space=pl.ANY),
    ]
    return pl.pallas_call(
        functools.partial(_prompt_mixer_ffn_kernel, steps_per_seq=steps),
        grid=(n_tiles + 2,),
        in_specs=in_specs,
        out_specs=out_specs,
        out_shape=out_shape,
        scratch_shapes=[
            pltpu.VMEM((RET_HEADS, HEAD_DIM, HEAD_DIM), F32),
            pltpu.VMEM((N_META + tile, POOL_WIDTH), F32),
            pltpu.VMEM((tile, D_MODEL), BF16),
            pltpu.VMEM((tile, D_MODEL), F32),
            pltpu.VMEM((tile, D_MODEL), BF16),
            pltpu.VMEM((tile, D_MODEL), F32),
            pltpu.VMEM((tile, D_FF), BF16),
            pltpu.VMEM((8, PAIR_WIDTH, PAIR_WIDTH), BF16),
            pltpu.VMEM((8, 2 * PAIR_WIDTH, PAIR_WIDTH), BF16),
        ],
        compiler_params=pltpu.CompilerParams(
            dimension_semantics=("arbitrary",), vmem_limit_bytes=VMEM_LIMIT_BYTES),
        name="prompt_mixer_ffn",
    )(proj, x1, proj_small, proj_small, proj_small,
      cosm, sinm, kdecm, mask, qdec, kdec, pool_w, pool_scale, w_out, ln2g, ln2b,
      wg, wu, wd, ln3g, ln3b, x2_dec)


def _decode_mixer_kernel(q_ref, k_ref, v_ref, g_ref, *rest, dec_seq):
    p_refs, rest = rest[:len(POOL_WINDOWS)], rest[len(POOL_WINDOWS):]
    (x1_ref, s0_ref, pref_ref, cos_ref, sin_ref, mask_ref, qdec_ref, kdec_ref,
     poolw_ref, pscale_ref, wout_ref, lng_ref, lnb_ref,
     o_ref, sret_ref, spool_ref, d_ref, mix_ref) = rest
    rows = q_ref.shape[0]
    nseq = rows // dec_seq
    seq_per_group = BF16_ROWS // dec_seq
    cos, sin = cos_ref[...], sin_ref[...]
    row_seq = lax.broadcasted_iota(jnp.int32, (BF16_ROWS, HEAD_DIM), 0) // dec_seq

    for h in range(RET_HEADS):
        hs = slice(h * HEAD_DIM, (h + 1) * HEAD_DIM)
        qr = _rope(q_ref[:, hs], cos, sin)
        kr = _rope(k_ref[:, hs], cos, sin)
        v = v_ref[:, hs]
        vb = v.astype(BF16)
        scores = _dot_nt(qr.astype(BF16), kr.astype(BF16)) * mask_ref[h]
        o_inner = _dot(scores.astype(BF16), vb)
        qd = qr * qdec_ref[h]
        kd = kr * kdec_ref[h]
        o_cross = []
        for grp in range(rows // BF16_ROWS):
            gr = slice(grp * BF16_ROWS, (grp + 1) * BF16_ROWS)
            qd_g = qd[gr].astype(BF16)
            kd_g = kd[gr].astype(BF16)
            v_g = v[gr]
            acc = jnp.zeros((BF16_ROWS, HEAD_DIM), F32)
            for j in range(seq_per_group):
                b = grp * seq_per_group + j
                s = s0_ref[b, h]
                acc = jnp.where(row_seq == j, _dot(qd_g, s.astype(BF16)), acc)
                v_b = jnp.where(row_seq == j, v_g, 0.0).astype(BF16)
                sret_ref[b, h] = (GAMMAS[h] ** dec_seq) * s + _dot_tn(kd_g, v_b)
            o_cross.append(acc)
        o = o_inner + jnp.concatenate(o_cross, axis=0)
        mix_ref[:, hs] = (_silu(g_ref[:, hs]) * _group_norm(o)).astype(BF16)

    for gi, (w, pg_ref) in enumerate(zip(POOL_WINDOWS, p_refs)):
        gs = slice(gi * POOL_GROUP, (gi + 1) * POOL_GROUP)
        steps = [pref_ref[j, :, gs] for j in range(POOL_BUF)]
        steps += [pg_ref[pl.ds(i, nseq, stride=dec_seq), :] for i in range(dec_seq)]
        for i in range(dec_seq):
            now = POOL_BUF + i
            wsum = steps[now]
            for back in range(1, w):
                wsum = wsum + steps[now - back]
            d_ref[gi, pl.ds(i, nseq, stride=dec_seq), :] = wsum * (1.0 / w) - steps[now]
        for j in range(POOL_BUF):
            spool_ref[j, :, gs] = steps[dec_seq + j]
        pooled = _dot(d_ref[gi].astype(BF16), poolw_ref[gi]) * pscale_ref[:, gs]
        mix_ref[:, RET_WIDTH + gi * POOL_GROUP:RET_WIDTH + (gi + 1) * POOL_GROUP] = pooled.astype(BF16)

    y = _dot(mix_ref[...], wout_ref[...])
    o_ref[...] = _layer_norm(ALPHA * x1_ref[...] + y, lng_ref[...], lnb_ref[...])


def _decode_mixer(proj, x1, state_ret, state_pool, pool_w, pool_scale, w_out, lng, lnb, nseq, dec_seq):
    assert BF16_ROWS % dec_seq == 0 and dec_seq <= POOL_BUF
    rows = DEC_SEQ_BLOCK * dec_seq
    steps = nseq // DEC_SEQ_BLOCK
    cos, sin = _rope_tables(PAST_LEN + (np.arange(rows) % dec_seq))
    mask, qdec, kdec = _decay_tables(rows, dec_seq)

    def col(j):
        return pl.BlockSpec((rows, RET_WIDTH), lambda i: (i, j))

    state_spec = pl.BlockSpec((DEC_SEQ_BLOCK, RET_HEADS, HEAD_DIM, HEAD_DIM), lambda i: (i, 0, 0, 0))
    state_pool = jnp.transpose(state_pool, (1, 0, 2))
    pool_spec = pl.BlockSpec((POOL_BUF, DEC_SEQ_BLOCK, POOL_WIDTH), lambda i: (0, i, 0))
    groups = len(POOL_WINDOWS)
    p_cols = 4 * RET_WIDTH // POOL_GROUP
    in_specs = [
        col(0), col(1), col(2), col(3),
        *[pl.BlockSpec((rows, POOL_GROUP), lambda i, gi=gi: (i, p_cols + gi)) for gi in range(groups)],
        pl.BlockSpec((rows, D_MODEL), lambda i: (i, 0)),
        state_spec, pool_spec,
        _const_spec(cos.shape), _const_spec(sin.shape),
        _const_spec(mask.shape), _const_spec(qdec.shape), _const_spec(kdec.shape),
        _const_spec(pool_w.shape), _const_spec(pool_scale.shape), _const_spec(w_out.shape),
        _const_spec(lng.shape), _const_spec(lnb.shape),
    ]
    out_shape = [
        jax.ShapeDtypeStruct((nseq * dec_seq, D_MODEL), F32),
        jax.ShapeDtypeStruct(state_ret.shape, F32),
        jax.ShapeDtypeStruct(state_pool.shape, F32),
    ]
    out_specs = [pl.BlockSpec((rows, D_MODEL), lambda i: (i, 0)), state_spec, pool_spec]
    x2, new_ret, new_pool = pl.pallas_call(
        functools.partial(_decode_mixer_kernel, dec_seq=dec_seq),
        grid=(steps,),
        in_specs=in_specs,
        out_specs=out_specs,
        out_shape=out_shape,
        scratch_shapes=[
            pltpu.VMEM((groups, rows, POOL_GROUP), F32),
            pltpu.VMEM((rows, D_MODEL), BF16),
        ],
        compiler_params=pltpu.CompilerParams(
            dimension_semantics=("arbitrary",), vmem_limit_bytes=VMEM_LIMIT_BYTES),
        name="decode_mixer",
    )(*([proj] * (4 + groups)), x1, state_ret, state_pool, cos, sin, mask, qdec, kdec,
      pool_w, pool_scale, w_out, lng, lnb)
    return x2, new_ret, jnp.transpose(new_pool, (1, 0, 2))


def kernel(x_prompt, x_sample, state_ret, state_pool, meta_tokens, ffn1_w_gate, ffn1_w_up, ffn1_w_down,
           ln1_g, ln1_b, w_in, pool_w, pool_scale, w_out, ln2_g, ln2_b, ffn2_w_gate, ffn2_w_up,
           ffn2_w_down, ln3_g, ln3_b):
    assert ffn1_w_gate.shape[0] == DEPTH == 1
    batch, seq, _ = x_prompt.shape
    nseq, dec_seq, _ = x_sample.shape
    n_dec = nseq * dec_seq
    assert n_dec % N_META == 0

    bf = lambda w: w[0].astype(BF16)
    row = lambda v: v[0].reshape(1, -1)
    pool_w_b = bf(pool_w)
    pscale, g2, b2 = row(pool_scale), row(ln2_g), row(ln2_b)

    xp = x_prompt.reshape(batch * seq, D_MODEL)
    x_small = jnp.concatenate([x_sample.reshape(n_dec, D_MODEL), meta_tokens.astype(x_prompt.dtype)], axis=0)

    x1s, projs, wg1, wu1, wd1, w_in_b = _ffn_proj_cast(
        x_small, ffn1_w_gate[0], ffn1_w_up[0], ffn1_w_down[0], row(ln1_g), row(ln1_b), w_in[0])
    x1p, projp, wg2, wu2, wd2, w_out_b = _ffn_proj_pipelined(
        xp, wg1, wu1, wd1, row(ln1_g), row(ln1_b), w_in_b,
        (ffn2_w_gate[0], ffn2_w_up[0], ffn2_w_down[0], w_out[0]), seq)
    f2 = (wg2, wu2, wd2, row(ln3_g), row(ln3_b))

    x2s, ret_s, pool_s = _decode_mixer(projs, x1s, state_ret[0], state_pool[0], pool_w_b, pscale,
                                       w_out_b, g2, b2, nseq, dec_seq)
    y_prompt, ret_p, pool_p, y_sample = _prompt_mixer_ffn(
        projp, x1p, projs, n_dec // N_META, pool_w_b, pscale, w_out_b, g2, b2, *f2, x2s, batch, seq)
    return (y_prompt.reshape(batch, seq, D_MODEL), y_sample.reshape(nseq, dec_seq, D_MODEL),
            ret_p[None], pool_p[None], ret_s[None], pool_s[None])
```

```python
import functools
import math

import jax
import jax.numpy as jnp
import numpy as np
from jax import lax
from jax.experimental import pallas as pl
from jax.experimental.pallas import tpu as pltpu

F32 = jnp.float32
BF16 = jnp.bfloat16

D_MODEL = 1024
D_FF = 2816
N_META = 16
PAST_LEN = 16384
RET_HEADS = 4
HEAD_DIM = 128
RET_WIDTH = RET_HEADS * HEAD_DIM
RET_CHUNK = 128
ROPE_THETA = 10000.0
POOL_WINDOWS = (2, 4, 8, 16)
POOL_GROUP = 128
POOL_WIDTH = POOL_GROUP * len(POOL_WINDOWS)
POOL_BUF = max(POOL_WINDOWS) - 1
IN_WIDTH = 4 * RET_WIDTH + POOL_WIDTH
DEPTH = 1
ALPHA = (2.0 * DEPTH) ** 0.25
LN_EPS = 1e-5
GN_EPS = 1e-5
QK_SCALE = HEAD_DIM ** -0.5
GAMMAS = tuple(1.0 - 2.0 ** (-5.0 - h) for h in range(RET_HEADS))

VMEM_LIMIT_BYTES = 56 * 1024 * 1024
FFN_TOKEN_TILE = 512
FFN_COL_CHUNK = 256
MIX_TOKEN_TILE = 512
DEC_SEQ_BLOCK = 32
BF16_ROWS = 16
PAIR_WIDTH = 2 * HEAD_DIM


def _layer_norm(z, g, b):
    mu = jnp.mean(z, axis=-1, keepdims=True)
    zc = z - mu
    var = jnp.mean(zc * zc, axis=-1, keepdims=True)
    return zc * lax.rsqrt(var + LN_EPS) * g + b


def _silu(x):
    return x * jax.nn.sigmoid(x)


def _dot(a, b):
    return jnp.dot(a, b, preferred_element_type=F32)


def _dot_nt(a, b):
    return lax.dot_general(a, b, (((1,), (1,)), ((), ())), preferred_element_type=F32)


def _dot_tn(a, b):
    return lax.dot_general(a, b, (((0,), (0,)), ((), ())), preferred_element_type=F32)


def _const_spec(shape):
    zeros = (0,) * len(shape)
    return pl.BlockSpec(shape, lambda *_: zeros, pipeline_mode=pl.Buffered(1))


def _ffn_proj_cast_kernel(x_ref, wg_ref, wu_ref, wd_ref, lng_ref, lnb_ref, win_ref,
                          x1_ref, proj_ref, wgb_ref, wub_ref, wdb_ref, winb_ref,
                          acc_ref, xb_ref, win_all_ref):
    k = pl.program_id(0)

    @pl.when(k == 0)
    def _start():
        xb_ref[...] = x_ref[...].astype(BF16)
        acc_ref[...] = jnp.zeros(acc_ref.shape, F32)

    wg, wu, wd = wg_ref[...].astype(BF16), wu_ref[...].astype(BF16), wd_ref[...].astype(BF16)
    wgb_ref[...] = wg
    wub_ref[...] = wu
    wdb_ref[...] = wd
    h = (_silu(_dot(xb_ref[...], wg)) * _dot(xb_ref[...], wu)).astype(BF16)
    acc_ref[...] += _dot(h, wd)

    for j in range(IN_WIDTH // FFN_COL_CHUNK):
        @pl.when(k == j)
        def _round_w_in_chunk(j=j):
            w_in = win_ref[...].astype(BF16)
            winb_ref[...] = w_in
            win_all_ref[:, j * FFN_COL_CHUNK:(j + 1) * FFN_COL_CHUNK] = w_in

    @pl.when(k == pl.num_programs(0) - 1)
    def _finish():
        x1 = _layer_norm(ALPHA * x_ref[...] + 0.5 * acc_ref[...], lng_ref[...], lnb_ref[...])
        x1_ref[...] = x1
        proj_ref[...] = _dot(x1.astype(BF16), win_all_ref[...])


def _ffn_proj_cast(x, wg, wu, wd, lng, lnb, w_in):
    n = x.shape[0]
    ck = FFN_COL_CHUNK
    steps = D_FF // ck
    in_chunks = IN_WIDTH // ck
    assert in_chunks <= steps
    col_chunk = pl.BlockSpec((D_MODEL, ck), lambda k: (0, k))
    row_chunk = pl.BlockSpec((ck, D_MODEL), lambda k: (k, 0))
    w_in_chunk = pl.BlockSpec((D_MODEL, ck), lambda k: (0, jnp.minimum(k, in_chunks - 1)))
    whole = lambda shape: pl.BlockSpec(shape, lambda k: (0,) * len(shape))
    return pl.pallas_call(
        _ffn_proj_cast_kernel,
        grid=(steps,),
        in_specs=[_const_spec(x.shape), col_chunk, col_chunk, row_chunk,
                  _const_spec(lng.shape), _const_spec(lnb.shape), w_in_chunk],
        out_specs=[whole((n, D_MODEL)), whole((n, IN_WIDTH)), col_chunk, col_chunk, row_chunk,
                   w_in_chunk],
        out_shape=[jax.ShapeDtypeStruct((n, D_MODEL), F32), jax.ShapeDtypeStruct((n, IN_WIDTH), F32),
                   jax.ShapeDtypeStruct(wg.shape, BF16), jax.ShapeDtypeStruct(wu.shape, BF16),
                   jax.ShapeDtypeStruct(wd.shape, BF16), jax.ShapeDtypeStruct(w_in.shape, BF16)],
        scratch_shapes=[pltpu.VMEM((n, D_MODEL), F32), pltpu.VMEM((n, D_MODEL), BF16),
                        pltpu.VMEM(w_in.shape, BF16)],
        compiler_params=pltpu.CompilerParams(
            dimension_semantics=("arbitrary",), vmem_limit_bytes=VMEM_LIMIT_BYTES),
        name="ffn_proj_cast",
    )(x, wg, wu, wd, lng, lnb, w_in)


def _interleave(major, starts):
    live = []
    for i, piece in enumerate(major):
        piece()
        live += [make() for make in starts.get(i, [])]
        live = [g for g in live if next(g, "done") != "done"]
    while live:
        live = [g for g in live if next(g, "done") != "done"]


def _run_all(starts):
    for i in sorted(starts):
        for make in starts[i]:
            for _ in make():
                pass


def _ffn_proj_pipelined_kernel(x_ref, wg_ref, wu_ref, wd_ref, lng_ref, lnb_ref, win_ref,
                               rope_ref, *rest, n_cast):
    cast_in, rest = rest[:n_cast], rest[n_cast:]
    x1_ref, proj_ref = rest[:2]
    cast_out, (xb_ref, z1_ref, x1b_ref, h_ref) = rest[2:2 + n_cast], rest[2 + n_cast:]
    t = pl.program_id(0)
    n_tiles = pl.num_programs(0) - 1
    tile = x_ref.shape[0]
    row_blocks = [slice(r, r + RET_CHUNK) for r in range(0, tile, RET_CHUNK)]
    proj_chunk = 2 * FFN_COL_CHUNK

    def ln1_piece(rows):
        def run():
            x1_rows = _layer_norm(z1_ref[rows, :], lng_ref[...], lnb_ref[...])
            x1_ref[rows, :] = x1_rows
            x1b_ref[rows, :] = x1_rows.astype(BF16)
            yield
        return run

    def proj_piece(nk):
        def run():
            cs = slice(nk * proj_chunk, (nk + 1) * proj_chunk)
            chunk = _dot(x1b_ref[...], win_ref[:, cs])
            if nk < 2:
                cos, sin = rope_ref[:, :HEAD_DIM], rope_ref[:, HEAD_DIM:]
                chunk = jnp.concatenate(
                    [_rope(chunk[:, h * HEAD_DIM:(h + 1) * HEAD_DIM], cos, sin)
                     for h in range(RET_HEADS)], axis=1)
            proj_ref[:, cs] = chunk
        return run

    def gate_up_piece(ck):
        def run():
            sl = slice(ck * FFN_COL_CHUNK, (ck + 1) * FFN_COL_CHUNK)
            g = _dot(xb_ref[...], wg_ref[:, sl])
            u = _dot(xb_ref[...], wu_ref[:, sl])
            h_ref[:, sl] = (_silu(g) * u).astype(BF16)
        return run

    def down_piece(nk):
        def run():
            cs = slice(nk * FFN_COL_CHUNK, (nk + 1) * FFN_COL_CHUNK)
            z1_ref[:, cs] = ALPHA * x_ref[:, cs] + 0.5 * _dot(h_ref[...], wd_ref[:, cs])
        return run

    ln1_starts = {i: [ln1_piece(rows)] for i, rows in enumerate(row_blocks)}
    proj_pieces = [proj_piece(nk) for nk in range(IN_WIDTH // proj_chunk)]

    @pl.when(t == 0)
    def _clear_pipeline():
        z1_ref[...] = jnp.zeros(z1_ref.shape, F32)

    @pl.when(t < n_tiles)
    def _steady():
        xb_ref[...] = x_ref[...].astype(BF16)
        for src_ref, dst_ref in zip(cast_in, cast_out):
            dst_ref[...] = src_ref[...].astype(BF16)
        gate_up = [gate_up_piece(ck) for ck in range(D_FF // FFN_COL_CHUNK)]
        first_proj = len(row_blocks) + 1
        major = gate_up[:first_proj]
        for i, piece in enumerate(gate_up[first_proj:]):
            major += proj_pieces[i:i + 1] + [piece]
        major += proj_pieces[len(gate_up) - first_proj:]
        _interleave(major, ln1_starts)
        for nk in range(D_MODEL // FFN_COL_CHUNK):
            down_piece(nk)()

    @pl.when(t == n_tiles)
    def _drain():
        _run_all(ln1_starts)
        for piece in proj_pieces:
            piece()


def _slab_rows(rows, max_slabs):
    for slab in range(BF16_ROWS, rows + 1, BF16_ROWS):
        if rows % slab == 0 and rows // slab <= max_slabs:
            return slab
    raise ValueError(f"no slab size for {rows} rows in {max_slabs} steps")


def _ffn_proj_pipelined(x, wg, wu, wd, lng, lnb, w_in, cast_weights, seq):
    n = x.shape[0]
    tm = FFN_TOKEN_TILE
    n_tiles = n // tm
    assert seq % tm == 0 and IN_WIDTH % (2 * FFN_COL_CHUNK) == 0 and RET_WIDTH == 2 * FFN_COL_CHUNK
    rope = np.concatenate(_rope_tables(N_META + np.arange(seq)), axis=1)
    in_tile = lambda t: (jnp.minimum(t, n_tiles - 1), 0)
    out_tile = lambda t: (jnp.maximum(t - 1, 0), 0)
    rope_tile = pl.BlockSpec((tm, 2 * HEAD_DIM), lambda t: (jnp.maximum(t - 1, 0) % (seq // tm), 0))

    def slab_spec(w):
        slab = _slab_rows(w.shape[0], n_tiles)
        last = w.shape[0] // slab - 1
        return pl.BlockSpec((slab, w.shape[1]), lambda t: (jnp.minimum(t, last), 0))

    cast_specs = [slab_spec(w) for w in cast_weights]
    return pl.pallas_call(
        functools.partial(_ffn_proj_pipelined_kernel, n_cast=len(cast_weights)),
        grid=(n_tiles + 1,),
        in_specs=[
            pl.BlockSpec((tm, D_MODEL), in_tile),
            _const_spec(wg.shape), _const_spec(wu.shape), _const_spec(wd.shape),
            _const_spec(lng.shape), _const_spec(lnb.shape), _const_spec(w_in.shape),
            rope_tile,
        ] + cast_specs,
        out_specs=[pl.BlockSpec((tm, D_MODEL), out_tile), pl.BlockSpec((tm, IN_WIDTH), out_tile)]
        + cast_specs,
        out_shape=[jax.ShapeDtypeStruct((n, D_MODEL), F32), jax.ShapeDtypeStruct((n, IN_WIDTH), F32)]
        + [jax.ShapeDtypeStruct(w.shape, BF16) for w in cast_weights],
        scratch_shapes=[
            pltpu.VMEM((tm, D_MODEL), BF16),
            pltpu.VMEM((tm, D_MODEL), F32),
            pltpu.VMEM((tm, D_MODEL), BF16),
            pltpu.VMEM((tm, D_FF), BF16),
        ],
        compiler_params=pltpu.CompilerParams(
            dimension_semantics=("arbitrary",), vmem_limit_bytes=VMEM_LIMIT_BYTES),
        name="ffn_proj_pipelined",
    )(x, wg, wu, wd, lng, lnb, w_in, rope, *cast_weights)


def _rope_tables(positions):
    half = HEAD_DIM // 2
    inv_freq = ROPE_THETA ** (-np.arange(0, HEAD_DIM, 2, dtype=np.float64) / HEAD_DIM)
    ang = np.asarray(positions, np.float64)[:, None] * inv_freq[None, :]
    cos, sin = np.cos(ang), np.sin(ang)
    assert cos.shape[1] == half
    return (np.concatenate([cos, cos], axis=1).astype(np.float32),
            np.concatenate([-sin, sin], axis=1).astype(np.float32))


def _decay_tables(chunk, seq_len):
    r = np.arange(chunk)
    seq, idx = r // seq_len, (r % seq_len).astype(np.float64)
    same = seq[:, None] == seq[None, :]
    diff = idx[:, None] - idx[None, :]
    mask, qdec, kdec = [], [], []
    for gamma in GAMMAS:
        lg = math.log(gamma)
        mask.append(np.where(same & (diff >= 0), np.exp(lg * np.maximum(diff, 0.0)), 0.0) * QK_SCALE)
        qdec.append(np.broadcast_to((np.exp(lg * (idx + 1.0)) * QK_SCALE)[:, None], (chunk, HEAD_DIM)))
        kdec.append(np.broadcast_to(np.exp(lg * (seq_len - 1.0 - idx))[:, None], (chunk, HEAD_DIM)))
    to32 = lambda t: np.stack(t).astype(np.float32)
    return to32(mask), to32(qdec), to32(kdec)


def _rope(x, cos, sin):
    return x * cos + pltpu.roll(x, HEAD_DIM // 2, 1) * sin


def _group_norm(o):
    mu = jnp.mean(o, axis=-1, keepdims=True)
    oc = o - mu
    var = jnp.mean(oc * oc, axis=-1, keepdims=True)
    return oc * lax.rsqrt(var + GN_EPS)


def _prompt_mixer_ffn_kernel(proj_ref, x1_ref,
                             km_ref, vm_ref, pm_ref, cosm_ref, sinm_ref, kdecm_ref,
                             mask_ref, qdec_ref, kdec_ref, poolw_ref, pscale_ref, wout_ref,
                             ln2g_ref, ln2b_ref, wg_ref, wu_ref, wd_ref, ln3g_ref, ln3b_ref,
                             x2dec_hbm,
                             y_ref, sret_ref, spool_ref, ydec_hbm,
                             s_ref, xp_ref, mix_ref, x2_ref, xb_ref, ypre_ref, h_ref, kb_ref, vs_ref,
                             *, steps_per_seq):
    t = pl.program_id(0)
    n_tiles = pl.num_programs(0) - 2
    c = t % steps_per_seq
    tile = proj_ref.shape[0]
    q_ref, k_ref, v_ref, g_ref, p_ref = (
        proj_ref.at[:, j * RET_WIDTH:(j + 1) * RET_WIDTH] for j in range(5))
    hist = N_META
    row_blocks = [slice(r, r + RET_CHUNK) for r in range(0, tile, RET_CHUNK)]

    @pl.when(jnp.logical_and(c == 0, t < n_tiles))
    def _init_from_meta():
        for h in range(RET_HEADS):
            hs = slice(h * HEAD_DIM, (h + 1) * HEAD_DIM)
            kr = _rope(km_ref[:, hs], cosm_ref[...], sinm_ref[...])
            kd = (kr * kdecm_ref[h]).astype(BF16)
            s_ref[h] = _dot_tn(kd, vm_ref[:, hs].astype(BF16))
        xp_ref[0:hist, :] = pm_ref[...]

    def ln3_fetch_piece(rows):
        def run():
            y_ref[rows, :] = ypre_ref[rows, :]
            yield
        return run

    def ln3_piece(rows):
        def run():
            y_ref[rows, :] = _layer_norm(y_ref[rows, :], ln3g_ref[...], ln3b_ref[...])
            yield
        return run

    state = [None] * RET_HEADS

    def retention_piece(ci, hp):
        def run():
            rows = slice(ci * RET_CHUNK, (ci + 1) * RET_CHUNK)
            pair = slice(hp * PAIR_WIDTH, (hp + 1) * PAIR_WIDTH)
            buf = ci * 2 + hp
            qr, kr = [], []
            for j in range(2):
                h = 2 * hp + j
                hs = slice(h * HEAD_DIM, (h + 1) * HEAD_DIM)
                blk = slice(j * HEAD_DIM, (j + 1) * HEAD_DIM)
                qr.append(q_ref[rows, hs])
                kr.append(k_ref[rows, hs])
                kb_ref[buf, blk, blk] = kr[j].astype(BF16)
                vs_ref[buf, blk, blk] = v_ref[rows, hs].astype(BF16)
                vs_ref[buf, PAIR_WIDTH + j * HEAD_DIM:PAIR_WIDTH + (j + 1) * HEAD_DIM, blk] = (
                    state[h].astype(BF16))
            q2 = jnp.concatenate(qr, axis=1)
            k2 = jnp.concatenate(kr, axis=1)
            scores = _dot_nt(q2.astype(BF16), kb_ref[buf])
            kd2 = (k2 * kdec_ref[hp]).astype(BF16)
            upd = _dot_tn(kd2, v_ref[rows, pair].astype(BF16))
            qd2 = (q2 * qdec_ref[hp]).astype(BF16)
            yield
            lhs = jnp.concatenate([(scores * mask_ref[hp]).astype(BF16), qd2], axis=1)
            o2 = _dot(lhs, vs_ref[buf])
            for j in range(2):
                blk = slice(j * HEAD_DIM, (j + 1) * HEAD_DIM)
                state[2 * hp + j] = (GAMMAS[2 * hp + j] ** RET_CHUNK) * state[2 * hp + j] + upd[blk, blk]
            yield
            for j in range(2):
                hs = slice((2 * hp + j) * HEAD_DIM, (2 * hp + j + 1) * HEAD_DIM)
                blk = slice(j * HEAD_DIM, (j + 1) * HEAD_DIM)
                mix_ref[rows, hs] = (_silu(g_ref[rows, hs]) * _group_norm(o2[:, blk])).astype(BF16)
        return run

    def pool_piece(pp):
        def run():
            d2 = []
            for j in range(2):
                gi = 2 * pp + j
                w = POOL_WINDOWS[gi]
                gs = slice(gi * POOL_GROUP, (gi + 1) * POOL_GROUP)
                xp_ref[hist:hist + tile, gs] = p_ref[:, gs]
                rows_all = xp_ref[:, gs]
                wsum, shift = rows_all, 1
                while shift < w:
                    wsum = wsum + pltpu.roll(wsum, shift, 0)
                    shift *= 2
                d2.append((wsum[hist:] * (1.0 / w) - rows_all[hist:]).astype(BF16))
                xp_ref[0:hist, gs] = xp_ref[tile:tile + hist, gs]
                if j == 0:
                    yield
            pair = slice(pp * PAIR_WIDTH, (pp + 1) * PAIR_WIDTH)
            pooled = _dot(jnp.concatenate(d2, axis=1), poolw_ref[pp])
            yield
            pooled = pooled * pscale_ref[:, pair]
            mix_ref[:, RET_WIDTH + pp * PAIR_WIDTH:RET_WIDTH + (pp + 1) * PAIR_WIDTH] = pooled.astype(BF16)
        return run

    def gate_up_piece(ck):
        def run():
            sl = slice(ck * FFN_COL_CHUNK, (ck + 1) * FFN_COL_CHUNK)
            g = _dot(xb_ref[...], wg_ref[:, sl])
            u = _dot(xb_ref[...], wu_ref[:, sl])
            h_ref[:, sl] = (_silu(g) * u).astype(BF16)
        return run

    def residual_piece():
        ypre_ref[...] = ALPHA * x2_ref[...]
        yield

    def w_out_piece():
        for h in range(RET_HEADS):
            s_ref[h] = state[h]
        x2_ref[...] = ALPHA * x1_ref[...] + _dot(mix_ref[...], wout_ref[...])

    def ln2_piece(rows):
        def run():
            x2_ref[rows, :] = _layer_norm(x2_ref[rows, :], ln2g_ref[...], ln2b_ref[...])
            yield
        return run

    def xb_piece(rows):
        def run():
            xb_ref[rows, :] = x2_ref[rows, :].astype(BF16)
            yield
        return run

    def down_piece(nk):
        def run():
            cs = slice(nk * FFN_COL_CHUNK, (nk + 1) * FFN_COL_CHUNK)
            ypre_ref[:, cs] = ypre_ref[:, cs] + 0.5 * _dot(h_ref[...], wd_ref[:, cs])
        return run

    def steady_body():
        starts = {}

        def start_at(i, piece):
            starts.setdefault(i, []).append(piece)

        for rows in row_blocks:
            start_at(0, ln3_fetch_piece(rows))
        start_at(1, residual_piece)
        for h in range(RET_HEADS):
            state[h] = s_ref[h]
        gate_up = [gate_up_piece(ck) for ck in range(D_FF // FFN_COL_CHUNK)]
        pieces = [(ci, hp) for ci in range(len(row_blocks)) for hp in range(RET_HEADS // 2)]
        w_out_at = len(pieces)
        for n, (ci, hp) in enumerate(pieces):
            start_at(n // 2 if n < 4 else n - 2, retention_piece(ci, hp))
        for pp in range(len(POOL_WINDOWS) // 2):
            start_at(2 * pp + 1, pool_piece(pp))
        major = gate_up[:w_out_at] + [w_out_piece] + gate_up[w_out_at:]
        for i, rows in enumerate(row_blocks):
            start_at(min(w_out_at + i, len(major) - 1), ln2_piece(rows))
        _interleave(major, starts)

        tail_starts = {i: [ln3_piece(rows), xb_piece(rows)] for i, rows in enumerate(row_blocks)}
        _interleave([down_piece(nk) for nk in range(D_MODEL // FFN_COL_CHUNK)], tail_starts)

    def drain_body():
        for rows in row_blocks:
            for make in (ln3_fetch_piece(rows), ln3_piece(rows)):
                for _ in make():
                    pass

    @pl.when(t == 0)
    def _prime_pipeline():
        pltpu.sync_copy(x2dec_hbm, x2_ref)
        xb_ref[...] = x2_ref[...].astype(BF16)
        kb_ref[...] = jnp.zeros(kb_ref.shape, BF16)
        vs_ref[...] = jnp.zeros(vs_ref.shape, BF16)
        ypre_ref[...] = jnp.zeros(ypre_ref.shape, F32)

    @pl.when(t <= n_tiles)
    def _steady():
        steady_body()

    @pl.when(t == 1)
    def _emit_decode_rows():
        pltpu.sync_copy(y_ref, ydec_hbm)

    @pl.when(t == n_tiles + 1)
    def _drain_last():
        drain_body()

    @pl.when(jnp.logical_and(c == steps_per_seq - 1, t < n_tiles))
    def _emit_state():
        sret_ref[0] = s_ref[...]
        spool_ref[0] = xp_ref[hist - POOL_BUF:hist, :]


def _prompt_mixer_ffn(proj, x1, proj_small, meta_row_block, pool_w, pool_scale, w_out, ln2g, ln2b,
                      wg, wu, wd, ln3g, ln3b, x2_dec, batch, seq):
    tile = MIX_TOKEN_TILE
    assert x2_dec.shape == (tile, D_MODEL)
    steps = seq // tile
    n_tiles = batch * steps
    cosm, sinm = _rope_tables(np.arange(N_META))
    pair_up = lambda tab: np.concatenate([tab[0::2], tab[1::2]], axis=2)
    mask, qdec, kdec = (pair_up(tab) for tab in _decay_tables(RET_CHUNK, RET_CHUNK))
    _, _, kdecm = _decay_tables(N_META, N_META)
    zero_blk = jnp.zeros_like(pool_w[0])
    pool_w = jnp.stack([jnp.block([[pool_w[2 * pp], zero_blk], [zero_blk, pool_w[2 * pp + 1]]])
                        for pp in range(len(POOL_WINDOWS) // 2)])

    mix_tile = lambda t: jnp.minimum(t, n_tiles - 1)
    ffn_tile = lambda t: jnp.maximum(t - 2, 0)

    def meta_col(j):
        return pl.BlockSpec((N_META, RET_WIDTH), lambda t: (meta_row_block, j))

    in_specs = [
        pl.BlockSpec((tile, IN_WIDTH), lambda t: (mix_tile(t), 0)),
        pl.BlockSpec((tile, D_MODEL), lambda t: (mix_tile(t), 0)),
        meta_col(1), meta_col(2), meta_col(4),
        _const_spec(cosm.shape), _const_spec(sinm.shape), _const_spec(kdecm.shape),
        _const_spec(mask.shape), _const_spec(qdec.shape), _const_spec(kdec.shape),
        _const_spec(pool_w.shape), _const_spec(pool_scale.shape), _const_spec(w_out.shape),
        _const_spec(ln2g.shape), _const_spec(ln2b.shape),
        _const_spec(wg.shape), _const_spec(wu.shape), _const_spec(wd.shape),
        _const_spec(ln3g.shape), _const_spec(ln3b.shape),
        pl.BlockSpec(memory_space=pl.ANY),
    ]
    out_shape = [
        jax.ShapeDtypeStruct((batch * seq, D_MODEL), F32),
        jax.ShapeDtypeStruct((batch, RET_HEADS, HEAD_DIM, HEAD_DIM), F32),
        jax.ShapeDtypeStruct((batch, POOL_BUF, POOL_WIDTH), F32),
        jax.ShapeDtypeStruct((tile, D_MODEL), F32),
    ]
    out_specs = [
        pl.BlockSpec((tile, D_MODEL), lambda t: (ffn_tile(t), 0)),
        pl.BlockSpec((1, RET_HEADS, HEAD_DIM, HEAD_DIM), lambda t: (mix_tile(t) // steps, 0, 0, 0)),
        pl.BlockSpec((1, POOL_BUF, POOL_WIDTH), lambda t: (mix_tile(t) // steps, 0, 0)),
        pl.BlockSpec(memory_space=pl.ANY),
    ]
    return pl.pallas_call(
        functools.partial(_prompt_mixer_ffn_kernel, steps_per_seq=steps),
        grid=(n_tiles + 2,),
        in_specs=in_specs,
        out_specs=out_specs,
        out_shape=out_shape,
        scratch_shapes=[
            pltpu.VMEM((RET_HEADS, HEAD_DIM, HEAD_DIM), F32),
            pltpu.VMEM((N_META + tile, POOL_WIDTH), F32),
            pltpu.VMEM((tile, D_MODEL), BF16),
            pltpu.VMEM((tile, D_MODEL), F32),
            pltpu.VMEM((tile, D_MODEL), BF16),
            pltpu.VMEM((tile, D_MODEL), F32),
            pltpu.VMEM((tile, D_FF), BF16),
            pltpu.VMEM((8, PAIR_WIDTH, PAIR_WIDTH), BF16),
            pltpu.VMEM((8, 2 * PAIR_WIDTH, PAIR_WIDTH), BF16),
        ],
        compiler_params=pltpu.CompilerParams(
            dimension_semantics=("arbitrary",), vmem_limit_bytes=VMEM_LIMIT_BYTES),
        name="prompt_mixer_ffn",
    )(proj, x1, proj_small, proj_small, proj_small,
      cosm, sinm, kdecm, mask, qdec, kdec, pool_w, pool_scale, w_out, ln2g, ln2b,
      wg, wu, wd, ln3g, ln3b, x2_dec)


def _decode_mixer_kernel(q_ref, k_ref, v_ref, g_ref, *rest, dec_seq):
    p_refs, rest = rest[:len(POOL_WINDOWS)], rest[len(POOL_WINDOWS):]
    (x1_ref, s0_ref, pref_ref, cos_ref, sin_ref, mask_ref, qdec_ref, kdec_ref,
     poolw_ref, pscale_ref, wout_ref, lng_ref, lnb_ref,
     o_ref, sret_ref, spool_ref, d_ref, mix_ref) = rest
    rows = q_ref.shape[0]
    nseq = rows // dec_seq
    seq_per_group = BF16_ROWS // dec_seq
    cos, sin = cos_ref[...], sin_ref[...]
    row_seq = lax.broadcasted_iota(jnp.int32, (BF16_ROWS, HEAD_DIM), 0) // dec_seq

    for h in range(RET_HEADS):
        hs = slice(h * HEAD_DIM, (h + 1) * HEAD_DIM)
        qr = _rope(q_ref[:, hs], cos, sin)
        kr = _rope(k_ref[:, hs], cos, sin)
        v = v_ref[:, hs]
        vb = v.astype(BF16)
        scores = _dot_nt(qr.astype(BF16), kr.astype(BF16)) * mask_ref[h]
        o_inner = _dot(scores.astype(BF16), vb)
        qd = qr * qdec_ref[h]
        kd = kr * kdec_ref[h]
        o_cross = []
        for grp in range(rows // BF16_ROWS):
            gr = slice(grp * BF16_ROWS, (grp + 1) * BF16_ROWS)
            qd_g = qd[gr].astype(BF16)
            kd_g = kd[gr].astype(BF16)
            v_g = v[gr]
            acc = jnp.zeros((BF16_ROWS, HEAD_DIM), F32)
            for j in range(seq_per_group):
                b = grp * seq_per_group + j
                s = s0_ref[b, h]
                acc = jnp.where(row_seq == j, _dot(qd_g, s.astype(BF16)), acc)
                v_b = jnp.where(row_seq == j, v_g, 0.0).astype(BF16)
                sret_ref[b, h] = (GAMMAS[h] ** dec_seq) * s + _dot_tn(kd_g, v_b)
            o_cross.append(acc)
        o = o_inner + jnp.concatenate(o_cross, axis=0)
        mix_ref[:, hs] = (_silu(g_ref[:, hs]) * _group_norm(o)).astype(BF16)

    for gi, (w, pg_ref) in enumerate(zip(POOL_WINDOWS, p_refs)):
        gs = slice(gi * POOL_GROUP, (gi + 1) * POOL_GROUP)
        steps = [pref_ref[j, :, gs] for j in range(POOL_BUF)]
        steps += [pg_ref[pl.ds(i, nseq, stride=dec_seq), :] for i in range(dec_seq)]
        for i in range(dec_seq):
            now = POOL_BUF + i
            wsum = steps[now]
            for back in range(1, w):
                wsum = wsum + steps[now - back]
            d_ref[gi, pl.ds(i, nseq, stride=dec_seq), :] = wsum * (1.0 / w) - steps[now]
        for j in range(POOL_BUF):
            spool_ref[j, :, gs] = steps[dec_seq + j]
        pooled = _dot(d_ref[gi].astype(BF16), poolw_ref[gi]) * pscale_ref[:, gs]
        mix_ref[:, RET_WIDTH + gi * POOL_GROUP:RET_WIDTH + (gi + 1) * POOL_GROUP] = pooled.astype(BF16)

    y = _dot(mix_ref[...], wout_ref[...])
    o_ref[...] = _layer_norm(ALPHA * x1_ref[...] + y, lng_ref[...], lnb_ref[...])


def _decode_mixer(proj, x1, state_ret, state_pool, pool_w, pool_scale, w_out, lng, lnb, nseq, dec_seq):
    assert BF16_ROWS % dec_seq == 0 and dec_seq <= POOL_BUF
    rows = DEC_SEQ_BLOCK * dec_seq
    steps = nseq // DEC_SEQ_BLOCK
    cos, sin = _rope_tables(PAST_LEN + (np.arange(rows) % dec_seq))
    mask, qdec, kdec = _decay_tables(rows, dec_seq)

    def col(j):
        return pl.BlockSpec((rows, RET_WIDTH), lambda i: (i, j))

    state_spec = pl.BlockSpec((DEC_SEQ_BLOCK, RET_HEADS, HEAD_DIM, HEAD_DIM), lambda i: (i, 0, 0, 0))
    state_pool = jnp.transpose(state_pool, (1, 0, 2))
    pool_spec = pl.BlockSpec((POOL_BUF, DEC_SEQ_BLOCK, POOL_WIDTH), lambda i: (0, i, 0))
    groups = len(POOL_WINDOWS)
    p_cols = 4 * RET_WIDTH // POOL_GROUP
    in_specs = [
        col(0), col(1), col(2), col(3),
        *[pl.BlockSpec((rows, POOL_GROUP), lambda i, gi=gi: (i, p_cols + gi)) for gi in range(groups)],
        pl.BlockSpec((rows, D_MODEL), lambda i: (i, 0)),
        state_spec, pool_spec,
        _const_spec(cos.shape), _const_spec(sin.shape),
        _const_spec(mask.shape), _const_spec(qdec.shape), _const_spec(kdec.shape),
        _const_spec(pool_w.shape), _const_spec(pool_scale.shape), _const_spec(w_out.shape),
        _const_spec(lng.shape), _const_spec(lnb.shape),
    ]
    out_shape = [
        jax.ShapeDtypeStruct((nseq * dec_seq, D_MODEL), F32),
        jax.ShapeDtypeStruct(state_ret.shape, F32),
        jax.ShapeDtypeStruct(state_pool.shape, F32),
    ]
    out_specs = [pl.BlockSpec((rows, D_MODEL), lambda i: (i, 0)), state_spec, pool_spec]
    x2, new_ret, new_pool = pl.pallas_call(
        functools.partial(_decode_mixer_kernel, dec_seq=dec_seq),
        grid=(steps,),
        in_specs=in_specs,
        out_specs=out_specs,
        out_shape=out_shape,
        scratch_shapes=[
            pltpu.VMEM((groups, rows, POOL_GROUP), F32),
            pltpu.VMEM((rows, D_MODEL), BF16),
        ],
        compiler_params=pltpu.CompilerParams(
            dimension_semantics=("arbitrary",), vmem_limit_bytes=VMEM_LIMIT_BYTES),
        name="decode_mixer",
    )(*([proj] * (4 + groups)), x1, state_ret, state_pool, cos, sin, mask, qdec, kdec,
      pool_w, pool_scale, w_out, lng, lnb)
    return x2, new_ret, jnp.transpose(new_pool, (1, 0, 2))


def kernel(x_prompt, x_sample, state_ret, state_pool, meta_tokens, ffn1_w_gate, ffn1_w_up, ffn1_w_down,
           ln1_g, ln1_b, w_in, pool_w, pool_scale, w_out, ln2_g, ln2_b, ffn2_w_gate, ffn2_w_up,
           ffn2_w_down, ln3_g, ln3_b):
    assert ffn1_w_gate.shape[0] == DEPTH == 1
    batch, seq, _ = x_prompt.shape
    nseq, dec_seq, _ = x_sample.shape
    n_dec = nseq * dec_seq
    assert n_dec % N_META == 0

    bf = lambda w: w[0].astype(BF16)
    row = lambda v: v[0].reshape(1, -1)
    pool_w_b = bf(pool_w)
    pscale, g2, b2 = row(pool_scale), row(ln2_g), row(ln2_b)

    xp = x_prompt.reshape(batch * seq, D_MODEL)
    x_small = jnp.concatenate([x_sample.reshape(n_dec, D_MODEL), meta_tokens.astype(x_prompt.dtype)], axis=0)

    x1s, projs, wg1, wu1, wd1, w_in_b = _ffn_proj_cast(
        x_small, ffn1_w_gate[0], ffn1_w_up[0], ffn1_w_down[0], row(ln1_g), row(ln1_b), w_in[0])
    x1p, projp, wg2, wu2, wd2, w_out_b = _ffn_proj_pipelined(
        xp, wg1, wu1, wd1, row(ln1_g), row(ln1_b), w_in_b,
        (ffn2_w_gate[0], ffn2_w_up[0], ffn2_w_down[0], w_out[0]), seq)
    f2 = (wg2, wu2, wd2, row(ln3_g), row(ln3_b))

    x2s, ret_s, pool_s = _decode_mixer(projs, x1s, state_ret[0], state_pool[0], pool_w_b, pscale,
                                       w_out_b, g2, b2, nseq, dec_seq)
    y_prompt, ret_p, pool_p, y_sample = _prompt_mixer_ffn(
        projp, x1p, projs, n_dec // N_META, pool_w_b, pscale, w_out_b, g2, b2, *f2, x2s, batch, seq)
    return (y_prompt.reshape(batch, seq, D_MODEL), y_sample.reshape(nseq, dec_seq, D_MODEL),
            ret_p[None], pool_p[None], ret_s[None], pool_s[None])
```

```python
import functools
import math

import jax
import jax.numpy as jnp
import numpy as np
from jax import lax
from jax.experimental import pallas as pl
from jax.experimental.pallas import tpu as pltpu

F32 = jnp.float32
BF16 = jnp.bfloat16

D_MODEL = 1024
D_FF = 2816
N_META = 16
PAST_LEN = 16384
RET_HEADS = 4
HEAD_DIM = 128
RET_WIDTH = RET_HEADS * HEAD_DIM
RET_CHUNK = 128
ROPE_THETA = 10000.0
POOL_WINDOWS = (2, 4, 8, 16)
POOL_GROUP = 128
POOL_WIDTH = POOL_GROUP * len(POOL_WINDOWS)
POOL_BUF = max(POOL_WINDOWS) - 1
IN_WIDTH = 4 * RET_WIDTH + POOL_WIDTH
DEPTH = 1
ALPHA = (2.0 * DEPTH) ** 0.25
LN_EPS = 1e-5
GN_EPS = 1e-5
QK_SCALE = HEAD_DIM ** -0.5
GAMMAS = tuple(1.0 - 2.0 ** (-5.0 - h) for h in range(RET_HEADS))

VMEM_LIMIT_BYTES = 56 * 1024 * 1024
FFN_TOKEN_TILE = 512
FFN_COL_CHUNK = 256
MIX_TOKEN_TILE = 512
DEC_SEQ_BLOCK = 32
BF16_ROWS = 16
PAIR_WIDTH = 2 * HEAD_DIM


def _layer_norm(z, g, b):
    mu = jnp.mean(z, axis=-1, keepdims=True)
    zc = z - mu
    var = jnp.mean(zc * zc, axis=-1, keepdims=True)
    return zc * lax.rsqrt(var + LN_EPS) * g + b


def _silu(x):
    return x * jax.nn.sigmoid(x)


def _dot(a, b):
    return jnp.dot(a, b, preferred_element_type=F32)


def _dot_nt(a, b):
    return lax.dot_general(a, b, (((1,), (1,)), ((), ())), preferred_element_type=F32)


def _dot_tn(a, b):
    return lax.dot_general(a, b, (((0,), (0,)), ((), ())), preferred_element_type=F32)


def _const_spec(shape):
    zeros = (0,) * len(shape)
    return pl.BlockSpec(shape, lambda *_: zeros, pipeline_mode=pl.Buffered(1))


def _ffn_proj_cast_kernel(x_ref, wg_ref, wu_ref, wd_ref, lng_ref, lnb_ref, win_ref,
                          x1_ref, proj_ref, wgb_ref, wub_ref, wdb_ref, winb_ref,
                          acc_ref, xb_ref, win_all_ref):
    k = pl.program_id(0)

    @pl.when(k == 0)
    def _start():
        xb_ref[...] = x_ref[...].astype(BF16)
        acc_ref[...] = jnp.zeros(acc_ref.shape, F32)

    wg, wu, wd = wg_ref[...].astype(BF16), wu_ref[...].astype(BF16), wd_ref[...].astype(BF16)
    wgb_ref[...] = wg
    wub_ref[...] = wu
    wdb_ref[...] = wd
    h = (_silu(_dot(xb_ref[...], wg)) * _dot(xb_ref[...], wu)).astype(BF16)
    acc_ref[...] += _dot(h, wd)

    for j in range(IN_WIDTH // FFN_COL_CHUNK):
        @pl.when(k == j)
        def _round_w_in_chunk(j=j):
            w_in = win_ref[...].astype(BF16)
            winb_ref[...] = w_in
            win_all_ref[:, j * FFN_COL_CHUNK:(j + 1) * FFN_COL_CHUNK] = w_in

    @pl.when(k == pl.num_programs(0) - 1)
    def _finish():
        x1 = _layer_norm(ALPHA * x_ref[...] + 0.5 * acc_ref[...], lng_ref[...], lnb_ref[...])
        x1_ref[...] = x1
        proj_ref[...] = _dot(x1.astype(BF16), win_all_ref[...])


def _ffn_proj_cast(x, wg, wu, wd, lng, lnb, w_in):
    n = x.shape[0]
    ck = FFN_COL_CHUNK
    steps = D_FF // ck
    in_chunks = IN_WIDTH // ck
    assert in_chunks <= steps
    col_chunk = pl.BlockSpec((D_MODEL, ck), lambda k: (0, k))
    row_chunk = pl.BlockSpec((ck, D_MODEL), lambda k: (k, 0))
    w_in_chunk = pl.BlockSpec((D_MODEL, ck), lambda k: (0, jnp.minimum(k, in_chunks - 1)))
    whole = lambda shape: pl.BlockSpec(shape, lambda k: (0,) * len(shape))
    return pl.pallas_call(
        _ffn_proj_cast_kernel,
        grid=(steps,),
        in_specs=[_const_spec(x.shape), col_chunk, col_chunk, row_chunk,
                  _const_spec(lng.shape), _const_spec(lnb.shape), w_in_chunk],
        out_specs=[whole((n, D_MODEL)), whole((n, IN_WIDTH)), col_chunk, col_chunk, row_chunk,
                   w_in_chunk],
        out_shape=[jax.ShapeDtypeStruct((n, D_MODEL), F32), jax.ShapeDtypeStruct((n, IN_WIDTH), F32),
                   jax.ShapeDtypeStruct(wg.shape, BF16), jax.ShapeDtypeStruct(wu.shape, BF16),
                   jax.ShapeDtypeStruct(wd.shape, BF16), jax.ShapeDtypeStruct(w_in.shape, BF16)],
        scratch_shapes=[pltpu.VMEM((n, D_MODEL), F32), pltpu.VMEM((n, D_MODEL), BF16),
                        pltpu.VMEM(w_in.shape, BF16)],
        compiler_params=pltpu.CompilerParams(
            dimension_semantics=("arbitrary",), vmem_limit_bytes=VMEM_LIMIT_BYTES),
        name="ffn_proj_cast",
    )(x, wg, wu, wd, lng, lnb, w_in)


def _interleave(major, starts):
    live = []
    for i, piece in enumerate(major):
        piece()
        live += [make() for make in starts.get(i, [])]
        live = [g for g in live if next(g, "done") != "done"]
    while live:
        live = [g for g in live if next(g, "done") != "done"]


def _run_all(starts):
    for i in sorted(starts):
        for make in starts[i]:
            for _ in make():
                pass


def _ffn_proj_pipelined_kernel(x_ref, wg_ref, wu_ref, wd_ref, lng_ref, lnb_ref, win_ref,
                               rope_ref, *rest, n_cast):
    cast_in, rest = rest[:n_cast], rest[n_cast:]
    x1_ref, proj_ref = rest[:2]
    cast_out, (xb_ref, z1_ref, x1b_ref, h_ref) = rest[2:2 + n_cast], rest[2 + n_cast:]
    t = pl.program_id(0)
    n_tiles = pl.num_programs(0) - 1
    tile = x_ref.shape[0]
    row_blocks = [slice(r, r + RET_CHUNK) for r in range(0, tile, RET_CHUNK)]
    proj_chunk = 2 * FFN_COL_CHUNK

    def ln1_piece(rows):
        def run():
            x1_rows = _layer_norm(z1_ref[rows, :], lng_ref[...], lnb_ref[...])
            x1_ref[rows, :] = x1_rows
            x1b_ref[rows, :] = x1_rows.astype(BF16)
            yield
        return run

    def proj_piece(nk):
        def run():
            cs = slice(nk * proj_chunk, (nk + 1) * proj_chunk)
            chunk = _dot(x1b_ref[...], win_ref[:, cs])
            if nk < 2:
                cos, sin = rope_ref[:, :HEAD_DIM], rope_ref[:, HEAD_DIM:]
                chunk = jnp.concatenate(
                    [_rope(chunk[:, h * HEAD_DIM:(h + 1) * HEAD_DIM], cos, sin)
                     for h in range(RET_HEADS)], axis=1)
            proj_ref[:, cs] = chunk
        return run

    def gate_up_piece(ck):
        def run():
            sl = slice(ck * FFN_COL_CHUNK, (ck + 1) * FFN_COL_CHUNK)
            g = _dot(xb_ref[...], wg_ref[:, sl])
            u = _dot(xb_ref[...], wu_ref[:, sl])
            h_ref[:, sl] = (_silu(g) * u).astype(BF16)
        return run

    def down_piece(nk):
        def run():
            cs = slice(nk * FFN_COL_CHUNK, (nk + 1) * FFN_COL_CHUNK)
            z1_ref[:, cs] = ALPHA * x_ref[:, cs] + 0.5 * _dot(h_ref[...], wd_ref[:, cs])
        return run

    ln1_starts = {i: [ln1_piece(rows)] for i, rows in enumerate(row_blocks)}
    proj_pieces = [proj_piece(nk) for nk in range(IN_WIDTH // proj_chunk)]

    @pl.when(t == 0)
    def _clear_pipeline():
        z1_ref[...] = jnp.zeros(z1_ref.shape, F32)

    @pl.when(t < n_tiles)
    def _steady():
        xb_ref[...] = x_ref[...].astype(BF16)
        for src_ref, dst_ref in zip(cast_in, cast_out):
            dst_ref[...] = src_ref[...].astype(BF16)
        gate_up = [gate_up_piece(ck) for ck in range(D_FF // FFN_COL_CHUNK)]
        first_proj = len(row_blocks) + 1
        major = gate_up[:first_proj]
        for i, piece in enumerate(gate_up[first_proj:]):
            major += proj_pieces[i:i + 1] + [piece]
        major += proj_pieces[len(gate_up) - first_proj:]
        _interleave(major, ln1_starts)
        for nk in range(D_MODEL // FFN_COL_CHUNK):
            down_piece(nk)()

    @pl.when(t == n_tiles)
    def _drain():
        _run_all(ln1_starts)
        for piece in proj_pieces:
            piece()


def _slab_rows(rows, max_slabs):
    for slab in range(BF16_ROWS, rows + 1, BF16_ROWS):
        if rows % slab == 0 and rows // slab <= max_slabs:
            return slab
    raise ValueError(f"no slab size for {rows} rows in {max_slabs} steps")


def _ffn_proj_pipelined(x, wg, wu, wd, lng, lnb, w_in, cast_weights, seq):
    n = x.shape[0]
    tm = FFN_TOKEN_TILE
    n_tiles = n // tm
    assert seq % tm == 0 and IN_WIDTH % (2 * FFN_COL_CHUNK) == 0 and RET_WIDTH == 2 * FFN_COL_CHUNK
    rope = np.concatenate(_rope_tables(N_META + np.arange(seq)), axis=1)
    in_tile = lambda t: (jnp.minimum(t, n_tiles - 1), 0)
    out_tile = lambda t: (jnp.maximum(t - 1, 0), 0)
    rope_tile = pl.BlockSpec((tm, 2 * HEAD_DIM), lambda t: (jnp.maximum(t - 1, 0) % (seq // tm), 0))

    def slab_spec(w):
        slab = _slab_rows(w.shape[0], n_tiles)
        last = w.shape[0] // slab - 1
        return pl.BlockSpec((slab, w.shape[1]), lambda t: (jnp.minimum(t, last), 0))

    cast_specs = [slab_spec(w) for w in cast_weights]
    return pl.pallas_call(
        functools.partial(_ffn_proj_pipelined_kernel, n_cast=len(cast_weights)),
        grid=(n_tiles + 1,),
        in_specs=[
            pl.BlockSpec((tm, D_MODEL), in_tile),
            _const_spec(wg.shape), _const_spec(wu.shape), _const_spec(wd.shape),
            _const_spec(lng.shape), _const_spec(lnb.shape), _const_spec(w_in.shape),
            rope_tile,
        ] + cast_specs,
        out_specs=[pl.BlockSpec((tm, D_MODEL), out_tile), pl.BlockSpec((tm, IN_WIDTH), out_tile)]
        + cast_specs,
        out_shape=[jax.ShapeDtypeStruct((n, D_MODEL), F32), jax.ShapeDtypeStruct((n, IN_WIDTH), F32)]
        + [jax.ShapeDtypeStruct(w.shape, BF16) for w in cast_weights],
        scratch_shapes=[
            pltpu.VMEM((tm, D_MODEL), BF16),
            pltpu.VMEM((tm, D_MODEL), F32),
            pltpu.VMEM((tm, D_MODEL), BF16),
            pltpu.VMEM((tm, D_FF), BF16),
        ],
        compiler_params=pltpu.CompilerParams(
            dimension_semantics=("arbitrary",), vmem_limit_bytes=VMEM_LIMIT_BYTES),
        name="ffn_proj_pipelined",
    )(x, wg, wu, wd, lng, lnb, w_in, rope, *cast_weights)


def _rope_tables(positions):
    half = HEAD_DIM // 2
    inv_freq = ROPE_THETA ** (-np.arange(0, HEAD_DIM, 2, dtype=np.float64) / HEAD_DIM)
    ang = np.asarray(positions, np.float64)[:, None] * inv_freq[None, :]
    cos, sin = np.cos(ang), np.sin(ang)
    assert cos.shape[1] == half
    return (np.concatenate([cos, cos], axis=1).astype(np.float32),
            np.concatenate([-sin, sin], axis=1).astype(np.float32))


def _decay_tables(chunk, seq_len):
    r = np.arange(chunk)
    seq, idx = r // seq_len, (r % seq_len).astype(np.float64)
    same = seq[:, None] == seq[None, :]
    diff = idx[:, None] - idx[None, :]
    mask, qdec, kdec = [], [], []
    for gamma in GAMMAS:
        lg = math.log(gamma)
        mask.append(np.where(same & (diff >= 0), np.exp(lg * np.maximum(diff, 0.0)), 0.0) * QK_SCALE)
        qdec.append(np.broadcast_to((np.exp(lg * (idx + 1.0)) * QK_SCALE)[:, None], (chunk, HEAD_DIM)))
        kdec.append(np.broadcast_to(np.exp(lg * (seq_len - 1.0 - idx))[:, None], (chunk, HEAD_DIM)))
    to32 = lambda t: np.stack(t).astype(np.float32)
    return to32(mask), to32(qdec), to32(kdec)


def _rope(x, cos, sin):
    return x * cos + pltpu.roll(x, HEAD_DIM // 2, 1) * sin


def _group_norm(o):
    mu = jnp.mean(o, axis=-1, keepdims=True)
    oc = o - mu
    var = jnp.mean(oc * oc, axis=-1, keepdims=True)
    return oc * lax.rsqrt(var + GN_EPS)


def _prompt_mixer_ffn_kernel(proj_ref, x1_ref,
                             km_ref, vm_ref, pm_ref, cosm_ref, sinm_ref, kdecm_ref,
                             mask_ref, qdec_ref, kdec_ref, poolw_ref, pscale_ref, wout_ref,
                             ln2g_ref, ln2b_ref, wg_ref, wu_ref, wd_ref, ln3g_ref, ln3b_ref,
                             x2dec_hbm,
                             y_ref, sret_ref, spool_ref, ydec_hbm,
                             s_ref, xp_ref, mix_ref, x2_ref, xb_ref, ypre_ref, h_ref, kb_ref, vs_ref,
                             *, steps_per_seq):
    t = pl.program_id(0)
    n_tiles = pl.num_programs(0) - 2
    c = t % steps_per_seq
    tile = proj_ref.shape[0]
    q_ref, k_ref, v_ref, g_ref, p_ref = (
        proj_ref.at[:, j * RET_WIDTH:(j + 1) * RET_WIDTH] for j in range(5))
    hist = N_META
    row_blocks = [slice(r, r + RET_CHUNK) for r in range(0, tile, RET_CHUNK)]

    @pl.when(jnp.logical_and(c == 0, t < n_tiles))
    def _init_from_meta():
        for h in range(RET_HEADS):
            hs = slice(h * HEAD_DIM, (h + 1) * HEAD_DIM)
            kr = _rope(km_ref[:, hs], cosm_ref[...], sinm_ref[...])
            kd = (kr * kdecm_ref[h]).astype(BF16)
            s_ref[h] = _dot_tn(kd, vm_ref[:, hs].astype(BF16))
        xp_ref[0:hist, :] = pm_ref[...]

    def ln3_fetch_piece(rows):
        def run():
            y_ref[rows, :] = ypre_ref[rows, :]
            yield
        return run

    def ln3_piece(rows):
        def run():
            y_ref[rows, :] = _layer_norm(y_ref[rows, :], ln3g_ref[...], ln3b_ref[...])
            yield
        return run

    state = [None] * RET_HEADS

    def retention_piece(ci, hp):
        def run():
            rows = slice(ci * RET_CHUNK, (ci + 1) * RET_CHUNK)
            pair = slice(hp * PAIR_WIDTH, (hp + 1) * PAIR_WIDTH)
            buf = ci * 2 + hp
            qr, kr = [], []
            for j in range(2):
                h = 2 * hp + j
                hs = slice(h * HEAD_DIM, (h + 1) * HEAD_DIM)
                blk = slice(j * HEAD_DIM, (j + 1) * HEAD_DIM)
                qr.append(q_ref[rows, hs])
                kr.append(k_ref[rows, hs])
                kb_ref[buf, blk, blk] = kr[j].astype(BF16)
                vs_ref[buf, blk, blk] = v_ref[rows, hs].astype(BF16)
                vs_ref[buf, PAIR_WIDTH + j * HEAD_DIM:PAIR_WIDTH + (j + 1) * HEAD_DIM, blk] = (
                    state[h].astype(BF16))
            q2 = jnp.concatenate(qr, axis=1)
            k2 = jnp.concatenate(kr, axis=1)
            scores = _dot_nt(q2.astype(BF16), kb_ref[buf])
            upd = _dot((k2 * kdec_ref[hp]).T.astype(BF16), v_ref[rows, pair].astype(BF16))
            qd2 = (q2 * qdec_ref[hp]).astype(BF16)
            yield
            lhs = jnp.concatenate([(scores * mask_ref[hp]).astype(BF16), qd2], axis=1)
            o2 = _dot(lhs, vs_ref[buf])
            for j in range(2):
                blk = slice(j * HEAD_DIM, (j + 1) * HEAD_DIM)
                state[2 * hp + j] = (GAMMAS[2 * hp + j] ** RET_CHUNK) * state[2 * hp + j] + upd[blk, blk]
            yield
            for j in range(2):
                hs = slice((2 * hp + j) * HEAD_DIM, (2 * hp + j + 1) * HEAD_DIM)
                blk = slice(j * HEAD_DIM, (j + 1) * HEAD_DIM)
                mix_ref[rows, hs] = (_silu(g_ref[rows, hs]) * _group_norm(o2[:, blk])).astype(BF16)
        return run

    def pool_piece(pp):
        def run():
            d2 = []
            for j in range(2):
                gi = 2 * pp + j
                w = POOL_WINDOWS[gi]
                gs = slice(gi * POOL_GROUP, (gi + 1) * POOL_GROUP)
                xp_ref[hist:hist + tile, gs] = p_ref[:, gs]
                rows_all = xp_ref[:, gs]
                wsum, shift = rows_all, 1
                while shift < w:
                    wsum = wsum + pltpu.roll(wsum, shift, 0)
                    shift *= 2
                d2.append((wsum[hist:] * (1.0 / w) - rows_all[hist:]).astype(BF16))
                xp_ref[0:hist, gs] = xp_ref[tile:tile + hist, gs]
                if j == 0:
                    yield
            pair = slice(pp * PAIR_WIDTH, (pp + 1) * PAIR_WIDTH)
            pooled = _dot(jnp.concatenate(d2, axis=1), poolw_ref[pp])
            yield
            pooled = pooled * pscale_ref[:, pair]
            mix_ref[:, RET_WIDTH + pp * PAIR_WIDTH:RET_WIDTH + (pp + 1) * PAIR_WIDTH] = pooled.astype(BF16)
        return run

    def gate_up_piece(ck):
        def run():
            sl = slice(ck * FFN_COL_CHUNK, (ck + 1) * FFN_COL_CHUNK)
            g = _dot(xb_ref[...], wg_ref[:, sl])
            u = _dot(xb_ref[...], wu_ref[:, sl])
            h_ref[:, sl] = (_silu(g) * u).astype(BF16)
        return run

    def residual_piece():
        ypre_ref[...] = ALPHA * x2_ref[...]
        yield

    def w_out_piece():
        for h in range(RET_HEADS):
            s_ref[h] = state[h]
        x2_ref[...] = ALPHA * x1_ref[...] + _dot(mix_ref[...], wout_ref[...])

    def ln2_piece(rows):
        def run():
            x2_ref[rows, :] = _layer_norm(x2_ref[rows, :], ln2g_ref[...], ln2b_ref[...])
            yield
        return run

    def xb_piece(rows):
        def run():
            xb_ref[rows, :] = x2_ref[rows, :].astype(BF16)
            yield
        return run

    def down_piece(nk):
        def run():
            cs = slice(nk * FFN_COL_CHUNK, (nk + 1) * FFN_COL_CHUNK)
            ypre_ref[:, cs] = ypre_ref[:, cs] + 0.5 * _dot(h_ref[...], wd_ref[:, cs])
        return run

    def steady_body():
        starts = {}

        def start_at(i, piece):
            starts.setdefault(i, []).append(piece)

        for rows in row_blocks:
            start_at(0, ln3_fetch_piece(rows))
        start_at(1, residual_piece)
        for h in range(RET_HEADS):
            state[h] = s_ref[h]
        gate_up = [gate_up_piece(ck) for ck in range(D_FF // FFN_COL_CHUNK)]
        pieces = [(ci, hp) for ci in range(len(row_blocks)) for hp in range(RET_HEADS // 2)]
        w_out_at = len(pieces)
        for n, (ci, hp) in enumerate(pieces):
            start_at(n // 2 if n < 4 else n - 2, retention_piece(ci, hp))
        for pp in range(len(POOL_WINDOWS) // 2):
            start_at(2 * pp + 1, pool_piece(pp))
        major = gate_up[:w_out_at] + [w_out_piece] + gate_up[w_out_at:]
        for i, rows in enumerate(row_blocks):
            start_at(min(w_out_at + i, len(major) - 1), ln2_piece(rows))
        _interleave(major, starts)

        tail_starts = {i: [ln3_piece(rows), xb_piece(rows)] for i, rows in enumerate(row_blocks)}
        _interleave([down_piece(nk) for nk in range(D_MODEL // FFN_COL_CHUNK)], tail_starts)

    def drain_body():
        for rows in row_blocks:
            for make in (ln3_fetch_piece(rows), ln3_piece(rows)):
                for _ in make():
                    pass

    @pl.when(t == 0)
    def _prime_pipeline():
        pltpu.sync_copy(x2dec_hbm, x2_ref)
        xb_ref[...] = x2_ref[...].astype(BF16)
        kb_ref[...] = jnp.zeros(kb_ref.shape, BF16)
        vs_ref[...] = jnp.zeros(vs_ref.shape, BF16)
        ypre_ref[...] = jnp.zeros(ypre_ref.shape, F32)

    @pl.when(t <= n_tiles)
    def _steady():
        steady_body()

    @pl.when(t == 1)
    def _emit_decode_rows():
        pltpu.sync_copy(y_ref, ydec_hbm)

    @pl.when(t == n_tiles + 1)
    def _drain_last():
        drain_body()

    @pl.when(jnp.logical_and(c == steps_per_seq - 1, t < n_tiles))
    def _emit_state():
        sret_ref[0] = s_ref[...]
        spool_ref[0] = xp_ref[hist - POOL_BUF:hist, :]


def _prompt_mixer_ffn(proj, x1, proj_small, meta_row_block, pool_w, pool_scale, w_out, ln2g, ln2b,
                      wg, wu, wd, ln3g, ln3b, x2_dec, batch, seq):
    tile = MIX_TOKEN_TILE
    assert x2_dec.shape == (tile, D_MODEL)
    steps = seq // tile
    n_tiles = batch * steps
    cosm, sinm = _rope_tables(np.arange(N_META))
    pair_up = lambda tab: np.concatenate([tab[0::2], tab[1::2]], axis=2)
    mask, qdec, kdec = (pair_up(tab) for tab in _decay_tables(RET_CHUNK, RET_CHUNK))
    _, _, kdecm = _decay_tables(N_META, N_META)
    zero_blk = jnp.zeros_like(pool_w[0])
    pool_w = jnp.stack([jnp.block([[pool_w[2 * pp], zero_blk], [zero_blk, pool_w[2 * pp + 1]]])
                        for pp in range(len(POOL_WINDOWS) // 2)])

    mix_tile = lambda t: jnp.minimum(t, n_tiles - 1)
    ffn_tile = lambda t: jnp.maximum(t - 2, 0)

    def meta_col(j):
        return pl.BlockSpec((N_META, RET_WIDTH), lambda t: (meta_row_block, j))

    in_specs = [
        pl.BlockSpec((tile, IN_WIDTH), lambda t: (mix_tile(t), 0)),
        pl.BlockSpec((tile, D_MODEL), lambda t: (mix_tile(t), 0)),
        meta_col(1), meta_col(2), meta_col(4),
        _const_spec(cosm.shape), _const_spec(sinm.shape), _const_spec(kdecm.shape),
        _const_spec(mask.shape), _const_spec(qdec.shape), _const_spec(kdec.shape),
        _const_spec(pool_w.shape), _const_spec(pool_scale.shape), _const_spec(w_out.shape),
        _const_spec(ln2g.shape), _const_spec(ln2b.shape),
        _const_spec(wg.shape), _const_spec(wu.shape), _const_spec(wd.shape),
        _const_spec(ln3g.shape), _const_spec(ln3b.shape),
        pl.BlockSpec(memory_space=pl.ANY),
    ]
    out_shape = [
        jax.ShapeDtypeStruct((batch * seq, D_MODEL), F32),
        jax.ShapeDtypeStruct((batch, RET_HEADS, HEAD_DIM, HEAD_DIM), F32),
        jax.ShapeDtypeStruct((batch, POOL_BUF, POOL_WIDTH), F32),
        jax.ShapeDtypeStruct((tile, D_MODEL), F32),
    ]
    out_specs = [
        pl.BlockSpec((tile, D_MODEL), lambda t: (ffn_tile(t), 0)),
        pl.BlockSpec((1, RET_HEADS, HEAD_DIM, HEAD_DIM), lambda t: (mix_tile(t) // steps, 0, 0, 0)),
        pl.BlockSpec((1, POOL_BUF, POOL_WIDTH), lambda t: (mix_tile(t) // steps, 0, 0)),
        pl.BlockSpec(memory_space=pl.ANY),
    ]
    return pl.pallas_call(
        functools.partial(_prompt_mixer_ffn_kernel, steps_per_seq=steps),
        grid=(n_tiles + 2,),
        in_specs=in_specs,
        out_specs=out_specs,
        out_shape=out_shape,
        scratch_shapes=[
            pltpu.VMEM((RET_HEADS, HEAD_DIM, HEAD_DIM), F32),
            pltpu.VMEM((N_META + tile, POOL_WIDTH), F32),
            pltpu.VMEM((tile, D_MODEL), BF16),
            pltpu.VMEM((tile, D_MODEL), F32),
            pltpu.VMEM((tile, D_MODEL), BF16),
            pltpu.VMEM((tile, D_MODEL), F32),
            pltpu.VMEM((tile, D_FF), BF16),
            pltpu.VMEM((8, PAIR_WIDTH, PAIR_WIDTH), BF16),
            pltpu.VMEM((8, 2 * PAIR_WIDTH, PAIR_WIDTH), BF16),
        ],
        compiler_params=pltpu.CompilerParams(
            dimension_semantics=("arbitrary",), vmem_limit_bytes=VMEM_LIMIT_BYTES),
        name="prompt_mixer_ffn",
    )(proj, x1, proj_small, proj_small, proj_small,
      cosm, sinm, kdecm, mask, qdec, kdec, pool_w, pool_scale, w_out, ln2g, ln2b,
      wg, wu, wd, ln3g, ln3b, x2_dec)


def _decode_mixer_kernel(q_ref, k_ref, v_ref, g_ref, *rest, dec_seq):
    p_refs, rest = rest[:len(POOL_WINDOWS)], rest[len(POOL_WINDOWS):]
    (x1_ref, s0_ref, pref_ref, cos_ref, sin_ref, mask_ref, qdec_ref, kdec_ref,
     poolw_ref, pscale_ref, wout_ref, lng_ref, lnb_ref,
     o_ref, sret_ref, spool_ref, d_ref, mix_ref) = rest
    rows = q_ref.shape[0]
    nseq = rows // dec_seq
    seq_per_group = BF16_ROWS // dec_seq
    cos, sin = cos_ref[...], sin_ref[...]
    row_seq = lax.broadcasted_iota(jnp.int32, (BF16_ROWS, HEAD_DIM), 0) // dec_seq

    for h in range(RET_HEADS):
        hs = slice(h * HEAD_DIM, (h + 1) * HEAD_DIM)
        qr = _rope(q_ref[:, hs], cos, sin)
        kr = _rope(k_ref[:, hs], cos, sin)
        v = v_ref[:, hs]
        vb = v.astype(BF16)
        scores = _dot_nt(qr.astype(BF16), kr.astype(BF16)) * mask_ref[h]
        o_inner = _dot(scores.astype(BF16), vb)
        qd = qr * qdec_ref[h]
        kd = kr * kdec_ref[h]
        o_cross = []
        for grp in range(rows // BF16_ROWS):
            gr = slice(grp * BF16_ROWS, (grp + 1) * BF16_ROWS)
            qd_g = qd[gr].astype(BF16)
            kd_g = kd[gr].astype(BF16)
            v_g = v[gr]
            acc = jnp.zeros((BF16_ROWS, HEAD_DIM), F32)
            for j in range(seq_per_group):
                b = grp * seq_per_group + j
                s = s0_ref[b, h]
                acc = jnp.where(row_seq == j, _dot(qd_g, s.astype(BF16)), acc)
                v_b = jnp.where(row_seq == j, v_g, 0.0).astype(BF16)
                sret_ref[b, h] = (GAMMAS[h] ** dec_seq) * s + _dot_tn(kd_g, v_b)
            o_cross.append(acc)
        o = o_inner + jnp.concatenate(o_cross, axis=0)
        mix_ref[:, hs] = (_silu(g_ref[:, hs]) * _group_norm(o)).astype(BF16)

    for gi, (w, pg_ref) in enumerate(zip(POOL_WINDOWS, p_refs)):
        gs = slice(gi * POOL_GROUP, (gi + 1) * POOL_GROUP)
        steps = [pref_ref[j, :, gs] for j in range(POOL_BUF)]
        steps += [pg_ref[pl.ds(i, nseq, stride=dec_seq), :] for i in range(dec_seq)]
        for i in range(dec_seq):
            now = POOL_BUF + i
            wsum = steps[now]
            for back in range(1, w):
                wsum = wsum + steps[now - back]
            d_ref[gi, pl.ds(i, nseq, stride=dec_seq), :] = wsum * (1.0 / w) - steps[now]
        for j in range(POOL_BUF):
            spool_ref[j, :, gs] = steps[dec_seq + j]
        pooled = _dot(d_ref[gi].astype(BF16), poolw_ref[gi]) * pscale_ref[:, gs]
        mix_ref[:, RET_WIDTH + gi * POOL_GROUP:RET_WIDTH + (gi + 1) * POOL_GROUP] = pooled.astype(BF16)

    y = _dot(mix_ref[...], wout_ref[...])
    o_ref[...] = _layer_norm(ALPHA * x1_ref[...] + y, lng_ref[...], lnb_ref[...])


def _decode_mixer(proj, x1, state_ret, state_pool, pool_w, pool_scale, w_out, lng, lnb, nseq, dec_seq):
    assert BF16_ROWS % dec_seq == 0 and dec_seq <= POOL_BUF
    rows = DEC_SEQ_BLOCK * dec_seq
    steps = nseq // DEC_SEQ_BLOCK
    cos, sin = _rope_tables(PAST_LEN + (np.arange(rows) % dec_seq))
    mask, qdec, kdec = _decay_tables(rows, dec_seq)

    def col(j):
        return pl.BlockSpec((rows, RET_WIDTH), lambda i: (i, j))

    state_spec = pl.BlockSpec((DEC_SEQ_BLOCK, RET_HEADS, HEAD_DIM, HEAD_DIM), lambda i: (i, 0, 0, 0))
    state_pool = jnp.transpose(state_pool, (1, 0, 2))
    pool_spec = pl.BlockSpec((POOL_BUF, DEC_SEQ_BLOCK, POOL_WIDTH), lambda i: (0, i, 0))
    groups = len(POOL_WINDOWS)
    p_cols = 4 * RET_WIDTH // POOL_GROUP
    in_specs = [
        col(0), col(1), col(2), col(3),
        *[pl.BlockSpec((rows, POOL_GROUP), lambda i, gi=gi: (i, p_cols + gi)) for gi in range(groups)],
        pl.BlockSpec((rows, D_MODEL), lambda i: (i, 0)),
        state_spec, pool_spec,
        _const_spec(cos.shape), _const_spec(sin.shape),
        _const_spec(mask.shape), _const_spec(qdec.shape), _const_spec(kdec.shape),
        _const_spec(pool_w.shape), _const_spec(pool_scale.shape), _const_spec(w_out.shape),
        _const_spec(lng.shape), _const_spec(lnb.shape),
    ]
    out_shape = [
        jax.ShapeDtypeStruct((nseq * dec_seq, D_MODEL), F32),
        jax.ShapeDtypeStruct(state_ret.shape, F32),
        jax.ShapeDtypeStruct(state_pool.shape, F32),
    ]
    out_specs = [pl.BlockSpec((rows, D_MODEL), lambda i: (i, 0)), state_spec, pool_spec]
    x2, new_ret, new_pool = pl.pallas_call(
        functools.partial(_decode_mixer_kernel, dec_seq=dec_seq),
        grid=(steps,),
        in_specs=in_specs,
        out_specs=out_specs,
        out_shape=out_shape,
        scratch_shapes=[
            pltpu.VMEM((groups, rows, POOL_GROUP), F32),
            pltpu.VMEM((rows, D_MODEL), BF16),
        ],
        compiler_params=pltpu.CompilerParams(
            dimension_semantics=("arbitrary",), vmem_limit_bytes=VMEM_LIMIT_BYTES),
        name="decode_mixer",
    )(*([proj] * (4 + groups)), x1, state_ret, state_pool, cos, sin, mask, qdec, kdec,
      pool_w, pool_scale, w_out, lng, lnb)
    return x2, new_ret, jnp.transpose(new_pool, (1, 0, 2))


def kernel(x_prompt, x_sample, state_ret, state_pool, meta_tokens, ffn1_w_gate, ffn1_w_up, ffn1_w_down,
           ln1_g, ln1_b, w_in, pool_w, pool_scale, w_out, ln2_g, ln2_b, ffn2_w_gate, ffn2_w_up,
           ffn2_w_down, ln3_g, ln3_b):
    assert ffn1_w_gate.shape[0] == DEPTH == 1
    batch, seq, _ = x_prompt.shape
    nseq, dec_seq, _ = x_sample.shape
    n_dec = nseq * dec_seq
    assert n_dec % N_META == 0

    bf = lambda w: w[0].astype(BF16)
    row = lambda v: v[0].reshape(1, -1)
    pool_w_b = bf(pool_w)
    pscale, g2, b2 = row(pool_scale), row(ln2_g), row(ln2_b)

    xp = x_prompt.reshape(batch * seq, D_MODEL)
    x_small = jnp.concatenate([x_sample.reshape(n_dec, D_MODEL), meta_tokens.astype(x_prompt.dtype)], axis=0)

    x1s, projs, wg1, wu1, wd1, w_in_b = _ffn_proj_cast(
        x_small, ffn1_w_gate[0], ffn1_w_up[0], ffn1_w_down[0], row(ln1_g), row(ln1_b), w_in[0])
    x1p, projp, wg2, wu2, wd2, w_out_b = _ffn_proj_pipelined(
        xp, wg1, wu1, wd1, row(ln1_g), row(ln1_b), w_in_b,
        (ffn2_w_gate[0], ffn2_w_up[0], ffn2_w_down[0], w_out[0]), seq)
    f2 = (wg2, wu2, wd2, row(ln3_g), row(ln3_b))

    x2s, ret_s, pool_s = _decode_mixer(projs, x1s, state_ret[0], state_pool[0], pool_w_b, pscale,
                                       w_out_b, g2, b2, nseq, dec_seq)
    y_prompt, ret_p, pool_p, y_sample = _prompt_mixer_ffn(
        projp, x1p, projs, n_dec // N_META, pool_w_b, pscale, w_out_b, g2, b2, *f2, x2s, batch, seq)
    return (y_prompt.reshape(batch, seq, D_MODEL), y_sample.reshape(nseq, dec_seq, D_MODEL),
            ret_p[None], pool_p[None], ret_s[None], pool_s[None])
```

```python
import functools
import math

import jax
import jax.numpy as jnp
import numpy as np
from jax import lax
from jax.experimental import pallas as pl
from jax.experimental.pallas import tpu as pltpu

F32 = jnp.float32
BF16 = jnp.bfloat16

D_MODEL = 1024
D_FF = 2816
N_META = 16
PAST_LEN = 16384
RET_HEADS = 4
HEAD_DIM = 128
RET_WIDTH = RET_HEADS * HEAD_DIM
RET_CHUNK = 128
ROPE_THETA = 10000.0
POOL_WINDOWS = (2, 4, 8, 16)
POOL_GROUP = 128
POOL_WIDTH = POOL_GROUP * len(POOL_WINDOWS)
POOL_BUF = max(POOL_WINDOWS) - 1
IN_WIDTH = 4 * RET_WIDTH + POOL_WIDTH
DEPTH = 1
ALPHA = (2.0 * DEPTH) ** 0.25
LN_EPS = 1e-5
GN_EPS = 1e-5
QK_SCALE = HEAD_DIM ** -0.5
GAMMAS = tuple(1.0 - 2.0 ** (-5.0 - h) for h in range(RET_HEADS))

VMEM_LIMIT_BYTES = 56 * 1024 * 1024
FFN_TOKEN_TILE = 512
FFN_COL_CHUNK = 256
MIX_TOKEN_TILE = 512
DEC_SEQ_BLOCK = 32
BF16_ROWS = 16
PAIR_WIDTH = 2 * HEAD_DIM


def _layer_norm(z, g, b):
    mu = jnp.mean(z, axis=-1, keepdims=True)
    zc = z - mu
    var = jnp.mean(zc * zc, axis=-1, keepdims=True)
    return zc * lax.rsqrt(var + LN_EPS) * g + b


def _silu(x):
    return x * jax.nn.sigmoid(x)


def _dot(a, b):
    return jnp.dot(a, b, preferred_element_type=F32)


def _dot_nt(a, b):
    return lax.dot_general(a, b, (((1,), (1,)), ((), ())), preferred_element_type=F32)


def _dot_tn(a, b):
    return lax.dot_general(a, b, (((0,), (0,)), ((), ())), preferred_element_type=F32)


def _const_spec(shape):
    zeros = (0,) * len(shape)
    return pl.BlockSpec(shape, lambda *_: zeros, pipeline_mode=pl.Buffered(1))


def _ffn_proj_cast_kernel(x_ref, wg_ref, wu_ref, wd_ref, lng_ref, lnb_ref, win_ref,
                          x1_ref, proj_ref, wgb_ref, wub_ref, wdb_ref, winb_ref,
                          acc_ref, xb_ref, win_all_ref):
    k = pl.program_id(0)

    @pl.when(k == 0)
    def _start():
        xb_ref[...] = x_ref[...].astype(BF16)
        acc_ref[...] = jnp.zeros(acc_ref.shape, F32)

    wg, wu, wd = wg_ref[...].astype(BF16), wu_ref[...].astype(BF16), wd_ref[...].astype(BF16)
    wgb_ref[...] = wg
    wub_ref[...] = wu
    wdb_ref[...] = wd
    h = (_silu(_dot(xb_ref[...], wg)) * _dot(xb_ref[...], wu)).astype(BF16)
    acc_ref[...] += _dot(h, wd)

    for j in range(IN_WIDTH // FFN_COL_CHUNK):
        @pl.when(k == j)
        def _round_w_in_chunk(j=j):
            w_in = win_ref[...].astype(BF16)
            winb_ref[...] = w_in
            win_all_ref[:, j * FFN_COL_CHUNK:(j + 1) * FFN_COL_CHUNK] = w_in

    @pl.when(k == pl.num_programs(0) - 1)
    def _finish():
        x1 = _layer_norm(ALPHA * x_ref[...] + 0.5 * acc_ref[...], lng_ref[...], lnb_ref[...])
        x1_ref[...] = x1
        proj_ref[...] = _dot(x1.astype(BF16), win_all_ref[...])


def _ffn_proj_cast(x, wg, wu, wd, lng, lnb, w_in):
    n = x.shape[0]
    ck = FFN_COL_CHUNK
    steps = D_FF // ck
    in_chunks = IN_WIDTH // ck
    assert in_chunks <= steps
    col_chunk = pl.BlockSpec((D_MODEL, ck), lambda k: (0, k))
    row_chunk = pl.BlockSpec((ck, D_MODEL), lambda k: (k, 0))
    w_in_chunk = pl.BlockSpec((D_MODEL, ck), lambda k: (0, jnp.minimum(k, in_chunks - 1)))
    whole = lambda shape: pl.BlockSpec(shape, lambda k: (0,) * len(shape))
    return pl.pallas_call(
        _ffn_proj_cast_kernel,
        grid=(steps,),
        in_specs=[_const_spec(x.shape), col_chunk, col_chunk, row_chunk,
                  _const_spec(lng.shape), _const_spec(lnb.shape), w_in_chunk],
        out_specs=[whole((n, D_MODEL)), whole((n, IN_WIDTH)), col_chunk, col_chunk, row_chunk,
                   w_in_chunk],
        out_shape=[jax.ShapeDtypeStruct((n, D_MODEL), F32), jax.ShapeDtypeStruct((n, IN_WIDTH), F32),
                   jax.ShapeDtypeStruct(wg.shape, BF16), jax.ShapeDtypeStruct(wu.shape, BF16),
                   jax.ShapeDtypeStruct(wd.shape, BF16), jax.ShapeDtypeStruct(w_in.shape, BF16)],
        scratch_shapes=[pltpu.VMEM((n, D_MODEL), F32), pltpu.VMEM((n, D_MODEL), BF16),
                        pltpu.VMEM(w_in.shape, BF16)],
        compiler_params=pltpu.CompilerParams(
            dimension_semantics=("arbitrary",), vmem_limit_bytes=VMEM_LIMIT_BYTES),
        name="ffn_proj_cast",
    )(x, wg, wu, wd, lng, lnb, w_in)


def _interleave(major, starts):
    live = []
    for i, piece in enumerate(major):
        piece()
        live += [make() for make in starts.get(i, [])]
        live = [g for g in live if next(g, "done") != "done"]
    while live:
        live = [g for g in live if next(g, "done") != "done"]


def _run_all(starts):
    for i in sorted(starts):
        for make in starts[i]:
            for _ in make():
                pass


def _ffn_proj_pipelined_kernel(x_ref, wg_ref, wu_ref, wd_ref, lng_ref, lnb_ref, win_ref,
                               rope_ref, pm_ref, *rest, n_cast, steps_per_seq):
    cast_in, rest = rest[:n_cast], rest[n_cast:]
    x1_ref, proj_ref, spool_ref = rest[:3]
    cast_out, (xb_ref, z1_ref, x1b_ref, h_ref, xp_ref) = rest[3:3 + n_cast], rest[3 + n_cast:]
    t = pl.program_id(0)
    n_tiles = pl.num_programs(0) - 1
    tile = x_ref.shape[0]
    row_blocks = [slice(r, r + RET_CHUNK) for r in range(0, tile, RET_CHUNK)]
    proj_chunk = 2 * FFN_COL_CHUNK
    hist = N_META
    seq_step = (t - 1) % steps_per_seq

    def ln1_piece(rows):
        def run():
            x1_rows = _layer_norm(z1_ref[rows, :], lng_ref[...], lnb_ref[...])
            x1_ref[rows, :] = x1_rows
            x1b_ref[rows, :] = x1_rows.astype(BF16)
            yield
        return run

    def proj_piece(nk):
        def run():
            cs = slice(nk * proj_chunk, (nk + 1) * proj_chunk)
            chunk = _dot(x1b_ref[...], win_ref[:, cs])
            if nk < 2:
                cos, sin = rope_ref[:, :HEAD_DIM], rope_ref[:, HEAD_DIM:]
                chunk = jnp.concatenate(
                    [_rope(chunk[:, h * HEAD_DIM:(h + 1) * HEAD_DIM], cos, sin)
                     for h in range(RET_HEADS)], axis=1)
            proj_ref[:, cs] = chunk
        return run

    def pool_input_piece():
        xp_ref[hist:hist + tile, :] = _dot(x1b_ref[...], win_ref[:, 4 * RET_WIDTH:])

    def pool_diff_piece(gi):
        def run():
            w = POOL_WINDOWS[gi]
            gs = slice(gi * POOL_GROUP, (gi + 1) * POOL_GROUP)
            rows_all = xp_ref[:, gs]
            wsum, shift = rows_all, 1
            while shift < w:
                wsum = wsum + pltpu.roll(wsum, shift, 0)
                shift *= 2
            proj_ref[:, 4 * RET_WIDTH + gi * POOL_GROUP:4 * RET_WIDTH + (gi + 1) * POOL_GROUP] = (
                wsum[hist:] * (1.0 / w) - rows_all[hist:])
            xp_ref[0:hist, gs] = xp_ref[tile:tile + hist, gs]
            yield
        return run

    def gate_up_piece(ck):
        def run():
            sl = slice(ck * FFN_COL_CHUNK, (ck + 1) * FFN_COL_CHUNK)
            g = _dot(xb_ref[...], wg_ref[:, sl])
            u = _dot(xb_ref[...], wu_ref[:, sl])
            h_ref[:, sl] = (_silu(g) * u).astype(BF16)
        return run

    def down_piece(nk):
        def run():
            cs = slice(nk * FFN_COL_CHUNK, (nk + 1) * FFN_COL_CHUNK)
            z1_ref[:, cs] = ALPHA * x_ref[:, cs] + 0.5 * _dot(h_ref[...], wd_ref[:, cs])
        return run

    ln1_starts = {i: [ln1_piece(rows)] for i, rows in enumerate(row_blocks)}
    proj_pieces = [pool_input_piece] + [proj_piece(nk) for nk in range(4 * RET_WIDTH // proj_chunk)]
    pool_diffs = [pool_diff_piece(gi) for gi in range(len(POOL_WINDOWS))]

    @pl.when(jnp.logical_and(t >= 1, seq_step == 0))
    def _history_from_meta():
        xp_ref[0:hist, :] = pm_ref[...]

    @pl.when(t == 0)
    def _clear_pipeline():
        z1_ref[...] = jnp.zeros(z1_ref.shape, F32)

    @pl.when(t < n_tiles)
    def _steady():
        xb_ref[...] = x_ref[...].astype(BF16)
        for src_ref, dst_ref in zip(cast_in, cast_out):
            dst_ref[...] = src_ref[...].astype(BF16)
        gate_up = [gate_up_piece(ck) for ck in range(D_FF // FFN_COL_CHUNK)]
        first_proj = len(row_blocks) + 1
        major = gate_up[:first_proj]
        for i, piece in enumerate(gate_up[first_proj:]):
            major += proj_pieces[i:i + 1] + [piece]
        major += proj_pieces[len(gate_up) - first_proj:]
        starts = dict(ln1_starts)
        half = len(pool_diffs) // 2
        starts[len(major) - 2] = pool_diffs[:half]
        _interleave(major, starts)
        _interleave([down_piece(nk) for nk in range(D_MODEL // FFN_COL_CHUNK)],
                    {i: [piece] for i, piece in enumerate(pool_diffs[half:])})

    @pl.when(t == n_tiles)
    def _drain():
        _run_all(ln1_starts)
        for piece in proj_pieces:
            piece()
        _run_all({0: pool_diffs})

    @pl.when(jnp.logical_and(t >= 1, seq_step == steps_per_seq - 1))
    def _emit_pool_state():
        spool_ref[0] = xp_ref[hist - POOL_BUF:hist, :]


def _slab_rows(rows, max_slabs):
    for slab in range(BF16_ROWS, rows + 1, BF16_ROWS):
        if rows % slab == 0 and rows // slab <= max_slabs:
            return slab
    raise ValueError(f"no slab size for {rows} rows in {max_slabs} steps")


def _ffn_proj_pipelined(x, wg, wu, wd, lng, lnb, w_in, cast_weights, seq, proj_small, meta_row_block):
    n = x.shape[0]
    tm = FFN_TOKEN_TILE
    n_tiles = n // tm
    steps_per_seq = seq // tm
    n_seq = n // seq
    assert seq % tm == 0 and IN_WIDTH % (2 * FFN_COL_CHUNK) == 0 and RET_WIDTH == 2 * FFN_COL_CHUNK
    rope = np.concatenate(_rope_tables(N_META + np.arange(seq)), axis=1)
    in_tile = lambda t: (jnp.minimum(t, n_tiles - 1), 0)
    out_tile = lambda t: (jnp.maximum(t - 1, 0), 0)
    rope_tile = pl.BlockSpec((tm, 2 * HEAD_DIM), lambda t: (jnp.maximum(t - 1, 0) % (seq // tm), 0))

    def slab_spec(w):
        slab = _slab_rows(w.shape[0], n_tiles)
        last = w.shape[0] // slab - 1
        return pl.BlockSpec((slab, w.shape[1]), lambda t: (jnp.minimum(t, last), 0))

    cast_specs = [slab_spec(w) for w in cast_weights]
    meta_p = pl.BlockSpec((N_META, POOL_WIDTH), lambda t: (meta_row_block, 4 * RET_WIDTH // POOL_WIDTH))
    pool_state = pl.BlockSpec(
        (1, POOL_BUF, POOL_WIDTH), lambda t: (jnp.clip((t - 1) // steps_per_seq, 0, n_seq - 1), 0, 0))
    return pl.pallas_call(
        functools.partial(_ffn_proj_pipelined_kernel, n_cast=len(cast_weights),
                          steps_per_seq=steps_per_seq),
        grid=(n_tiles + 1,),
        in_specs=[
            pl.BlockSpec((tm, D_MODEL), in_tile),
            _const_spec(wg.shape), _const_spec(wu.shape), _const_spec(wd.shape),
            _const_spec(lng.shape), _const_spec(lnb.shape), _const_spec(w_in.shape),
            rope_tile, meta_p,
        ] + cast_specs,
        out_specs=[pl.BlockSpec((tm, D_MODEL), out_tile), pl.BlockSpec((tm, IN_WIDTH), out_tile),
                   pool_state] + cast_specs,
        out_shape=[jax.ShapeDtypeStruct((n, D_MODEL), F32), jax.ShapeDtypeStruct((n, IN_WIDTH), F32),
                   jax.ShapeDtypeStruct((n_seq, POOL_BUF, POOL_WIDTH), F32)]
        + [jax.ShapeDtypeStruct(w.shape, BF16) for w in cast_weights],
        scratch_shapes=[
            pltpu.VMEM((tm, D_MODEL), BF16),
            pltpu.VMEM((tm, D_MODEL), F32),
            pltpu.VMEM((tm, D_MODEL), BF16),
            pltpu.VMEM((tm, D_FF), BF16),
            pltpu.VMEM((N_META + tm, POOL_WIDTH), F32),
        ],
        compiler_params=pltpu.CompilerParams(
            dimension_semantics=("arbitrary",), vmem_limit_bytes=VMEM_LIMIT_BYTES),
        name="ffn_proj_pipelined",
    )(x, wg, wu, wd, lng, lnb, w_in, rope, proj_small, *cast_weights)


def _rope_tables(positions):
    half = HEAD_DIM // 2
    inv_freq = ROPE_THETA ** (-np.arange(0, HEAD_DIM, 2, dtype=np.float64) / HEAD_DIM)
    ang = np.asarray(positions, np.float64)[:, None] * inv_freq[None, :]
    cos, sin = np.cos(ang), np.sin(ang)
    assert cos.shape[1] == half
    return (np.concatenate([cos, cos], axis=1).astype(np.float32),
            np.concatenate([-sin, sin], axis=1).astype(np.float32))


def _decay_tables(chunk, seq_len):
    r = np.arange(chunk)
    seq, idx = r // seq_len, (r % seq_len).astype(np.float64)
    same = seq[:, None] == seq[None, :]
    diff = idx[:, None] - idx[None, :]
    mask, qdec, kdec = [], [], []
    for gamma in GAMMAS:
        lg = math.log(gamma)
        mask.append(np.where(same & (diff >= 0), np.exp(lg * np.maximum(diff, 0.0)), 0.0) * QK_SCALE)
        qdec.append(np.broadcast_to((np.exp(lg * (idx + 1.0)) * QK_SCALE)[:, None], (chunk, HEAD_DIM)))
        kdec.append(np.broadcast_to(np.exp(lg * (seq_len - 1.0 - idx))[:, None], (chunk, HEAD_DIM)))
    to32 = lambda t: np.stack(t).astype(np.float32)
    return to32(mask), to32(qdec), to32(kdec)


def _rope(x, cos, sin):
    return x * cos + pltpu.roll(x, HEAD_DIM // 2, 1) * sin


def _group_norm(o):
    mu = jnp.mean(o, axis=-1, keepdims=True)
    oc = o - mu
    var = jnp.mean(oc * oc, axis=-1, keepdims=True)
    return oc * lax.rsqrt(var + GN_EPS)


def _prompt_mixer_ffn_kernel(proj_ref, x1_ref,
                             km_ref, vm_ref, cosm_ref, sinm_ref, kdecm_ref,
                             mask_ref, qdec_ref, kdec_ref, poolw_ref, pscale_ref, wout_ref,
                             ln2g_ref, ln2b_ref, wg_ref, wu_ref, wd_ref, ln3g_ref, ln3b_ref,
                             x2dec_hbm,
                             y_ref, sret_ref, ydec_hbm,
                             s_ref, mix_ref, x2_ref, xb_ref, ypre_ref, h_ref, kb_ref, vs_ref,
                             *, steps_per_seq):
    t = pl.program_id(0)
    n_tiles = pl.num_programs(0) - 2
    c = t % steps_per_seq
    tile = proj_ref.shape[0]
    q_ref, k_ref, v_ref, g_ref, p_ref = (
        proj_ref.at[:, j * RET_WIDTH:(j + 1) * RET_WIDTH] for j in range(5))
    row_blocks = [slice(r, r + RET_CHUNK) for r in range(0, tile, RET_CHUNK)]

    @pl.when(jnp.logical_and(c == 0, t < n_tiles))
    def _init_from_meta():
        for h in range(RET_HEADS):
            hs = slice(h * HEAD_DIM, (h + 1) * HEAD_DIM)
            kr = _rope(km_ref[:, hs], cosm_ref[...], sinm_ref[...])
            kd = (kr * kdecm_ref[h]).astype(BF16)
            s_ref[h] = _dot_tn(kd, vm_ref[:, hs].astype(BF16))

    def ln3_fetch_piece(rows):
        def run():
            y_ref[rows, :] = ypre_ref[rows, :]
            yield
        return run

    def ln3_piece(rows):
        def run():
            y_ref[rows, :] = _layer_norm(y_ref[rows, :], ln3g_ref[...], ln3b_ref[...])
            yield
        return run

    state = [None] * RET_HEADS

    def retention_piece(ci, hp):
        def run():
            rows = slice(ci * RET_CHUNK, (ci + 1) * RET_CHUNK)
            pair = slice(hp * PAIR_WIDTH, (hp + 1) * PAIR_WIDTH)
            buf = ci * 2 + hp
            qr, kr = [], []
            for j in range(2):
                h = 2 * hp + j
                hs = slice(h * HEAD_DIM, (h + 1) * HEAD_DIM)
                blk = slice(j * HEAD_DIM, (j + 1) * HEAD_DIM)
                qr.append(q_ref[rows, hs])
                kr.append(k_ref[rows, hs])
                kb_ref[buf, blk, blk] = kr[j].astype(BF16)
                vs_ref[buf, blk, blk] = v_ref[rows, hs].astype(BF16)
                vs_ref[buf, PAIR_WIDTH + j * HEAD_DIM:PAIR_WIDTH + (j + 1) * HEAD_DIM, blk] = (
                    state[h].astype(BF16))
            q2 = jnp.concatenate(qr, axis=1)
            k2 = jnp.concatenate(kr, axis=1)
            scores = _dot_nt(q2.astype(BF16), kb_ref[buf])
            upd = _dot((k2 * kdec_ref[hp]).T.astype(BF16), v_ref[rows, pair].astype(BF16))
            qd2 = (q2 * qdec_ref[hp]).astype(BF16)
            yield
            lhs = jnp.concatenate([(scores * mask_ref[hp]).astype(BF16), qd2], axis=1)
            o2 = _dot(lhs, vs_ref[buf])
            for j in range(2):
                blk = slice(j * HEAD_DIM, (j + 1) * HEAD_DIM)
                state[2 * hp + j] = (GAMMAS[2 * hp + j] ** RET_CHUNK) * state[2 * hp + j] + upd[blk, blk]
            yield
            for j in range(2):
                hs = slice((2 * hp + j) * HEAD_DIM, (2 * hp + j + 1) * HEAD_DIM)
                blk = slice(j * HEAD_DIM, (j + 1) * HEAD_DIM)
                mix_ref[rows, hs] = (_silu(g_ref[rows, hs]) * _group_norm(o2[:, blk])).astype(BF16)
        return run

    def pool_piece(pp):
        def run():
            pair = slice(pp * PAIR_WIDTH, (pp + 1) * PAIR_WIDTH)
            pooled = _dot(p_ref[:, pair].astype(BF16), poolw_ref[pp])
            yield
            pooled = pooled * pscale_ref[:, pair]
            mix_ref[:, RET_WIDTH + pp * PAIR_WIDTH:RET_WIDTH + (pp + 1) * PAIR_WIDTH] = pooled.astype(BF16)
        return run

    def gate_up_piece(ck):
        def run():
            sl = slice(ck * FFN_COL_CHUNK, (ck + 1) * FFN_COL_CHUNK)
            g = _dot(xb_ref[...], wg_ref[:, sl])
            u = _dot(xb_ref[...], wu_ref[:, sl])
            h_ref[:, sl] = (_silu(g) * u).astype(BF16)
        return run

    def residual_piece():
        ypre_ref[...] = ALPHA * x2_ref[...]
        yield

    def w_out_piece():
        for h in range(RET_HEADS):
            s_ref[h] = state[h]
        x2_ref[...] = ALPHA * x1_ref[...] + _dot(mix_ref[...], wout_ref[...])

    def ln2_piece(rows):
        def run():
            x2_ref[rows, :] = _layer_norm(x2_ref[rows, :], ln2g_ref[...], ln2b_ref[...])
            yield
        return run

    def xb_piece(rows):
        def run():
            xb_ref[rows, :] = x2_ref[rows, :].astype(BF16)
            yield
        return run

    def down_piece(nk):
        def run():
            cs = slice(nk * FFN_COL_CHUNK, (nk + 1) * FFN_COL_CHUNK)
            ypre_ref[:, cs] = ypre_ref[:, cs] + 0.5 * _dot(h_ref[...], wd_ref[:, cs])
        return run

    def steady_body():
        starts = {}

        def start_at(i, piece):
            starts.setdefault(i, []).append(piece)

        for rows in row_blocks:
            start_at(0, ln3_fetch_piece(rows))
        start_at(1, residual_piece)
        for h in range(RET_HEADS):
            state[h] = s_ref[h]
        gate_up = [gate_up_piece(ck) for ck in range(D_FF // FFN_COL_CHUNK)]
        pieces = [(ci, hp) for ci in range(len(row_blocks)) for hp in range(RET_HEADS // 2)]
        w_out_at = len(pieces)
        for n, (ci, hp) in enumerate(pieces):
            start_at(n // 2 if n < 4 else n - 2, retention_piece(ci, hp))
        for pp in range(len(POOL_WINDOWS) // 2):
            start_at(2 * pp + 1, pool_piece(pp))
        major = gate_up[:w_out_at] + [w_out_piece] + gate_up[w_out_at:]
        for i, rows in enumerate(row_blocks):
            start_at(min(w_out_at + i, len(major) - 1), ln2_piece(rows))
        _interleave(major, starts)

        tail_starts = {i: [ln3_piece(rows), xb_piece(rows)] for i, rows in enumerate(row_blocks)}
        _interleave([down_piece(nk) for nk in range(D_MODEL // FFN_COL_CHUNK)], tail_starts)

    def drain_body():
        for rows in row_blocks:
            for make in (ln3_fetch_piece(rows), ln3_piece(rows)):
                for _ in make():
                    pass

    @pl.when(t == 0)
    def _prime_pipeline():
        pltpu.sync_copy(x2dec_hbm, x2_ref)
        xb_ref[...] = x2_ref[...].astype(BF16)
        kb_ref[...] = jnp.zeros(kb_ref.shape, BF16)
        vs_ref[...] = jnp.zeros(vs_ref.shape, BF16)
        ypre_ref[...] = jnp.zeros(ypre_ref.shape, F32)

    @pl.when(t <= n_tiles)
    def _steady():
        steady_body()

    @pl.when(t == 1)
    def _emit_decode_rows():
        pltpu.sync_copy(y_ref, ydec_hbm)

    @pl.when(t == n_tiles + 1)
    def _drain_last():
        drain_body()

    @pl.when(jnp.logical_and(c == steps_per_seq - 1, t < n_tiles))
    def _emit_state():
        sret_ref[0] = s_ref[...]


def _prompt_mixer_ffn(proj, x1, proj_small, meta_row_block, pool_w, pool_scale, w_out, ln2g, ln2b,
                      wg, wu, wd, ln3g, ln3b, x2_dec, batch, seq):
    tile = MIX_TOKEN_TILE
    assert x2_dec.shape == (tile, D_MODEL)
    steps = seq // tile
    n_tiles = batch * steps
    cosm, sinm = _rope_tables(np.arange(N_META))
    pair_up = lambda tab: np.concatenate([tab[0::2], tab[1::2]], axis=2)
    mask, qdec, kdec = (pair_up(tab) for tab in _decay_tables(RET_CHUNK, RET_CHUNK))
    _, _, kdecm = _decay_tables(N_META, N_META)
    zero_blk = jnp.zeros_like(pool_w[0])
    pool_w = jnp.stack([jnp.block([[pool_w[2 * pp], zero_blk], [zero_blk, pool_w[2 * pp + 1]]])
                        for pp in range(len(POOL_WINDOWS) // 2)])

    mix_tile = lambda t: jnp.minimum(t, n_tiles - 1)
    ffn_tile = lambda t: jnp.maximum(t - 2, 0)

    def meta_col(j):
        return pl.BlockSpec((N_META, RET_WIDTH), lambda t: (meta_row_block, j))

    in_specs = [
        pl.BlockSpec((tile, IN_WIDTH), lambda t: (mix_tile(t), 0)),
        pl.BlockSpec((tile, D_MODEL), lambda t: (mix_tile(t), 0)),
        meta_col(1), meta_col(2),
        _const_spec(cosm.shape), _const_spec(sinm.shape), _const_spec(kdecm.shape),
        _const_spec(mask.shape), _const_spec(qdec.shape), _const_spec(kdec.shape),
        _const_spec(pool_w.shape), _const_spec(pool_scale.shape), _const_spec(w_out.shape),
        _const_spec(ln2g.shape), _const_spec(ln2b.shape),
        _const_spec(wg.shape), _const_spec(wu.shape), _const_spec(wd.shape),
        _const_spec(ln3g.shape), _const_spec(ln3b.shape),
        pl.BlockSpec(memory_space=pl.ANY),
    ]
    out_shape = [
        jax.ShapeDtypeStruct((batch * seq, D_MODEL), F32),
        jax.ShapeDtypeStruct((batch, RET_HEADS, HEAD_DIM, HEAD_DIM), F32),
        jax.ShapeDtypeStruct((tile, D_MODEL), F32),
    ]
    out_specs = [
        pl.BlockSpec((tile, D_MODEL), lambda t: (ffn_tile(t), 0)),
        pl.BlockSpec((1, RET_HEADS, HEAD_DIM, HEAD_DIM), lambda t: (mix_tile(t) // steps, 0, 0, 0)),
        pl.BlockSpec(memory_space=pl.ANY),
    ]
    return pl.pallas_call(
        functools.partial(_prompt_mixer_ffn_kernel, steps_per_seq=steps),
        grid=(n_tiles + 2,),
        in_specs=in_specs,
        out_specs=out_specs,
        out_shape=out_shape,
        scratch_shapes=[
            pltpu.VMEM((RET_HEADS, HEAD_DIM, HEAD_DIM), F32),
            pltpu.VMEM((tile, D_MODEL), BF16),
            pltpu.VMEM((tile, D_MODEL), F32),
            pltpu.VMEM((tile, D_MODEL), BF16),
            pltpu.VMEM((tile, D_MODEL), F32),
            pltpu.VMEM((tile, D_FF), BF16),
            pltpu.VMEM((8, PAIR_WIDTH, PAIR_WIDTH), BF16),
            pltpu.VMEM((8, 2 * PAIR_WIDTH, PAIR_WIDTH), BF16),
        ],
        compiler_params=pltpu.CompilerParams(
            dimension_semantics=("arbitrary",), vmem_limit_bytes=VMEM_LIMIT_BYTES),
        name="prompt_mixer_ffn",
    )(proj, x1, proj_small, proj_small,
      cosm, sinm, kdecm, mask, qdec, kdec, pool_w, pool_scale, w_out, ln2g, ln2b,
      wg, wu, wd, ln3g, ln3b, x2_dec)


def _decode_mixer_kernel(q_ref, k_ref, v_ref, g_ref, *rest, dec_seq):
    p_refs, rest = rest[:len(POOL_WINDOWS)], rest[len(POOL_WINDOWS):]
    (x1_ref, s0_ref, pref_ref, cos_ref, sin_ref, mask_ref, qdec_ref, kdec_ref,
     poolw_ref, pscale_ref, wout_ref, lng_ref, lnb_ref,
     o_ref, sret_ref, spool_ref, d_ref, mix_ref) = rest
    rows = q_ref.shape[0]
    nseq = rows // dec_seq
    seq_per_group = BF16_ROWS // dec_seq
    cos, sin = cos_ref[...], sin_ref[...]
    row_seq = lax.broadcasted_iota(jnp.int32, (BF16_ROWS, HEAD_DIM), 0) // dec_seq

    for h in range(RET_HEADS):
        hs = slice(h * HEAD_DIM, (h + 1) * HEAD_DIM)
        qr = _rope(q_ref[:, hs], cos, sin)
        kr = _rope(k_ref[:, hs], cos, sin)
        v = v_ref[:, hs]
        vb = v.astype(BF16)
        scores = _dot_nt(qr.astype(BF16), kr.astype(BF16)) * mask_ref[h]
        o_inner = _dot(scores.astype(BF16), vb)
        qd = qr * qdec_ref[h]
        kd = kr * kdec_ref[h]
        o_cross = []
        for grp in range(rows // BF16_ROWS):
            gr = slice(grp * BF16_ROWS, (grp + 1) * BF16_ROWS)
            qd_g = qd[gr].astype(BF16)
            kd_g = kd[gr].astype(BF16)
            v_g = v[gr]
            acc = jnp.zeros((BF16_ROWS, HEAD_DIM), F32)
            for j in range(seq_per_group):
                b = grp * seq_per_group + j
                s = s0_ref[b, h]
                acc = jnp.where(row_seq == j, _dot(qd_g, s.astype(BF16)), acc)
                v_b = jnp.where(row_seq == j, v_g, 0.0).astype(BF16)
                sret_ref[b, h] = (GAMMAS[h] ** dec_seq) * s + _dot_tn(kd_g, v_b)
            o_cross.append(acc)
        o = o_inner + jnp.concatenate(o_cross, axis=0)
        mix_ref[:, hs] = (_silu(g_ref[:, hs]) * _group_norm(o)).astype(BF16)

    for gi, (w, pg_ref) in enumerate(zip(POOL_WINDOWS, p_refs)):
        gs = slice(gi * POOL_GROUP, (gi + 1) * POOL_GROUP)
        steps = [pref_ref[j, :, gs] for j in range(POOL_BUF)]
        steps += [pg_ref[pl.ds(i, nseq, stride=dec_seq), :] for i in range(dec_seq)]
        for i in range(dec_seq):
            now = POOL_BUF + i
            wsum = steps[now]
            for back in range(1, w):
                wsum = wsum + steps[now - back]
            d_ref[gi, pl.ds(i, nseq, stride=dec_seq), :] = wsum * (1.0 / w) - steps[now]
        for j in range(POOL_BUF):
            spool_ref[j, :, gs] = steps[dec_seq + j]
        pooled = _dot(d_ref[gi].astype(BF16), poolw_ref[gi]) * pscale_ref[:, gs]
        mix_ref[:, RET_WIDTH + gi * POOL_GROUP:RET_WIDTH + (gi + 1) * POOL_GROUP] = pooled.astype(BF16)

    y = _dot(mix_ref[...], wout_ref[...])
    o_ref[...] = _layer_norm(ALPHA * x1_ref[...] + y, lng_ref[...], lnb_ref[...])


def _decode_mixer(proj, x1, state_ret, state_pool, pool_w, pool_scale, w_out, lng, lnb, nseq, dec_seq):
    assert BF16_ROWS % dec_seq == 0 and dec_seq <= POOL_BUF
    rows = DEC_SEQ_BLOCK * dec_seq
    steps = nseq // DEC_SEQ_BLOCK
    cos, sin = _rope_tables(PAST_LEN + (np.arange(rows) % dec_seq))
    mask, qdec, kdec = _decay_tables(rows, dec_seq)

    def col(j):
        return pl.BlockSpec((rows, RET_WIDTH), lambda i: (i, j))

    state_spec = pl.BlockSpec((DEC_SEQ_BLOCK, RET_HEADS, HEAD_DIM, HEAD_DIM), lambda i: (i, 0, 0, 0))
    state_pool = jnp.transpose(state_pool, (1, 0, 2))
    pool_spec = pl.BlockSpec((POOL_BUF, DEC_SEQ_BLOCK, POOL_WIDTH), lambda i: (0, i, 0))
    groups = len(POOL_WINDOWS)
    p_cols = 4 * RET_WIDTH // POOL_GROUP
    in_specs = [
        col(0), col(1), col(2), col(3),
        *[pl.BlockSpec((rows, POOL_GROUP), lambda i, gi=gi: (i, p_cols + gi)) for gi in range(groups)],
        pl.BlockSpec((rows, D_MODEL), lambda i: (i, 0)),
        state_spec, pool_spec,
        _const_spec(cos.shape), _const_spec(sin.shape),
        _const_spec(mask.shape), _const_spec(qdec.shape), _const_spec(kdec.shape),
        _const_spec(pool_w.shape), _const_spec(pool_scale.shape), _const_spec(w_out.shape),
        _const_spec(lng.shape), _const_spec(lnb.shape),
    ]
    out_shape = [
        jax.ShapeDtypeStruct((nseq * dec_seq, D_MODEL), F32),
        jax.ShapeDtypeStruct(state_ret.shape, F32),
        jax.ShapeDtypeStruct(state_pool.shape, F32),
    ]
    out_specs = [pl.BlockSpec((rows, D_MODEL), lambda i: (i, 0)), state_spec, pool_spec]
    x2, new_ret, new_pool = pl.pallas_call(
        functools.partial(_decode_mixer_kernel, dec_seq=dec_seq),
        grid=(steps,),
        in_specs=in_specs,
        out_specs=out_specs,
        out_shape=out_shape,
        scratch_shapes=[
            pltpu.VMEM((groups, rows, POOL_GROUP), F32),
            pltpu.VMEM((rows, D_MODEL), BF16),
        ],
        compiler_params=pltpu.CompilerParams(
            dimension_semantics=("arbitrary",), vmem_limit_bytes=VMEM_LIMIT_BYTES),
        name="decode_mixer",
    )(*([proj] * (4 + groups)), x1, state_ret, state_pool, cos, sin, mask, qdec, kdec,
      pool_w, pool_scale, w_out, lng, lnb)
    return x2, new_ret, jnp.transpose(new_pool, (1, 0, 2))


def kernel(x_prompt, x_sample, state_ret, state_pool, meta_tokens, ffn1_w_gate, ffn1_w_up, ffn1_w_down,
           ln1_g, ln1_b, w_in, pool_w, pool_scale, w_out, ln2_g, ln2_b, ffn2_w_gate, ffn2_w_up,
           ffn2_w_down, ln3_g, ln3_b):
    assert ffn1_w_gate.shape[0] == DEPTH == 1
    batch, seq, _ = x_prompt.shape
    nseq, dec_seq, _ = x_sample.shape
    n_dec = nseq * dec_seq
    assert n_dec % N_META == 0

    bf = lambda w: w[0].astype(BF16)
    row = lambda v: v[0].reshape(1, -1)
    pool_w_b = bf(pool_w)
    pscale, g2, b2 = row(pool_scale), row(ln2_g), row(ln2_b)

    xp = x_prompt.reshape(batch * seq, D_MODEL)
    x_small = jnp.concatenate([x_sample.reshape(n_dec, D_MODEL), meta_tokens.astype(x_prompt.dtype)], axis=0)

    x1s, projs, wg1, wu1, wd1, w_in_b = _ffn_proj_cast(
        x_small, ffn1_w_gate[0], ffn1_w_up[0], ffn1_w_down[0], row(ln1_g), row(ln1_b), w_in[0])
    meta_rows = n_dec // N_META
    x1p, projp, pool_p, wg2, wu2, wd2, w_out_b = _ffn_proj_pipelined(
        xp, wg1, wu1, wd1, row(ln1_g), row(ln1_b), w_in_b,
        (ffn2_w_gate[0], ffn2_w_up[0], ffn2_w_down[0], w_out[0]), seq, projs, meta_rows)
    f2 = (wg2, wu2, wd2, row(ln3_g), row(ln3_b))

    x2s, ret_s, pool_s = _decode_mixer(projs, x1s, state_ret[0], state_pool[0], pool_w_b, pscale,
                                       w_out_b, g2, b2, nseq, dec_seq)
    y_prompt, ret_p, y_sample = _prompt_mixer_ffn(
        projp, x1p, projs, meta_rows, pool_w_b, pscale, w_out_b, g2, b2, *f2, x2s, batch, seq)
    return (y_prompt.reshape(batch, seq, D_MODEL), y_sample.reshape(nseq, dec_seq, D_MODEL),
            ret_p[None], pool_p[None], ret_s[None], pool_s[None])
```

```python
import functools
import math

import jax
import jax.numpy as jnp
import numpy as np
from jax import lax
from jax.experimental import pallas as pl
from jax.experimental.pallas import tpu as pltpu

F32 = jnp.float32
BF16 = jnp.bfloat16

D_MODEL = 1024
D_FF = 2816
N_META = 16
PAST_LEN = 16384
RET_HEADS = 4
HEAD_DIM = 128
RET_WIDTH = RET_HEADS * HEAD_DIM
RET_CHUNK = 128
ROPE_THETA = 10000.0
POOL_WINDOWS = (2, 4, 8, 16)
POOL_GROUP = 128
POOL_WIDTH = POOL_GROUP * len(POOL_WINDOWS)
POOL_BUF = max(POOL_WINDOWS) - 1
IN_WIDTH = 4 * RET_WIDTH + POOL_WIDTH
DEPTH = 1
ALPHA = (2.0 * DEPTH) ** 0.25
LN_EPS = 1e-5
GN_EPS = 1e-5
QK_SCALE = HEAD_DIM ** -0.5
GAMMAS = tuple(1.0 - 2.0 ** (-5.0 - h) for h in range(RET_HEADS))

VMEM_LIMIT_BYTES = 56 * 1024 * 1024
FFN_TOKEN_TILE = 512
FFN_COL_CHUNK = 256
MIX_TOKEN_TILE = 512
DEC_SEQ_BLOCK = 32
BF16_ROWS = 16
PAIR_WIDTH = 2 * HEAD_DIM


def _layer_norm(z, g, b):
    mu = jnp.mean(z, axis=-1, keepdims=True)
    zc = z - mu
    var = jnp.mean(zc * zc, axis=-1, keepdims=True)
    return zc * lax.rsqrt(var + LN_EPS) * g + b


def _silu(x):
    return x * jax.nn.sigmoid(x)


def _dot(a, b):
    return jnp.dot(a, b, preferred_element_type=F32)


def _dot_nt(a, b):
    return lax.dot_general(a, b, (((1,), (1,)), ((), ())), preferred_element_type=F32)


def _dot_tn(a, b):
    return lax.dot_general(a, b, (((0,), (0,)), ((), ())), preferred_element_type=F32)


def _const_spec(shape):
    zeros = (0,) * len(shape)
    return pl.BlockSpec(shape, lambda *_: zeros, pipeline_mode=pl.Buffered(1))


def _ffn_proj_cast_kernel(x_ref, wg_ref, wu_ref, wd_ref, lng_ref, lnb_ref, win_ref,
                          x1_ref, proj_ref, wgb_ref, wub_ref, wdb_ref, winb_ref,
                          acc_ref, xb_ref, win_all_ref):
    k = pl.program_id(0)

    @pl.when(k == 0)
    def _start():
        xb_ref[...] = x_ref[...].astype(BF16)
        acc_ref[...] = jnp.zeros(acc_ref.shape, F32)

    wg, wu, wd = wg_ref[...].astype(BF16), wu_ref[...].astype(BF16), wd_ref[...].astype(BF16)
    wgb_ref[...] = wg
    wub_ref[...] = wu
    wdb_ref[...] = wd
    h = (_silu(_dot(xb_ref[...], wg)) * _dot(xb_ref[...], wu)).astype(BF16)
    acc_ref[...] += _dot(h, wd)

    for j in range(IN_WIDTH // FFN_COL_CHUNK):
        @pl.when(k == j)
        def _round_w_in_chunk(j=j):
            w_in = win_ref[...].astype(BF16)
            winb_ref[...] = w_in
            win_all_ref[:, j * FFN_COL_CHUNK:(j + 1) * FFN_COL_CHUNK] = w_in

    @pl.when(k == pl.num_programs(0) - 1)
    def _finish():
        x1 = _layer_norm(ALPHA * x_ref[...] + 0.5 * acc_ref[...], lng_ref[...], lnb_ref[...])
        x1_ref[...] = x1
        proj_ref[...] = _dot(x1.astype(BF16), win_all_ref[...])


def _ffn_proj_cast(x, wg, wu, wd, lng, lnb, w_in):
    n = x.shape[0]
    ck = FFN_COL_CHUNK
    steps = D_FF // ck
    in_chunks = IN_WIDTH // ck
    assert in_chunks <= steps
    col_chunk = pl.BlockSpec((D_MODEL, ck), lambda k: (0, k))
    row_chunk = pl.BlockSpec((ck, D_MODEL), lambda k: (k, 0))
    w_in_chunk = pl.BlockSpec((D_MODEL, ck), lambda k: (0, jnp.minimum(k, in_chunks - 1)))
    whole = lambda shape: pl.BlockSpec(shape, lambda k: (0,) * len(shape))
    return pl.pallas_call(
        _ffn_proj_cast_kernel,
        grid=(steps,),
        in_specs=[_const_spec(x.shape), col_chunk, col_chunk, row_chunk,
                  _const_spec(lng.shape), _const_spec(lnb.shape), w_in_chunk],
        out_specs=[whole((n, D_MODEL)), whole((n, IN_WIDTH)), col_chunk, col_chunk, row_chunk,
                   w_in_chunk],
        out_shape=[jax.ShapeDtypeStruct((n, D_MODEL), F32), jax.ShapeDtypeStruct((n, IN_WIDTH), F32),
                   jax.ShapeDtypeStruct(wg.shape, BF16), jax.ShapeDtypeStruct(wu.shape, BF16),
                   jax.ShapeDtypeStruct(wd.shape, BF16), jax.ShapeDtypeStruct(w_in.shape, BF16)],
        scratch_shapes=[pltpu.VMEM((n, D_MODEL), F32), pltpu.VMEM((n, D_MODEL), BF16),
                        pltpu.VMEM(w_in.shape, BF16)],
        compiler_params=pltpu.CompilerParams(
            dimension_semantics=("arbitrary",), vmem_limit_bytes=VMEM_LIMIT_BYTES),
        name="ffn_proj_cast",
    )(x, wg, wu, wd, lng, lnb, w_in)


def _interleave(major, starts):
    live = []
    for i, piece in enumerate(major):
        piece()
        live += [make() for make in starts.get(i, [])]
        live = [g for g in live if next(g, "done") != "done"]
    while live:
        live = [g for g in live if next(g, "done") != "done"]


def _run_all(starts):
    for i in sorted(starts):
        for make in starts[i]:
            for _ in make():
                pass


def _ffn_proj_pipelined_kernel(x_ref, wg_ref, wu_ref, wd_ref, lng_ref, lnb_ref, win_ref,
                               rope_ref, pm_ref, *rest, n_cast, steps_per_seq):
    cast_in, rest = rest[:n_cast], rest[n_cast:]
    x1_ref, proj_ref, spool_ref = rest[:3]
    cast_out, (xb_ref, z1_ref, x1b_ref, h_ref, xp_ref) = rest[3:3 + n_cast], rest[3 + n_cast:]
    t = pl.program_id(0)
    n_tiles = pl.num_programs(0) - 1
    tile = x_ref.shape[0]
    row_blocks = [slice(r, r + RET_CHUNK) for r in range(0, tile, RET_CHUNK)]
    proj_chunk = 2 * FFN_COL_CHUNK
    hist = N_META
    seq_step = (t - 1) % steps_per_seq

    def ln1_piece(rows):
        def run():
            x1_rows = _layer_norm(z1_ref[rows, :], lng_ref[...], lnb_ref[...])
            x1_ref[rows, :] = x1_rows
            x1b_ref[rows, :] = x1_rows.astype(BF16)
            yield
        return run

    def proj_piece(nk):
        def run():
            cs = slice(nk * proj_chunk, (nk + 1) * proj_chunk)
            chunk = _dot(x1b_ref[...], win_ref[:, cs])
            if nk < 2:
                cos, sin = rope_ref[:, :HEAD_DIM], rope_ref[:, HEAD_DIM:]
                chunk = jnp.concatenate(
                    [_rope(chunk[:, h * HEAD_DIM:(h + 1) * HEAD_DIM], cos, sin)
                     for h in range(RET_HEADS)], axis=1)
            proj_ref[:, cs] = chunk
        return run

    def pool_input_piece():
        xp_ref[hist:hist + tile, :] = _dot(x1b_ref[...], win_ref[:, 4 * RET_WIDTH:])

    def pool_diff_piece(gi):
        def run():
            w = POOL_WINDOWS[gi]
            gs = slice(gi * POOL_GROUP, (gi + 1) * POOL_GROUP)
            rows_all = xp_ref[:, gs]
            wsum, shift = rows_all, 1
            while shift < w:
                wsum = wsum + pltpu.roll(wsum, shift, 0)
                shift *= 2
            proj_ref[:, 4 * RET_WIDTH + gi * POOL_GROUP:4 * RET_WIDTH + (gi + 1) * POOL_GROUP] = (
                wsum[hist:] * (1.0 / w) - rows_all[hist:])
            xp_ref[0:hist, gs] = xp_ref[tile:tile + hist, gs]
            yield
        return run

    def gate_up_piece(ck):
        def run():
            sl = slice(ck * FFN_COL_CHUNK, (ck + 1) * FFN_COL_CHUNK)
            g = _dot(xb_ref[...], wg_ref[:, sl])
            u = _dot(xb_ref[...], wu_ref[:, sl])
            h_ref[:, sl] = (_silu(g) * u).astype(BF16)
        return run

    def down_piece(nk):
        def run():
            cs = slice(nk * FFN_COL_CHUNK, (nk + 1) * FFN_COL_CHUNK)
            z1_ref[:, cs] = ALPHA * x_ref[:, cs] + 0.5 * _dot(h_ref[...], wd_ref[:, cs])
        return run

    ln1_starts = {i: [ln1_piece(rows)] for i, rows in enumerate(row_blocks)}
    proj_pieces = [pool_input_piece] + [proj_piece(nk) for nk in range(4 * RET_WIDTH // proj_chunk)]
    pool_diffs = [pool_diff_piece(gi) for gi in range(len(POOL_WINDOWS))]

    @pl.when(jnp.logical_and(t >= 1, seq_step == 0))
    def _history_from_meta():
        xp_ref[0:hist, :] = pm_ref[...]

    @pl.when(t == 0)
    def _clear_pipeline():
        z1_ref[...] = jnp.zeros(z1_ref.shape, F32)
        xp_ref[0:hist, :] = jnp.zeros((hist, POOL_WIDTH), F32)

    @pl.when(t < n_tiles)
    def _steady():
        xb_ref[...] = x_ref[...].astype(BF16)
        for src_ref, dst_ref in zip(cast_in, cast_out):
            dst_ref[...] = src_ref[...].astype(BF16)
        gate_up = [gate_up_piece(ck) for ck in range(D_FF // FFN_COL_CHUNK)]
        first_proj = len(row_blocks) + 1
        major = gate_up[:first_proj]
        for i, piece in enumerate(gate_up[first_proj:]):
            major += proj_pieces[i:i + 1] + [piece]
        major += proj_pieces[len(gate_up) - first_proj:]
        starts = dict(ln1_starts)
        half = len(pool_diffs) // 2
        starts[len(major) - 2] = pool_diffs[:half]
        _interleave(major, starts)
        _interleave([down_piece(nk) for nk in range(D_MODEL // FFN_COL_CHUNK)],
                    {i: [piece] for i, piece in enumerate(pool_diffs[half:])})

    @pl.when(t == n_tiles)
    def _drain():
        _run_all(ln1_starts)
        for piece in proj_pieces:
            piece()
        _run_all({0: pool_diffs})

    @pl.when(jnp.logical_and(t >= 1, seq_step == steps_per_seq - 1))
    def _emit_pool_state():
        spool_ref[0] = xp_ref[hist - POOL_BUF:hist, :]


def _slab_rows(rows, max_slabs):
    for slab in range(BF16_ROWS, rows + 1, BF16_ROWS):
        if rows % slab == 0 and rows // slab <= max_slabs:
            return slab
    raise ValueError(f"no slab size for {rows} rows in {max_slabs} steps")


def _ffn_proj_pipelined(x, wg, wu, wd, lng, lnb, w_in, cast_weights, seq, proj_small, meta_row_block):
    n = x.shape[0]
    tm = FFN_TOKEN_TILE
    n_tiles = n // tm
    steps_per_seq = seq // tm
    n_seq = n // seq
    assert seq % tm == 0 and IN_WIDTH % (2 * FFN_COL_CHUNK) == 0 and RET_WIDTH == 2 * FFN_COL_CHUNK
    rope = np.concatenate(_rope_tables(N_META + np.arange(seq)), axis=1)
    in_tile = lambda t: (jnp.minimum(t, n_tiles - 1), 0)
    out_tile = lambda t: (jnp.maximum(t - 1, 0), 0)
    rope_tile = pl.BlockSpec((tm, 2 * HEAD_DIM), lambda t: (jnp.maximum(t - 1, 0) % (seq // tm), 0))

    def slab_spec(w):
        slab = _slab_rows(w.shape[0], n_tiles)
        last = w.shape[0] // slab - 1
        return pl.BlockSpec((slab, w.shape[1]), lambda t: (jnp.minimum(t, last), 0))

    cast_specs = [slab_spec(w) for w in cast_weights]
    meta_p = pl.BlockSpec((N_META, POOL_WIDTH), lambda t: (meta_row_block, 4 * RET_WIDTH // POOL_WIDTH))
    pool_state = pl.BlockSpec(
        (1, POOL_BUF, POOL_WIDTH), lambda t: (jnp.clip((t - 1) // steps_per_seq, 0, n_seq - 1), 0, 0))
    return pl.pallas_call(
        functools.partial(_ffn_proj_pipelined_kernel, n_cast=len(cast_weights),
                          steps_per_seq=steps_per_seq),
        grid=(n_tiles + 1,),
        in_specs=[
            pl.BlockSpec((tm, D_MODEL), in_tile),
            _const_spec(wg.shape), _const_spec(wu.shape), _const_spec(wd.shape),
            _const_spec(lng.shape), _const_spec(lnb.shape), _const_spec(w_in.shape),
            rope_tile, meta_p,
        ] + cast_specs,
        out_specs=[pl.BlockSpec((tm, D_MODEL), out_tile), pl.BlockSpec((tm, IN_WIDTH), out_tile),
                   pool_state] + cast_specs,
        out_shape=[jax.ShapeDtypeStruct((n, D_MODEL), F32), jax.ShapeDtypeStruct((n, IN_WIDTH), F32),
                   jax.ShapeDtypeStruct((n_seq, POOL_BUF, POOL_WIDTH), F32)]
        + [jax.ShapeDtypeStruct(w.shape, BF16) for w in cast_weights],
        scratch_shapes=[
            pltpu.VMEM((tm, D_MODEL), BF16),
            pltpu.VMEM((tm, D_MODEL), F32),
            pltpu.VMEM((tm, D_MODEL), BF16),
            pltpu.VMEM((tm, D_FF), BF16),
            pltpu.VMEM((N_META + tm, POOL_WIDTH), F32),
        ],
        compiler_params=pltpu.CompilerParams(
            dimension_semantics=("arbitrary",), vmem_limit_bytes=VMEM_LIMIT_BYTES),
        name="ffn_proj_pipelined",
    )(x, wg, wu, wd, lng, lnb, w_in, rope, proj_small, *cast_weights)


def _rope_tables(positions):
    half = HEAD_DIM // 2
    inv_freq = ROPE_THETA ** (-np.arange(0, HEAD_DIM, 2, dtype=np.float64) / HEAD_DIM)
    ang = np.asarray(positions, np.float64)[:, None] * inv_freq[None, :]
    cos, sin = np.cos(ang), np.sin(ang)
    assert cos.shape[1] == half
    return (np.concatenate([cos, cos], axis=1).astype(np.float32),
            np.concatenate([-sin, sin], axis=1).astype(np.float32))


def _decay_tables(chunk, seq_len):
    r = np.arange(chunk)
    seq, idx = r // seq_len, (r % seq_len).astype(np.float64)
    same = seq[:, None] == seq[None, :]
    diff = idx[:, None] - idx[None, :]
    mask, qdec, kdec = [], [], []
    for gamma in GAMMAS:
        lg = math.log(gamma)
        mask.append(np.where(same & (diff >= 0), np.exp(lg * np.maximum(diff, 0.0)), 0.0) * QK_SCALE)
        qdec.append(np.broadcast_to((np.exp(lg * (idx + 1.0)) * QK_SCALE)[:, None], (chunk, HEAD_DIM)))
        kdec.append(np.broadcast_to(np.exp(lg * (seq_len - 1.0 - idx))[:, None], (chunk, HEAD_DIM)))
    to32 = lambda t: np.stack(t).astype(np.float32)
    return to32(mask), to32(qdec), to32(kdec)


def _rope(x, cos, sin):
    return x * cos + pltpu.roll(x, HEAD_DIM // 2, 1) * sin


def _group_norm(o):
    mu = jnp.mean(o, axis=-1, keepdims=True)
    oc = o - mu
    var = jnp.mean(oc * oc, axis=-1, keepdims=True)
    return oc * lax.rsqrt(var + GN_EPS)


def _prompt_mixer_ffn_kernel(proj_ref, x1_ref,
                             km_ref, vm_ref, cosm_ref, sinm_ref, kdecm_ref,
                             mask_ref, qdec_ref, kdec_ref, poolw_ref, pscale_ref, wout_ref,
                             ln2g_ref, ln2b_ref, wg_ref, wu_ref, wd_ref, ln3g_ref, ln3b_ref,
                             x2dec_hbm,
                             y_ref, sret_ref, ydec_hbm,
                             s_ref, mix_ref, x2_ref, xb_ref, ypre_ref, h_ref, kb_ref, vs_ref,
                             *, steps_per_seq):
    t = pl.program_id(0)
    n_tiles = pl.num_programs(0) - 2
    c = t % steps_per_seq
    tile = proj_ref.shape[0]
    q_ref, k_ref, v_ref, g_ref, p_ref = (
        proj_ref.at[:, j * RET_WIDTH:(j + 1) * RET_WIDTH] for j in range(5))
    row_blocks = [slice(r, r + RET_CHUNK) for r in range(0, tile, RET_CHUNK)]

    @pl.when(jnp.logical_and(c == 0, t < n_tiles))
    def _init_from_meta():
        for h in range(RET_HEADS):
            hs = slice(h * HEAD_DIM, (h + 1) * HEAD_DIM)
            kr = _rope(km_ref[:, hs], cosm_ref[...], sinm_ref[...])
            kd = (kr * kdecm_ref[h]).astype(BF16)
            s_ref[h] = _dot_tn(kd, vm_ref[:, hs].astype(BF16))

    def ln3_fetch_piece(rows):
        def run():
            y_ref[rows, :] = ypre_ref[rows, :]
            yield
        return run

    def ln3_piece(rows):
        def run():
            y_ref[rows, :] = _layer_norm(y_ref[rows, :], ln3g_ref[...], ln3b_ref[...])
            yield
        return run

    state = [None] * RET_HEADS

    def retention_piece(ci, hp):
        def run():
            rows = slice(ci * RET_CHUNK, (ci + 1) * RET_CHUNK)
            pair = slice(hp * PAIR_WIDTH, (hp + 1) * PAIR_WIDTH)
            buf = ci * 2 + hp
            qr, kr = [], []
            for j in range(2):
                h = 2 * hp + j
                hs = slice(h * HEAD_DIM, (h + 1) * HEAD_DIM)
                blk = slice(j * HEAD_DIM, (j + 1) * HEAD_DIM)
                qr.append(q_ref[rows, hs])
                kr.append(k_ref[rows, hs])
                kb_ref[buf, blk, blk] = kr[j].astype(BF16)
                vs_ref[buf, blk, blk] = v_ref[rows, hs].astype(BF16)
                vs_ref[buf, PAIR_WIDTH + j * HEAD_DIM:PAIR_WIDTH + (j + 1) * HEAD_DIM, blk] = (
                    state[h].astype(BF16))
            q2 = jnp.concatenate(qr, axis=1)
            k2 = jnp.concatenate(kr, axis=1)
            scores = _dot_nt(q2.astype(BF16), kb_ref[buf])
            upd = _dot((k2 * kdec_ref[hp]).T.astype(BF16), v_ref[rows, pair].astype(BF16))
            qd2 = (q2 * qdec_ref[hp]).astype(BF16)
            yield
            lhs = jnp.concatenate([(scores * mask_ref[hp]).astype(BF16), qd2], axis=1)
            o2 = _dot(lhs, vs_ref[buf])
            for j in range(2):
                blk = slice(j * HEAD_DIM, (j + 1) * HEAD_DIM)
                state[2 * hp + j] = (GAMMAS[2 * hp + j] ** RET_CHUNK) * state[2 * hp + j] + upd[blk, blk]
            yield
            for j in range(2):
                hs = slice((2 * hp + j) * HEAD_DIM, (2 * hp + j + 1) * HEAD_DIM)
                blk = slice(j * HEAD_DIM, (j + 1) * HEAD_DIM)
                mix_ref[rows, hs] = (_silu(g_ref[rows, hs]) * _group_norm(o2[:, blk])).astype(BF16)
        return run

    def pool_piece(pp):
        def run():
            pair = slice(pp * PAIR_WIDTH, (pp + 1) * PAIR_WIDTH)
            pooled = _dot(p_ref[:, pair].astype(BF16), poolw_ref[pp])
            yield
            pooled = pooled * pscale_ref[:, pair]
            mix_ref[:, RET_WIDTH + pp * PAIR_WIDTH:RET_WIDTH + (pp + 1) * PAIR_WIDTH] = pooled.astype(BF16)
        return run

    def gate_up_piece(ck):
        def run():
            sl = slice(ck * FFN_COL_CHUNK, (ck + 1) * FFN_COL_CHUNK)
            g = _dot(xb_ref[...], wg_ref[:, sl])
            u = _dot(xb_ref[...], wu_ref[:, sl])
            h_ref[:, sl] = (_silu(g) * u).astype(BF16)
        return run

    def residual_piece():
        ypre_ref[...] = ALPHA * x2_ref[...]
        yield

    def w_out_piece():
        for h in range(RET_HEADS):
            s_ref[h] = state[h]
        x2_ref[...] = ALPHA * x1_ref[...] + _dot(mix_ref[...], wout_ref[...])

    def ln2_piece(rows):
        def run():
            x2_ref[rows, :] = _layer_norm(x2_ref[rows, :], ln2g_ref[...], ln2b_ref[...])
            yield
        return run

    def xb_piece(rows):
        def run():
            xb_ref[rows, :] = x2_ref[rows, :].astype(BF16)
            yield
        return run

    def down_piece(nk):
        def run():
            cs = slice(nk * FFN_COL_CHUNK, (nk + 1) * FFN_COL_CHUNK)
            ypre_ref[:, cs] = ypre_ref[:, cs] + 0.5 * _dot(h_ref[...], wd_ref[:, cs])
        return run

    def steady_body():
        starts = {}

        def start_at(i, piece):
            starts.setdefault(i, []).append(piece)

        for rows in row_blocks:
            start_at(0, ln3_fetch_piece(rows))
        start_at(1, residual_piece)
        for h in range(RET_HEADS):
            state[h] = s_ref[h]
        gate_up = [gate_up_piece(ck) for ck in range(D_FF // FFN_COL_CHUNK)]
        pieces = [(ci, hp) for ci in range(len(row_blocks)) for hp in range(RET_HEADS // 2)]
        w_out_at = len(pieces)
        for n, (ci, hp) in enumerate(pieces):
            start_at(n // 2 if n < 4 else n - 2, retention_piece(ci, hp))
        for pp in range(len(POOL_WINDOWS) // 2):
            start_at(2 * pp + 1, pool_piece(pp))
        major = gate_up[:w_out_at] + [w_out_piece] + gate_up[w_out_at:]
        for i, rows in enumerate(row_blocks):
            start_at(min(w_out_at + i, len(major) - 1), ln2_piece(rows))
        _interleave(major, starts)

        tail_starts = {i: [ln3_piece(rows), xb_piece(rows)] for i, rows in enumerate(row_blocks)}
        _interleave([down_piece(nk) for nk in range(D_MODEL // FFN_COL_CHUNK)], tail_starts)

    def drain_body():
        for rows in row_blocks:
            for make in (ln3_fetch_piece(rows), ln3_piece(rows)):
                for _ in make():
                    pass

    @pl.when(t == 0)
    def _prime_pipeline():
        pltpu.sync_copy(x2dec_hbm, x2_ref)
        xb_ref[...] = x2_ref[...].astype(BF16)
        kb_ref[...] = jnp.zeros(kb_ref.shape, BF16)
        vs_ref[...] = jnp.zeros(vs_ref.shape, BF16)
        ypre_ref[...] = jnp.zeros(ypre_ref.shape, F32)

    @pl.when(t <= n_tiles)
    def _steady():
        steady_body()

    @pl.when(t == 1)
    def _emit_decode_rows():
        pltpu.sync_copy(y_ref, ydec_hbm)

    @pl.when(t == n_tiles + 1)
    def _drain_last():
        drain_body()

    @pl.when(jnp.logical_and(c == steps_per_seq - 1, t < n_tiles))
    def _emit_state():
        sret_ref[0] = s_ref[...]


def _prompt_mixer_ffn(proj, x1, proj_small, meta_row_block, pool_w, pool_scale, w_out, ln2g, ln2b,
                      wg, wu, wd, ln3g, ln3b, x2_dec, batch, seq):
    tile = MIX_TOKEN_TILE
    assert x2_dec.shape == (tile, D_MODEL)
    steps = seq // tile
    n_tiles = batch * steps
    cosm, sinm = _rope_tables(np.arange(N_META))
    pair_up = lambda tab: np.concatenate([tab[0::2], tab[1::2]], axis=2)
    mask, qdec, kdec = (pair_up(tab) for tab in _decay_tables(RET_CHUNK, RET_CHUNK))
    _, _, kdecm = _decay_tables(N_META, N_META)
    zero_blk = jnp.zeros_like(pool_w[0])
    pool_w = jnp.stack([jnp.block([[pool_w[2 * pp], zero_blk], [zero_blk, pool_w[2 * pp + 1]]])
                        for pp in range(len(POOL_WINDOWS) // 2)])

    mix_tile = lambda t: jnp.minimum(t, n_tiles - 1)
    ffn_tile = lambda t: jnp.maximum(t - 2, 0)

    def meta_col(j):
        return pl.BlockSpec((N_META, RET_WIDTH), lambda t: (meta_row_block, j))

    in_specs = [
        pl.BlockSpec((tile, IN_WIDTH), lambda t: (mix_tile(t), 0)),
        pl.BlockSpec((tile, D_MODEL), lambda t: (mix_tile(t), 0)),
        meta_col(1), meta_col(2),
        _const_spec(cosm.shape), _const_spec(sinm.shape), _const_spec(kdecm.shape),
        _const_spec(mask.shape), _const_spec(qdec.shape), _const_spec(kdec.shape),
        _const_spec(pool_w.shape), _const_spec(pool_scale.shape), _const_spec(w_out.shape),
        _const_spec(ln2g.shape), _const_spec(ln2b.shape),
        _const_spec(wg.shape), _const_spec(wu.shape), _const_spec(wd.shape),
        _const_spec(ln3g.shape), _const_spec(ln3b.shape),
        pl.BlockSpec(memory_space=pl.ANY),
    ]
    out_shape = [
        jax.ShapeDtypeStruct((batch * seq, D_MODEL), F32),
        jax.ShapeDtypeStruct((batch, RET_HEADS, HEAD_DIM, HEAD_DIM), F32),
        jax.ShapeDtypeStruct((tile, D_MODEL), F32),
    ]
    out_specs = [
        pl.BlockSpec((tile, D_MODEL), lambda t: (ffn_tile(t), 0)),
        pl.BlockSpec((1, RET_HEADS, HEAD_DIM, HEAD_DIM), lambda t: (mix_tile(t) // steps, 0, 0, 0)),
        pl.BlockSpec(memory_space=pl.ANY),
    ]
    return pl.pallas_call(
        functools.partial(_prompt_mixer_ffn_kernel, steps_per_seq=steps),
        grid=(n_tiles + 2,),
        in_specs=in_specs,
        out_specs=out_specs,
        out_shape=out_shape,
        scratch_shapes=[
            pltpu.VMEM((RET_HEADS, HEAD_DIM, HEAD_DIM), F32),
            pltpu.VMEM((tile, D_MODEL), BF16),
            pltpu.VMEM((tile, D_MODEL), F32),
            pltpu.VMEM((tile, D_MODEL), BF16),
            pltpu.VMEM((tile, D_MODEL), F32),
            pltpu.VMEM((tile, D_FF), BF16),
            pltpu.VMEM((8, PAIR_WIDTH, PAIR_WIDTH), BF16),
            pltpu.VMEM((8, 2 * PAIR_WIDTH, PAIR_WIDTH), BF16),
        ],
        compiler_params=pltpu.CompilerParams(
            dimension_semantics=("arbitrary",), vmem_limit_bytes=VMEM_LIMIT_BYTES),
        name="prompt_mixer_ffn",
    )(proj, x1, proj_small, proj_small,
      cosm, sinm, kdecm, mask, qdec, kdec, pool_w, pool_scale, w_out, ln2g, ln2b,
      wg, wu, wd, ln3g, ln3b, x2_dec)


def _decode_mixer_kernel(q_ref, k_ref, v_ref, g_ref, *rest, dec_seq):
    p_refs, rest = rest[:len(POOL_WINDOWS)], rest[len(POOL_WINDOWS):]
    (x1_ref, s0_ref, pref_ref, cos_ref, sin_ref, mask_ref, qdec_ref, kdec_ref,
     poolw_ref, pscale_ref, wout_ref, lng_ref, lnb_ref,
     o_ref, sret_ref, spool_ref, d_ref, mix_ref) = rest
    rows = q_ref.shape[0]
    nseq = rows // dec_seq
    seq_per_group = BF16_ROWS // dec_seq
    cos, sin = cos_ref[...], sin_ref[...]
    row_seq = lax.broadcasted_iota(jnp.int32, (BF16_ROWS, HEAD_DIM), 0) // dec_seq

    for h in range(RET_HEADS):
        hs = slice(h * HEAD_DIM, (h + 1) * HEAD_DIM)
        qr = _rope(q_ref[:, hs], cos, sin)
        kr = _rope(k_ref[:, hs], cos, sin)
        v = v_ref[:, hs]
        vb = v.astype(BF16)
        scores = _dot_nt(qr.astype(BF16), kr.astype(BF16)) * mask_ref[h]
        o_inner = _dot(scores.astype(BF16), vb)
        qd = qr * qdec_ref[h]
        kd = kr * kdec_ref[h]
        o_cross = []
        for grp in range(rows // BF16_ROWS):
            gr = slice(grp * BF16_ROWS, (grp + 1) * BF16_ROWS)
            qd_g = qd[gr].astype(BF16)
            kd_g = kd[gr].astype(BF16)
            v_g = v[gr]
            acc = jnp.zeros((BF16_ROWS, HEAD_DIM), F32)
            for j in range(seq_per_group):
                b = grp * seq_per_group + j
                s = s0_ref[b, h]
                acc = jnp.where(row_seq == j, _dot(qd_g, s.astype(BF16)), acc)
                v_b = jnp.where(row_seq == j, v_g, 0.0).astype(BF16)
                sret_ref[b, h] = (GAMMAS[h] ** dec_seq) * s + _dot_tn(kd_g, v_b)
            o_cross.append(acc)
        o = o_inner + jnp.concatenate(o_cross, axis=0)
        mix_ref[:, hs] = (_silu(g_ref[:, hs]) * _group_norm(o)).astype(BF16)

    for gi, (w, pg_ref) in enumerate(zip(POOL_WINDOWS, p_refs)):
        gs = slice(gi * POOL_GROUP, (gi + 1) * POOL_GROUP)
        steps = [pref_ref[j, :, gs] for j in range(POOL_BUF)]
        steps += [pg_ref[pl.ds(i, nseq, stride=dec_seq), :] for i in range(dec_seq)]
        for i in range(dec_seq):
            now = POOL_BUF + i
            wsum = steps[now]
            for back in range(1, w):
                wsum = wsum + steps[now - back]
            d_ref[gi, pl.ds(i, nseq, stride=dec_seq), :] = wsum * (1.0 / w) - steps[now]
        for j in range(POOL_BUF):
            spool_ref[j, :, gs] = steps[dec_seq + j]
        pooled = _dot(d_ref[gi].astype(BF16), poolw_ref[gi]) * pscale_ref[:, gs]
        mix_ref[:, RET_WIDTH + gi * POOL_GROUP:RET_WIDTH + (gi + 1) * POOL_GROUP] = pooled.astype(BF16)

    y = _dot(mix_ref[...], wout_ref[...])
    o_ref[...] = _layer_norm(ALPHA * x1_ref[...] + y, lng_ref[...], lnb_ref[...])


def _decode_mixer(proj, x1, state_ret, state_pool, pool_w, pool_scale, w_out, lng, lnb, nseq, dec_seq):
    assert BF16_ROWS % dec_seq == 0 and dec_seq <= POOL_BUF
    rows = DEC_SEQ_BLOCK * dec_seq
    steps = nseq // DEC_SEQ_BLOCK
    cos, sin = _rope_tables(PAST_LEN + (np.arange(rows) % dec_seq))
    mask, qdec, kdec = _decay_tables(rows, dec_seq)

    def col(j):
        return pl.BlockSpec((rows, RET_WIDTH), lambda i: (i, j))

    state_spec = pl.BlockSpec((DEC_SEQ_BLOCK, RET_HEADS, HEAD_DIM, HEAD_DIM), lambda i: (i, 0, 0, 0))
    state_pool = jnp.transpose(state_pool, (1, 0, 2))
    pool_spec = pl.BlockSpec((POOL_BUF, DEC_SEQ_BLOCK, POOL_WIDTH), lambda i: (0, i, 0))
    groups = len(POOL_WINDOWS)
    p_cols = 4 * RET_WIDTH // POOL_GROUP
    in_specs = [
        col(0), col(1), col(2), col(3),
        *[pl.BlockSpec((rows, POOL_GROUP), lambda i, gi=gi: (i, p_cols + gi)) for gi in range(groups)],
        pl.BlockSpec((rows, D_MODEL), lambda i: (i, 0)),
        state_spec, pool_spec,
        _const_spec(cos.shape), _const_spec(sin.shape),
        _const_spec(mask.shape), _const_spec(qdec.shape), _const_spec(kdec.shape),
        _const_spec(pool_w.shape), _const_spec(pool_scale.shape), _const_spec(w_out.shape),
        _const_spec(lng.shape), _const_spec(lnb.shape),
    ]
    out_shape = [
        jax.ShapeDtypeStruct((nseq * dec_seq, D_MODEL), F32),
        jax.ShapeDtypeStruct(state_ret.shape, F32),
        jax.ShapeDtypeStruct(state_pool.shape, F32),
    ]
    out_specs = [pl.BlockSpec((rows, D_MODEL), lambda i: (i, 0)), state_spec, pool_spec]
    x2, new_ret, new_pool = pl.pallas_call(
        functools.partial(_decode_mixer_kernel, dec_seq=dec_seq),
        grid=(steps,),
        in_specs=in_specs,
        out_specs=out_specs,
        out_shape=out_shape,
        scratch_shapes=[
            pltpu.VMEM((groups, rows, POOL_GROUP), F32),
            pltpu.VMEM((rows, D_MODEL), BF16),
        ],
        compiler_params=pltpu.CompilerParams(
            dimension_semantics=("arbitrary",), vmem_limit_bytes=VMEM_LIMIT_BYTES),
        name="decode_mixer",
    )(*([proj] * (4 + groups)), x1, state_ret, state_pool, cos, sin, mask, qdec, kdec,
      pool_w, pool_scale, w_out, lng, lnb)
    return x2, new_ret, jnp.transpose(new_pool, (1, 0, 2))


def kernel(x_prompt, x_sample, state_ret, state_pool, meta_tokens, ffn1_w_gate, ffn1_w_up, ffn1_w_down,
           ln1_g, ln1_b, w_in, pool_w, pool_scale, w_out, ln2_g, ln2_b, ffn2_w_gate, ffn2_w_up,
           ffn2_w_down, ln3_g, ln3_b):
    assert ffn1_w_gate.shape[0] == DEPTH == 1
    batch, seq, _ = x_prompt.shape
    nseq, dec_seq, _ = x_sample.shape
    n_dec = nseq * dec_seq
    assert n_dec % N_META == 0

    bf = lambda w: w[0].astype(BF16)
    row = lambda v: v[0].reshape(1, -1)
    pool_w_b = bf(pool_w)
    pscale, g2, b2 = row(pool_scale), row(ln2_g), row(ln2_b)

    xp = x_prompt.reshape(batch * seq, D_MODEL)
    x_small = jnp.concatenate([x_sample.reshape(n_dec, D_MODEL), meta_tokens.astype(x_prompt.dtype)], axis=0)

    x1s, projs, wg1, wu1, wd1, w_in_b = _ffn_proj_cast(
        x_small, ffn1_w_gate[0], ffn1_w_up[0], ffn1_w_down[0], row(ln1_g), row(ln1_b), w_in[0])
    meta_rows = n_dec // N_META
    x1p, projp, pool_p, wg2, wu2, wd2, w_out_b = _ffn_proj_pipelined(
        xp, wg1, wu1, wd1, row(ln1_g), row(ln1_b), w_in_b,
        (ffn2_w_gate[0], ffn2_w_up[0], ffn2_w_down[0], w_out[0]), seq, projs, meta_rows)
    f2 = (wg2, wu2, wd2, row(ln3_g), row(ln3_b))

    x2s, ret_s, pool_s = _decode_mixer(projs, x1s, state_ret[0], state_pool[0], pool_w_b, pscale,
                                       w_out_b, g2, b2, nseq, dec_seq)
    y_prompt, ret_p, y_sample = _prompt_mixer_ffn(
        projp, x1p, projs, meta_rows, pool_w_b, pscale, w_out_b, g2, b2, *f2, x2s, batch, seq)
    return (y_prompt.reshape(batch, seq, D_MODEL), y_sample.reshape(nseq, dec_seq, D_MODEL),
            ret_p[None], pool_p[None], ret_s[None], pool_s[None])
```

```python
import functools
import math

import jax
import jax.numpy as jnp
import numpy as np
from jax import lax
from jax.experimental import pallas as pl
from jax.experimental.pallas import tpu as pltpu

F32 = jnp.float32
BF16 = jnp.bfloat16

D_MODEL = 1024
D_FF = 2816
N_META = 16
PAST_LEN = 16384
RET_HEADS = 4
HEAD_DIM = 128
RET_WIDTH = RET_HEADS * HEAD_DIM
RET_CHUNK = 128
ROPE_THETA = 10000.0
POOL_WINDOWS = (2, 4, 8, 16)
POOL_GROUP = 128
POOL_WIDTH = POOL_GROUP * len(POOL_WINDOWS)
POOL_BUF = max(POOL_WINDOWS) - 1
IN_WIDTH = 4 * RET_WIDTH + POOL_WIDTH
DEPTH = 1
ALPHA = (2.0 * DEPTH) ** 0.25
LN_EPS = 1e-5
GN_EPS = 1e-5
QK_SCALE = HEAD_DIM ** -0.5
GAMMAS = tuple(1.0 - 2.0 ** (-5.0 - h) for h in range(RET_HEADS))

VMEM_LIMIT_BYTES = 56 * 1024 * 1024
FFN_TOKEN_TILE = 512
FFN_COL_CHUNK = 256
MIX_TOKEN_TILE = 512
DEC_SEQ_BLOCK = 32
BF16_ROWS = 16
PAIR_WIDTH = 2 * HEAD_DIM


def _layer_norm(z, g, b):
    mu = jnp.mean(z, axis=-1, keepdims=True)
    zc = z - mu
    var = jnp.mean(zc * zc, axis=-1, keepdims=True)
    return zc * lax.rsqrt(var + LN_EPS) * g + b


def _silu(x):
    return x * jax.nn.sigmoid(x)


def _dot(a, b):
    return jnp.dot(a, b, preferred_element_type=F32)


def _dot_nt(a, b):
    return lax.dot_general(a, b, (((1,), (1,)), ((), ())), preferred_element_type=F32)


def _dot_tn(a, b):
    return lax.dot_general(a, b, (((0,), (0,)), ((), ())), preferred_element_type=F32)


def _const_spec(shape):
    zeros = (0,) * len(shape)
    return pl.BlockSpec(shape, lambda *_: zeros, pipeline_mode=pl.Buffered(1))


def _ffn_proj_cast_kernel(x_ref, wg_ref, wu_ref, wd_ref, lng_ref, lnb_ref, win_ref,
                          x1_ref, proj_ref, wgb_ref, wub_ref, wdb_ref, winb_ref,
                          acc_ref, xb_ref, win_all_ref):
    k = pl.program_id(0)

    @pl.when(k == 0)
    def _start():
        xb_ref[...] = x_ref[...].astype(BF16)
        acc_ref[...] = jnp.zeros(acc_ref.shape, F32)

    wg, wu, wd = wg_ref[...].astype(BF16), wu_ref[...].astype(BF16), wd_ref[...].astype(BF16)
    wgb_ref[...] = wg
    wub_ref[...] = wu
    wdb_ref[...] = wd
    h = (_silu(_dot(xb_ref[...], wg)) * _dot(xb_ref[...], wu)).astype(BF16)
    acc_ref[...] += _dot(h, wd)

    for j in range(IN_WIDTH // FFN_COL_CHUNK):
        @pl.when(k == j)
        def _round_w_in_chunk(j=j):
            w_in = win_ref[...].astype(BF16)
            winb_ref[...] = w_in
            win_all_ref[:, j * FFN_COL_CHUNK:(j + 1) * FFN_COL_CHUNK] = w_in

    @pl.when(k == pl.num_programs(0) - 1)
    def _finish():
        x1 = _layer_norm(ALPHA * x_ref[...] + 0.5 * acc_ref[...], lng_ref[...], lnb_ref[...])
        x1_ref[...] = x1
        proj_ref[...] = _dot(x1.astype(BF16), win_all_ref[...])


def _ffn_proj_cast(x, wg, wu, wd, lng, lnb, w_in):
    n = x.shape[0]
    ck = FFN_COL_CHUNK
    steps = D_FF // ck
    in_chunks = IN_WIDTH // ck
    assert in_chunks <= steps
    col_chunk = pl.BlockSpec((D_MODEL, ck), lambda k: (0, k))
    row_chunk = pl.BlockSpec((ck, D_MODEL), lambda k: (k, 0))
    w_in_chunk = pl.BlockSpec((D_MODEL, ck), lambda k: (0, jnp.minimum(k, in_chunks - 1)))
    whole = lambda shape: pl.BlockSpec(shape, lambda k: (0,) * len(shape))
    return pl.pallas_call(
        _ffn_proj_cast_kernel,
        grid=(steps,),
        in_specs=[_const_spec(x.shape), col_chunk, col_chunk, row_chunk,
                  _const_spec(lng.shape), _const_spec(lnb.shape), w_in_chunk],
        out_specs=[whole((n, D_MODEL)), whole((n, IN_WIDTH)), col_chunk, col_chunk, row_chunk,
                   w_in_chunk],
        out_shape=[jax.ShapeDtypeStruct((n, D_MODEL), F32), jax.ShapeDtypeStruct((n, IN_WIDTH), F32),
                   jax.ShapeDtypeStruct(wg.shape, BF16), jax.ShapeDtypeStruct(wu.shape, BF16),
                   jax.ShapeDtypeStruct(wd.shape, BF16), jax.ShapeDtypeStruct(w_in.shape, BF16)],
        scratch_shapes=[pltpu.VMEM((n, D_MODEL), F32), pltpu.VMEM((n, D_MODEL), BF16),
                        pltpu.VMEM(w_in.shape, BF16)],
        compiler_params=pltpu.CompilerParams(
            dimension_semantics=("arbitrary",), vmem_limit_bytes=VMEM_LIMIT_BYTES),
        name="ffn_proj_cast",
    )(x, wg, wu, wd, lng, lnb, w_in)


def _interleave(major, starts):
    live = []
    for i, piece in enumerate(major):
        piece()
        live += [make() for make in starts.get(i, [])]
        live = [g for g in live if next(g, "done") != "done"]
    while live:
        live = [g for g in live if next(g, "done") != "done"]


def _run_all(starts):
    for i in sorted(starts):
        for make in starts[i]:
            for _ in make():
                pass


def _ffn_proj_pipelined_kernel(x_ref, wg_ref, wu_ref, wd_ref, lng_ref, lnb_ref, win_ref,
                               rope_ref, pm_ref, *rest, n_cast, steps_per_seq):
    cast_in, rest = rest[:n_cast], rest[n_cast:]
    x1_ref, proj_ref, spool_ref = rest[:3]
    cast_out, (xb_ref, z1_ref, x1b_ref, h_ref, xp_ref) = rest[3:3 + n_cast], rest[3 + n_cast:]
    t = pl.program_id(0)
    n_tiles = pl.num_programs(0) - 1
    tile = x_ref.shape[0]
    row_blocks = [slice(r, r + RET_CHUNK) for r in range(0, tile, RET_CHUNK)]
    proj_chunk = 2 * FFN_COL_CHUNK
    hist = N_META
    seq_step = (t - 1) % steps_per_seq

    def ln1_piece(rows):
        def run():
            x1_rows = _layer_norm(z1_ref[rows, :], lng_ref[...], lnb_ref[...])
            x1_ref[rows, :] = x1_rows
            x1b_ref[rows, :] = x1_rows.astype(BF16)
            yield
        return run

    def proj_piece(nk):
        def run():
            cs = slice(nk * proj_chunk, (nk + 1) * proj_chunk)
            chunk = _dot(x1b_ref[...], win_ref[:, cs])
            if nk < 2:
                cos, sin = rope_ref[:, :HEAD_DIM], rope_ref[:, HEAD_DIM:]
                chunk = jnp.concatenate(
                    [_rope(chunk[:, h * HEAD_DIM:(h + 1) * HEAD_DIM], cos, sin)
                     for h in range(RET_HEADS)], axis=1)
            proj_ref[:, cs] = chunk
        return run

    def pool_input_piece(half):
        def run():
            cs = slice(half * PAIR_WIDTH, (half + 1) * PAIR_WIDTH)
            xp_ref[hist:hist + tile, cs] = _dot(
                x1b_ref[...], win_ref[:, 4 * RET_WIDTH + half * PAIR_WIDTH:4 * RET_WIDTH + (half + 1) * PAIR_WIDTH])
        return run

    def pool_diff_piece(gi):
        def run():
            w = POOL_WINDOWS[gi]
            gs = slice(gi * POOL_GROUP, (gi + 1) * POOL_GROUP)
            rows_all = xp_ref[:, gs]
            wsum, shift = rows_all, 1
            while shift < w:
                wsum = wsum + pltpu.roll(wsum, shift, 0)
                shift *= 2
            proj_ref[:, 4 * RET_WIDTH + gi * POOL_GROUP:4 * RET_WIDTH + (gi + 1) * POOL_GROUP] = (
                wsum[hist:] * (1.0 / w) - rows_all[hist:])
            xp_ref[0:hist, gs] = xp_ref[tile:tile + hist, gs]
            yield
        return run

    def gate_up_piece(ck):
        def run():
            sl = slice(ck * FFN_COL_CHUNK, (ck + 1) * FFN_COL_CHUNK)
            g = _dot(xb_ref[...], wg_ref[:, sl])
            u = _dot(xb_ref[...], wu_ref[:, sl])
            h_ref[:, sl] = (_silu(g) * u).astype(BF16)
        return run

    def down_piece(nk):
        def run():
            cs = slice(nk * FFN_COL_CHUNK, (nk + 1) * FFN_COL_CHUNK)
            z1_ref[:, cs] = ALPHA * x_ref[:, cs] + 0.5 * _dot(h_ref[...], wd_ref[:, cs])
        return run

    ln1_starts = {i: [ln1_piece(rows)] for i, rows in enumerate(row_blocks)}
    qkvg = [proj_piece(nk) for nk in range(4 * RET_WIDTH // proj_chunk)]
    proj_pieces = [pool_input_piece(0)] + qkvg[:2] + [pool_input_piece(1)] + qkvg[2:]
    pool_diffs = [pool_diff_piece(gi) for gi in range(len(POOL_WINDOWS))]

    @pl.when(jnp.logical_and(t >= 1, seq_step == 0))
    def _history_from_meta():
        xp_ref[0:hist, :] = pm_ref[...]

    @pl.when(t == 0)
    def _clear_pipeline():
        z1_ref[...] = jnp.zeros(z1_ref.shape, F32)
        xp_ref[0:hist, :] = jnp.zeros((hist, POOL_WIDTH), F32)

    @pl.when(t < n_tiles)
    def _steady():
        xb_ref[...] = x_ref[...].astype(BF16)
        for src_ref, dst_ref in zip(cast_in, cast_out):
            dst_ref[...] = src_ref[...].astype(BF16)
        gate_up = [gate_up_piece(ck) for ck in range(D_FF // FFN_COL_CHUNK)]
        first_proj = len(row_blocks) + 1
        major = gate_up[:first_proj]
        for i, piece in enumerate(gate_up[first_proj:]):
            major += proj_pieces[i:i + 1] + [piece]
        major += proj_pieces[len(gate_up) - first_proj:]
        starts = dict(ln1_starts)
        half = len(pool_diffs) // 2
        starts[len(major) - 2] = pool_diffs[:half]
        _interleave(major, starts)
        _interleave([down_piece(nk) for nk in range(D_MODEL // FFN_COL_CHUNK)],
                    {i: [piece] for i, piece in enumerate(pool_diffs[half:])})

    @pl.when(t == n_tiles)
    def _drain():
        _run_all(ln1_starts)
        for piece in proj_pieces:
            piece()
        _run_all({0: pool_diffs})

    @pl.when(jnp.logical_and(t >= 1, seq_step == steps_per_seq - 1))
    def _emit_pool_state():
        spool_ref[0] = xp_ref[hist - POOL_BUF:hist, :]


def _slab_rows(rows, max_slabs):
    for slab in range(BF16_ROWS, rows + 1, BF16_ROWS):
        if rows % slab == 0 and rows // slab <= max_slabs:
            return slab
    raise ValueError(f"no slab size for {rows} rows in {max_slabs} steps")


def _ffn_proj_pipelined(x, wg, wu, wd, lng, lnb, w_in, cast_weights, seq, proj_small, meta_row_block):
    n = x.shape[0]
    tm = FFN_TOKEN_TILE
    n_tiles = n // tm
    steps_per_seq = seq // tm
    n_seq = n // seq
    assert seq % tm == 0 and IN_WIDTH % (2 * FFN_COL_CHUNK) == 0 and RET_WIDTH == 2 * FFN_COL_CHUNK
    rope = np.concatenate(_rope_tables(N_META + np.arange(seq)), axis=1)
    in_tile = lambda t: (jnp.minimum(t, n_tiles - 1), 0)
    out_tile = lambda t: (jnp.maximum(t - 1, 0), 0)
    rope_tile = pl.BlockSpec((tm, 2 * HEAD_DIM), lambda t: (jnp.maximum(t - 1, 0) % (seq // tm), 0))

    def slab_spec(w):
        slab = _slab_rows(w.shape[0], n_tiles)
        last = w.shape[0] // slab - 1
        return pl.BlockSpec((slab, w.shape[1]), lambda t: (jnp.minimum(t, last), 0))

    cast_specs = [slab_spec(w) for w in cast_weights]
    meta_p = pl.BlockSpec((N_META, POOL_WIDTH), lambda t: (meta_row_block, 4 * RET_WIDTH // POOL_WIDTH))
    pool_state = pl.BlockSpec(
        (1, POOL_BUF, POOL_WIDTH), lambda t: (jnp.clip((t - 1) // steps_per_seq, 0, n_seq - 1), 0, 0))
    return pl.pallas_call(
        functools.partial(_ffn_proj_pipelined_kernel, n_cast=len(cast_weights),
                          steps_per_seq=steps_per_seq),
        grid=(n_tiles + 1,),
        in_specs=[
            pl.BlockSpec((tm, D_MODEL), in_tile),
            _const_spec(wg.shape), _const_spec(wu.shape), _const_spec(wd.shape),
            _const_spec(lng.shape), _const_spec(lnb.shape), _const_spec(w_in.shape),
            rope_tile, meta_p,
        ] + cast_specs,
        out_specs=[pl.BlockSpec((tm, D_MODEL), out_tile), pl.BlockSpec((tm, IN_WIDTH), out_tile),
                   pool_state] + cast_specs,
        out_shape=[jax.ShapeDtypeStruct((n, D_MODEL), F32), jax.ShapeDtypeStruct((n, IN_WIDTH), F32),
                   jax.ShapeDtypeStruct((n_seq, POOL_BUF, POOL_WIDTH), F32)]
        + [jax.ShapeDtypeStruct(w.shape, BF16) for w in cast_weights],
        scratch_shapes=[
            pltpu.VMEM((tm, D_MODEL), BF16),
            pltpu.VMEM((tm, D_MODEL), F32),
            pltpu.VMEM((tm, D_MODEL), BF16),
            pltpu.VMEM((tm, D_FF), BF16),
            pltpu.VMEM((N_META + tm, POOL_WIDTH), F32),
        ],
        compiler_params=pltpu.CompilerParams(
            dimension_semantics=("arbitrary",), vmem_limit_bytes=VMEM_LIMIT_BYTES),
        name="ffn_proj_pipelined",
    )(x, wg, wu, wd, lng, lnb, w_in, rope, proj_small, *cast_weights)


def _rope_tables(positions):
    half = HEAD_DIM // 2
    inv_freq = ROPE_THETA ** (-np.arange(0, HEAD_DIM, 2, dtype=np.float64) / HEAD_DIM)
    ang = np.asarray(positions, np.float64)[:, None] * inv_freq[None, :]
    cos, sin = np.cos(ang), np.sin(ang)
    assert cos.shape[1] == half
    return (np.concatenate([cos, cos], axis=1).astype(np.float32),
            np.concatenate([-sin, sin], axis=1).astype(np.float32))


def _decay_tables(chunk, seq_len):
    r = np.arange(chunk)
    seq, idx = r // seq_len, (r % seq_len).astype(np.float64)
    same = seq[:, None] == seq[None, :]
    diff = idx[:, None] - idx[None, :]
    mask, qdec, kdec = [], [], []
    for gamma in GAMMAS:
        lg = math.log(gamma)
        mask.append(np.where(same & (diff >= 0), np.exp(lg * np.maximum(diff, 0.0)), 0.0) * QK_SCALE)
        qdec.append(np.broadcast_to((np.exp(lg * (idx + 1.0)) * QK_SCALE)[:, None], (chunk, HEAD_DIM)))
        kdec.append(np.broadcast_to(np.exp(lg * (seq_len - 1.0 - idx))[:, None], (chunk, HEAD_DIM)))
    to32 = lambda t: np.stack(t).astype(np.float32)
    return to32(mask), to32(qdec), to32(kdec)


def _rope(x, cos, sin):
    return x * cos + pltpu.roll(x, HEAD_DIM // 2, 1) * sin


def _group_norm(o):
    mu = jnp.mean(o, axis=-1, keepdims=True)
    oc = o - mu
    var = jnp.mean(oc * oc, axis=-1, keepdims=True)
    return oc * lax.rsqrt(var + GN_EPS)


def _prompt_mixer_ffn_kernel(proj_ref, x1_ref,
                             km_ref, vm_ref, cosm_ref, sinm_ref, kdecm_ref,
                             mask_ref, qdec_ref, kdec_ref, poolw_ref, pscale_ref, wout_ref,
                             ln2g_ref, ln2b_ref, wg_ref, wu_ref, wd_ref, ln3g_ref, ln3b_ref,
                             x2dec_hbm,
                             y_ref, sret_ref, ydec_hbm,
                             s_ref, mix_ref, x2_ref, xb_ref, ypre_ref, h_ref, kb_ref, vs_ref,
                             *, steps_per_seq):
    t = pl.program_id(0)
    n_tiles = pl.num_programs(0) - 2
    c = t % steps_per_seq
    tile = proj_ref.shape[0]
    q_ref, k_ref, v_ref, g_ref, p_ref = (
        proj_ref.at[:, j * RET_WIDTH:(j + 1) * RET_WIDTH] for j in range(5))
    row_blocks = [slice(r, r + RET_CHUNK) for r in range(0, tile, RET_CHUNK)]

    @pl.when(jnp.logical_and(c == 0, t < n_tiles))
    def _init_from_meta():
        for h in range(RET_HEADS):
            hs = slice(h * HEAD_DIM, (h + 1) * HEAD_DIM)
            kr = _rope(km_ref[:, hs], cosm_ref[...], sinm_ref[...])
            kd = (kr * kdecm_ref[h]).astype(BF16)
            s_ref[h] = _dot_tn(kd, vm_ref[:, hs].astype(BF16))

    def ln3_fetch_piece(rows):
        def run():
            y_ref[rows, :] = ypre_ref[rows, :]
            yield
        return run

    def ln3_piece(rows):
        def run():
            y_ref[rows, :] = _layer_norm(y_ref[rows, :], ln3g_ref[...], ln3b_ref[...])
            yield
        return run

    state = [None] * RET_HEADS

    def retention_piece(ci, hp):
        def run():
            rows = slice(ci * RET_CHUNK, (ci + 1) * RET_CHUNK)
            pair = slice(hp * PAIR_WIDTH, (hp + 1) * PAIR_WIDTH)
            buf = ci * 2 + hp
            qr, kr = [], []
            for j in range(2):
                h = 2 * hp + j
                hs = slice(h * HEAD_DIM, (h + 1) * HEAD_DIM)
                blk = slice(j * HEAD_DIM, (j + 1) * HEAD_DIM)
                qr.append(q_ref[rows, hs])
                kr.append(k_ref[rows, hs])
                kb_ref[buf, blk, blk] = kr[j].astype(BF16)
                vs_ref[buf, blk, blk] = v_ref[rows, hs].astype(BF16)
                vs_ref[buf, PAIR_WIDTH + j * HEAD_DIM:PAIR_WIDTH + (j + 1) * HEAD_DIM, blk] = (
                    state[h].astype(BF16))
            q2 = jnp.concatenate(qr, axis=1)
            k2 = jnp.concatenate(kr, axis=1)
            scores = _dot_nt(q2.astype(BF16), kb_ref[buf])
            upd = _dot((k2 * kdec_ref[hp]).T.astype(BF16), v_ref[rows, pair].astype(BF16))
            qd2 = (q2 * qdec_ref[hp]).astype(BF16)
            yield
            lhs = jnp.concatenate([(scores * mask_ref[hp]).astype(BF16), qd2], axis=1)
            o2 = _dot(lhs, vs_ref[buf])
            for j in range(2):
                blk = slice(j * HEAD_DIM, (j + 1) * HEAD_DIM)
                state[2 * hp + j] = (GAMMAS[2 * hp + j] ** RET_CHUNK) * state[2 * hp + j] + upd[blk, blk]
            yield
            for j in range(2):
                hs = slice((2 * hp + j) * HEAD_DIM, (2 * hp + j + 1) * HEAD_DIM)
                blk = slice(j * HEAD_DIM, (j + 1) * HEAD_DIM)
                mix_ref[rows, hs] = (_silu(g_ref[rows, hs]) * _group_norm(o2[:, blk])).astype(BF16)
        return run

    def pool_piece(pp):
        def run():
            pair = slice(pp * PAIR_WIDTH, (pp + 1) * PAIR_WIDTH)
            pooled = _dot(p_ref[:, pair].astype(BF16), poolw_ref[pp])
            yield
            pooled = pooled * pscale_ref[:, pair]
            mix_ref[:, RET_WIDTH + pp * PAIR_WIDTH:RET_WIDTH + (pp + 1) * PAIR_WIDTH] = pooled.astype(BF16)
        return run

    def gate_up_piece(ck):
        def run():
            sl = slice(ck * FFN_COL_CHUNK, (ck + 1) * FFN_COL_CHUNK)
            g = _dot(xb_ref[...], wg_ref[:, sl])
            u = _dot(xb_ref[...], wu_ref[:, sl])
            h_ref[:, sl] = (_silu(g) * u).astype(BF16)
        return run

    def residual_piece():
        ypre_ref[...] = ALPHA * x2_ref[...]
        yield

    def w_out_piece():
        for h in range(RET_HEADS):
            s_ref[h] = state[h]
        x2_ref[...] = ALPHA * x1_ref[...] + _dot(mix_ref[...], wout_ref[...])

    def ln2_piece(rows):
        def run():
            x2_ref[rows, :] = _layer_norm(x2_ref[rows, :], ln2g_ref[...], ln2b_ref[...])
            yield
        return run

    def xb_piece(rows):
        def run():
            xb_ref[rows, :] = x2_ref[rows, :].astype(BF16)
            yield
        return run

    def down_piece(nk):
        def run():
            cs = slice(nk * FFN_COL_CHUNK, (nk + 1) * FFN_COL_CHUNK)
            ypre_ref[:, cs] = ypre_ref[:, cs] + 0.5 * _dot(h_ref[...], wd_ref[:, cs])
        return run

    def steady_body():
        starts = {}

        def start_at(i, piece):
            starts.setdefault(i, []).append(piece)

        for rows in row_blocks:
            start_at(0, ln3_fetch_piece(rows))
        start_at(1, residual_piece)
        for h in range(RET_HEADS):
            state[h] = s_ref[h]
        gate_up = [gate_up_piece(ck) for ck in range(D_FF // FFN_COL_CHUNK)]
        pieces = [(ci, hp) for ci in range(len(row_blocks)) for hp in range(RET_HEADS // 2)]
        w_out_at = len(pieces)
        for n, (ci, hp) in enumerate(pieces):
            start_at(n // 2 if n < 4 else n - 2, retention_piece(ci, hp))
        for pp in range(len(POOL_WINDOWS) // 2):
            start_at(2 * pp + 1, pool_piece(pp))
        major = gate_up[:w_out_at] + [w_out_piece] + gate_up[w_out_at:]
        for i, rows in enumerate(row_blocks):
            start_at(min(w_out_at + i, len(major) - 1), ln2_piece(rows))
        _interleave(major, starts)

        tail_starts = {i: [ln3_piece(rows), xb_piece(rows)] for i, rows in enumerate(row_blocks)}
        _interleave([down_piece(nk) for nk in range(D_MODEL // FFN_COL_CHUNK)], tail_starts)

    def drain_body():
        for rows in row_blocks:
            for make in (ln3_fetch_piece(rows), ln3_piece(rows)):
                for _ in make():
                    pass

    @pl.when(t == 0)
    def _prime_pipeline():
        pltpu.sync_copy(x2dec_hbm, x2_ref)
        xb_ref[...] = x2_ref[...].astype(BF16)
        kb_ref[...] = jnp.zeros(kb_ref.shape, BF16)
        vs_ref[...] = jnp.zeros(vs_ref.shape, BF16)
        ypre_ref[...] = jnp.zeros(ypre_ref.shape, F32)

    @pl.when(t <= n_tiles)
    def _steady():
        steady_body()

    @pl.when(t == 1)
    def _emit_decode_rows():
        pltpu.sync_copy(y_ref, ydec_hbm)

    @pl.when(t == n_tiles + 1)
    def _drain_last():
        drain_body()

    @pl.when(jnp.logical_and(c == steps_per_seq - 1, t < n_tiles))
    def _emit_state():
        sret_ref[0] = s_ref[...]


def _prompt_mixer_ffn(proj, x1, proj_small, meta_row_block, pool_w, pool_scale, w_out, ln2g, ln2b,
                      wg, wu, wd, ln3g, ln3b, x2_dec, batch, seq):
    tile = MIX_TOKEN_TILE
    assert x2_dec.shape == (tile, D_MODEL)
    steps = seq // tile
    n_tiles = batch * steps
    cosm, sinm = _rope_tables(np.arange(N_META))
    pair_up = lambda tab: np.concatenate([tab[0::2], tab[1::2]], axis=2)
    mask, qdec, kdec = (pair_up(tab) for tab in _decay_tables(RET_CHUNK, RET_CHUNK))
    _, _, kdecm = _decay_tables(N_META, N_META)
    zero_blk = jnp.zeros_like(pool_w[0])
    pool_w = jnp.stack([jnp.block([[pool_w[2 * pp], zero_blk], [zero_blk, pool_w[2 * pp + 1]]])
                        for pp in range(len(POOL_WINDOWS) // 2)])

    mix_tile = lambda t: jnp.minimum(t, n_tiles - 1)
    ffn_tile = lambda t: jnp.maximum(t - 2, 0)

    def meta_col(j):
        return pl.BlockSpec((N_META, RET_WIDTH), lambda t: (meta_row_block, j))

    in_specs = [
        pl.BlockSpec((tile, IN_WIDTH), lambda t: (mix_tile(t), 0)),
        pl.BlockSpec((tile, D_MODEL), lambda t: (mix_tile(t), 0)),
        meta_col(1), meta_col(2),
        _const_spec(cosm.shape), _const_spec(sinm.shape), _const_spec(kdecm.shape),
        _const_spec(mask.shape), _const_spec(qdec.shape), _const_spec(kdec.shape),
        _const_spec(pool_w.shape), _const_spec(pool_scale.shape), _const_spec(w_out.shape),
        _const_spec(ln2g.shape), _const_spec(ln2b.shape),
        _const_spec(wg.shape), _const_spec(wu.shape), _const_spec(wd.shape),
        _const_spec(ln3g.shape), _const_spec(ln3b.shape),
        pl.BlockSpec(memory_space=pl.ANY),
    ]
    out_shape = [
        jax.ShapeDtypeStruct((batch * seq, D_MODEL), F32),
        jax.ShapeDtypeStruct((batch, RET_HEADS, HEAD_DIM, HEAD_DIM), F32),
        jax.ShapeDtypeStruct((tile, D_MODEL), F32),
    ]
    out_specs = [
        pl.BlockSpec((tile, D_MODEL), lambda t: (ffn_tile(t), 0)),
        pl.BlockSpec((1, RET_HEADS, HEAD_DIM, HEAD_DIM), lambda t: (mix_tile(t) // steps, 0, 0, 0)),
        pl.BlockSpec(memory_space=pl.ANY),
    ]
    return pl.pallas_call(
        functools.partial(_prompt_mixer_ffn_kernel, steps_per_seq=steps),
        grid=(n_tiles + 2,),
        in_specs=in_specs,
        out_specs=out_specs,
        out_shape=out_shape,
        scratch_shapes=[
            pltpu.VMEM((RET_HEADS, HEAD_DIM, HEAD_DIM), F32),
            pltpu.VMEM((tile, D_MODEL), BF16),
            pltpu.VMEM((tile, D_MODEL), F32),
            pltpu.VMEM((tile, D_MODEL), BF16),
            pltpu.VMEM((tile, D_MODEL), F32),
            pltpu.VMEM((tile, D_FF), BF16),
            pltpu.VMEM((8, PAIR_WIDTH, PAIR_WIDTH), BF16),
            pltpu.VMEM((8, 2 * PAIR_WIDTH, PAIR_WIDTH), BF16),
        ],
        compiler_params=pltpu.CompilerParams(
            dimension_semantics=("arbitrary",), vmem_limit_bytes=VMEM_LIMIT_BYTES),
        name="prompt_mixer_ffn",
    )(proj, x1, proj_small, proj_small,
      cosm, sinm, kdecm, mask, qdec, kdec, pool_w, pool_scale, w_out, ln2g, ln2b,
      wg, wu, wd, ln3g, ln3b, x2_dec)


def _decode_mixer_kernel(q_ref, k_ref, v_ref, g_ref, *rest, dec_seq):
    p_refs, rest = rest[:len(POOL_WINDOWS)], rest[len(POOL_WINDOWS):]
    (x1_ref, s0_ref, pref_ref, cos_ref, sin_ref, mask_ref, qdec_ref, kdec_ref,
     poolw_ref, pscale_ref, wout_ref, lng_ref, lnb_ref,
     o_ref, sret_ref, spool_ref, d_ref, mix_ref) = rest
    rows = q_ref.shape[0]
    nseq = rows // dec_seq
    seq_per_group = BF16_ROWS // dec_seq
    cos, sin = cos_ref[...], sin_ref[...]
    row_seq = lax.broadcasted_iota(jnp.int32, (BF16_ROWS, HEAD_DIM), 0) // dec_seq

    for h in range(RET_HEADS):
        hs = slice(h * HEAD_DIM, (h + 1) * HEAD_DIM)
        qr = _rope(q_ref[:, hs], cos, sin)
        kr = _rope(k_ref[:, hs], cos, sin)
        v = v_ref[:, hs]
        vb = v.astype(BF16)
        scores = _dot_nt(qr.astype(BF16), kr.astype(BF16)) * mask_ref[h]
        o_inner = _dot(scores.astype(BF16), vb)
        qd = qr * qdec_ref[h]
        kd = kr * kdec_ref[h]
        o_cross = []
        for grp in range(rows // BF16_ROWS):
            gr = slice(grp * BF16_ROWS, (grp + 1) * BF16_ROWS)
            qd_g = qd[gr].astype(BF16)
            kd_g = kd[gr].astype(BF16)
            v_g = v[gr]
            acc = jnp.zeros((BF16_ROWS, HEAD_DIM), F32)
            for j in range(seq_per_group):
                b = grp * seq_per_group + j
                s = s0_ref[b, h]
                acc = jnp.where(row_seq == j, _dot(qd_g, s.astype(BF16)), acc)
                v_b = jnp.where(row_seq == j, v_g, 0.0).astype(BF16)
                sret_ref[b, h] = (GAMMAS[h] ** dec_seq) * s + _dot_tn(kd_g, v_b)
            o_cross.append(acc)
        o = o_inner + jnp.concatenate(o_cross, axis=0)
        mix_ref[:, hs] = (_silu(g_ref[:, hs]) * _group_norm(o)).astype(BF16)

    for gi, (w, pg_ref) in enumerate(zip(POOL_WINDOWS, p_refs)):
        gs = slice(gi * POOL_GROUP, (gi + 1) * POOL_GROUP)
        steps = [pref_ref[j, :, gs] for j in range(POOL_BUF)]
        steps += [pg_ref[pl.ds(i, nseq, stride=dec_seq), :] for i in range(dec_seq)]
        for i in range(dec_seq):
            now = POOL_BUF + i
            wsum = steps[now]
            for back in range(1, w):
                wsum = wsum + steps[now - back]
            d_ref[gi, pl.ds(i, nseq, stride=dec_seq), :] = wsum * (1.0 / w) - steps[now]
        for j in range(POOL_BUF):
            spool_ref[j, :, gs] = steps[dec_seq + j]
        pooled = _dot(d_ref[gi].astype(BF16), poolw_ref[gi]) * pscale_ref[:, gs]
        mix_ref[:, RET_WIDTH + gi * POOL_GROUP:RET_WIDTH + (gi + 1) * POOL_GROUP] = pooled.astype(BF16)

    y = _dot(mix_ref[...], wout_ref[...])
    o_ref[...] = _layer_norm(ALPHA * x1_ref[...] + y, lng_ref[...], lnb_ref[...])


def _decode_mixer(proj, x1, state_ret, state_pool, pool_w, pool_scale, w_out, lng, lnb, nseq, dec_seq):
    assert BF16_ROWS % dec_seq == 0 and dec_seq <= POOL_BUF
    rows = DEC_SEQ_BLOCK * dec_seq
    steps = nseq // DEC_SEQ_BLOCK
    cos, sin = _rope_tables(PAST_LEN + (np.arange(rows) % dec_seq))
    mask, qdec, kdec = _decay_tables(rows, dec_seq)

    def col(j):
        return pl.BlockSpec((rows, RET_WIDTH), lambda i: (i, j))

    state_spec = pl.BlockSpec((DEC_SEQ_BLOCK, RET_HEADS, HEAD_DIM, HEAD_DIM), lambda i: (i, 0, 0, 0))
    state_pool = jnp.transpose(state_pool, (1, 0, 2))
    pool_spec = pl.BlockSpec((POOL_BUF, DEC_SEQ_BLOCK, POOL_WIDTH), lambda i: (0, i, 0))
    groups = len(POOL_WINDOWS)
    p_cols = 4 * RET_WIDTH // POOL_GROUP
    in_specs = [
        col(0), col(1), col(2), col(3),
        *[pl.BlockSpec((rows, POOL_GROUP), lambda i, gi=gi: (i, p_cols + gi)) for gi in range(groups)],
        pl.BlockSpec((rows, D_MODEL), lambda i: (i, 0)),
        state_spec, pool_spec,
        _const_spec(cos.shape), _const_spec(sin.shape),
        _const_spec(mask.shape), _const_spec(qdec.shape), _const_spec(kdec.shape),
        _const_spec(pool_w.shape), _const_spec(pool_scale.shape), _const_spec(w_out.shape),
        _const_spec(lng.shape), _const_spec(lnb.shape),
    ]
    out_shape = [
        jax.ShapeDtypeStruct((nseq * dec_seq, D_MODEL), F32),
        jax.ShapeDtypeStruct(state_ret.shape, F32),
        jax.ShapeDtypeStruct(state_pool.shape, F32),
    ]
    out_specs = [pl.BlockSpec((rows, D_MODEL), lambda i: (i, 0)), state_spec, pool_spec]
    x2, new_ret, new_pool = pl.pallas_call(
        functools.partial(_decode_mixer_kernel, dec_seq=dec_seq),
        grid=(steps,),
        in_specs=in_specs,
        out_specs=out_specs,
        out_shape=out_shape,
        scratch_shapes=[
            pltpu.VMEM((groups, rows, POOL_GROUP), F32),
            pltpu.VMEM((rows, D_MODEL), BF16),
        ],
        compiler_params=pltpu.CompilerParams(
            dimension_semantics=("arbitrary",), vmem_limit_bytes=VMEM_LIMIT_BYTES),
        name="decode_mixer",
    )(*([proj] * (4 + groups)), x1, state_ret, state_pool, cos, sin, mask, qdec, kdec,
      pool_w, pool_scale, w_out, lng, lnb)
    return x2, new_ret, jnp.transpose(new_pool, (1, 0, 2))


def kernel(x_prompt, x_sample, state_ret, state_pool, meta_tokens, ffn1_w_gate, ffn1_w_up, ffn1_w_down,
           ln1_g, ln1_b, w_in, pool_w, pool_scale, w_out, ln2_g, ln2_b, ffn2_w_gate, ffn2_w_up,
           ffn2_w_down, ln3_g, ln3_b):
    assert ffn1_w_gate.shape[0] == DEPTH == 1
    batch, seq, _ = x_prompt.shape
    nseq, dec_seq, _ = x_sample.shape
    n_dec = nseq * dec_seq
    assert n_dec % N_META == 0

    bf = lambda w: w[0].astype(BF16)
    row = lambda v: v[0].reshape(1, -1)
    pool_w_b = bf(pool_w)
    pscale, g2, b2 = row(pool_scale), row(ln2_g), row(ln2_b)

    xp = x_prompt.reshape(batch * seq, D_MODEL)
    x_small = jnp.concatenate([x_sample.reshape(n_dec, D_MODEL), meta_tokens.astype(x_prompt.dtype)], axis=0)

    x1s, projs, wg1, wu1, wd1, w_in_b = _ffn_proj_cast(
        x_small, ffn1_w_gate[0], ffn1_w_up[0], ffn1_w_down[0], row(ln1_g), row(ln1_b), w_in[0])
    meta_rows = n_dec // N_META
    x1p, projp, pool_p, wg2, wu2, wd2, w_out_b = _ffn_proj_pipelined(
        xp, wg1, wu1, wd1, row(ln1_g), row(ln1_b), w_in_b,
        (ffn2_w_gate[0], ffn2_w_up[0], ffn2_w_down[0], w_out[0]), seq, projs, meta_rows)
    f2 = (wg2, wu2, wd2, row(ln3_g), row(ln3_b))

    x2s, ret_s, pool_s = _decode_mixer(projs, x1s, state_ret[0], state_pool[0], pool_w_b, pscale,
                                       w_out_b, g2, b2, nseq, dec_seq)
    y_prompt, ret_p, y_sample = _prompt_mixer_ffn(
        projp, x1p, projs, meta_rows, pool_w_b, pscale, w_out_b, g2, b2, *f2, x2s, batch, seq)
    return (y_prompt.reshape(batch, seq, D_MODEL), y_sample.reshape(nseq, dec_seq, D_MODEL),
            ret_p[None], pool_p[None], ret_s[None], pool_s[None])
```

```python
import functools
import math

import jax
import jax.numpy as jnp
import numpy as np
from jax import lax
from jax.experimental import pallas as pl
from jax.experimental.pallas import tpu as pltpu

F32 = jnp.float32
BF16 = jnp.bfloat16

D_MODEL = 1024
D_FF = 2816
N_META = 16
PAST_LEN = 16384
RET_HEADS = 4
HEAD_DIM = 128
RET_WIDTH = RET_HEADS * HEAD_DIM
RET_CHUNK = 128
ROPE_THETA = 10000.0
POOL_WINDOWS = (2, 4, 8, 16)
POOL_GROUP = 128
POOL_WIDTH = POOL_GROUP * len(POOL_WINDOWS)
POOL_BUF = max(POOL_WINDOWS) - 1
IN_WIDTH = 4 * RET_WIDTH + POOL_WIDTH
DEPTH = 1
ALPHA = (2.0 * DEPTH) ** 0.25
LN_EPS = 1e-5
GN_EPS = 1e-5
QK_SCALE = HEAD_DIM ** -0.5
GAMMAS = tuple(1.0 - 2.0 ** (-5.0 - h) for h in range(RET_HEADS))

VMEM_LIMIT_BYTES = 56 * 1024 * 1024
FFN_TOKEN_TILE = 512
FFN_COL_CHUNK = 256
MIX_TOKEN_TILE = 512
DEC_SEQ_BLOCK = 32
BF16_ROWS = 16
PAIR_WIDTH = 2 * HEAD_DIM
RING_SLOTS = 3


def _layer_norm(z, g, b):
    mu = jnp.mean(z, axis=-1, keepdims=True)
    zc = z - mu
    var = jnp.mean(zc * zc, axis=-1, keepdims=True)
    return zc * lax.rsqrt(var + LN_EPS) * g + b


def _silu(x):
    return x * jax.nn.sigmoid(x)


def _dot(a, b):
    return jnp.dot(a, b, preferred_element_type=F32)


def _dot_nt(a, b):
    return lax.dot_general(a, b, (((1,), (1,)), ((), ())), preferred_element_type=F32)


def _dot_tn(a, b):
    return lax.dot_general(a, b, (((0,), (0,)), ((), ())), preferred_element_type=F32)


def _const_spec(shape):
    zeros = (0,) * len(shape)
    return pl.BlockSpec(shape, lambda *_: zeros, pipeline_mode=pl.Buffered(1))


def _ffn_proj_cast_kernel(x_ref, wg_ref, wu_ref, wd_ref, lng_ref, lnb_ref, win_ref,
                          x1_ref, proj_ref, wgb_ref, wub_ref, wdb_ref, winb_ref,
                          acc_ref, xb_ref, win_all_ref, wg_buf, wu_buf, wd_buf, win_buf, sems):
    k = pl.program_id(0)
    ck = FFN_COL_CHUNK
    in_chunks = IN_WIDTH // ck
    streams = ((wg_ref, wg_buf, D_FF // ck, 1), (wu_ref, wu_buf, D_FF // ck, 1),
               (wd_ref, wd_buf, D_FF // ck, 0), (win_ref, win_buf, in_chunks, 1))

    def chunk_copy(stream, chunk):
        src, ring, _, axis = streams[stream]
        idx = (pl.ds(chunk * ck, ck), slice(None)) if axis == 0 else (slice(None), pl.ds(chunk * ck, ck))
        slot = chunk % RING_SLOTS
        return pltpu.make_async_copy(src.at[idx], ring.at[slot], sems.at[stream, slot])

    def start_chunk(chunk):
        for stream, (_, _, n_chunks, _) in enumerate(streams):
            @pl.when(chunk < n_chunks)
            def _(stream=stream):
                chunk_copy(stream, chunk).start()

    @pl.when(k == 0)
    def _start():
        for ahead in range(RING_SLOTS - 1):
            start_chunk(jnp.int32(ahead))
        xb_ref[...] = x_ref[...].astype(BF16)
        acc_ref[...] = jnp.zeros(acc_ref.shape, F32)

    start_chunk(k + RING_SLOTS - 1)
    slot = k % RING_SLOTS
    for stream in range(3):
        chunk_copy(stream, k).wait()
    wg, wu, wd = wg_buf[slot].astype(BF16), wu_buf[slot].astype(BF16), wd_buf[slot].astype(BF16)
    wgb_ref[...] = wg
    wub_ref[...] = wu
    wdb_ref[...] = wd
    h = (_silu(_dot(xb_ref[...], wg)) * _dot(xb_ref[...], wu)).astype(BF16)
    acc_ref[...] += _dot(h, wd)

    for j in range(in_chunks):
        @pl.when(k == j)
        def _round_w_in_chunk(j=j):
            chunk_copy(3, j).wait()
            w_in = win_buf[j % RING_SLOTS].astype(BF16)
            winb_ref[...] = w_in
            win_all_ref[:, j * ck:(j + 1) * ck] = w_in

    @pl.when(k == pl.num_programs(0) - 1)
    def _finish():
        x1 = _layer_norm(ALPHA * x_ref[...] + 0.5 * acc_ref[...], lng_ref[...], lnb_ref[...])
        x1_ref[...] = x1
        proj_ref[...] = _dot(x1.astype(BF16), win_all_ref[...])


def _ffn_proj_cast(x, wg, wu, wd, lng, lnb, w_in):
    n = x.shape[0]
    ck = FFN_COL_CHUNK
    steps = D_FF // ck
    in_chunks = IN_WIDTH // ck
    assert in_chunks <= steps
    col_chunk = pl.BlockSpec((D_MODEL, ck), lambda k: (0, k))
    row_chunk = pl.BlockSpec((ck, D_MODEL), lambda k: (k, 0))
    w_in_chunk = pl.BlockSpec((D_MODEL, ck), lambda k: (0, jnp.minimum(k, in_chunks - 1)))
    whole = lambda shape: pl.BlockSpec(shape, lambda k: (0,) * len(shape))
    in_hbm = pl.BlockSpec(memory_space=pl.ANY)
    return pl.pallas_call(
        _ffn_proj_cast_kernel,
        grid=(steps,),
        in_specs=[_const_spec(x.shape), in_hbm, in_hbm, in_hbm,
                  _const_spec(lng.shape), _const_spec(lnb.shape), in_hbm],
        out_specs=[whole((n, D_MODEL)), whole((n, IN_WIDTH)), col_chunk, col_chunk, row_chunk,
                   w_in_chunk],
        out_shape=[jax.ShapeDtypeStruct((n, D_MODEL), F32), jax.ShapeDtypeStruct((n, IN_WIDTH), F32),
                   jax.ShapeDtypeStruct(wg.shape, BF16), jax.ShapeDtypeStruct(wu.shape, BF16),
                   jax.ShapeDtypeStruct(wd.shape, BF16), jax.ShapeDtypeStruct(w_in.shape, BF16)],
        scratch_shapes=[pltpu.VMEM((n, D_MODEL), F32), pltpu.VMEM((n, D_MODEL), BF16),
                        pltpu.VMEM(w_in.shape, BF16),
                        pltpu.VMEM((RING_SLOTS, D_MODEL, ck), F32), pltpu.VMEM((RING_SLOTS, D_MODEL, ck), F32),
                        pltpu.VMEM((RING_SLOTS, ck, D_MODEL), F32), pltpu.VMEM((RING_SLOTS, D_MODEL, ck), F32),
                        pltpu.SemaphoreType.DMA((4, RING_SLOTS))],
        compiler_params=pltpu.CompilerParams(
            dimension_semantics=("arbitrary",), vmem_limit_bytes=VMEM_LIMIT_BYTES),
        name="ffn_proj_cast",
    )(x, wg, wu, wd, lng, lnb, w_in)


def _interleave(major, starts):
    live = []
    for i, piece in enumerate(major):
        piece()
        live += [make() for make in starts.get(i, [])]
        live = [g for g in live if next(g, "done") != "done"]
    while live:
        live = [g for g in live if next(g, "done") != "done"]


def _run_all(starts):
    for i in sorted(starts):
        for make in starts[i]:
            for _ in make():
                pass


def _ffn_proj_pipelined_kernel(x_ref, wg_ref, wu_ref, wd_ref, lng_ref, lnb_ref, win_ref,
                               rope_ref, pm_ref, *rest, n_cast, steps_per_seq):
    cast_in, rest = rest[:n_cast], rest[n_cast:]
    x1_ref, proj_ref, spool_ref = rest[:3]
    cast_out, (xb_ref, z1_ref, x1b_ref, h_ref, xp_ref) = rest[3:3 + n_cast], rest[3 + n_cast:]
    t = pl.program_id(0)
    n_tiles = pl.num_programs(0) - 1
    tile = x_ref.shape[0]
    row_blocks = [slice(r, r + RET_CHUNK) for r in range(0, tile, RET_CHUNK)]
    proj_chunk = 2 * FFN_COL_CHUNK
    hist = N_META
    seq_step = (t - 1) % steps_per_seq

    def ln1_piece(rows):
        def run():
            x1_rows = _layer_norm(z1_ref[rows, :], lng_ref[...], lnb_ref[...])
            x1_ref[rows, :] = x1_rows
            x1b_ref[rows, :] = x1_rows.astype(BF16)
            yield
        return run

    def proj_piece(nk):
        def run():
            cs = slice(nk * proj_chunk, (nk + 1) * proj_chunk)
            chunk = _dot(x1b_ref[...], win_ref[:, cs])
            if nk < 2:
                cos, sin = rope_ref[:, :HEAD_DIM], rope_ref[:, HEAD_DIM:]
                chunk = jnp.concatenate(
                    [_rope(chunk[:, h * HEAD_DIM:(h + 1) * HEAD_DIM], cos, sin)
                     for h in range(RET_HEADS)], axis=1)
            proj_ref[:, cs] = chunk
        return run

    def pool_input_piece(half):
        def run():
            cs = slice(half * PAIR_WIDTH, (half + 1) * PAIR_WIDTH)
            xp_ref[hist:hist + tile, cs] = _dot(
                x1b_ref[...], win_ref[:, 4 * RET_WIDTH + half * PAIR_WIDTH:4 * RET_WIDTH + (half + 1) * PAIR_WIDTH])
        return run

    def pool_diff_piece(gi):
        def run():
            w = POOL_WINDOWS[gi]
            gs = slice(gi * POOL_GROUP, (gi + 1) * POOL_GROUP)
            rows_all = xp_ref[:, gs]
            wsum, shift = rows_all, 1
            while shift < w:
                wsum = wsum + pltpu.roll(wsum, shift, 0)
                shift *= 2
            proj_ref[:, 4 * RET_WIDTH + gi * POOL_GROUP:4 * RET_WIDTH + (gi + 1) * POOL_GROUP] = (
                wsum[hist:] * (1.0 / w) - rows_all[hist:])
            xp_ref[0:hist, gs] = xp_ref[tile:tile + hist, gs]
            yield
        return run

    def gate_up_piece(ck):
        def run():
            sl = slice(ck * FFN_COL_CHUNK, (ck + 1) * FFN_COL_CHUNK)
            g = _dot(xb_ref[...], wg_ref[:, sl])
            u = _dot(xb_ref[...], wu_ref[:, sl])
            h_ref[:, sl] = (_silu(g) * u).astype(BF16)
        return run

    def down_piece(nk):
        def run():
            cs = slice(nk * FFN_COL_CHUNK, (nk + 1) * FFN_COL_CHUNK)
            z1_ref[:, cs] = ALPHA * x_ref[:, cs] + 0.5 * _dot(h_ref[...], wd_ref[:, cs])
        return run

    ln1_starts = {i: [ln1_piece(rows)] for i, rows in enumerate(row_blocks)}
    qkvg = [proj_piece(nk) for nk in range(4 * RET_WIDTH // proj_chunk)]
    proj_pieces = [pool_input_piece(0)] + qkvg[:2] + [pool_input_piece(1)] + qkvg[2:]
    pool_diffs = [pool_diff_piece(gi) for gi in range(len(POOL_WINDOWS))]

    @pl.when(jnp.logical_and(t >= 1, seq_step == 0))
    def _history_from_meta():
        xp_ref[0:hist, :] = pm_ref[...]

    @pl.when(t == 0)
    def _clear_pipeline():
        z1_ref[...] = jnp.zeros(z1_ref.shape, F32)
        xp_ref[0:hist, :] = jnp.zeros((hist, POOL_WIDTH), F32)

    @pl.when(t < n_tiles)
    def _steady():
        xb_ref[...] = x_ref[...].astype(BF16)
        for src_ref, dst_ref in zip(cast_in, cast_out):
            dst_ref[...] = src_ref[...].astype(BF16)
        gate_up = [gate_up_piece(ck) for ck in range(D_FF // FFN_COL_CHUNK)]
        first_proj = len(row_blocks) + 1
        major = gate_up[:first_proj]
        for i, piece in enumerate(gate_up[first_proj:]):
            major += proj_pieces[i:i + 1] + [piece]
        major += proj_pieces[len(gate_up) - first_proj:]
        starts = dict(ln1_starts)
        half = len(pool_diffs) // 2
        starts[len(major) - 2] = pool_diffs[:half]
        _interleave(major, starts)
        _interleave([down_piece(nk) for nk in range(D_MODEL // FFN_COL_CHUNK)],
                    {i: [piece] for i, piece in enumerate(pool_diffs[half:])})

    @pl.when(t == n_tiles)
    def _drain():
        _run_all(ln1_starts)
        for piece in proj_pieces:
            piece()
        _run_all({0: pool_diffs})

    @pl.when(jnp.logical_and(t >= 1, seq_step == steps_per_seq - 1))
    def _emit_pool_state():
        spool_ref[0] = xp_ref[hist - POOL_BUF:hist, :]


def _slab_rows(rows, max_slabs):
    for slab in range(BF16_ROWS, rows + 1, BF16_ROWS):
        if rows % slab == 0 and rows // slab <= max_slabs:
            return slab
    raise ValueError(f"no slab size for {rows} rows in {max_slabs} steps")


def _ffn_proj_pipelined(x, wg, wu, wd, lng, lnb, w_in, cast_weights, seq, proj_small, meta_row_block):
    n = x.shape[0]
    tm = FFN_TOKEN_TILE
    n_tiles = n // tm
    steps_per_seq = seq // tm
    n_seq = n // seq
    assert seq % tm == 0 and IN_WIDTH % (2 * FFN_COL_CHUNK) == 0 and RET_WIDTH == 2 * FFN_COL_CHUNK
    rope = np.concatenate(_rope_tables(N_META + np.arange(seq)), axis=1)
    in_tile = lambda t: (jnp.minimum(t, n_tiles - 1), 0)
    out_tile = lambda t: (jnp.maximum(t - 1, 0), 0)
    rope_tile = pl.BlockSpec((tm, 2 * HEAD_DIM), lambda t: (jnp.maximum(t - 1, 0) % (seq // tm), 0))

    def slab_spec(w):
        slab = _slab_rows(w.shape[0], n_tiles)
        last = w.shape[0] // slab - 1
        return pl.BlockSpec((slab, w.shape[1]), lambda t: (jnp.minimum(t, last), 0))

    cast_specs = [slab_spec(w) for w in cast_weights]
    meta_p = pl.BlockSpec((N_META, POOL_WIDTH), lambda t: (meta_row_block, 4 * RET_WIDTH // POOL_WIDTH))
    pool_state = pl.BlockSpec(
        (1, POOL_BUF, POOL_WIDTH), lambda t: (jnp.clip((t - 1) // steps_per_seq, 0, n_seq - 1), 0, 0))
    return pl.pallas_call(
        functools.partial(_ffn_proj_pipelined_kernel, n_cast=len(cast_weights),
                          steps_per_seq=steps_per_seq),
        grid=(n_tiles + 1,),
        in_specs=[
            pl.BlockSpec((tm, D_MODEL), in_tile),
            _const_spec(wg.shape), _const_spec(wu.shape), _const_spec(wd.shape),
            _const_spec(lng.shape), _const_spec(lnb.shape), _const_spec(w_in.shape),
            rope_tile, meta_p,
        ] + cast_specs,
        out_specs=[pl.BlockSpec((tm, D_MODEL), out_tile), pl.BlockSpec((tm, IN_WIDTH), out_tile),
                   pool_state] + cast_specs,
        out_shape=[jax.ShapeDtypeStruct((n, D_MODEL), F32), jax.ShapeDtypeStruct((n, IN_WIDTH), F32),
                   jax.ShapeDtypeStruct((n_seq, POOL_BUF, POOL_WIDTH), F32)]
        + [jax.ShapeDtypeStruct(w.shape, BF16) for w in cast_weights],
        scratch_shapes=[
            pltpu.VMEM((tm, D_MODEL), BF16),
            pltpu.VMEM((tm, D_MODEL), F32),
            pltpu.VMEM((tm, D_MODEL), BF16),
            pltpu.VMEM((tm, D_FF), BF16),
            pltpu.VMEM((N_META + tm, POOL_WIDTH), F32),
        ],
        compiler_params=pltpu.CompilerParams(
            dimension_semantics=("arbitrary",), vmem_limit_bytes=VMEM_LIMIT_BYTES),
        name="ffn_proj_pipelined",
    )(x, wg, wu, wd, lng, lnb, w_in, rope, proj_small, *cast_weights)


def _rope_tables(positions):
    half = HEAD_DIM // 2
    inv_freq = ROPE_THETA ** (-np.arange(0, HEAD_DIM, 2, dtype=np.float64) / HEAD_DIM)
    ang = np.asarray(positions, np.float64)[:, None] * inv_freq[None, :]
    cos, sin = np.cos(ang), np.sin(ang)
    assert cos.shape[1] == half
    return (np.concatenate([cos, cos], axis=1).astype(np.float32),
            np.concatenate([-sin, sin], axis=1).astype(np.float32))


def _decay_tables(chunk, seq_len):
    r = np.arange(chunk)
    seq, idx = r // seq_len, (r % seq_len).astype(np.float64)
    same = seq[:, None] == seq[None, :]
    diff = idx[:, None] - idx[None, :]
    mask, qdec, kdec = [], [], []
    for gamma in GAMMAS:
        lg = math.log(gamma)
        mask.append(np.where(same & (diff >= 0), np.exp(lg * np.maximum(diff, 0.0)), 0.0) * QK_SCALE)
        qdec.append(np.broadcast_to((np.exp(lg * (idx + 1.0)) * QK_SCALE)[:, None], (chunk, HEAD_DIM)))
        kdec.append(np.broadcast_to(np.exp(lg * (seq_len - 1.0 - idx))[:, None], (chunk, HEAD_DIM)))
    to32 = lambda t: np.stack(t).astype(np.float32)
    return to32(mask), to32(qdec), to32(kdec)


def _rope(x, cos, sin):
    return x * cos + pltpu.roll(x, HEAD_DIM // 2, 1) * sin


def _group_norm(o):
    mu = jnp.mean(o, axis=-1, keepdims=True)
    oc = o - mu
    var = jnp.mean(oc * oc, axis=-1, keepdims=True)
    return oc * lax.rsqrt(var + GN_EPS)


def _prompt_mixer_ffn_kernel(proj_ref, x1_ref,
                             km_ref, vm_ref, cosm_ref, sinm_ref, kdecm_ref,
                             mask_ref, qdec_ref, kdec_ref, poolw_ref, pscale_ref, wout_ref,
                             ln2g_ref, ln2b_ref, wg_ref, wu_ref, wd_ref, ln3g_ref, ln3b_ref,
                             x2dec_hbm,
                             y_ref, sret_ref, ydec_hbm,
                             s_ref, mix_ref, x2_ref, xb_ref, ypre_ref, h_ref, kb_ref, vs_ref,
                             *, steps_per_seq):
    t = pl.program_id(0)
    n_tiles = pl.num_programs(0) - 2
    c = t % steps_per_seq
    tile = proj_ref.shape[0]
    q_ref, k_ref, v_ref, g_ref, p_ref = (
        proj_ref.at[:, j * RET_WIDTH:(j + 1) * RET_WIDTH] for j in range(5))
    row_blocks = [slice(r, r + RET_CHUNK) for r in range(0, tile, RET_CHUNK)]

    @pl.when(jnp.logical_and(c == 0, t < n_tiles))
    def _init_from_meta():
        for h in range(RET_HEADS):
            hs = slice(h * HEAD_DIM, (h + 1) * HEAD_DIM)
            kr = _rope(km_ref[:, hs], cosm_ref[...], sinm_ref[...])
            kd = (kr * kdecm_ref[h]).astype(BF16)
            s_ref[h] = _dot_tn(kd, vm_ref[:, hs].astype(BF16))

    def ln3_fetch_piece(rows):
        def run():
            y_ref[rows, :] = ypre_ref[rows, :]
            yield
        return run

    def ln3_piece(rows):
        def run():
            y_ref[rows, :] = _layer_norm(y_ref[rows, :], ln3g_ref[...], ln3b_ref[...])
            yield
        return run

    state = [None] * RET_HEADS

    def retention_piece(ci, hp):
        def run():
            rows = slice(ci * RET_CHUNK, (ci + 1) * RET_CHUNK)
            pair = slice(hp * PAIR_WIDTH, (hp + 1) * PAIR_WIDTH)
            buf = ci * 2 + hp
            qr, kr = [], []
            for j in range(2):
                h = 2 * hp + j
                hs = slice(h * HEAD_DIM, (h + 1) * HEAD_DIM)
                blk = slice(j * HEAD_DIM, (j + 1) * HEAD_DIM)
                qr.append(q_ref[rows, hs])
                kr.append(k_ref[rows, hs])
                kb_ref[buf, blk, blk] = kr[j].astype(BF16)
                vs_ref[buf, blk, blk] = v_ref[rows, hs].astype(BF16)
                vs_ref[buf, PAIR_WIDTH + j * HEAD_DIM:PAIR_WIDTH + (j + 1) * HEAD_DIM, blk] = (
                    state[h].astype(BF16))
            q2 = jnp.concatenate(qr, axis=1)
            k2 = jnp.concatenate(kr, axis=1)
            scores = _dot_nt(q2.astype(BF16), kb_ref[buf])
            upd = _dot((k2 * kdec_ref[hp]).T.astype(BF16), v_ref[rows, pair].astype(BF16))
            qd2 = (q2 * qdec_ref[hp]).astype(BF16)
            yield
            lhs = jnp.concatenate([(scores * mask_ref[hp]).astype(BF16), qd2], axis=1)
            o2 = _dot(lhs, vs_ref[buf])
            for j in range(2):
                blk = slice(j * HEAD_DIM, (j + 1) * HEAD_DIM)
                state[2 * hp + j] = (GAMMAS[2 * hp + j] ** RET_CHUNK) * state[2 * hp + j] + upd[blk, blk]
            yield
            for j in range(2):
                hs = slice((2 * hp + j) * HEAD_DIM, (2 * hp + j + 1) * HEAD_DIM)
                blk = slice(j * HEAD_DIM, (j + 1) * HEAD_DIM)
                mix_ref[rows, hs] = (_silu(g_ref[rows, hs]) * _group_norm(o2[:, blk])).astype(BF16)
        return run

    def pool_piece(pp):
        def run():
            pair = slice(pp * PAIR_WIDTH, (pp + 1) * PAIR_WIDTH)
            pooled = _dot(p_ref[:, pair].astype(BF16), poolw_ref[pp])
            yield
            pooled = pooled * pscale_ref[:, pair]
            mix_ref[:, RET_WIDTH + pp * PAIR_WIDTH:RET_WIDTH + (pp + 1) * PAIR_WIDTH] = pooled.astype(BF16)
        return run

    def gate_up_piece(ck):
        def run():
            sl = slice(ck * FFN_COL_CHUNK, (ck + 1) * FFN_COL_CHUNK)
            g = _dot(xb_ref[...], wg_ref[:, sl])
            u = _dot(xb_ref[...], wu_ref[:, sl])
            h_ref[:, sl] = (_silu(g) * u).astype(BF16)
        return run

    def residual_piece():
        ypre_ref[...] = ALPHA * x2_ref[...]
        yield

    def w_out_piece():
        for h in range(RET_HEADS):
            s_ref[h] = state[h]
        x2_ref[...] = ALPHA * x1_ref[...] + _dot(mix_ref[...], wout_ref[...])

    def ln2_piece(rows):
        def run():
            x2_ref[rows, :] = _layer_norm(x2_ref[rows, :], ln2g_ref[...], ln2b_ref[...])
            yield
        return run

    def xb_piece(rows):
        def run():
            xb_ref[rows, :] = x2_ref[rows, :].astype(BF16)
            yield
        return run

    def down_piece(nk):
        def run():
            cs = slice(nk * FFN_COL_CHUNK, (nk + 1) * FFN_COL_CHUNK)
            ypre_ref[:, cs] = ypre_ref[:, cs] + 0.5 * _dot(h_ref[...], wd_ref[:, cs])
        return run

    def steady_body():
        starts = {}

        def start_at(i, piece):
            starts.setdefault(i, []).append(piece)

        for rows in row_blocks:
            start_at(0, ln3_fetch_piece(rows))
        start_at(1, residual_piece)
        for h in range(RET_HEADS):
            state[h] = s_ref[h]
        gate_up = [gate_up_piece(ck) for ck in range(D_FF // FFN_COL_CHUNK)]
        pieces = [(ci, hp) for ci in range(len(row_blocks)) for hp in range(RET_HEADS // 2)]
        w_out_at = len(pieces)
        for n, (ci, hp) in enumerate(pieces):
            start_at(n // 2 if n < 4 else n - 2, retention_piece(ci, hp))
        for pp in range(len(POOL_WINDOWS) // 2):
            start_at(2 * pp + 1, pool_piece(pp))
        major = gate_up[:w_out_at] + [w_out_piece] + gate_up[w_out_at:]
        for i, rows in enumerate(row_blocks):
            start_at(min(w_out_at + i, len(major) - 1), ln2_piece(rows))
        _interleave(major, starts)

        tail_starts = {i: [ln3_piece(rows), xb_piece(rows)] for i, rows in enumerate(row_blocks)}
        _interleave([down_piece(nk) for nk in range(D_MODEL // FFN_COL_CHUNK)], tail_starts)

    def drain_body():
        for rows in row_blocks:
            for make in (ln3_fetch_piece(rows), ln3_piece(rows)):
                for _ in make():
                    pass

    @pl.when(t == 0)
    def _prime_pipeline():
        pltpu.sync_copy(x2dec_hbm, x2_ref)
        xb_ref[...] = x2_ref[...].astype(BF16)
        kb_ref[...] = jnp.zeros(kb_ref.shape, BF16)
        vs_ref[...] = jnp.zeros(vs_ref.shape, BF16)
        ypre_ref[...] = jnp.zeros(ypre_ref.shape, F32)

    @pl.when(t <= n_tiles)
    def _steady():
        steady_body()

    @pl.when(t == 1)
    def _emit_decode_rows():
        pltpu.sync_copy(y_ref, ydec_hbm)

    @pl.when(t == n_tiles + 1)
    def _drain_last():
        drain_body()

    @pl.when(jnp.logical_and(c == steps_per_seq - 1, t < n_tiles))
    def _emit_state():
        sret_ref[0] = s_ref[...]


def _prompt_mixer_ffn(proj, x1, proj_small, meta_row_block, pool_w, pool_scale, w_out, ln2g, ln2b,
                      wg, wu, wd, ln3g, ln3b, x2_dec, batch, seq):
    tile = MIX_TOKEN_TILE
    assert x2_dec.shape == (tile, D_MODEL)
    steps = seq // tile
    n_tiles = batch * steps
    cosm, sinm = _rope_tables(np.arange(N_META))
    pair_up = lambda tab: np.concatenate([tab[0::2], tab[1::2]], axis=2)
    mask, qdec, kdec = (pair_up(tab) for tab in _decay_tables(RET_CHUNK, RET_CHUNK))
    _, _, kdecm = _decay_tables(N_META, N_META)
    zero_blk = jnp.zeros_like(pool_w[0])
    pool_w = jnp.stack([jnp.block([[pool_w[2 * pp], zero_blk], [zero_blk, pool_w[2 * pp + 1]]])
                        for pp in range(len(POOL_WINDOWS) // 2)])

    mix_tile = lambda t: jnp.minimum(t, n_tiles - 1)
    ffn_tile = lambda t: jnp.maximum(t - 2, 0)

    def meta_col(j):
        return pl.BlockSpec((N_META, RET_WIDTH), lambda t: (meta_row_block, j))

    in_specs = [
        pl.BlockSpec((tile, IN_WIDTH), lambda t: (mix_tile(t), 0)),
        pl.BlockSpec((tile, D_MODEL), lambda t: (mix_tile(t), 0)),
        meta_col(1), meta_col(2),
        _const_spec(cosm.shape), _const_spec(sinm.shape), _const_spec(kdecm.shape),
        _const_spec(mask.shape), _const_spec(qdec.shape), _const_spec(kdec.shape),
        _const_spec(pool_w.shape), _const_spec(pool_scale.shape), _const_spec(w_out.shape),
        _const_spec(ln2g.shape), _const_spec(ln2b.shape),
        _const_spec(wg.shape), _const_spec(wu.shape), _const_spec(wd.shape),
        _const_spec(ln3g.shape), _const_spec(ln3b.shape),
        pl.BlockSpec(memory_space=pl.ANY),
    ]
    out_shape = [
        jax.ShapeDtypeStruct((batch * seq, D_MODEL), F32),
        jax.ShapeDtypeStruct((batch, RET_HEADS, HEAD_DIM, HEAD_DIM), F32),
        jax.ShapeDtypeStruct((tile, D_MODEL), F32),
    ]
    out_specs = [
        pl.BlockSpec((tile, D_MODEL), lambda t: (ffn_tile(t), 0)),
        pl.BlockSpec((1, RET_HEADS, HEAD_DIM, HEAD_DIM), lambda t: (mix_tile(t) // steps, 0, 0, 0)),
        pl.BlockSpec(memory_space=pl.ANY),
    ]
    return pl.pallas_call(
        functools.partial(_prompt_mixer_ffn_kernel, steps_per_seq=steps),
        grid=(n_tiles + 2,),
        in_specs=in_specs,
        out_specs=out_specs,
        out_shape=out_shape,
        scratch_shapes=[
            pltpu.VMEM((RET_HEADS, HEAD_DIM, HEAD_DIM), F32),
            pltpu.VMEM((tile, D_MODEL), BF16),
            pltpu.VMEM((tile, D_MODEL), F32),
            pltpu.VMEM((tile, D_MODEL), BF16),
            pltpu.VMEM((tile, D_MODEL), F32),
            pltpu.VMEM((tile, D_FF), BF16),
            pltpu.VMEM((8, PAIR_WIDTH, PAIR_WIDTH), BF16),
            pltpu.VMEM((8, 2 * PAIR_WIDTH, PAIR_WIDTH), BF16),
        ],
        compiler_params=pltpu.CompilerParams(
            dimension_semantics=("arbitrary",), vmem_limit_bytes=VMEM_LIMIT_BYTES),
        name="prompt_mixer_ffn",
    )(proj, x1, proj_small, proj_small,
      cosm, sinm, kdecm, mask, qdec, kdec, pool_w, pool_scale, w_out, ln2g, ln2b,
      wg, wu, wd, ln3g, ln3b, x2_dec)


def _decode_mixer_kernel(q_ref, k_ref, v_ref, g_ref, *rest, dec_seq):
    p_refs, rest = rest[:len(POOL_WINDOWS)], rest[len(POOL_WINDOWS):]
    (x1_ref, s0_ref, pref_ref, cos_ref, sin_ref, mask_ref, qdec_ref, kdec_ref,
     poolw_ref, pscale_ref, wout_ref, lng_ref, lnb_ref,
     o_ref, sret_ref, spool_ref, d_ref, mix_ref) = rest
    rows = q_ref.shape[0]
    nseq = rows // dec_seq
    seq_per_group = BF16_ROWS // dec_seq
    cos, sin = cos_ref[...], sin_ref[...]
    row_seq = lax.broadcasted_iota(jnp.int32, (BF16_ROWS, HEAD_DIM), 0) // dec_seq

    for h in range(RET_HEADS):
        hs = slice(h * HEAD_DIM, (h + 1) * HEAD_DIM)
        qr = _rope(q_ref[:, hs], cos, sin)
        kr = _rope(k_ref[:, hs], cos, sin)
        v = v_ref[:, hs]
        vb = v.astype(BF16)
        scores = _dot_nt(qr.astype(BF16), kr.astype(BF16)) * mask_ref[h]
        o_inner = _dot(scores.astype(BF16), vb)
        qd = qr * qdec_ref[h]
        kd = kr * kdec_ref[h]
        o_cross = []
        for grp in range(rows // BF16_ROWS):
            gr = slice(grp * BF16_ROWS, (grp + 1) * BF16_ROWS)
            qd_g = qd[gr].astype(BF16)
            kd_g = kd[gr].astype(BF16)
            v_g = v[gr]
            acc = jnp.zeros((BF16_ROWS, HEAD_DIM), F32)
            for j in range(seq_per_group):
                b = grp * seq_per_group + j
                s = s0_ref[b, h]
                acc = jnp.where(row_seq == j, _dot(qd_g, s.astype(BF16)), acc)
                v_b = jnp.where(row_seq == j, v_g, 0.0).astype(BF16)
                sret_ref[b, h] = (GAMMAS[h] ** dec_seq) * s + _dot_tn(kd_g, v_b)
            o_cross.append(acc)
        o = o_inner + jnp.concatenate(o_cross, axis=0)
        mix_ref[:, hs] = (_silu(g_ref[:, hs]) * _group_norm(o)).astype(BF16)

    for gi, (w, pg_ref) in enumerate(zip(POOL_WINDOWS, p_refs)):
        gs = slice(gi * POOL_GROUP, (gi + 1) * POOL_GROUP)
        steps = [pref_ref[j, :, gs] for j in range(POOL_BUF)]
        steps += [pg_ref[pl.ds(i, nseq, stride=dec_seq), :] for i in range(dec_seq)]
        for i in range(dec_seq):
            now = POOL_BUF + i
            wsum = steps[now]
            for back in range(1, w):
                wsum = wsum + steps[now - back]
            d_ref[gi, pl.ds(i, nseq, stride=dec_seq), :] = wsum * (1.0 / w) - steps[now]
        for j in range(POOL_BUF):
            spool_ref[j, :, gs] = steps[dec_seq + j]
        pooled = _dot(d_ref[gi].astype(BF16), poolw_ref[gi]) * pscale_ref[:, gs]
        mix_ref[:, RET_WIDTH + gi * POOL_GROUP:RET_WIDTH + (gi + 1) * POOL_GROUP] = pooled.astype(BF16)

    y = _dot(mix_ref[...], wout_ref[...])
    o_ref[...] = _layer_norm(ALPHA * x1_ref[...] + y, lng_ref[...], lnb_ref[...])


def _decode_mixer(proj, x1, state_ret, state_pool, pool_w, pool_scale, w_out, lng, lnb, nseq, dec_seq):
    assert BF16_ROWS % dec_seq == 0 and dec_seq <= POOL_BUF
    rows = DEC_SEQ_BLOCK * dec_seq
    steps = nseq // DEC_SEQ_BLOCK
    cos, sin = _rope_tables(PAST_LEN + (np.arange(rows) % dec_seq))
    mask, qdec, kdec = _decay_tables(rows, dec_seq)

    def col(j):
        return pl.BlockSpec((rows, RET_WIDTH), lambda i: (i, j))

    state_spec = pl.BlockSpec((DEC_SEQ_BLOCK, RET_HEADS, HEAD_DIM, HEAD_DIM), lambda i: (i, 0, 0, 0))
    state_pool = jnp.transpose(state_pool, (1, 0, 2))
    pool_spec = pl.BlockSpec((POOL_BUF, DEC_SEQ_BLOCK, POOL_WIDTH), lambda i: (0, i, 0))
    groups = len(POOL_WINDOWS)
    p_cols = 4 * RET_WIDTH // POOL_GROUP
    in_specs = [
        col(0), col(1), col(2), col(3),
        *[pl.BlockSpec((rows, POOL_GROUP), lambda i, gi=gi: (i, p_cols + gi)) for gi in range(groups)],
        pl.BlockSpec((rows, D_MODEL), lambda i: (i, 0)),
        state_spec, pool_spec,
        _const_spec(cos.shape), _const_spec(sin.shape),
        _const_spec(mask.shape), _const_spec(qdec.shape), _const_spec(kdec.shape),
        _const_spec(pool_w.shape), _const_spec(pool_scale.shape), _const_spec(w_out.shape),
        _const_spec(lng.shape), _const_spec(lnb.shape),
    ]
    out_shape = [
        jax.ShapeDtypeStruct((nseq * dec_seq, D_MODEL), F32),
        jax.ShapeDtypeStruct(state_ret.shape, F32),
        jax.ShapeDtypeStruct(state_pool.shape, F32),
    ]
    out_specs = [pl.BlockSpec((rows, D_MODEL), lambda i: (i, 0)), state_spec, pool_spec]
    x2, new_ret, new_pool = pl.pallas_call(
        functools.partial(_decode_mixer_kernel, dec_seq=dec_seq),
        grid=(steps,),
        in_specs=in_specs,
        out_specs=out_specs,
        out_shape=out_shape,
        scratch_shapes=[
            pltpu.VMEM((groups, rows, POOL_GROUP), F32),
            pltpu.VMEM((rows, D_MODEL), BF16),
        ],
        compiler_params=pltpu.CompilerParams(
            dimension_semantics=("arbitrary",), vmem_limit_bytes=VMEM_LIMIT_BYTES),
        name="decode_mixer",
    )(*([proj] * (4 + groups)), x1, state_ret, state_pool, cos, sin, mask, qdec, kdec,
      pool_w, pool_scale, w_out, lng, lnb)
    return x2, new_ret, jnp.transpose(new_pool, (1, 0, 2))


def kernel(x_prompt, x_sample, state_ret, state_pool, meta_tokens, ffn1_w_gate, ffn1_w_up, ffn1_w_down,
           ln1_g, ln1_b, w_in, pool_w, pool_scale, w_out, ln2_g, ln2_b, ffn2_w_gate, ffn2_w_up,
           ffn2_w_down, ln3_g, ln3_b):
    assert ffn1_w_gate.shape[0] == DEPTH == 1
    batch, seq, _ = x_prompt.shape
    nseq, dec_seq, _ = x_sample.shape
    n_dec = nseq * dec_seq
    assert n_dec % N_META == 0

    bf = lambda w: w[0].astype(BF16)
    row = lambda v: v[0].reshape(1, -1)
    pool_w_b = bf(pool_w)
    pscale, g2, b2 = row(pool_scale), row(ln2_g), row(ln2_b)

    xp = x_prompt.reshape(batch * seq, D_MODEL)
    x_small = jnp.concatenate([x_sample.reshape(n_dec, D_MODEL), meta_tokens.astype(x_prompt.dtype)], axis=0)

    x1s, projs, wg1, wu1, wd1, w_in_b = _ffn_proj_cast(
        x_small, ffn1_w_gate[0], ffn1_w_up[0], ffn1_w_down[0], row(ln1_g), row(ln1_b), w_in[0])
    meta_rows = n_dec // N_META
    x1p, projp, pool_p, wg2, wu2, wd2, w_out_b = _ffn_proj_pipelined(
        xp, wg1, wu1, wd1, row(ln1_g), row(ln1_b), w_in_b,
        (ffn2_w_gate[0], ffn2_w_up[0], ffn2_w_down[0], w_out[0]), seq, projs, meta_rows)
    f2 = (wg2, wu2, wd2, row(ln3_g), row(ln3_b))

    x2s, ret_s, pool_s = _decode_mixer(projs, x1s, state_ret[0], state_pool[0], pool_w_b, pscale,
                                       w_out_b, g2, b2, nseq, dec_seq)
    y_prompt, ret_p, y_sample = _prompt_mixer_ffn(
        projp, x1p, projs, meta_rows, pool_w_b, pscale, w_out_b, g2, b2, *f2, x2s, batch, seq)
    return (y_prompt.reshape(batch, seq, D_MODEL), y_sample.reshape(nseq, dec_seq, D_MODEL),
            ret_p[None], pool_p[None], ret_s[None], pool_s[None])
```

```python
import functools
import math

import jax
import jax.numpy as jnp
import numpy as np
from jax import lax
from jax.experimental import pallas as pl
from jax.experimental.pallas import tpu as pltpu

F32 = jnp.float32
BF16 = jnp.bfloat16

D_MODEL = 1024
D_FF = 2816
N_META = 16
PAST_LEN = 16384
RET_HEADS = 4
HEAD_DIM = 128
RET_WIDTH = RET_HEADS * HEAD_DIM
RET_CHUNK = 128
ROPE_THETA = 10000.0
POOL_WINDOWS = (2, 4, 8, 16)
POOL_GROUP = 128
POOL_WIDTH = POOL_GROUP * len(POOL_WINDOWS)
POOL_BUF = max(POOL_WINDOWS) - 1
IN_WIDTH = 4 * RET_WIDTH + POOL_WIDTH
DEPTH = 1
ALPHA = (2.0 * DEPTH) ** 0.25
LN_EPS = 1e-5
GN_EPS = 1e-5
QK_SCALE = HEAD_DIM ** -0.5
GAMMAS = tuple(1.0 - 2.0 ** (-5.0 - h) for h in range(RET_HEADS))

VMEM_LIMIT_BYTES = 56 * 1024 * 1024
FFN_TOKEN_TILE = 512
FFN_COL_CHUNK = 256
MIX_TOKEN_TILE = 512
DEC_SEQ_BLOCK = 32
BF16_ROWS = 16
PAIR_WIDTH = 2 * HEAD_DIM
RING_SLOTS = 3


def _layer_norm(z, g, b):
    mu = jnp.mean(z, axis=-1, keepdims=True)
    zc = z - mu
    var = jnp.mean(zc * zc, axis=-1, keepdims=True)
    return zc * lax.rsqrt(var + LN_EPS) * g + b


def _silu(x):
    return x * jax.nn.sigmoid(x)


def _dot(a, b):
    return jnp.dot(a, b, preferred_element_type=F32)


def _dot_nt(a, b):
    return lax.dot_general(a, b, (((1,), (1,)), ((), ())), preferred_element_type=F32)


def _dot_tn(a, b):
    return lax.dot_general(a, b, (((0,), (0,)), ((), ())), preferred_element_type=F32)


def _const_spec(shape):
    zeros = (0,) * len(shape)
    return pl.BlockSpec(shape, lambda *_: zeros, pipeline_mode=pl.Buffered(1))


def _ffn_proj_cast_kernel(xd_ref, xm_ref, wg_ref, wu_ref, wd_ref, lng_ref, lnb_ref, win_ref,
                          x1_ref, proj_ref, wgb_ref, wub_ref, wdb_ref, winb_ref,
                          acc_ref, xb_ref, win_all_ref, wg_buf, wu_buf, wd_buf, win_buf, x_ref, sems):
    k = pl.program_id(0)
    ck = FFN_COL_CHUNK
    in_chunks = IN_WIDTH // ck
    streams = ((wg_ref, wg_buf, D_FF // ck, 1), (wu_ref, wu_buf, D_FF // ck, 1),
               (wd_ref, wd_buf, D_FF // ck, 0), (win_ref, win_buf, in_chunks, 1))

    def chunk_copy(stream, chunk):
        src, ring, _, axis = streams[stream]
        idx = (pl.ds(chunk * ck, ck), slice(None)) if axis == 0 else (slice(None), pl.ds(chunk * ck, ck))
        slot = chunk % RING_SLOTS
        return pltpu.make_async_copy(src.at[idx], ring.at[slot], sems.at[stream, slot])

    def start_chunk(chunk):
        for stream, (_, _, n_chunks, _) in enumerate(streams):
            @pl.when(chunk < n_chunks)
            def _(stream=stream):
                chunk_copy(stream, chunk).start()

    @pl.when(k == 0)
    def _start():
        for ahead in range(RING_SLOTS - 1):
            start_chunk(jnp.int32(ahead))
        n_dec = xd_ref.shape[0]
        x_ref[0:n_dec, :] = xd_ref[...]
        x_ref[n_dec:, :] = xm_ref[...]
        xb_ref[...] = x_ref[...].astype(BF16)
        acc_ref[...] = jnp.zeros(acc_ref.shape, F32)

    start_chunk(k + RING_SLOTS - 1)
    slot = k % RING_SLOTS
    for stream in range(3):
        chunk_copy(stream, k).wait()
    wg, wu, wd = wg_buf[slot].astype(BF16), wu_buf[slot].astype(BF16), wd_buf[slot].astype(BF16)
    wgb_ref[...] = wg
    wub_ref[...] = wu
    wdb_ref[...] = wd
    h = (_silu(_dot(xb_ref[...], wg)) * _dot(xb_ref[...], wu)).astype(BF16)
    acc_ref[...] += _dot(h, wd)

    for j in range(in_chunks):
        @pl.when(k == j)
        def _round_w_in_chunk(j=j):
            chunk_copy(3, j).wait()
            w_in = win_buf[j % RING_SLOTS].astype(BF16)
            winb_ref[...] = w_in
            win_all_ref[:, j * ck:(j + 1) * ck] = w_in

    @pl.when(k == pl.num_programs(0) - 1)
    def _finish():
        x1 = _layer_norm(ALPHA * x_ref[...] + 0.5 * acc_ref[...], lng_ref[...], lnb_ref[...])
        x1_ref[...] = x1
        proj_ref[...] = _dot(x1.astype(BF16), win_all_ref[...])


def _ffn_proj_cast(x_dec, x_meta, wg, wu, wd, lng, lnb, w_in):
    n = x_dec.shape[0] + x_meta.shape[0]
    ck = FFN_COL_CHUNK
    steps = D_FF // ck
    in_chunks = IN_WIDTH // ck
    assert in_chunks <= steps
    col_chunk = pl.BlockSpec((D_MODEL, ck), lambda k: (0, k))
    row_chunk = pl.BlockSpec((ck, D_MODEL), lambda k: (k, 0))
    w_in_chunk = pl.BlockSpec((D_MODEL, ck), lambda k: (0, jnp.minimum(k, in_chunks - 1)))
    whole = lambda shape: pl.BlockSpec(shape, lambda k: (0,) * len(shape))
    in_hbm = pl.BlockSpec(memory_space=pl.ANY)
    return pl.pallas_call(
        _ffn_proj_cast_kernel,
        grid=(steps,),
        in_specs=[_const_spec(x_dec.shape), _const_spec(x_meta.shape), in_hbm, in_hbm, in_hbm,
                  _const_spec(lng.shape), _const_spec(lnb.shape), in_hbm],
        out_specs=[whole((n, D_MODEL)), whole((n, IN_WIDTH)), col_chunk, col_chunk, row_chunk,
                   w_in_chunk],
        out_shape=[jax.ShapeDtypeStruct((n, D_MODEL), F32), jax.ShapeDtypeStruct((n, IN_WIDTH), F32),
                   jax.ShapeDtypeStruct(wg.shape, BF16), jax.ShapeDtypeStruct(wu.shape, BF16),
                   jax.ShapeDtypeStruct(wd.shape, BF16), jax.ShapeDtypeStruct(w_in.shape, BF16)],
        scratch_shapes=[pltpu.VMEM((n, D_MODEL), F32), pltpu.VMEM((n, D_MODEL), BF16),
                        pltpu.VMEM(w_in.shape, BF16),
                        pltpu.VMEM((RING_SLOTS, D_MODEL, ck), F32), pltpu.VMEM((RING_SLOTS, D_MODEL, ck), F32),
                        pltpu.VMEM((RING_SLOTS, ck, D_MODEL), F32), pltpu.VMEM((RING_SLOTS, D_MODEL, ck), F32),
                        pltpu.VMEM((n, D_MODEL), F32),
                        pltpu.SemaphoreType.DMA((4, RING_SLOTS))],
        compiler_params=pltpu.CompilerParams(
            dimension_semantics=("arbitrary",), vmem_limit_bytes=VMEM_LIMIT_BYTES),
        name="ffn_proj_cast",
    )(x_dec, x_meta, wg, wu, wd, lng, lnb, w_in)


def _interleave(major, starts):
    live = []
    for i, piece in enumerate(major):
        piece()
        live += [make() for make in starts.get(i, [])]
        live = [g for g in live if next(g, "done") != "done"]
    while live:
        live = [g for g in live if next(g, "done") != "done"]


def _run_all(starts):
    for i in sorted(starts):
        for make in starts[i]:
            for _ in make():
                pass


def _ffn_proj_pipelined_kernel(x_ref, wg_ref, wu_ref, wd_ref, lng_ref, lnb_ref, win_ref,
                               rope_ref, pm_ref, *rest, n_cast, steps_per_seq):
    cast_in, rest = rest[:n_cast], rest[n_cast:]
    x1_ref, proj_ref, spool_ref = rest[:3]
    cast_out, (xb_ref, z1_ref, x1b_ref, h_ref, xp_ref) = rest[3:3 + n_cast], rest[3 + n_cast:]
    t = pl.program_id(0)
    n_tiles = pl.num_programs(0) - 1
    tile = x_ref.shape[0]
    row_blocks = [slice(r, r + RET_CHUNK) for r in range(0, tile, RET_CHUNK)]
    proj_chunk = 2 * FFN_COL_CHUNK
    hist = N_META
    seq_step = (t - 1) % steps_per_seq

    def ln1_piece(rows):
        def run():
            x1_rows = _layer_norm(z1_ref[rows, :], lng_ref[...], lnb_ref[...])
            x1_ref[rows, :] = x1_rows
            x1b_ref[rows, :] = x1_rows.astype(BF16)
            yield
        return run

    def proj_piece(nk):
        def run():
            cs = slice(nk * proj_chunk, (nk + 1) * proj_chunk)
            chunk = _dot(x1b_ref[...], win_ref[:, cs])
            if nk < 2:
                cos, sin = rope_ref[:, :HEAD_DIM], rope_ref[:, HEAD_DIM:]
                chunk = jnp.concatenate(
                    [_rope(chunk[:, h * HEAD_DIM:(h + 1) * HEAD_DIM], cos, sin)
                     for h in range(RET_HEADS)], axis=1)
            proj_ref[:, cs] = chunk
        return run

    def pool_input_piece(half):
        def run():
            cs = slice(half * PAIR_WIDTH, (half + 1) * PAIR_WIDTH)
            xp_ref[hist:hist + tile, cs] = _dot(
                x1b_ref[...], win_ref[:, 4 * RET_WIDTH + half * PAIR_WIDTH:4 * RET_WIDTH + (half + 1) * PAIR_WIDTH])
        return run

    def pool_diff_piece(gi):
        def run():
            w = POOL_WINDOWS[gi]
            gs = slice(gi * POOL_GROUP, (gi + 1) * POOL_GROUP)
            rows_all = xp_ref[:, gs]
            wsum, shift = rows_all, 1
            while shift < w:
                wsum = wsum + pltpu.roll(wsum, shift, 0)
                shift *= 2
            proj_ref[:, 4 * RET_WIDTH + gi * POOL_GROUP:4 * RET_WIDTH + (gi + 1) * POOL_GROUP] = (
                wsum[hist:] * (1.0 / w) - rows_all[hist:])
            xp_ref[0:hist, gs] = xp_ref[tile:tile + hist, gs]
            yield
        return run

    def gate_up_piece(ck):
        def run():
            sl = slice(ck * FFN_COL_CHUNK, (ck + 1) * FFN_COL_CHUNK)
            g = _dot(xb_ref[...], wg_ref[:, sl])
            u = _dot(xb_ref[...], wu_ref[:, sl])
            h_ref[:, sl] = (_silu(g) * u).astype(BF16)
        return run

    def down_piece(nk):
        def run():
            cs = slice(nk * FFN_COL_CHUNK, (nk + 1) * FFN_COL_CHUNK)
            z1_ref[:, cs] = ALPHA * x_ref[:, cs] + 0.5 * _dot(h_ref[...], wd_ref[:, cs])
        return run

    ln1_starts = {i: [ln1_piece(rows)] for i, rows in enumerate(row_blocks)}
    qkvg = [proj_piece(nk) for nk in range(4 * RET_WIDTH // proj_chunk)]
    proj_pieces = [pool_input_piece(0)] + qkvg[:2] + [pool_input_piece(1)] + qkvg[2:]
    pool_diffs = [pool_diff_piece(gi) for gi in range(len(POOL_WINDOWS))]

    @pl.when(jnp.logical_and(t >= 1, seq_step == 0))
    def _history_from_meta():
        xp_ref[0:hist, :] = pm_ref[...]

    @pl.when(t == 0)
    def _clear_pipeline():
        z1_ref[...] = jnp.zeros(z1_ref.shape, F32)
        xp_ref[0:hist, :] = jnp.zeros((hist, POOL_WIDTH), F32)

    @pl.when(t < n_tiles)
    def _steady():
        xb_ref[...] = x_ref[...].astype(BF16)
        for src_ref, dst_ref in zip(cast_in, cast_out):
            dst_ref[...] = src_ref[...].astype(BF16)
        gate_up = [gate_up_piece(ck) for ck in range(D_FF // FFN_COL_CHUNK)]
        first_proj = len(row_blocks) + 1
        major = gate_up[:first_proj]
        for i, piece in enumerate(gate_up[first_proj:]):
            major += proj_pieces[i:i + 1] + [piece]
        major += proj_pieces[len(gate_up) - first_proj:]
        starts = dict(ln1_starts)
        half = len(pool_diffs) // 2
        starts[len(major) - 2] = pool_diffs[:half]
        _interleave(major, starts)
        _interleave([down_piece(nk) for nk in range(D_MODEL // FFN_COL_CHUNK)],
                    {i: [piece] for i, piece in enumerate(pool_diffs[half:])})

    @pl.when(t == n_tiles)
    def _drain():
        _run_all(ln1_starts)
        for piece in proj_pieces:
            piece()
        _run_all({0: pool_diffs})

    @pl.when(jnp.logical_and(t >= 1, seq_step == steps_per_seq - 1))
    def _emit_pool_state():
        spool_ref[0] = xp_ref[hist - POOL_BUF:hist, :]


def _slab_rows(rows, max_slabs):
    for slab in range(BF16_ROWS, rows + 1, BF16_ROWS):
        if rows % slab == 0 and rows // slab <= max_slabs:
            return slab
    raise ValueError(f"no slab size for {rows} rows in {max_slabs} steps")


def _ffn_proj_pipelined(x, wg, wu, wd, lng, lnb, w_in, cast_weights, seq, proj_small, meta_row_block):
    n = x.shape[0]
    tm = FFN_TOKEN_TILE
    n_tiles = n // tm
    steps_per_seq = seq // tm
    n_seq = n // seq
    assert seq % tm == 0 and IN_WIDTH % (2 * FFN_COL_CHUNK) == 0 and RET_WIDTH == 2 * FFN_COL_CHUNK
    rope = np.concatenate(_rope_tables(N_META + np.arange(seq)), axis=1)
    in_tile = lambda t: (jnp.minimum(t, n_tiles - 1), 0)
    out_tile = lambda t: (jnp.maximum(t - 1, 0), 0)
    rope_tile = pl.BlockSpec((tm, 2 * HEAD_DIM), lambda t: (jnp.maximum(t - 1, 0) % (seq // tm), 0))

    def slab_spec(w):
        slab = _slab_rows(w.shape[0], n_tiles)
        last = w.shape[0] // slab - 1
        return pl.BlockSpec((slab, w.shape[1]), lambda t: (jnp.minimum(t, last), 0))

    cast_specs = [slab_spec(w) for w in cast_weights]
    meta_p = pl.BlockSpec((N_META, POOL_WIDTH), lambda t: (meta_row_block, 4 * RET_WIDTH // POOL_WIDTH))
    pool_state = pl.BlockSpec(
        (1, POOL_BUF, POOL_WIDTH), lambda t: (jnp.clip((t - 1) // steps_per_seq, 0, n_seq - 1), 0, 0))
    return pl.pallas_call(
        functools.partial(_ffn_proj_pipelined_kernel, n_cast=len(cast_weights),
                          steps_per_seq=steps_per_seq),
        grid=(n_tiles + 1,),
        in_specs=[
            pl.BlockSpec((tm, D_MODEL), in_tile),
            _const_spec(wg.shape), _const_spec(wu.shape), _const_spec(wd.shape),
            _const_spec(lng.shape), _const_spec(lnb.shape), _const_spec(w_in.shape),
            rope_tile, meta_p,
        ] + cast_specs,
        out_specs=[pl.BlockSpec((tm, D_MODEL), out_tile), pl.BlockSpec((tm, IN_WIDTH), out_tile),
                   pool_state] + cast_specs,
        out_shape=[jax.ShapeDtypeStruct((n, D_MODEL), F32), jax.ShapeDtypeStruct((n, IN_WIDTH), F32),
                   jax.ShapeDtypeStruct((n_seq, POOL_BUF, POOL_WIDTH), F32)]
        + [jax.ShapeDtypeStruct(w.shape, BF16) for w in cast_weights],
        scratch_shapes=[
            pltpu.VMEM((tm, D_MODEL), BF16),
            pltpu.VMEM((tm, D_MODEL), F32),
            pltpu.VMEM((tm, D_MODEL), BF16),
            pltpu.VMEM((tm, D_FF), BF16),
            pltpu.VMEM((N_META + tm, POOL_WIDTH), F32),
        ],
        compiler_params=pltpu.CompilerParams(
            dimension_semantics=("arbitrary",), vmem_limit_bytes=VMEM_LIMIT_BYTES),
        name="ffn_proj_pipelined",
    )(x, wg, wu, wd, lng, lnb, w_in, rope, proj_small, *cast_weights)


def _rope_tables(positions):
    half = HEAD_DIM // 2
    inv_freq = ROPE_THETA ** (-np.arange(0, HEAD_DIM, 2, dtype=np.float64) / HEAD_DIM)
    ang = np.asarray(positions, np.float64)[:, None] * inv_freq[None, :]
    cos, sin = np.cos(ang), np.sin(ang)
    assert cos.shape[1] == half
    return (np.concatenate([cos, cos], axis=1).astype(np.float32),
            np.concatenate([-sin, sin], axis=1).astype(np.float32))


def _decay_tables(chunk, seq_len):
    r = np.arange(chunk)
    seq, idx = r // seq_len, (r % seq_len).astype(np.float64)
    same = seq[:, None] == seq[None, :]
    diff = idx[:, None] - idx[None, :]
    mask, qdec, kdec = [], [], []
    for gamma in GAMMAS:
        lg = math.log(gamma)
        mask.append(np.where(same & (diff >= 0), np.exp(lg * np.maximum(diff, 0.0)), 0.0) * QK_SCALE)
        qdec.append(np.broadcast_to((np.exp(lg * (idx + 1.0)) * QK_SCALE)[:, None], (chunk, HEAD_DIM)))
        kdec.append(np.broadcast_to(np.exp(lg * (seq_len - 1.0 - idx))[:, None], (chunk, HEAD_DIM)))
    to32 = lambda t: np.stack(t).astype(np.float32)
    return to32(mask), to32(qdec), to32(kdec)


def _rope(x, cos, sin):
    return x * cos + pltpu.roll(x, HEAD_DIM // 2, 1) * sin


def _group_norm(o):
    mu = jnp.mean(o, axis=-1, keepdims=True)
    oc = o - mu
    var = jnp.mean(oc * oc, axis=-1, keepdims=True)
    return oc * lax.rsqrt(var + GN_EPS)


def _prompt_mixer_ffn_kernel(proj_ref, x1_ref,
                             km_ref, vm_ref, cosm_ref, sinm_ref, kdecm_ref,
                             mask_ref, qdec_ref, kdec_ref, poolw_ref, pscale_ref, wout_ref,
                             ln2g_ref, ln2b_ref, wg_ref, wu_ref, wd_ref, ln3g_ref, ln3b_ref,
                             x2dec_hbm,
                             y_ref, sret_ref, ydec_hbm,
                             s_ref, mix_ref, x2_ref, xb_ref, ypre_ref, h_ref, kb_ref, vs_ref, pwbd_ref,
                             *, steps_per_seq):
    t = pl.program_id(0)
    n_tiles = pl.num_programs(0) - 2
    c = t % steps_per_seq
    tile = proj_ref.shape[0]
    q_ref, k_ref, v_ref, g_ref, p_ref = (
        proj_ref.at[:, j * RET_WIDTH:(j + 1) * RET_WIDTH] for j in range(5))
    row_blocks = [slice(r, r + RET_CHUNK) for r in range(0, tile, RET_CHUNK)]

    @pl.when(jnp.logical_and(c == 0, t < n_tiles))
    def _init_from_meta():
        for h in range(RET_HEADS):
            hs = slice(h * HEAD_DIM, (h + 1) * HEAD_DIM)
            kr = _rope(km_ref[:, hs], cosm_ref[...], sinm_ref[...])
            kd = (kr * kdecm_ref[h]).astype(BF16)
            s_ref[h] = _dot_tn(kd, vm_ref[:, hs].astype(BF16))

    def ln3_fetch_piece(rows):
        def run():
            y_ref[rows, :] = ypre_ref[rows, :]
            yield
        return run

    def ln3_piece(rows):
        def run():
            y_ref[rows, :] = _layer_norm(y_ref[rows, :], ln3g_ref[...], ln3b_ref[...])
            yield
        return run

    state = [None] * RET_HEADS

    def retention_piece(ci, hp):
        def run():
            rows = slice(ci * RET_CHUNK, (ci + 1) * RET_CHUNK)
            pair = slice(hp * PAIR_WIDTH, (hp + 1) * PAIR_WIDTH)
            buf = ci * 2 + hp
            qr, kr = [], []
            for j in range(2):
                h = 2 * hp + j
                hs = slice(h * HEAD_DIM, (h + 1) * HEAD_DIM)
                blk = slice(j * HEAD_DIM, (j + 1) * HEAD_DIM)
                qr.append(q_ref[rows, hs])
                kr.append(k_ref[rows, hs])
                kb_ref[buf, blk, blk] = kr[j].astype(BF16)
                vs_ref[buf, blk, blk] = v_ref[rows, hs].astype(BF16)
                vs_ref[buf, PAIR_WIDTH + j * HEAD_DIM:PAIR_WIDTH + (j + 1) * HEAD_DIM, blk] = (
                    state[h].astype(BF16))
            q2 = jnp.concatenate(qr, axis=1)
            k2 = jnp.concatenate(kr, axis=1)
            scores = _dot_nt(q2.astype(BF16), kb_ref[buf])
            upd = _dot((k2 * kdec_ref[hp]).T.astype(BF16), v_ref[rows, pair].astype(BF16))
            qd2 = (q2 * qdec_ref[hp]).astype(BF16)
            yield
            lhs = jnp.concatenate([(scores * mask_ref[hp]).astype(BF16), qd2], axis=1)
            o2 = _dot(lhs, vs_ref[buf])
            for j in range(2):
                blk = slice(j * HEAD_DIM, (j + 1) * HEAD_DIM)
                state[2 * hp + j] = (GAMMAS[2 * hp + j] ** RET_CHUNK) * state[2 * hp + j] + upd[blk, blk]
            yield
            for j in range(2):
                hs = slice((2 * hp + j) * HEAD_DIM, (2 * hp + j + 1) * HEAD_DIM)
                blk = slice(j * HEAD_DIM, (j + 1) * HEAD_DIM)
                mix_ref[rows, hs] = (_silu(g_ref[rows, hs]) * _group_norm(o2[:, blk])).astype(BF16)
        return run

    def pool_piece(pp):
        def run():
            pair = slice(pp * PAIR_WIDTH, (pp + 1) * PAIR_WIDTH)
            pooled = _dot(p_ref[:, pair].astype(BF16), pwbd_ref[pp])
            yield
            pooled = pooled * pscale_ref[:, pair]
            mix_ref[:, RET_WIDTH + pp * PAIR_WIDTH:RET_WIDTH + (pp + 1) * PAIR_WIDTH] = pooled.astype(BF16)
        return run

    def gate_up_piece(ck):
        def run():
            sl = slice(ck * FFN_COL_CHUNK, (ck + 1) * FFN_COL_CHUNK)
            g = _dot(xb_ref[...], wg_ref[:, sl])
            u = _dot(xb_ref[...], wu_ref[:, sl])
            h_ref[:, sl] = (_silu(g) * u).astype(BF16)
        return run

    def residual_piece():
        ypre_ref[...] = ALPHA * x2_ref[...]
        yield

    def w_out_piece():
        for h in range(RET_HEADS):
            s_ref[h] = state[h]
        x2_ref[...] = ALPHA * x1_ref[...] + _dot(mix_ref[...], wout_ref[...])

    def ln2_piece(rows):
        def run():
            x2_ref[rows, :] = _layer_norm(x2_ref[rows, :], ln2g_ref[...], ln2b_ref[...])
            yield
        return run

    def xb_piece(rows):
        def run():
            xb_ref[rows, :] = x2_ref[rows, :].astype(BF16)
            yield
        return run

    def down_piece(nk):
        def run():
            cs = slice(nk * FFN_COL_CHUNK, (nk + 1) * FFN_COL_CHUNK)
            ypre_ref[:, cs] = ypre_ref[:, cs] + 0.5 * _dot(h_ref[...], wd_ref[:, cs])
        return run

    def steady_body():
        starts = {}

        def start_at(i, piece):
            starts.setdefault(i, []).append(piece)

        for rows in row_blocks:
            start_at(0, ln3_fetch_piece(rows))
        start_at(1, residual_piece)
        for h in range(RET_HEADS):
            state[h] = s_ref[h]
        gate_up = [gate_up_piece(ck) for ck in range(D_FF // FFN_COL_CHUNK)]
        pieces = [(ci, hp) for ci in range(len(row_blocks)) for hp in range(RET_HEADS // 2)]
        w_out_at = len(pieces)
        for n, (ci, hp) in enumerate(pieces):
            start_at(n // 2 if n < 4 else n - 2, retention_piece(ci, hp))
        for pp in range(len(POOL_WINDOWS) // 2):
            start_at(2 * pp + 1, pool_piece(pp))
        major = gate_up[:w_out_at] + [w_out_piece] + gate_up[w_out_at:]
        for i, rows in enumerate(row_blocks):
            start_at(min(w_out_at + i, len(major) - 1), ln2_piece(rows))
        _interleave(major, starts)

        tail_starts = {i: [ln3_piece(rows), xb_piece(rows)] for i, rows in enumerate(row_blocks)}
        _interleave([down_piece(nk) for nk in range(D_MODEL // FFN_COL_CHUNK)], tail_starts)

    def drain_body():
        for rows in row_blocks:
            for make in (ln3_fetch_piece(rows), ln3_piece(rows)):
                for _ in make():
                    pass

    @pl.when(t == 0)
    def _prime_pipeline():
        pltpu.sync_copy(x2dec_hbm, x2_ref)
        xb_ref[...] = x2_ref[...].astype(BF16)
        kb_ref[...] = jnp.zeros(kb_ref.shape, BF16)
        vs_ref[...] = jnp.zeros(vs_ref.shape, BF16)
        ypre_ref[...] = jnp.zeros(ypre_ref.shape, F32)
        pwbd_ref[...] = jnp.zeros(pwbd_ref.shape, BF16)
        for gi in range(len(POOL_WINDOWS)):
            blk = slice((gi % 2) * POOL_GROUP, (gi % 2 + 1) * POOL_GROUP)
            pwbd_ref[gi // 2, blk, blk] = poolw_ref[gi].astype(BF16)

    @pl.when(t <= n_tiles)
    def _steady():
        steady_body()

    @pl.when(t == 1)
    def _emit_decode_rows():
        pltpu.sync_copy(y_ref, ydec_hbm)

    @pl.when(t == n_tiles + 1)
    def _drain_last():
        drain_body()

    @pl.when(jnp.logical_and(c == steps_per_seq - 1, t < n_tiles))
    def _emit_state():
        sret_ref[0] = s_ref[...]


def _prompt_mixer_ffn(proj, x1, proj_small, meta_row_block, pool_w, pool_scale, w_out, ln2g, ln2b,
                      wg, wu, wd, ln3g, ln3b, x2_dec, batch, seq):
    tile = MIX_TOKEN_TILE
    assert x2_dec.shape == (tile, D_MODEL)
    steps = seq // tile
    n_tiles = batch * steps
    cosm, sinm = _rope_tables(np.arange(N_META))
    pair_up = lambda tab: np.concatenate([tab[0::2], tab[1::2]], axis=2)
    mask, qdec, kdec = (pair_up(tab) for tab in _decay_tables(RET_CHUNK, RET_CHUNK))
    _, _, kdecm = _decay_tables(N_META, N_META)

    mix_tile = lambda t: jnp.minimum(t, n_tiles - 1)
    ffn_tile = lambda t: jnp.maximum(t - 2, 0)

    def meta_col(j):
        return pl.BlockSpec((N_META, RET_WIDTH), lambda t: (meta_row_block, j))

    in_specs = [
        pl.BlockSpec((tile, IN_WIDTH), lambda t: (mix_tile(t), 0)),
        pl.BlockSpec((tile, D_MODEL), lambda t: (mix_tile(t), 0)),
        meta_col(1), meta_col(2),
        _const_spec(cosm.shape), _const_spec(sinm.shape), _const_spec(kdecm.shape),
        _const_spec(mask.shape), _const_spec(qdec.shape), _const_spec(kdec.shape),
        _const_spec(pool_w.shape), _const_spec(pool_scale.shape), _const_spec(w_out.shape),
        _const_spec(ln2g.shape), _const_spec(ln2b.shape),
        _const_spec(wg.shape), _const_spec(wu.shape), _const_spec(wd.shape),
        _const_spec(ln3g.shape), _const_spec(ln3b.shape),
        pl.BlockSpec(memory_space=pl.ANY),
    ]
    out_shape = [
        jax.ShapeDtypeStruct((batch * seq, D_MODEL), F32),
        jax.ShapeDtypeStruct((batch, RET_HEADS, HEAD_DIM, HEAD_DIM), F32),
        jax.ShapeDtypeStruct((tile, D_MODEL), F32),
    ]
    out_specs = [
        pl.BlockSpec((tile, D_MODEL), lambda t: (ffn_tile(t), 0)),
        pl.BlockSpec((1, RET_HEADS, HEAD_DIM, HEAD_DIM), lambda t: (mix_tile(t) // steps, 0, 0, 0)),
        pl.BlockSpec(memory_space=pl.ANY),
    ]
    return pl.pallas_call(
        functools.partial(_prompt_mixer_ffn_kernel, steps_per_seq=steps),
        grid=(n_tiles + 2,),
        in_specs=in_specs,
        out_specs=out_specs,
        out_shape=out_shape,
        scratch_shapes=[
            pltpu.VMEM((RET_HEADS, HEAD_DIM, HEAD_DIM), F32),
            pltpu.VMEM((tile, D_MODEL), BF16),
            pltpu.VMEM((tile, D_MODEL), F32),
            pltpu.VMEM((tile, D_MODEL), BF16),
            pltpu.VMEM((tile, D_MODEL), F32),
            pltpu.VMEM((tile, D_FF), BF16),
            pltpu.VMEM((8, PAIR_WIDTH, PAIR_WIDTH), BF16),
            pltpu.VMEM((8, 2 * PAIR_WIDTH, PAIR_WIDTH), BF16),
            pltpu.VMEM((len(POOL_WINDOWS) // 2, PAIR_WIDTH, PAIR_WIDTH), BF16),
        ],
        compiler_params=pltpu.CompilerParams(
            dimension_semantics=("arbitrary",), vmem_limit_bytes=VMEM_LIMIT_BYTES),
        name="prompt_mixer_ffn",
    )(proj, x1, proj_small, proj_small,
      cosm, sinm, kdecm, mask, qdec, kdec, pool_w, pool_scale, w_out, ln2g, ln2b,
      wg, wu, wd, ln3g, ln3b, x2_dec)


def _decode_mixer_kernel(q_ref, k_ref, v_ref, g_ref, *rest, dec_seq):
    p_refs, rest = rest[:len(POOL_WINDOWS)], rest[len(POOL_WINDOWS):]
    (x1_ref, s0_ref, pref_ref, cos_ref, sin_ref, mask_ref, qdec_ref, kdec_ref,
     poolw_ref, pscale_ref, wout_ref, lng_ref, lnb_ref,
     o_ref, sret_ref, spool_ref, d_ref, mix_ref) = rest
    rows = q_ref.shape[0]
    nseq = rows // dec_seq
    seq_per_group = BF16_ROWS // dec_seq
    cos, sin = cos_ref[...], sin_ref[...]
    row_seq = lax.broadcasted_iota(jnp.int32, (BF16_ROWS, HEAD_DIM), 0) // dec_seq

    for h in range(RET_HEADS):
        hs = slice(h * HEAD_DIM, (h + 1) * HEAD_DIM)
        qr = _rope(q_ref[:, hs], cos, sin)
        kr = _rope(k_ref[:, hs], cos, sin)
        v = v_ref[:, hs]
        vb = v.astype(BF16)
        scores = _dot_nt(qr.astype(BF16), kr.astype(BF16)) * mask_ref[h]
        o_inner = _dot(scores.astype(BF16), vb)
        qd = qr * qdec_ref[h]
        kd = kr * kdec_ref[h]
        o_cross = []
        for grp in range(rows // BF16_ROWS):
            gr = slice(grp * BF16_ROWS, (grp + 1) * BF16_ROWS)
            qd_g = qd[gr].astype(BF16)
            kd_g = kd[gr].astype(BF16)
            v_g = v[gr]
            acc = jnp.zeros((BF16_ROWS, HEAD_DIM), F32)
            for j in range(seq_per_group):
                b = grp * seq_per_group + j
                s = s0_ref[b, h]
                acc = jnp.where(row_seq == j, _dot(qd_g, s.astype(BF16)), acc)
                v_b = jnp.where(row_seq == j, v_g, 0.0).astype(BF16)
                sret_ref[b, h] = (GAMMAS[h] ** dec_seq) * s + _dot_tn(kd_g, v_b)
            o_cross.append(acc)
        o = o_inner + jnp.concatenate(o_cross, axis=0)
        mix_ref[:, hs] = (_silu(g_ref[:, hs]) * _group_norm(o)).astype(BF16)

    for gi, (w, pg_ref) in enumerate(zip(POOL_WINDOWS, p_refs)):
        gs = slice(gi * POOL_GROUP, (gi + 1) * POOL_GROUP)
        steps = [pref_ref[j, :, gs] for j in range(POOL_BUF)]
        steps += [pg_ref[pl.ds(i, nseq, stride=dec_seq), :] for i in range(dec_seq)]
        for i in range(dec_seq):
            now = POOL_BUF + i
            wsum = steps[now]
            for back in range(1, w):
                wsum = wsum + steps[now - back]
            d_ref[gi, pl.ds(i, nseq, stride=dec_seq), :] = wsum * (1.0 / w) - steps[now]
        for j in range(POOL_BUF):
            spool_ref[j, :, gs] = steps[dec_seq + j]
        pooled = _dot(d_ref[gi].astype(BF16), poolw_ref[gi].astype(BF16)) * pscale_ref[:, gs]
        mix_ref[:, RET_WIDTH + gi * POOL_GROUP:RET_WIDTH + (gi + 1) * POOL_GROUP] = pooled.astype(BF16)

    y = _dot(mix_ref[...], wout_ref[...])
    o_ref[...] = _layer_norm(ALPHA * x1_ref[...] + y, lng_ref[...], lnb_ref[...])


def _decode_mixer(proj, x1, state_ret, state_pool, pool_w, pool_scale, w_out, lng, lnb, nseq, dec_seq):
    assert BF16_ROWS % dec_seq == 0 and dec_seq <= POOL_BUF
    rows = DEC_SEQ_BLOCK * dec_seq
    steps = nseq // DEC_SEQ_BLOCK
    cos, sin = _rope_tables(PAST_LEN + (np.arange(rows) % dec_seq))
    mask, qdec, kdec = _decay_tables(rows, dec_seq)

    def col(j):
        return pl.BlockSpec((rows, RET_WIDTH), lambda i: (i, j))

    state_spec = pl.BlockSpec((DEC_SEQ_BLOCK, RET_HEADS, HEAD_DIM, HEAD_DIM), lambda i: (i, 0, 0, 0))
    state_pool = jnp.transpose(state_pool, (1, 0, 2))
    pool_spec = pl.BlockSpec((POOL_BUF, DEC_SEQ_BLOCK, POOL_WIDTH), lambda i: (0, i, 0))
    groups = len(POOL_WINDOWS)
    p_cols = 4 * RET_WIDTH // POOL_GROUP
    in_specs = [
        col(0), col(1), col(2), col(3),
        *[pl.BlockSpec((rows, POOL_GROUP), lambda i, gi=gi: (i, p_cols + gi)) for gi in range(groups)],
        pl.BlockSpec((rows, D_MODEL), lambda i: (i, 0)),
        state_spec, pool_spec,
        _const_spec(cos.shape), _const_spec(sin.shape),
        _const_spec(mask.shape), _const_spec(qdec.shape), _const_spec(kdec.shape),
        _const_spec(pool_w.shape), _const_spec(pool_scale.shape), _const_spec(w_out.shape),
        _const_spec(lng.shape), _const_spec(lnb.shape),
    ]
    out_shape = [
        jax.ShapeDtypeStruct((nseq * dec_seq, D_MODEL), F32),
        jax.ShapeDtypeStruct(state_ret.shape, F32),
        jax.ShapeDtypeStruct(state_pool.shape, F32),
    ]
    out_specs = [pl.BlockSpec((rows, D_MODEL), lambda i: (i, 0)), state_spec, pool_spec]
    x2, new_ret, new_pool = pl.pallas_call(
        functools.partial(_decode_mixer_kernel, dec_seq=dec_seq),
        grid=(steps,),
        in_specs=in_specs,
        out_specs=out_specs,
        out_shape=out_shape,
        scratch_shapes=[
            pltpu.VMEM((groups, rows, POOL_GROUP), F32),
            pltpu.VMEM((rows, D_MODEL), BF16),
        ],
        compiler_params=pltpu.CompilerParams(
            dimension_semantics=("arbitrary",), vmem_limit_bytes=VMEM_LIMIT_BYTES),
        name="decode_mixer",
    )(*([proj] * (4 + groups)), x1, state_ret, state_pool, cos, sin, mask, qdec, kdec,
      pool_w, pool_scale, w_out, lng, lnb)
    return x2, new_ret, jnp.transpose(new_pool, (1, 0, 2))


def kernel(x_prompt, x_sample, state_ret, state_pool, meta_tokens, ffn1_w_gate, ffn1_w_up, ffn1_w_down,
           ln1_g, ln1_b, w_in, pool_w, pool_scale, w_out, ln2_g, ln2_b, ffn2_w_gate, ffn2_w_up,
           ffn2_w_down, ln3_g, ln3_b):
    assert ffn1_w_gate.shape[0] == DEPTH == 1
    batch, seq, _ = x_prompt.shape
    nseq, dec_seq, _ = x_sample.shape
    n_dec = nseq * dec_seq
    assert n_dec % N_META == 0

    row = lambda v: v[0].reshape(1, -1)
    pool_w_b = pool_w[0]
    pscale, g2, b2 = row(pool_scale), row(ln2_g), row(ln2_b)

    xp = x_prompt.reshape(batch * seq, D_MODEL)

    x1s, projs, wg1, wu1, wd1, w_in_b = _ffn_proj_cast(
        x_sample.reshape(n_dec, D_MODEL), meta_tokens.astype(x_prompt.dtype),
        ffn1_w_gate[0], ffn1_w_up[0], ffn1_w_down[0], row(ln1_g), row(ln1_b), w_in[0])
    meta_rows = n_dec // N_META
    x1p, projp, pool_p, wg2, wu2, wd2, w_out_b = _ffn_proj_pipelined(
        xp, wg1, wu1, wd1, row(ln1_g), row(ln1_b), w_in_b,
        (ffn2_w_gate[0], ffn2_w_up[0], ffn2_w_down[0], w_out[0]), seq, projs, meta_rows)
    f2 = (wg2, wu2, wd2, row(ln3_g), row(ln3_b))

    x2s, ret_s, pool_s = _decode_mixer(projs, x1s, state_ret[0], state_pool[0], pool_w_b, pscale,
                                       w_out_b, g2, b2, nseq, dec_seq)
    y_prompt, ret_p, y_sample = _prompt_mixer_ffn(
        projp, x1p, projs, meta_rows, pool_w_b, pscale, w_out_b, g2, b2, *f2, x2s, batch, seq)
    return (y_prompt.reshape(batch, seq, D_MODEL), y_sample.reshape(nseq, dec_seq, D_MODEL),
            ret_p[None], pool_p[None], ret_s[None], pool_s[None])
```

```python
import functools
import math

import jax
import jax.numpy as jnp
import numpy as np
from jax import lax
from jax.experimental import pallas as pl
from jax.experimental.pallas import tpu as pltpu

F32 = jnp.float32
BF16 = jnp.bfloat16

D_MODEL = 1024
D_FF = 2816
N_META = 16
PAST_LEN = 16384
RET_HEADS = 4
HEAD_DIM = 128
RET_WIDTH = RET_HEADS * HEAD_DIM
RET_CHUNK = 128
ROPE_THETA = 10000.0
POOL_WINDOWS = (2, 4, 8, 16)
POOL_GROUP = 128
POOL_WIDTH = POOL_GROUP * len(POOL_WINDOWS)
POOL_BUF = max(POOL_WINDOWS) - 1
IN_WIDTH = 4 * RET_WIDTH + POOL_WIDTH
DEPTH = 1
ALPHA = (2.0 * DEPTH) ** 0.25
LN_EPS = 1e-5
GN_EPS = 1e-5
QK_SCALE = HEAD_DIM ** -0.5
GAMMAS = tuple(1.0 - 2.0 ** (-5.0 - h) for h in range(RET_HEADS))

VMEM_LIMIT_BYTES = 56 * 1024 * 1024
FFN_TOKEN_TILE = 512
FFN_COL_CHUNK = 256
MIX_TOKEN_TILE = 512
DEC_SEQ_BLOCK = 32
BF16_ROWS = 16
PAIR_WIDTH = 2 * HEAD_DIM
RING_SLOTS = 3


def _layer_norm(z, g, b):
    mu = jnp.mean(z, axis=-1, keepdims=True)
    zc = z - mu
    var = jnp.mean(zc * zc, axis=-1, keepdims=True)
    return zc * lax.rsqrt(var + LN_EPS) * g + b


def _silu(x):
    return x * jax.nn.sigmoid(x)


def _dot(a, b):
    return jnp.dot(a, b, preferred_element_type=F32)


def _dot_nt(a, b):
    return lax.dot_general(a, b, (((1,), (1,)), ((), ())), preferred_element_type=F32)


def _dot_tn(a, b):
    return lax.dot_general(a, b, (((0,), (0,)), ((), ())), preferred_element_type=F32)


def _const_spec(shape):
    zeros = (0,) * len(shape)
    return pl.BlockSpec(shape, lambda *_: zeros, pipeline_mode=pl.Buffered(1))


def _ffn_proj_cast_kernel(xd_ref, xm_ref, wg_ref, wu_ref, wd_ref, lng_ref, lnb_ref, win_ref,
                          x1_ref, proj_ref, wgb_ref, wub_ref, wdb_ref, winb_ref,
                          acc_ref, xb_ref, win_all_ref, wg_buf, wu_buf, wd_buf, win_buf, x_ref, sems):
    k = pl.program_id(0)
    ck = FFN_COL_CHUNK
    in_chunks = IN_WIDTH // ck
    streams = ((wg_ref, wg_buf, D_FF // ck, 1), (wu_ref, wu_buf, D_FF // ck, 1),
               (wd_ref, wd_buf, D_FF // ck, 0), (win_ref, win_buf, in_chunks, 1))

    def chunk_copy(stream, chunk):
        src, ring, _, axis = streams[stream]
        idx = (pl.ds(chunk * ck, ck), slice(None)) if axis == 0 else (slice(None), pl.ds(chunk * ck, ck))
        slot = chunk % RING_SLOTS
        return pltpu.make_async_copy(src.at[idx], ring.at[slot], sems.at[stream, slot])

    def start_chunk(chunk):
        for stream, (_, _, n_chunks, _) in enumerate(streams):
            @pl.when(chunk < n_chunks)
            def _(stream=stream):
                chunk_copy(stream, chunk).start()

    @pl.when(k == 0)
    def _start():
        for ahead in range(RING_SLOTS - 1):
            start_chunk(jnp.int32(ahead))
        n_dec = xd_ref.shape[0]
        x_ref[0:n_dec, :] = xd_ref[...]
        x_ref[n_dec:, :] = xm_ref[...]
        xb_ref[...] = x_ref[...].astype(BF16)
        acc_ref[...] = jnp.zeros(acc_ref.shape, F32)

    start_chunk(k + RING_SLOTS - 1)
    slot = k % RING_SLOTS
    for stream in range(3):
        chunk_copy(stream, k).wait()
    wg, wu, wd = wg_buf[slot].astype(BF16), wu_buf[slot].astype(BF16), wd_buf[slot].astype(BF16)
    wgb_ref[...] = wg
    wub_ref[...] = wu
    wdb_ref[...] = wd
    h = (_silu(_dot(xb_ref[...], wg)) * _dot(xb_ref[...], wu)).astype(BF16)
    acc_ref[...] += _dot(h, wd)

    for j in range(in_chunks):
        @pl.when(k == j)
        def _round_w_in_chunk(j=j):
            chunk_copy(3, j).wait()
            w_in = win_buf[j % RING_SLOTS].astype(BF16)
            winb_ref[...] = w_in
            win_all_ref[:, j * ck:(j + 1) * ck] = w_in

    @pl.when(k == pl.num_programs(0) - 1)
    def _finish():
        x1 = _layer_norm(ALPHA * x_ref[...] + 0.5 * acc_ref[...], lng_ref[...], lnb_ref[...])
        x1_ref[...] = x1
        proj_ref[...] = _dot(x1.astype(BF16), win_all_ref[...])


def _ffn_proj_cast(x_dec, x_meta, wg, wu, wd, lng, lnb, w_in):
    n = x_dec.shape[0] + x_meta.shape[0]
    ck = FFN_COL_CHUNK
    steps = D_FF // ck
    in_chunks = IN_WIDTH // ck
    assert in_chunks <= steps
    col_chunk = pl.BlockSpec((D_MODEL, ck), lambda k: (0, k))
    row_chunk = pl.BlockSpec((ck, D_MODEL), lambda k: (k, 0))
    w_in_chunk = pl.BlockSpec((D_MODEL, ck), lambda k: (0, jnp.minimum(k, in_chunks - 1)))
    whole = lambda shape: pl.BlockSpec(shape, lambda k: (0,) * len(shape))
    in_hbm = pl.BlockSpec(memory_space=pl.ANY)
    return pl.pallas_call(
        _ffn_proj_cast_kernel,
        grid=(steps,),
        in_specs=[_const_spec(x_dec.shape), _const_spec(x_meta.shape), in_hbm, in_hbm, in_hbm,
                  _const_spec(lng.shape), _const_spec(lnb.shape), in_hbm],
        out_specs=[whole((n, D_MODEL)), whole((n, IN_WIDTH)), col_chunk, col_chunk, row_chunk,
                   w_in_chunk],
        out_shape=[jax.ShapeDtypeStruct((n, D_MODEL), F32), jax.ShapeDtypeStruct((n, IN_WIDTH), F32),
                   jax.ShapeDtypeStruct(wg.shape, BF16), jax.ShapeDtypeStruct(wu.shape, BF16),
                   jax.ShapeDtypeStruct(wd.shape, BF16), jax.ShapeDtypeStruct(w_in.shape, BF16)],
        scratch_shapes=[pltpu.VMEM((n, D_MODEL), F32), pltpu.VMEM((n, D_MODEL), BF16),
                        pltpu.VMEM(w_in.shape, BF16),
                        pltpu.VMEM((RING_SLOTS, D_MODEL, ck), F32), pltpu.VMEM((RING_SLOTS, D_MODEL, ck), F32),
                        pltpu.VMEM((RING_SLOTS, ck, D_MODEL), F32), pltpu.VMEM((RING_SLOTS, D_MODEL, ck), F32),
                        pltpu.VMEM((n, D_MODEL), F32),
                        pltpu.SemaphoreType.DMA((4, RING_SLOTS))],
        compiler_params=pltpu.CompilerParams(
            dimension_semantics=("arbitrary",), vmem_limit_bytes=VMEM_LIMIT_BYTES),
        name="ffn_proj_cast",
    )(x_dec, x_meta, wg, wu, wd, lng, lnb, w_in)


def _interleave(major, starts):
    live = []
    for i, piece in enumerate(major):
        piece()
        live += [make() for make in starts.get(i, [])]
        live = [g for g in live if next(g, "done") != "done"]
    while live:
        live = [g for g in live if next(g, "done") != "done"]


def _run_all(starts):
    for i in sorted(starts):
        for make in starts[i]:
            for _ in make():
                pass


def _ffn_proj_pipelined_kernel(x_ref, wg_ref, wu_ref, wd_ref, lng_ref, lnb_ref, win_ref,
                               rope_ref, pm_ref, *rest, n_cast, steps_per_seq):
    cast_in, rest = rest[:n_cast], rest[n_cast:]
    x1_ref, proj_ref, spool_ref = rest[:3]
    cast_out, (xb_ref, z1_ref, x1b_ref, h_ref, xp_ref) = rest[3:3 + n_cast], rest[3 + n_cast:]
    t = pl.program_id(0)
    n_tiles = pl.num_programs(0) - 1
    tile = x_ref.shape[0]
    row_blocks = [slice(r, r + RET_CHUNK) for r in range(0, tile, RET_CHUNK)]
    proj_chunk = 2 * FFN_COL_CHUNK
    hist = N_META
    seq_step = (t - 1) % steps_per_seq

    def ln1_piece(rows):
        def run():
            x1_rows = _layer_norm(z1_ref[rows, :], lng_ref[...], lnb_ref[...])
            x1_ref[rows, :] = x1_rows
            x1b_ref[rows, :] = x1_rows.astype(BF16)
            yield
        return run

    def proj_piece(nk):
        def run():
            cs = slice(nk * proj_chunk, (nk + 1) * proj_chunk)
            chunk = _dot(x1b_ref[...], win_ref[:, cs])
            if nk < 2:
                cos, sin = rope_ref[:, :HEAD_DIM], rope_ref[:, HEAD_DIM:]
                chunk = jnp.concatenate(
                    [_rope(chunk[:, h * HEAD_DIM:(h + 1) * HEAD_DIM], cos, sin)
                     for h in range(RET_HEADS)], axis=1)
            proj_ref[:, cs] = chunk
        return run

    def pool_input_piece(half):
        def run():
            cs = slice(half * PAIR_WIDTH, (half + 1) * PAIR_WIDTH)
            xp_ref[hist:hist + tile, cs] = _dot(
                x1b_ref[...], win_ref[:, 4 * RET_WIDTH + half * PAIR_WIDTH:4 * RET_WIDTH + (half + 1) * PAIR_WIDTH])
        return run

    def pool_diff_piece(gi):
        def run():
            w = POOL_WINDOWS[gi]
            gs = slice(gi * POOL_GROUP, (gi + 1) * POOL_GROUP)
            rows_all = xp_ref[:, gs]
            wsum, shift = rows_all, 1
            while shift < w:
                wsum = wsum + pltpu.roll(wsum, shift, 0)
                shift *= 2
            proj_ref[:, 4 * RET_WIDTH + gi * POOL_GROUP:4 * RET_WIDTH + (gi + 1) * POOL_GROUP] = (
                wsum[hist:] * (1.0 / w) - rows_all[hist:])
            xp_ref[0:hist, gs] = xp_ref[tile:tile + hist, gs]
            yield
        return run

    def gate_up_piece(ck):
        def run():
            sl = slice(ck * FFN_COL_CHUNK, (ck + 1) * FFN_COL_CHUNK)
            g = _dot(xb_ref[...], wg_ref[:, sl])
            u = _dot(xb_ref[...], wu_ref[:, sl])
            h_ref[:, sl] = (_silu(g) * u).astype(BF16)
        return run

    def down_piece(nk):
        def run():
            cs = slice(nk * FFN_COL_CHUNK, (nk + 1) * FFN_COL_CHUNK)
            z1_ref[:, cs] = ALPHA * x_ref[:, cs] + 0.5 * _dot(h_ref[...], wd_ref[:, cs])
        return run

    ln1_starts = {i: [ln1_piece(rows)] for i, rows in enumerate(row_blocks)}
    qkvg = [proj_piece(nk) for nk in range(4 * RET_WIDTH // proj_chunk)]
    proj_pieces = [pool_input_piece(0)] + qkvg[:2] + [pool_input_piece(1)] + qkvg[2:]
    pool_diffs = [pool_diff_piece(gi) for gi in range(len(POOL_WINDOWS))]

    @pl.when(jnp.logical_and(t >= 1, seq_step == 0))
    def _history_from_meta():
        xp_ref[0:hist, :] = pm_ref[...]

    @pl.when(t == 0)
    def _clear_pipeline():
        z1_ref[...] = jnp.zeros(z1_ref.shape, F32)
        xp_ref[0:hist, :] = jnp.zeros((hist, POOL_WIDTH), F32)

    @pl.when(t < n_tiles)
    def _steady():
        xb_ref[...] = x_ref[...].astype(BF16)
        for src_ref, dst_ref in zip(cast_in, cast_out):
            dst_ref[...] = src_ref[...].astype(BF16)
        gate_up = [gate_up_piece(ck) for ck in range(D_FF // FFN_COL_CHUNK)]
        first_proj = len(row_blocks) + 1
        major = gate_up[:first_proj]
        for i, piece in enumerate(gate_up[first_proj:]):
            major += proj_pieces[i:i + 1] + [piece]
        major += proj_pieces[len(gate_up) - first_proj:]
        starts = dict(ln1_starts)
        half = len(pool_diffs) // 2
        starts[len(major) - 2] = pool_diffs[:half]
        _interleave(major, starts)
        _interleave([down_piece(nk) for nk in range(D_MODEL // FFN_COL_CHUNK)],
                    {i: [piece] for i, piece in enumerate(pool_diffs[half:])})

    @pl.when(t == n_tiles)
    def _drain():
        _run_all(ln1_starts)
        for piece in proj_pieces:
            piece()
        _run_all({0: pool_diffs})

    for b in range(spool_ref.shape[1]):
        @pl.when(t == (b + 1) * steps_per_seq)
        def _emit_pool_state(b=b):
            spool_ref[:, b, :] = xp_ref[hist - POOL_BUF:hist, :]


def _slab_rows(rows, max_slabs):
    for slab in range(BF16_ROWS, rows + 1, BF16_ROWS):
        if rows % slab == 0 and rows // slab <= max_slabs:
            return slab
    raise ValueError(f"no slab size for {rows} rows in {max_slabs} steps")


def _ffn_proj_pipelined(x, wg, wu, wd, lng, lnb, w_in, cast_weights, seq, proj_small, meta_row_block):
    n = x.shape[0]
    tm = FFN_TOKEN_TILE
    n_tiles = n // tm
    steps_per_seq = seq // tm
    n_seq = n // seq
    assert seq % tm == 0 and IN_WIDTH % (2 * FFN_COL_CHUNK) == 0 and RET_WIDTH == 2 * FFN_COL_CHUNK
    rope = np.concatenate(_rope_tables(N_META + np.arange(seq)), axis=1)
    in_tile = lambda t: (jnp.minimum(t, n_tiles - 1), 0)
    out_tile = lambda t: (jnp.maximum(t - 1, 0), 0)
    rope_tile = pl.BlockSpec((tm, 2 * HEAD_DIM), lambda t: (jnp.maximum(t - 1, 0) % (seq // tm), 0))

    def slab_spec(w):
        slab = _slab_rows(w.shape[0], n_tiles)
        last = w.shape[0] // slab - 1
        return pl.BlockSpec((slab, w.shape[1]), lambda t: (jnp.minimum(t, last), 0))

    cast_specs = [slab_spec(w) for w in cast_weights]
    meta_p = pl.BlockSpec((N_META, POOL_WIDTH), lambda t: (meta_row_block, 4 * RET_WIDTH // POOL_WIDTH))
    pool_state = pl.BlockSpec((POOL_BUF, n_seq, POOL_WIDTH), lambda t: (0, 0, 0))
    return pl.pallas_call(
        functools.partial(_ffn_proj_pipelined_kernel, n_cast=len(cast_weights),
                          steps_per_seq=steps_per_seq),
        grid=(n_tiles + 1,),
        in_specs=[
            pl.BlockSpec((tm, D_MODEL), in_tile),
            _const_spec(wg.shape), _const_spec(wu.shape), _const_spec(wd.shape),
            _const_spec(lng.shape), _const_spec(lnb.shape), _const_spec(w_in.shape),
            rope_tile, meta_p,
        ] + cast_specs,
        out_specs=[pl.BlockSpec((tm, D_MODEL), out_tile), pl.BlockSpec((tm, IN_WIDTH), out_tile),
                   pool_state] + cast_specs,
        out_shape=[jax.ShapeDtypeStruct((n, D_MODEL), F32), jax.ShapeDtypeStruct((n, IN_WIDTH), F32),
                   jax.ShapeDtypeStruct((POOL_BUF, n_seq, POOL_WIDTH), F32)]
        + [jax.ShapeDtypeStruct(w.shape, BF16) for w in cast_weights],
        scratch_shapes=[
            pltpu.VMEM((tm, D_MODEL), BF16),
            pltpu.VMEM((tm, D_MODEL), F32),
            pltpu.VMEM((tm, D_MODEL), BF16),
            pltpu.VMEM((tm, D_FF), BF16),
            pltpu.VMEM((N_META + tm, POOL_WIDTH), F32),
        ],
        compiler_params=pltpu.CompilerParams(
            dimension_semantics=("arbitrary",), vmem_limit_bytes=VMEM_LIMIT_BYTES),
        name="ffn_proj_pipelined",
    )(x, wg, wu, wd, lng, lnb, w_in, rope, proj_small, *cast_weights)


def _rope_tables(positions):
    half = HEAD_DIM // 2
    inv_freq = ROPE_THETA ** (-np.arange(0, HEAD_DIM, 2, dtype=np.float64) / HEAD_DIM)
    ang = np.asarray(positions, np.float64)[:, None] * inv_freq[None, :]
    cos, sin = np.cos(ang), np.sin(ang)
    assert cos.shape[1] == half
    return (np.concatenate([cos, cos], axis=1).astype(np.float32),
            np.concatenate([-sin, sin], axis=1).astype(np.float32))


def _decay_tables(chunk, seq_len):
    r = np.arange(chunk)
    seq, idx = r // seq_len, (r % seq_len).astype(np.float64)
    same = seq[:, None] == seq[None, :]
    diff = idx[:, None] - idx[None, :]
    mask, qdec, kdec = [], [], []
    for gamma in GAMMAS:
        lg = math.log(gamma)
        mask.append(np.where(same & (diff >= 0), np.exp(lg * np.maximum(diff, 0.0)), 0.0) * QK_SCALE)
        qdec.append(np.broadcast_to((np.exp(lg * (idx + 1.0)) * QK_SCALE)[:, None], (chunk, HEAD_DIM)))
        kdec.append(np.broadcast_to(np.exp(lg * (seq_len - 1.0 - idx))[:, None], (chunk, HEAD_DIM)))
    to32 = lambda t: np.stack(t).astype(np.float32)
    return to32(mask), to32(qdec), to32(kdec)


def _rope(x, cos, sin):
    return x * cos + pltpu.roll(x, HEAD_DIM // 2, 1) * sin


def _group_norm(o):
    mu = jnp.mean(o, axis=-1, keepdims=True)
    oc = o - mu
    var = jnp.mean(oc * oc, axis=-1, keepdims=True)
    return oc * lax.rsqrt(var + GN_EPS)


def _prompt_mixer_ffn_kernel(proj_ref, x1_ref,
                             km_ref, vm_ref, cosm_ref, sinm_ref, kdecm_ref,
                             mask_ref, qdec_ref, kdec_ref, poolw_ref, pscale_ref, wout_ref,
                             ln2g_ref, ln2b_ref, wg_ref, wu_ref, wd_ref, ln3g_ref, ln3b_ref,
                             x2dec_hbm,
                             y_ref, sret_ref, ydec_hbm,
                             s_ref, mix_ref, x2_ref, xb_ref, ypre_ref, h_ref, kb_ref, vs_ref, pwbd_ref,
                             *, steps_per_seq):
    t = pl.program_id(0)
    n_tiles = pl.num_programs(0) - 2
    c = t % steps_per_seq
    tile = proj_ref.shape[0]
    q_ref, k_ref, v_ref, g_ref, p_ref = (
        proj_ref.at[:, j * RET_WIDTH:(j + 1) * RET_WIDTH] for j in range(5))
    row_blocks = [slice(r, r + RET_CHUNK) for r in range(0, tile, RET_CHUNK)]

    @pl.when(jnp.logical_and(c == 0, t < n_tiles))
    def _init_from_meta():
        for h in range(RET_HEADS):
            hs = slice(h * HEAD_DIM, (h + 1) * HEAD_DIM)
            kr = _rope(km_ref[:, hs], cosm_ref[...], sinm_ref[...])
            kd = (kr * kdecm_ref[h]).astype(BF16)
            s_ref[h] = _dot_tn(kd, vm_ref[:, hs].astype(BF16))

    def ln3_fetch_piece(rows):
        def run():
            y_ref[rows, :] = ypre_ref[rows, :]
            yield
        return run

    def ln3_piece(rows):
        def run():
            y_ref[rows, :] = _layer_norm(y_ref[rows, :], ln3g_ref[...], ln3b_ref[...])
            yield
        return run

    state = [None] * RET_HEADS

    def retention_piece(ci, hp):
        def run():
            rows = slice(ci * RET_CHUNK, (ci + 1) * RET_CHUNK)
            pair = slice(hp * PAIR_WIDTH, (hp + 1) * PAIR_WIDTH)
            buf = ci * 2 + hp
            qr, kr = [], []
            for j in range(2):
                h = 2 * hp + j
                hs = slice(h * HEAD_DIM, (h + 1) * HEAD_DIM)
                blk = slice(j * HEAD_DIM, (j + 1) * HEAD_DIM)
                qr.append(q_ref[rows, hs])
                kr.append(k_ref[rows, hs])
                kb_ref[buf, blk, blk] = kr[j].astype(BF16)
                vs_ref[buf, blk, blk] = v_ref[rows, hs].astype(BF16)
                vs_ref[buf, PAIR_WIDTH + j * HEAD_DIM:PAIR_WIDTH + (j + 1) * HEAD_DIM, blk] = (
                    state[h].astype(BF16))
            q2 = jnp.concatenate(qr, axis=1)
            k2 = jnp.concatenate(kr, axis=1)
            scores = _dot_nt(q2.astype(BF16), kb_ref[buf])
            upd = _dot((k2 * kdec_ref[hp]).T.astype(BF16), v_ref[rows, pair].astype(BF16))
            qd2 = (q2 * qdec_ref[hp]).astype(BF16)
            yield
            lhs = jnp.concatenate([(scores * mask_ref[hp]).astype(BF16), qd2], axis=1)
            o2 = _dot(lhs, vs_ref[buf])
            for j in range(2):
                blk = slice(j * HEAD_DIM, (j + 1) * HEAD_DIM)
                state[2 * hp + j] = (GAMMAS[2 * hp + j] ** RET_CHUNK) * state[2 * hp + j] + upd[blk, blk]
            yield
            for j in range(2):
                hs = slice((2 * hp + j) * HEAD_DIM, (2 * hp + j + 1) * HEAD_DIM)
                blk = slice(j * HEAD_DIM, (j + 1) * HEAD_DIM)
                mix_ref[rows, hs] = (_silu(g_ref[rows, hs]) * _group_norm(o2[:, blk])).astype(BF16)
        return run

    def pool_piece(pp):
        def run():
            pair = slice(pp * PAIR_WIDTH, (pp + 1) * PAIR_WIDTH)
            pooled = _dot(p_ref[:, pair].astype(BF16), pwbd_ref[pp])
            yield
            pooled = pooled * pscale_ref[:, pair]
            mix_ref[:, RET_WIDTH + pp * PAIR_WIDTH:RET_WIDTH + (pp + 1) * PAIR_WIDTH] = pooled.astype(BF16)
        return run

    def gate_up_piece(ck):
        def run():
            sl = slice(ck * FFN_COL_CHUNK, (ck + 1) * FFN_COL_CHUNK)
            g = _dot(xb_ref[...], wg_ref[:, sl])
            u = _dot(xb_ref[...], wu_ref[:, sl])
            h_ref[:, sl] = (_silu(g) * u).astype(BF16)
        return run

    def residual_piece():
        ypre_ref[...] = ALPHA * x2_ref[...]
        yield

    def w_out_piece():
        for h in range(RET_HEADS):
            s_ref[h] = state[h]
        x2_ref[...] = ALPHA * x1_ref[...] + _dot(mix_ref[...], wout_ref[...])

    def ln2_piece(rows):
        def run():
            x2_ref[rows, :] = _layer_norm(x2_ref[rows, :], ln2g_ref[...], ln2b_ref[...])
            yield
        return run

    def xb_piece(rows):
        def run():
            xb_ref[rows, :] = x2_ref[rows, :].astype(BF16)
            yield
        return run

    def down_piece(nk):
        def run():
            cs = slice(nk * FFN_COL_CHUNK, (nk + 1) * FFN_COL_CHUNK)
            ypre_ref[:, cs] = ypre_ref[:, cs] + 0.5 * _dot(h_ref[...], wd_ref[:, cs])
        return run

    def steady_body():
        starts = {}

        def start_at(i, piece):
            starts.setdefault(i, []).append(piece)

        for rows in row_blocks:
            start_at(0, ln3_fetch_piece(rows))
        start_at(1, residual_piece)
        for h in range(RET_HEADS):
            state[h] = s_ref[h]
        gate_up = [gate_up_piece(ck) for ck in range(D_FF // FFN_COL_CHUNK)]
        pieces = [(ci, hp) for ci in range(len(row_blocks)) for hp in range(RET_HEADS // 2)]
        w_out_at = len(pieces)
        for n, (ci, hp) in enumerate(pieces):
            start_at(n // 2 if n < 4 else n - 2, retention_piece(ci, hp))
        for pp in range(len(POOL_WINDOWS) // 2):
            start_at(2 * pp + 1, pool_piece(pp))
        major = gate_up[:w_out_at] + [w_out_piece] + gate_up[w_out_at:]
        for i, rows in enumerate(row_blocks):
            start_at(min(w_out_at + i, len(major) - 1), ln2_piece(rows))
        _interleave(major, starts)

        tail_starts = {i: [ln3_piece(rows), xb_piece(rows)] for i, rows in enumerate(row_blocks)}
        _interleave([down_piece(nk) for nk in range(D_MODEL // FFN_COL_CHUNK)], tail_starts)

    def drain_body():
        for rows in row_blocks:
            for make in (ln3_fetch_piece(rows), ln3_piece(rows)):
                for _ in make():
                    pass

    @pl.when(t == 0)
    def _prime_pipeline():
        pltpu.sync_copy(x2dec_hbm, x2_ref)
        xb_ref[...] = x2_ref[...].astype(BF16)
        kb_ref[...] = jnp.zeros(kb_ref.shape, BF16)
        vs_ref[...] = jnp.zeros(vs_ref.shape, BF16)
        ypre_ref[...] = jnp.zeros(ypre_ref.shape, F32)
        pwbd_ref[...] = jnp.zeros(pwbd_ref.shape, BF16)
        for gi in range(len(POOL_WINDOWS)):
            blk = slice((gi % 2) * POOL_GROUP, (gi % 2 + 1) * POOL_GROUP)
            pwbd_ref[gi // 2, blk, blk] = poolw_ref[gi].astype(BF16)

    @pl.when(t <= n_tiles)
    def _steady():
        steady_body()

    @pl.when(t == 1)
    def _emit_decode_rows():
        pltpu.sync_copy(y_ref, ydec_hbm)

    @pl.when(t == n_tiles + 1)
    def _drain_last():
        drain_body()

    @pl.when(jnp.logical_and(c == steps_per_seq - 1, t < n_tiles))
    def _emit_state():
        sret_ref[0] = s_ref[...]


def _prompt_mixer_ffn(proj, x1, proj_small, meta_row_block, pool_w, pool_scale, w_out, ln2g, ln2b,
                      wg, wu, wd, ln3g, ln3b, x2_dec, batch, seq):
    tile = MIX_TOKEN_TILE
    assert x2_dec.shape == (tile, D_MODEL)
    steps = seq // tile
    n_tiles = batch * steps
    cosm, sinm = _rope_tables(np.arange(N_META))
    pair_up = lambda tab: np.concatenate([tab[0::2], tab[1::2]], axis=2)
    mask, qdec, kdec = (pair_up(tab) for tab in _decay_tables(RET_CHUNK, RET_CHUNK))
    _, _, kdecm = _decay_tables(N_META, N_META)

    mix_tile = lambda t: jnp.minimum(t, n_tiles - 1)
    ffn_tile = lambda t: jnp.maximum(t - 2, 0)

    def meta_col(j):
        return pl.BlockSpec((N_META, RET_WIDTH), lambda t: (meta_row_block, j))

    in_specs = [
        pl.BlockSpec((tile, IN_WIDTH), lambda t: (mix_tile(t), 0)),
        pl.BlockSpec((tile, D_MODEL), lambda t: (mix_tile(t), 0)),
        meta_col(1), meta_col(2),
        _const_spec(cosm.shape), _const_spec(sinm.shape), _const_spec(kdecm.shape),
        _const_spec(mask.shape), _const_spec(qdec.shape), _const_spec(kdec.shape),
        _const_spec(pool_w.shape), _const_spec(pool_scale.shape), _const_spec(w_out.shape),
        _const_spec(ln2g.shape), _const_spec(ln2b.shape),
        _const_spec(wg.shape), _const_spec(wu.shape), _const_spec(wd.shape),
        _const_spec(ln3g.shape), _const_spec(ln3b.shape),
        pl.BlockSpec(memory_space=pl.ANY),
    ]
    out_shape = [
        jax.ShapeDtypeStruct((batch * seq, D_MODEL), F32),
        jax.ShapeDtypeStruct((batch, RET_HEADS, HEAD_DIM, HEAD_DIM), F32),
        jax.ShapeDtypeStruct((tile, D_MODEL), F32),
    ]
    out_specs = [
        pl.BlockSpec((tile, D_MODEL), lambda t: (ffn_tile(t), 0)),
        pl.BlockSpec((1, RET_HEADS, HEAD_DIM, HEAD_DIM), lambda t: (mix_tile(t) // steps, 0, 0, 0)),
        pl.BlockSpec(memory_space=pl.ANY),
    ]
    return pl.pallas_call(
        functools.partial(_prompt_mixer_ffn_kernel, steps_per_seq=steps),
        grid=(n_tiles + 2,),
        in_specs=in_specs,
        out_specs=out_specs,
        out_shape=out_shape,
        scratch_shapes=[
            pltpu.VMEM((RET_HEADS, HEAD_DIM, HEAD_DIM), F32),
            pltpu.VMEM((tile, D_MODEL), BF16),
            pltpu.VMEM((tile, D_MODEL), F32),
            pltpu.VMEM((tile, D_MODEL), BF16),
            pltpu.VMEM((tile, D_MODEL), F32),
            pltpu.VMEM((tile, D_FF), BF16),
            pltpu.VMEM((8, PAIR_WIDTH, PAIR_WIDTH), BF16),
            pltpu.VMEM((8, 2 * PAIR_WIDTH, PAIR_WIDTH), BF16),
            pltpu.VMEM((len(POOL_WINDOWS) // 2, PAIR_WIDTH, PAIR_WIDTH), BF16),
        ],
        compiler_params=pltpu.CompilerParams(
            dimension_semantics=("arbitrary",), vmem_limit_bytes=VMEM_LIMIT_BYTES),
        name="prompt_mixer_ffn",
    )(proj, x1, proj_small, proj_small,
      cosm, sinm, kdecm, mask, qdec, kdec, pool_w, pool_scale, w_out, ln2g, ln2b,
      wg, wu, wd, ln3g, ln3b, x2_dec)


def _decode_mixer_kernel(q_ref, k_ref, v_ref, g_ref, *rest, dec_seq):
    p_refs, rest = rest[:len(POOL_WINDOWS)], rest[len(POOL_WINDOWS):]
    (x1_ref, s0_ref, pref_ref, cos_ref, sin_ref, mask_ref, qdec_ref, kdec_ref,
     poolw_ref, pscale_ref, wout_ref, lng_ref, lnb_ref,
     o_ref, sret_ref, spool_ref, d_ref, mix_ref) = rest
    rows = q_ref.shape[0]
    nseq = rows // dec_seq
    seq_per_group = BF16_ROWS // dec_seq
    cos, sin = cos_ref[...], sin_ref[...]
    row_seq = lax.broadcasted_iota(jnp.int32, (BF16_ROWS, HEAD_DIM), 0) // dec_seq

    for h in range(RET_HEADS):
        hs = slice(h * HEAD_DIM, (h + 1) * HEAD_DIM)
        qr = _rope(q_ref[:, hs], cos, sin)
        kr = _rope(k_ref[:, hs], cos, sin)
        v = v_ref[:, hs]
        vb = v.astype(BF16)
        scores = _dot_nt(qr.astype(BF16), kr.astype(BF16)) * mask_ref[h]
        o_inner = _dot(scores.astype(BF16), vb)
        qd = qr * qdec_ref[h]
        kd = kr * kdec_ref[h]
        o_cross = []
        for grp in range(rows // BF16_ROWS):
            gr = slice(grp * BF16_ROWS, (grp + 1) * BF16_ROWS)
            qd_g = qd[gr].astype(BF16)
            kd_g = kd[gr].astype(BF16)
            v_g = v[gr]
            acc = jnp.zeros((BF16_ROWS, HEAD_DIM), F32)
            for j in range(seq_per_group):
                b = grp * seq_per_group + j
                s = s0_ref[b, h]
                acc = jnp.where(row_seq == j, _dot(qd_g, s.astype(BF16)), acc)
                v_b = jnp.where(row_seq == j, v_g, 0.0).astype(BF16)
                sret_ref[b, h] = (GAMMAS[h] ** dec_seq) * s + _dot_tn(kd_g, v_b)
            o_cross.append(acc)
        o = o_inner + jnp.concatenate(o_cross, axis=0)
        mix_ref[:, hs] = (_silu(g_ref[:, hs]) * _group_norm(o)).astype(BF16)

    for gi, (w, pg_ref) in enumerate(zip(POOL_WINDOWS, p_refs)):
        gs = slice(gi * POOL_GROUP, (gi + 1) * POOL_GROUP)
        steps = [pref_ref[j, :, gs] for j in range(POOL_BUF)]
        steps += [pg_ref[pl.ds(i, nseq, stride=dec_seq), :] for i in range(dec_seq)]
        for i in range(dec_seq):
            now = POOL_BUF + i
            wsum = steps[now]
            for back in range(1, w):
                wsum = wsum + steps[now - back]
            d_ref[gi, pl.ds(i, nseq, stride=dec_seq), :] = wsum * (1.0 / w) - steps[now]
        for j in range(POOL_BUF):
            spool_ref[j, :, gs] = steps[dec_seq + j]
        pooled = _dot(d_ref[gi].astype(BF16), poolw_ref[gi].astype(BF16)) * pscale_ref[:, gs]
        mix_ref[:, RET_WIDTH + gi * POOL_GROUP:RET_WIDTH + (gi + 1) * POOL_GROUP] = pooled.astype(BF16)

    y = _dot(mix_ref[...], wout_ref[...])
    o_ref[...] = _layer_norm(ALPHA * x1_ref[...] + y, lng_ref[...], lnb_ref[...])


def _decode_mixer(proj, x1, state_ret, state_pool, pool_w, pool_scale, w_out, lng, lnb, nseq, dec_seq):
    assert BF16_ROWS % dec_seq == 0 and dec_seq <= POOL_BUF
    rows = DEC_SEQ_BLOCK * dec_seq
    steps = nseq // DEC_SEQ_BLOCK
    cos, sin = _rope_tables(PAST_LEN + (np.arange(rows) % dec_seq))
    mask, qdec, kdec = _decay_tables(rows, dec_seq)

    def col(j):
        return pl.BlockSpec((rows, RET_WIDTH), lambda i: (i, j))

    state_spec = pl.BlockSpec((DEC_SEQ_BLOCK, RET_HEADS, HEAD_DIM, HEAD_DIM), lambda i: (i, 0, 0, 0))
    state_pool = jnp.transpose(state_pool, (1, 0, 2))
    pool_spec = pl.BlockSpec((POOL_BUF, DEC_SEQ_BLOCK, POOL_WIDTH), lambda i: (0, i, 0))
    groups = len(POOL_WINDOWS)
    p_cols = 4 * RET_WIDTH // POOL_GROUP
    in_specs = [
        col(0), col(1), col(2), col(3),
        *[pl.BlockSpec((rows, POOL_GROUP), lambda i, gi=gi: (i, p_cols + gi)) for gi in range(groups)],
        pl.BlockSpec((rows, D_MODEL), lambda i: (i, 0)),
        state_spec, pool_spec,
        _const_spec(cos.shape), _const_spec(sin.shape),
        _const_spec(mask.shape), _const_spec(qdec.shape), _const_spec(kdec.shape),
        _const_spec(pool_w.shape), _const_spec(pool_scale.shape), _const_spec(w_out.shape),
        _const_spec(lng.shape), _const_spec(lnb.shape),
    ]
    out_shape = [
        jax.ShapeDtypeStruct((nseq * dec_seq, D_MODEL), F32),
        jax.ShapeDtypeStruct(state_ret.shape, F32),
        jax.ShapeDtypeStruct(state_pool.shape, F32),
    ]
    out_specs = [pl.BlockSpec((rows, D_MODEL), lambda i: (i, 0)), state_spec, pool_spec]
    x2, new_ret, new_pool = pl.pallas_call(
        functools.partial(_decode_mixer_kernel, dec_seq=dec_seq),
        grid=(steps,),
        in_specs=in_specs,
        out_specs=out_specs,
        out_shape=out_shape,
        scratch_shapes=[
            pltpu.VMEM((groups, rows, POOL_GROUP), F32),
            pltpu.VMEM((rows, D_MODEL), BF16),
        ],
        compiler_params=pltpu.CompilerParams(
            dimension_semantics=("arbitrary",), vmem_limit_bytes=VMEM_LIMIT_BYTES),
        name="decode_mixer",
    )(*([proj] * (4 + groups)), x1, state_ret, state_pool, cos, sin, mask, qdec, kdec,
      pool_w, pool_scale, w_out, lng, lnb)
    return x2, new_ret, jnp.transpose(new_pool, (1, 0, 2))


def kernel(x_prompt, x_sample, state_ret, state_pool, meta_tokens, ffn1_w_gate, ffn1_w_up, ffn1_w_down,
           ln1_g, ln1_b, w_in, pool_w, pool_scale, w_out, ln2_g, ln2_b, ffn2_w_gate, ffn2_w_up,
           ffn2_w_down, ln3_g, ln3_b):
    assert ffn1_w_gate.shape[0] == DEPTH == 1
    batch, seq, _ = x_prompt.shape
    nseq, dec_seq, _ = x_sample.shape
    n_dec = nseq * dec_seq
    assert n_dec % N_META == 0

    row = lambda v: v[0].reshape(1, -1)
    pool_w_b = pool_w[0]
    pscale, g2, b2 = row(pool_scale), row(ln2_g), row(ln2_b)

    xp = x_prompt.reshape(batch * seq, D_MODEL)

    x1s, projs, wg1, wu1, wd1, w_in_b = _ffn_proj_cast(
        x_sample.reshape(n_dec, D_MODEL), meta_tokens.astype(x_prompt.dtype),
        ffn1_w_gate[0], ffn1_w_up[0], ffn1_w_down[0], row(ln1_g), row(ln1_b), w_in[0])
    meta_rows = n_dec // N_META
    x1p, projp, pool_p, wg2, wu2, wd2, w_out_b = _ffn_proj_pipelined(
        xp, wg1, wu1, wd1, row(ln1_g), row(ln1_b), w_in_b,
        (ffn2_w_gate[0], ffn2_w_up[0], ffn2_w_down[0], w_out[0]), seq, projs, meta_rows)
    f2 = (wg2, wu2, wd2, row(ln3_g), row(ln3_b))

    x2s, ret_s, pool_s = _decode_mixer(projs, x1s, state_ret[0], state_pool[0], pool_w_b, pscale,
                                       w_out_b, g2, b2, nseq, dec_seq)
    y_prompt, ret_p, y_sample = _prompt_mixer_ffn(
        projp, x1p, projs, meta_rows, pool_w_b, pscale, w_out_b, g2, b2, *f2, x2s, batch, seq)
    return (y_prompt.reshape(batch, seq, D_MODEL), y_sample.reshape(nseq, dec_seq, D_MODEL),
            ret_p[None], jnp.transpose(pool_p, (1, 0, 2))[None], ret_s[None], pool_s[None])
```

```python
import functools
import math

import jax
import jax.numpy as jnp
import numpy as np
from jax import lax
from jax.experimental import pallas as pl
from jax.experimental.pallas import tpu as pltpu

F32 = jnp.float32
BF16 = jnp.bfloat16

D_MODEL = 1024
D_FF = 2816
N_META = 16
PAST_LEN = 16384
RET_HEADS = 4
HEAD_DIM = 128
RET_WIDTH = RET_HEADS * HEAD_DIM
RET_CHUNK = 128
ROPE_THETA = 10000.0
POOL_WINDOWS = (2, 4, 8, 16)
POOL_GROUP = 128
POOL_WIDTH = POOL_GROUP * len(POOL_WINDOWS)
POOL_BUF = max(POOL_WINDOWS) - 1
IN_WIDTH = 4 * RET_WIDTH + POOL_WIDTH
DEPTH = 1
ALPHA = (2.0 * DEPTH) ** 0.25
LN_EPS = 1e-5
GN_EPS = 1e-5
QK_SCALE = HEAD_DIM ** -0.5
GAMMAS = tuple(1.0 - 2.0 ** (-5.0 - h) for h in range(RET_HEADS))

VMEM_LIMIT_BYTES = 56 * 1024 * 1024
FFN_TOKEN_TILE = 512
FFN_COL_CHUNK = 256
MIX_TOKEN_TILE = 512
DEC_SEQ_BLOCK = 16
BF16_ROWS = 16
PAIR_WIDTH = 2 * HEAD_DIM
RING_SLOTS = 3


def _layer_norm(z, g, b):
    mu = jnp.mean(z, axis=-1, keepdims=True)
    zc = z - mu
    var = jnp.mean(zc * zc, axis=-1, keepdims=True)
    return zc * lax.rsqrt(var + LN_EPS) * g + b


def _silu(x):
    return x * jax.nn.sigmoid(x)


def _dot(a, b):
    return jnp.dot(a, b, preferred_element_type=F32)


def _dot_nt(a, b):
    return lax.dot_general(a, b, (((1,), (1,)), ((), ())), preferred_element_type=F32)


def _dot_tn(a, b):
    return lax.dot_general(a, b, (((0,), (0,)), ((), ())), preferred_element_type=F32)


def _const_spec(shape):
    zeros = (0,) * len(shape)
    return pl.BlockSpec(shape, lambda *_: zeros, pipeline_mode=pl.Buffered(1))


def _ffn_proj_cast_kernel(xd_ref, xm_ref, wg_ref, wu_ref, wd_ref, lng_ref, lnb_ref, win_ref,
                          x1_ref, proj_ref, wgb_ref, wub_ref, wdb_ref, winb_ref,
                          acc_ref, xb_ref, win_all_ref, wg_buf, wu_buf, wd_buf, win_buf, x_ref, sems):
    k = pl.program_id(0)
    ck = FFN_COL_CHUNK
    in_chunks = IN_WIDTH // ck
    streams = ((wg_ref, wg_buf, D_FF // ck, 1), (wu_ref, wu_buf, D_FF // ck, 1),
               (wd_ref, wd_buf, D_FF // ck, 0), (win_ref, win_buf, in_chunks, 1))

    def chunk_copy(stream, chunk):
        src, ring, _, axis = streams[stream]
        idx = (pl.ds(chunk * ck, ck), slice(None)) if axis == 0 else (slice(None), pl.ds(chunk * ck, ck))
        slot = chunk % RING_SLOTS
        return pltpu.make_async_copy(src.at[idx], ring.at[slot], sems.at[stream, slot])

    def start_chunk(chunk):
        for stream, (_, _, n_chunks, _) in enumerate(streams):
            @pl.when(chunk < n_chunks)
            def _(stream=stream):
                chunk_copy(stream, chunk).start()

    @pl.when(k == 0)
    def _start():
        for ahead in range(RING_SLOTS - 1):
            start_chunk(jnp.int32(ahead))
        n_dec = xd_ref.shape[0]
        x_ref[0:n_dec, :] = xd_ref[...]
        x_ref[n_dec:, :] = xm_ref[...]
        xb_ref[...] = x_ref[...].astype(BF16)
        acc_ref[...] = jnp.zeros(acc_ref.shape, F32)

    start_chunk(k + RING_SLOTS - 1)
    slot = k % RING_SLOTS
    for stream in range(3):
        chunk_copy(stream, k).wait()
    wg, wu, wd = wg_buf[slot].astype(BF16), wu_buf[slot].astype(BF16), wd_buf[slot].astype(BF16)
    wgb_ref[...] = wg
    wub_ref[...] = wu
    wdb_ref[...] = wd
    h = (_silu(_dot(xb_ref[...], wg)) * _dot(xb_ref[...], wu)).astype(BF16)
    acc_ref[...] += _dot(h, wd)

    for j in range(in_chunks):
        @pl.when(k == j)
        def _round_w_in_chunk(j=j):
            chunk_copy(3, j).wait()
            w_in = win_buf[j % RING_SLOTS].astype(BF16)
            winb_ref[...] = w_in
            win_all_ref[:, j * ck:(j + 1) * ck] = w_in

    @pl.when(k == pl.num_programs(0) - 1)
    def _finish():
        x1 = _layer_norm(ALPHA * x_ref[...] + 0.5 * acc_ref[...], lng_ref[...], lnb_ref[...])
        x1_ref[...] = x1
        proj_ref[...] = _dot(x1.astype(BF16), win_all_ref[...])


def _ffn_proj_cast(x_dec, x_meta, wg, wu, wd, lng, lnb, w_in):
    n = x_dec.shape[0] + x_meta.shape[0]
    ck = FFN_COL_CHUNK
    steps = D_FF // ck
    in_chunks = IN_WIDTH // ck
    assert in_chunks <= steps
    col_chunk = pl.BlockSpec((D_MODEL, ck), lambda k: (0, k))
    row_chunk = pl.BlockSpec((ck, D_MODEL), lambda k: (k, 0))
    w_in_chunk = pl.BlockSpec((D_MODEL, ck), lambda k: (0, jnp.minimum(k, in_chunks - 1)))
    whole = lambda shape: pl.BlockSpec(shape, lambda k: (0,) * len(shape))
    in_hbm = pl.BlockSpec(memory_space=pl.ANY)
    return pl.pallas_call(
        _ffn_proj_cast_kernel,
        grid=(steps,),
        in_specs=[_const_spec(x_dec.shape), _const_spec(x_meta.shape), in_hbm, in_hbm, in_hbm,
                  _const_spec(lng.shape), _const_spec(lnb.shape), in_hbm],
        out_specs=[whole((n, D_MODEL)), whole((n, IN_WIDTH)), col_chunk, col_chunk, row_chunk,
                   w_in_chunk],
        out_shape=[jax.ShapeDtypeStruct((n, D_MODEL), F32), jax.ShapeDtypeStruct((n, IN_WIDTH), F32),
                   jax.ShapeDtypeStruct(wg.shape, BF16), jax.ShapeDtypeStruct(wu.shape, BF16),
                   jax.ShapeDtypeStruct(wd.shape, BF16), jax.ShapeDtypeStruct(w_in.shape, BF16)],
        scratch_shapes=[pltpu.VMEM((n, D_MODEL), F32), pltpu.VMEM((n, D_MODEL), BF16),
                        pltpu.VMEM(w_in.shape, BF16),
                        pltpu.VMEM((RING_SLOTS, D_MODEL, ck), F32), pltpu.VMEM((RING_SLOTS, D_MODEL, ck), F32),
                        pltpu.VMEM((RING_SLOTS, ck, D_MODEL), F32), pltpu.VMEM((RING_SLOTS, D_MODEL, ck), F32),
                        pltpu.VMEM((n, D_MODEL), F32),
                        pltpu.SemaphoreType.DMA((4, RING_SLOTS))],
        compiler_params=pltpu.CompilerParams(
            dimension_semantics=("arbitrary",), vmem_limit_bytes=VMEM_LIMIT_BYTES),
        name="ffn_proj_cast",
    )(x_dec, x_meta, wg, wu, wd, lng, lnb, w_in)


def _interleave(major, starts):
    live = []
    for i, piece in enumerate(major):
        piece()
        live += [make() for make in starts.get(i, [])]
        live = [g for g in live if next(g, "done") != "done"]
    while live:
        live = [g for g in live if next(g, "done") != "done"]


def _run_all(starts):
    for i in sorted(starts):
        for make in starts[i]:
            for _ in make():
                pass


def _ffn_proj_pipelined_kernel(x_ref, wg_ref, wu_ref, wd_ref, lng_ref, lnb_ref, win_ref,
                               rope_ref, pm_ref, *rest, n_cast, steps_per_seq):
    cast_in, rest = rest[:n_cast], rest[n_cast:]
    x1_ref, proj_ref, spool_ref = rest[:3]
    cast_out, (xb_ref, z1_ref, x1b_ref, h_ref, xp_ref) = rest[3:3 + n_cast], rest[3 + n_cast:]
    t = pl.program_id(0)
    n_tiles = pl.num_programs(0) - 1
    tile = x_ref.shape[0]
    row_blocks = [slice(r, r + RET_CHUNK) for r in range(0, tile, RET_CHUNK)]
    proj_chunk = 2 * FFN_COL_CHUNK
    hist = N_META
    seq_step = (t - 1) % steps_per_seq

    def ln1_piece(rows):
        def run():
            x1_rows = _layer_norm(z1_ref[rows, :], lng_ref[...], lnb_ref[...])
            x1_ref[rows, :] = x1_rows
            x1b_ref[rows, :] = x1_rows.astype(BF16)
            yield
        return run

    def proj_piece(nk):
        def run():
            cs = slice(nk * proj_chunk, (nk + 1) * proj_chunk)
            chunk = _dot(x1b_ref[...], win_ref[:, cs])
            if nk < 2:
                cos, sin = rope_ref[:, :HEAD_DIM], rope_ref[:, HEAD_DIM:]
                chunk = jnp.concatenate(
                    [_rope(chunk[:, h * HEAD_DIM:(h + 1) * HEAD_DIM], cos, sin)
                     for h in range(RET_HEADS)], axis=1)
            proj_ref[:, cs] = chunk
        return run

    def pool_input_piece(half):
        def run():
            cs = slice(half * PAIR_WIDTH, (half + 1) * PAIR_WIDTH)
            xp_ref[hist:hist + tile, cs] = _dot(
                x1b_ref[...], win_ref[:, 4 * RET_WIDTH + half * PAIR_WIDTH:4 * RET_WIDTH + (half + 1) * PAIR_WIDTH])
        return run

    def pool_diff_piece(gi):
        def run():
            w = POOL_WINDOWS[gi]
            gs = slice(gi * POOL_GROUP, (gi + 1) * POOL_GROUP)
            rows_all = xp_ref[:, gs]
            wsum, shift = rows_all, 1
            while shift < w:
                wsum = wsum + pltpu.roll(wsum, shift, 0)
                shift *= 2
            proj_ref[:, 4 * RET_WIDTH + gi * POOL_GROUP:4 * RET_WIDTH + (gi + 1) * POOL_GROUP] = (
                wsum[hist:] * (1.0 / w) - rows_all[hist:])
            xp_ref[0:hist, gs] = xp_ref[tile:tile + hist, gs]
            yield
        return run

    def gate_up_piece(ck):
        def run():
            sl = slice(ck * FFN_COL_CHUNK, (ck + 1) * FFN_COL_CHUNK)
            g = _dot(xb_ref[...], wg_ref[:, sl])
            u = _dot(xb_ref[...], wu_ref[:, sl])
            h_ref[:, sl] = (_silu(g) * u).astype(BF16)
        return run

    def down_piece(nk):
        def run():
            cs = slice(nk * FFN_COL_CHUNK, (nk + 1) * FFN_COL_CHUNK)
            z1_ref[:, cs] = ALPHA * x_ref[:, cs] + 0.5 * _dot(h_ref[...], wd_ref[:, cs])
        return run

    ln1_starts = {i: [ln1_piece(rows)] for i, rows in enumerate(row_blocks)}
    qkvg = [proj_piece(nk) for nk in range(4 * RET_WIDTH // proj_chunk)]
    proj_pieces = [pool_input_piece(0)] + qkvg[:2] + [pool_input_piece(1)] + qkvg[2:]
    pool_diffs = [pool_diff_piece(gi) for gi in range(len(POOL_WINDOWS))]

    @pl.when(jnp.logical_and(t >= 1, seq_step == 0))
    def _history_from_meta():
        xp_ref[0:hist, :] = pm_ref[...]

    @pl.when(t == 0)
    def _clear_pipeline():
        z1_ref[...] = jnp.zeros(z1_ref.shape, F32)
        xp_ref[0:hist, :] = jnp.zeros((hist, POOL_WIDTH), F32)

    @pl.when(t < n_tiles)
    def _steady():
        xb_ref[...] = x_ref[...].astype(BF16)
        for src_ref, dst_ref in zip(cast_in, cast_out):
            dst_ref[...] = src_ref[...].astype(BF16)
        gate_up = [gate_up_piece(ck) for ck in range(D_FF // FFN_COL_CHUNK)]
        first_proj = len(row_blocks) + 1
        major = gate_up[:first_proj]
        for i, piece in enumerate(gate_up[first_proj:]):
            major += proj_pieces[i:i + 1] + [piece]
        major += proj_pieces[len(gate_up) - first_proj:]
        starts = dict(ln1_starts)
        half = len(pool_diffs) // 2
        starts[len(major) - 2] = pool_diffs[:half]
        _interleave(major, starts)
        _interleave([down_piece(nk) for nk in range(D_MODEL // FFN_COL_CHUNK)],
                    {i: [piece] for i, piece in enumerate(pool_diffs[half:])})

    @pl.when(t == n_tiles)
    def _drain():
        _run_all(ln1_starts)
        for piece in proj_pieces:
            piece()
        _run_all({0: pool_diffs})

    @pl.when(jnp.logical_and(t >= 1, seq_step == steps_per_seq - 1))
    def _emit_pool_state():
        spool_ref[0] = xp_ref[hist - POOL_BUF:hist, :]


def _slab_rows(rows, max_slabs):
    for slab in range(BF16_ROWS, rows + 1, BF16_ROWS):
        if rows % slab == 0 and rows // slab <= max_slabs:
            return slab
    raise ValueError(f"no slab size for {rows} rows in {max_slabs} steps")


def _ffn_proj_pipelined(x, wg, wu, wd, lng, lnb, w_in, cast_weights, seq, proj_small, meta_row_block):
    n = x.shape[0]
    tm = FFN_TOKEN_TILE
    n_tiles = n // tm
    steps_per_seq = seq // tm
    n_seq = n // seq
    assert seq % tm == 0 and IN_WIDTH % (2 * FFN_COL_CHUNK) == 0 and RET_WIDTH == 2 * FFN_COL_CHUNK
    rope = np.concatenate(_rope_tables(N_META + np.arange(seq)), axis=1)
    in_tile = lambda t: (jnp.minimum(t, n_tiles - 1), 0)
    out_tile = lambda t: (jnp.maximum(t - 1, 0), 0)
    rope_tile = pl.BlockSpec((tm, 2 * HEAD_DIM), lambda t: (jnp.maximum(t - 1, 0) % (seq // tm), 0))

    def slab_spec(w):
        slab = _slab_rows(w.shape[0], n_tiles)
        last = w.shape[0] // slab - 1
        return pl.BlockSpec((slab, w.shape[1]), lambda t: (jnp.minimum(t, last), 0))

    cast_specs = [slab_spec(w) for w in cast_weights]
    meta_p = pl.BlockSpec((N_META, POOL_WIDTH), lambda t: (meta_row_block, 4 * RET_WIDTH // POOL_WIDTH))
    pool_state = pl.BlockSpec(
        (1, POOL_BUF, POOL_WIDTH), lambda t: (jnp.clip((t - 1) // steps_per_seq, 0, n_seq - 1), 0, 0))
    return pl.pallas_call(
        functools.partial(_ffn_proj_pipelined_kernel, n_cast=len(cast_weights),
                          steps_per_seq=steps_per_seq),
        grid=(n_tiles + 1,),
        in_specs=[
            pl.BlockSpec((tm, D_MODEL), in_tile),
            _const_spec(wg.shape), _const_spec(wu.shape), _const_spec(wd.shape),
            _const_spec(lng.shape), _const_spec(lnb.shape), _const_spec(w_in.shape),
            rope_tile, meta_p,
        ] + cast_specs,
        out_specs=[pl.BlockSpec((tm, D_MODEL), out_tile), pl.BlockSpec((tm, IN_WIDTH), out_tile),
                   pool_state] + cast_specs,
        out_shape=[jax.ShapeDtypeStruct((n, D_MODEL), F32), jax.ShapeDtypeStruct((n, IN_WIDTH), F32),
                   jax.ShapeDtypeStruct((n_seq, POOL_BUF, POOL_WIDTH), F32)]
        + [jax.ShapeDtypeStruct(w.shape, BF16) for w in cast_weights],
        scratch_shapes=[
            pltpu.VMEM((tm, D_MODEL), BF16),
            pltpu.VMEM((tm, D_MODEL), F32),
            pltpu.VMEM((tm, D_MODEL), BF16),
            pltpu.VMEM((tm, D_FF), BF16),
            pltpu.VMEM((N_META + tm, POOL_WIDTH), F32),
        ],
        compiler_params=pltpu.CompilerParams(
            dimension_semantics=("arbitrary",), vmem_limit_bytes=VMEM_LIMIT_BYTES),
        name="ffn_proj_pipelined",
    )(x, wg, wu, wd, lng, lnb, w_in, rope, proj_small, *cast_weights)


def _rope_tables(positions):
    half = HEAD_DIM // 2
    inv_freq = ROPE_THETA ** (-np.arange(0, HEAD_DIM, 2, dtype=np.float64) / HEAD_DIM)
    ang = np.asarray(positions, np.float64)[:, None] * inv_freq[None, :]
    cos, sin = np.cos(ang), np.sin(ang)
    assert cos.shape[1] == half
    return (np.concatenate([cos, cos], axis=1).astype(np.float32),
            np.concatenate([-sin, sin], axis=1).astype(np.float32))


def _decay_tables(chunk, seq_len):
    r = np.arange(chunk)
    seq, idx = r // seq_len, (r % seq_len).astype(np.float64)
    same = seq[:, None] == seq[None, :]
    diff = idx[:, None] - idx[None, :]
    mask, qdec, kdec = [], [], []
    for gamma in GAMMAS:
        lg = math.log(gamma)
        mask.append(np.where(same & (diff >= 0), np.exp(lg * np.maximum(diff, 0.0)), 0.0) * QK_SCALE)
        qdec.append(np.broadcast_to((np.exp(lg * (idx + 1.0)) * QK_SCALE)[:, None], (chunk, HEAD_DIM)))
        kdec.append(np.broadcast_to(np.exp(lg * (seq_len - 1.0 - idx))[:, None], (chunk, HEAD_DIM)))
    to32 = lambda t: np.stack(t).astype(np.float32)
    return to32(mask), to32(qdec), to32(kdec)


def _rope(x, cos, sin):
    return x * cos + pltpu.roll(x, HEAD_DIM // 2, 1) * sin


def _group_norm(o):
    mu = jnp.mean(o, axis=-1, keepdims=True)
    oc = o - mu
    var = jnp.mean(oc * oc, axis=-1, keepdims=True)
    return oc * lax.rsqrt(var + GN_EPS)


def _prompt_mixer_ffn_kernel(proj_ref, x1_ref,
                             km_ref, vm_ref, cosm_ref, sinm_ref, kdecm_ref,
                             mask_ref, qdec_ref, kdec_ref, poolw_ref, pscale_ref, wout_ref,
                             ln2g_ref, ln2b_ref, wg_ref, wu_ref, wd_ref, ln3g_ref, ln3b_ref,
                             x2dec_hbm,
                             y_ref, sret_ref, ydec_hbm,
                             s_ref, mix_ref, x2_ref, xb_ref, ypre_ref, h_ref, kb_ref, vs_ref, pwbd_ref,
                             *, steps_per_seq):
    t = pl.program_id(0)
    n_tiles = pl.num_programs(0) - 2
    c = t % steps_per_seq
    tile = proj_ref.shape[0]
    q_ref, k_ref, v_ref, g_ref, p_ref = (
        proj_ref.at[:, j * RET_WIDTH:(j + 1) * RET_WIDTH] for j in range(5))
    row_blocks = [slice(r, r + RET_CHUNK) for r in range(0, tile, RET_CHUNK)]

    @pl.when(jnp.logical_and(c == 0, t < n_tiles))
    def _init_from_meta():
        for h in range(RET_HEADS):
            hs = slice(h * HEAD_DIM, (h + 1) * HEAD_DIM)
            kr = _rope(km_ref[:, hs], cosm_ref[...], sinm_ref[...])
            kd = (kr * kdecm_ref[h]).astype(BF16)
            s_ref[h] = _dot_tn(kd, vm_ref[:, hs].astype(BF16))

    def ln3_fetch_piece(rows):
        def run():
            y_ref[rows, :] = ypre_ref[rows, :]
            yield
        return run

    def ln3_piece(rows):
        def run():
            y_ref[rows, :] = _layer_norm(y_ref[rows, :], ln3g_ref[...], ln3b_ref[...])
            yield
        return run

    state = [None] * RET_HEADS

    def retention_piece(ci, hp):
        def run():
            rows = slice(ci * RET_CHUNK, (ci + 1) * RET_CHUNK)
            pair = slice(hp * PAIR_WIDTH, (hp + 1) * PAIR_WIDTH)
            buf = ci * 2 + hp
            qr, kr = [], []
            for j in range(2):
                h = 2 * hp + j
                hs = slice(h * HEAD_DIM, (h + 1) * HEAD_DIM)
                blk = slice(j * HEAD_DIM, (j + 1) * HEAD_DIM)
                qr.append(q_ref[rows, hs])
                kr.append(k_ref[rows, hs])
                kb_ref[buf, blk, blk] = kr[j].astype(BF16)
                vs_ref[buf, blk, blk] = v_ref[rows, hs].astype(BF16)
                vs_ref[buf, PAIR_WIDTH + j * HEAD_DIM:PAIR_WIDTH + (j + 1) * HEAD_DIM, blk] = (
                    state[h].astype(BF16))
            q2 = jnp.concatenate(qr, axis=1)
            k2 = jnp.concatenate(kr, axis=1)
            scores = _dot_nt(q2.astype(BF16), kb_ref[buf])
            upd = _dot((k2 * kdec_ref[hp]).T.astype(BF16), v_ref[rows, pair].astype(BF16))
            qd2 = (q2 * qdec_ref[hp]).astype(BF16)
            yield
            lhs = jnp.concatenate([(scores * mask_ref[hp]).astype(BF16), qd2], axis=1)
            o2 = _dot(lhs, vs_ref[buf])
            for j in range(2):
                blk = slice(j * HEAD_DIM, (j + 1) * HEAD_DIM)
                state[2 * hp + j] = (GAMMAS[2 * hp + j] ** RET_CHUNK) * state[2 * hp + j] + upd[blk, blk]
            yield
            for j in range(2):
                hs = slice((2 * hp + j) * HEAD_DIM, (2 * hp + j + 1) * HEAD_DIM)
                blk = slice(j * HEAD_DIM, (j + 1) * HEAD_DIM)
                mix_ref[rows, hs] = (_silu(g_ref[rows, hs]) * _group_norm(o2[:, blk])).astype(BF16)
        return run

    def pool_piece(pp):
        def run():
            pair = slice(pp * PAIR_WIDTH, (pp + 1) * PAIR_WIDTH)
            pooled = _dot(p_ref[:, pair].astype(BF16), pwbd_ref[pp])
            yield
            pooled = pooled * pscale_ref[:, pair]
            mix_ref[:, RET_WIDTH + pp * PAIR_WIDTH:RET_WIDTH + (pp + 1) * PAIR_WIDTH] = pooled.astype(BF16)
        return run

    def gate_up_piece(ck):
        def run():
            sl = slice(ck * FFN_COL_CHUNK, (ck + 1) * FFN_COL_CHUNK)
            g = _dot(xb_ref[...], wg_ref[:, sl])
            u = _dot(xb_ref[...], wu_ref[:, sl])
            h_ref[:, sl] = (_silu(g) * u).astype(BF16)
        return run

    def residual_piece():
        ypre_ref[...] = ALPHA * x2_ref[...]
        yield

    def w_out_piece():
        for h in range(RET_HEADS):
            s_ref[h] = state[h]
        x2_ref[...] = ALPHA * x1_ref[...] + _dot(mix_ref[...], wout_ref[...])

    def ln2_piece(rows):
        def run():
            x2_ref[rows, :] = _layer_norm(x2_ref[rows, :], ln2g_ref[...], ln2b_ref[...])
            yield
        return run

    def xb_piece(rows):
        def run():
            xb_ref[rows, :] = x2_ref[rows, :].astype(BF16)
            yield
        return run

    def down_piece(nk):
        def run():
            cs = slice(nk * FFN_COL_CHUNK, (nk + 1) * FFN_COL_CHUNK)
            ypre_ref[:, cs] = ypre_ref[:, cs] + 0.5 * _dot(h_ref[...], wd_ref[:, cs])
        return run

    def steady_body():
        starts = {}

        def start_at(i, piece):
            starts.setdefault(i, []).append(piece)

        for rows in row_blocks:
            start_at(0, ln3_fetch_piece(rows))
        start_at(1, residual_piece)
        for h in range(RET_HEADS):
            state[h] = s_ref[h]
        gate_up = [gate_up_piece(ck) for ck in range(D_FF // FFN_COL_CHUNK)]
        pieces = [(ci, hp) for ci in range(len(row_blocks)) for hp in range(RET_HEADS // 2)]
        w_out_at = len(pieces)
        for n, (ci, hp) in enumerate(pieces):
            start_at(n // 2 if n < 4 else n - 2, retention_piece(ci, hp))
        for pp in range(len(POOL_WINDOWS) // 2):
            start_at(2 * pp + 1, pool_piece(pp))
        major = gate_up[:w_out_at] + [w_out_piece] + gate_up[w_out_at:]
        for i, rows in enumerate(row_blocks):
            start_at(min(w_out_at + i, len(major) - 1), ln2_piece(rows))
        _interleave(major, starts)

        tail_starts = {i: [ln3_piece(rows), xb_piece(rows)] for i, rows in enumerate(row_blocks)}
        _interleave([down_piece(nk) for nk in range(D_MODEL // FFN_COL_CHUNK)], tail_starts)

    def drain_body():
        for rows in row_blocks:
            for make in (ln3_fetch_piece(rows), ln3_piece(rows)):
                for _ in make():
                    pass

    @pl.when(t == 0)
    def _prime_pipeline():
        pltpu.sync_copy(x2dec_hbm, x2_ref)
        xb_ref[...] = x2_ref[...].astype(BF16)
        kb_ref[...] = jnp.zeros(kb_ref.shape, BF16)
        vs_ref[...] = jnp.zeros(vs_ref.shape, BF16)
        ypre_ref[...] = jnp.zeros(ypre_ref.shape, F32)
        pwbd_ref[...] = jnp.zeros(pwbd_ref.shape, BF16)
        for gi in range(len(POOL_WINDOWS)):
            blk = slice((gi % 2) * POOL_GROUP, (gi % 2 + 1) * POOL_GROUP)
            pwbd_ref[gi // 2, blk, blk] = poolw_ref[gi].astype(BF16)

    @pl.when(t <= n_tiles)
    def _steady():
        steady_body()

    @pl.when(t == 1)
    def _emit_decode_rows():
        pltpu.sync_copy(y_ref, ydec_hbm)

    @pl.when(t == n_tiles + 1)
    def _drain_last():
        drain_body()

    @pl.when(jnp.logical_and(c == steps_per_seq - 1, t < n_tiles))
    def _emit_state():
        sret_ref[0] = s_ref[...]


def _prompt_mixer_ffn(proj, x1, proj_small, meta_row_block, pool_w, pool_scale, w_out, ln2g, ln2b,
                      wg, wu, wd, ln3g, ln3b, x2_dec, batch, seq):
    tile = MIX_TOKEN_TILE
    assert x2_dec.shape == (tile, D_MODEL)
    steps = seq // tile
    n_tiles = batch * steps
    cosm, sinm = _rope_tables(np.arange(N_META))
    pair_up = lambda tab: np.concatenate([tab[0::2], tab[1::2]], axis=2)
    mask, qdec, kdec = (pair_up(tab) for tab in _decay_tables(RET_CHUNK, RET_CHUNK))
    _, _, kdecm = _decay_tables(N_META, N_META)

    mix_tile = lambda t: jnp.minimum(t, n_tiles - 1)
    ffn_tile = lambda t: jnp.maximum(t - 2, 0)

    def meta_col(j):
        return pl.BlockSpec((N_META, RET_WIDTH), lambda t: (meta_row_block, j))

    in_specs = [
        pl.BlockSpec((tile, IN_WIDTH), lambda t: (mix_tile(t), 0)),
        pl.BlockSpec((tile, D_MODEL), lambda t: (mix_tile(t), 0)),
        meta_col(1), meta_col(2),
        _const_spec(cosm.shape), _const_spec(sinm.shape), _const_spec(kdecm.shape),
        _const_spec(mask.shape), _const_spec(qdec.shape), _const_spec(kdec.shape),
        _const_spec(pool_w.shape), _const_spec(pool_scale.shape), _const_spec(w_out.shape),
        _const_spec(ln2g.shape), _const_spec(ln2b.shape),
        _const_spec(wg.shape), _const_spec(wu.shape), _const_spec(wd.shape),
        _const_spec(ln3g.shape), _const_spec(ln3b.shape),
        pl.BlockSpec(memory_space=pl.ANY),
    ]
    out_shape = [
        jax.ShapeDtypeStruct((batch * seq, D_MODEL), F32),
        jax.ShapeDtypeStruct((batch, RET_HEADS, HEAD_DIM, HEAD_DIM), F32),
        jax.ShapeDtypeStruct((tile, D_MODEL), F32),
    ]
    out_specs = [
        pl.BlockSpec((tile, D_MODEL), lambda t: (ffn_tile(t), 0)),
        pl.BlockSpec((1, RET_HEADS, HEAD_DIM, HEAD_DIM), lambda t: (mix_tile(t) // steps, 0, 0, 0)),
        pl.BlockSpec(memory_space=pl.ANY),
    ]
    return pl.pallas_call(
        functools.partial(_prompt_mixer_ffn_kernel, steps_per_seq=steps),
        grid=(n_tiles + 2,),
        in_specs=in_specs,
        out_specs=out_specs,
        out_shape=out_shape,
        scratch_shapes=[
            pltpu.VMEM((RET_HEADS, HEAD_DIM, HEAD_DIM), F32),
            pltpu.VMEM((tile, D_MODEL), BF16),
            pltpu.VMEM((tile, D_MODEL), F32),
            pltpu.VMEM((tile, D_MODEL), BF16),
            pltpu.VMEM((tile, D_MODEL), F32),
            pltpu.VMEM((tile, D_FF), BF16),
            pltpu.VMEM((8, PAIR_WIDTH, PAIR_WIDTH), BF16),
            pltpu.VMEM((8, 2 * PAIR_WIDTH, PAIR_WIDTH), BF16),
            pltpu.VMEM((len(POOL_WINDOWS) // 2, PAIR_WIDTH, PAIR_WIDTH), BF16),
        ],
        compiler_params=pltpu.CompilerParams(
            dimension_semantics=("arbitrary",), vmem_limit_bytes=VMEM_LIMIT_BYTES),
        name="prompt_mixer_ffn",
    )(proj, x1, proj_small, proj_small,
      cosm, sinm, kdecm, mask, qdec, kdec, pool_w, pool_scale, w_out, ln2g, ln2b,
      wg, wu, wd, ln3g, ln3b, x2_dec)


def _decode_mixer_kernel(q_ref, k_ref, v_ref, g_ref, *rest, dec_seq):
    p_refs, rest = rest[:len(POOL_WINDOWS)], rest[len(POOL_WINDOWS):]
    (x1_ref, s0_hbm, pref_ref, cos_ref, sin_ref, mask_ref, qdec_ref, kdec_ref,
     poolw_ref, pscale_ref, wout_ref, lng_ref, lnb_ref,
     o_ref, sret_ref, spool_ref, d_ref, mix_ref, s0_buf, s0_sems) = rest
    rows = q_ref.shape[0]
    nseq = rows // dec_seq
    seq_per_group = BF16_ROWS // dec_seq

    step, n_steps = pl.program_id(0), pl.num_programs(0)

    def state_copy(block):
        slot = block % RING_SLOTS
        return pltpu.make_async_copy(s0_hbm.at[pl.ds(block * nseq, nseq)], s0_buf.at[slot], s0_sems.at[slot])

    def start_block(block):
        @pl.when(block < n_steps)
        def _():
            state_copy(block).start()

    @pl.when(step == 0)
    def _prime_ring():
        for ahead in range(RING_SLOTS - 1):
            start_block(jnp.int32(ahead))

    start_block(step + RING_SLOTS - 1)
    state_copy(step).wait()
    s0_ref = s0_buf.at[step % RING_SLOTS]
    cos, sin = cos_ref[...], sin_ref[...]
    row_seq = lax.broadcasted_iota(jnp.int32, (BF16_ROWS, HEAD_DIM), 0) // dec_seq

    for h in range(RET_HEADS):
        hs = slice(h * HEAD_DIM, (h + 1) * HEAD_DIM)
        qr = _rope(q_ref[:, hs], cos, sin)
        kr = _rope(k_ref[:, hs], cos, sin)
        v = v_ref[:, hs]
        vb = v.astype(BF16)
        scores = _dot_nt(qr.astype(BF16), kr.astype(BF16)) * mask_ref[h]
        o_inner = _dot(scores.astype(BF16), vb)
        qd = qr * qdec_ref[h]
        kd = kr * kdec_ref[h]
        o_cross = []
        for grp in range(rows // BF16_ROWS):
            gr = slice(grp * BF16_ROWS, (grp + 1) * BF16_ROWS)
            qd_g = qd[gr].astype(BF16)
            kd_g = kd[gr].astype(BF16)
            v_g = v[gr]
            acc = jnp.zeros((BF16_ROWS, HEAD_DIM), F32)
            for j in range(seq_per_group):
                b = grp * seq_per_group + j
                s = s0_ref[b, h]
                acc = jnp.where(row_seq == j, _dot(qd_g, s.astype(BF16)), acc)
                v_b = jnp.where(row_seq == j, v_g, 0.0).astype(BF16)
                sret_ref[b, h] = (GAMMAS[h] ** dec_seq) * s + _dot_tn(kd_g, v_b)
            o_cross.append(acc)
        o = o_inner + jnp.concatenate(o_cross, axis=0)
        mix_ref[:, hs] = (_silu(g_ref[:, hs]) * _group_norm(o)).astype(BF16)

    for gi, (w, pg_ref) in enumerate(zip(POOL_WINDOWS, p_refs)):
        gs = slice(gi * POOL_GROUP, (gi + 1) * POOL_GROUP)
        steps = [pref_ref[j, :, gs] for j in range(POOL_BUF)]
        steps += [pg_ref[pl.ds(i, nseq, stride=dec_seq), :] for i in range(dec_seq)]
        for i in range(dec_seq):
            now = POOL_BUF + i
            wsum = steps[now]
            for back in range(1, w):
                wsum = wsum + steps[now - back]
            d_ref[gi, pl.ds(i, nseq, stride=dec_seq), :] = wsum * (1.0 / w) - steps[now]
        for j in range(POOL_BUF):
            spool_ref[j, :, gs] = steps[dec_seq + j]
        pooled = _dot(d_ref[gi].astype(BF16), poolw_ref[gi].astype(BF16)) * pscale_ref[:, gs]
        mix_ref[:, RET_WIDTH + gi * POOL_GROUP:RET_WIDTH + (gi + 1) * POOL_GROUP] = pooled.astype(BF16)

    y = _dot(mix_ref[...], wout_ref[...])
    o_ref[...] = _layer_norm(ALPHA * x1_ref[...] + y, lng_ref[...], lnb_ref[...])


def _decode_mixer(proj, x1, state_ret, state_pool, pool_w, pool_scale, w_out, lng, lnb, nseq, dec_seq):
    assert BF16_ROWS % dec_seq == 0 and dec_seq <= POOL_BUF
    rows = DEC_SEQ_BLOCK * dec_seq
    steps = nseq // DEC_SEQ_BLOCK
    cos, sin = _rope_tables(PAST_LEN + (np.arange(rows) % dec_seq))
    mask, qdec, kdec = _decay_tables(rows, dec_seq)

    def col(j):
        return pl.BlockSpec((rows, RET_WIDTH), lambda i: (i, j))

    state_spec = pl.BlockSpec((DEC_SEQ_BLOCK, RET_HEADS, HEAD_DIM, HEAD_DIM), lambda i: (i, 0, 0, 0))
    state_pool = jnp.transpose(state_pool, (1, 0, 2))
    pool_spec = pl.BlockSpec((POOL_BUF, DEC_SEQ_BLOCK, POOL_WIDTH), lambda i: (0, i, 0))
    groups = len(POOL_WINDOWS)
    p_cols = 4 * RET_WIDTH // POOL_GROUP
    in_specs = [
        col(0), col(1), col(2), col(3),
        *[pl.BlockSpec((rows, POOL_GROUP), lambda i, gi=gi: (i, p_cols + gi)) for gi in range(groups)],
        pl.BlockSpec((rows, D_MODEL), lambda i: (i, 0)),
        pl.BlockSpec(memory_space=pl.ANY), pool_spec,
        _const_spec(cos.shape), _const_spec(sin.shape),
        _const_spec(mask.shape), _const_spec(qdec.shape), _const_spec(kdec.shape),
        _const_spec(pool_w.shape), _const_spec(pool_scale.shape), _const_spec(w_out.shape),
        _const_spec(lng.shape), _const_spec(lnb.shape),
    ]
    out_shape = [
        jax.ShapeDtypeStruct((nseq * dec_seq, D_MODEL), F32),
        jax.ShapeDtypeStruct(state_ret.shape, F32),
        jax.ShapeDtypeStruct(state_pool.shape, F32),
    ]
    out_specs = [pl.BlockSpec((rows, D_MODEL), lambda i: (i, 0)), state_spec, pool_spec]
    x2, new_ret, new_pool = pl.pallas_call(
        functools.partial(_decode_mixer_kernel, dec_seq=dec_seq),
        grid=(steps,),
        in_specs=in_specs,
        out_specs=out_specs,
        out_shape=out_shape,
        scratch_shapes=[
            pltpu.VMEM((groups, rows, POOL_GROUP), F32),
            pltpu.VMEM((rows, D_MODEL), BF16),
            pltpu.VMEM((RING_SLOTS, DEC_SEQ_BLOCK, RET_HEADS, HEAD_DIM, HEAD_DIM), F32),
            pltpu.SemaphoreType.DMA((RING_SLOTS,)),
        ],
        compiler_params=pltpu.CompilerParams(
            dimension_semantics=("arbitrary",), vmem_limit_bytes=VMEM_LIMIT_BYTES),
        name="decode_mixer",
    )(*([proj] * (4 + groups)), x1, state_ret, state_pool, cos, sin, mask, qdec, kdec,
      pool_w, pool_scale, w_out, lng, lnb)
    return x2, new_ret, jnp.transpose(new_pool, (1, 0, 2))


def kernel(x_prompt, x_sample, state_ret, state_pool, meta_tokens, ffn1_w_gate, ffn1_w_up, ffn1_w_down,
           ln1_g, ln1_b, w_in, pool_w, pool_scale, w_out, ln2_g, ln2_b, ffn2_w_gate, ffn2_w_up,
           ffn2_w_down, ln3_g, ln3_b):
    assert ffn1_w_gate.shape[0] == DEPTH == 1
    batch, seq, _ = x_prompt.shape
    nseq, dec_seq, _ = x_sample.shape
    n_dec = nseq * dec_seq
    assert n_dec % N_META == 0

    row = lambda v: v[0].reshape(1, -1)
    pool_w_b = pool_w[0]
    pscale, g2, b2 = row(pool_scale), row(ln2_g), row(ln2_b)

    xp = x_prompt.reshape(batch * seq, D_MODEL)

    x1s, projs, wg1, wu1, wd1, w_in_b = _ffn_proj_cast(
        x_sample.reshape(n_dec, D_MODEL), meta_tokens.astype(x_prompt.dtype),
        ffn1_w_gate[0], ffn1_w_up[0], ffn1_w_down[0], row(ln1_g), row(ln1_b), w_in[0])
    meta_rows = n_dec // N_META
    x1p, projp, pool_p, wg2, wu2, wd2, w_out_b = _ffn_proj_pipelined(
        xp, wg1, wu1, wd1, row(ln1_g), row(ln1_b), w_in_b,
        (ffn2_w_gate[0], ffn2_w_up[0], ffn2_w_down[0], w_out[0]), seq, projs, meta_rows)
    f2 = (wg2, wu2, wd2, row(ln3_g), row(ln3_b))

    x2s, ret_s, pool_s = _decode_mixer(projs, x1s, state_ret[0], state_pool[0], pool_w_b, pscale,
                                       w_out_b, g2, b2, nseq, dec_seq)
    y_prompt, ret_p, y_sample = _prompt_mixer_ffn(
        projp, x1p, projs, meta_rows, pool_w_b, pscale, w_out_b, g2, b2, *f2, x2s, batch, seq)
    return (y_prompt.reshape(batch, seq, D_MODEL), y_sample.reshape(nseq, dec_seq, D_MODEL),
            ret_p[None], pool_p[None], ret_s[None], pool_s[None])
```
